```python
import math
import jax
import jax.numpy as jnp
from jax import lax
import numpy as np

D_MODEL = 1024
BATCH = 8
SEQ = 2048
DEPTH = 2

N_BRANCH = 4
MIX_W = D_MODEL // 2
RMS_EPS = 1e-6
HG_DK = 128
HG_HEADS = MIX_W // HG_DK
HG_DV = MIX_W // HG_HEADS
HG_CHUNK = 64
HG_TINY = 1e-30
RW_HEAD = 64
RW_HEADS = MIX_W // RW_HEAD
RW_DECAY_LORA = 64
RW_A_LORA = 64
RW_V_LORA = 32
RW_G_LORA = 128
RW_LN_EPS = 64e-5
S5_GROUP = 16
S5_GROUPS = MIX_W // S5_GROUP
S5_STATE = 64
MB_HEADDIM = 64
MB_HEADS = MIX_W // MB_HEADDIM
MB_GROUPS = 2
MB_STATE = 128
MB_CONV = 4
MB_CHUNK = 128
FF_HIDDEN = -(-8 * D_MODEL // (3 * 256)) * 256
GATE_COLS = N_BRANCH * D_MODEL
HG_COLS = 4 * MIX_W
RW_COLS = 3 * MIX_W + RW_DECAY_LORA + RW_A_LORA + RW_G_LORA
S5_COLS = MIX_W
MB_XBC = MIX_W + 2 * MB_GROUPS * MB_STATE
MB_COLS = MIX_W + MB_XBC + MB_HEADS
IN_COLS = GATE_COLS + HG_COLS + RW_COLS + S5_COLS + MB_COLS

kernel_name = 'hybrid_hgrn2_rwkv7_s5_mamba2_block'


def rmsnorm(x, w):
    xf = x.astype(jnp.float32)
    y = xf * lax.rsqrt(jnp.mean(xf * xf, axis=-1, keepdims=True) + RMS_EPS)
    return (y * w.astype(jnp.float32)).astype(x.dtype)


def causal_shift(z):
    return jnp.pad(z, ((0, 0), (1, 0), (0, 0)))[:, :-1]


def segsum_exp(a):
    T = a.shape[-1]
    rep = jnp.broadcast_to(a[..., :, None], a.shape + (T,))
    strict = jnp.tril(jnp.ones((T, T), dtype=bool), -1)
    cs = jnp.cumsum(jnp.where(strict, rep, 0.0), axis=-2)
    return jnp.where(jnp.tril(jnp.ones((T, T), dtype=bool)), jnp.exp(cs), 0.0)


def hgrn2_mixer(z, lower_bound, norm_w):
    bsz, seq, _ = z.shape
    nc = seq // HG_CHUNK
    q, f, i, g = jnp.split(z.astype(jnp.float32), 4, axis=-1)
    lb = lower_bound.astype(jnp.float32)
    q = jax.nn.silu(q)
    log_f = jnp.log(jnp.maximum(lb + (1.0 - lb) * jax.nn.sigmoid(f), HG_TINY))
    k = (1.0 - lb) * jax.nn.sigmoid(-f)

    def to_chunks(t):
        return t.reshape(bsz, nc, HG_CHUNK, HG_HEADS, -1).transpose(1, 0, 3, 2, 4)

    causal = jnp.tril(jnp.ones((HG_CHUNK, HG_CHUNK), dtype=bool))[None, None, :, :, None]

    def chunk_step(state, inp):
        qc, kc, vc, lfc = inp
        b = jnp.cumsum(lfc, axis=2)
        diff = b[:, :, :, None, :] - b[:, :, None, :, :]
        decay = jnp.where(causal, jnp.exp(jnp.where(causal, diff, 0.0)), 0.0)
        att = jnp.sum(qc[:, :, :, None, :] * kc[:, :, None, :, :] * decay, axis=-1)
        o = jnp.einsum('bhts,bhsv->bhtv', att, vc) + jnp.einsum('bhtk,bhkv->bhtv', qc * jnp.exp(b), state)
        b_last = b[:, :, -1:, :]
        state = (jnp.exp(b_last[:, :, 0, :])[..., None] * state
                 + jnp.einsum('bhsk,bhsv->bhkv', kc * jnp.exp(b_last - b), vc))
        return state, o

    state0 = jnp.zeros((bsz, HG_HEADS, HG_DK, HG_DV), jnp.float32)
    _, o = lax.scan(chunk_step, state0, (to_chunks(q), to_chunks(k), to_chunks(i), to_chunks(log_f)))
    o = o.transpose(1, 0, 3, 2, 4).reshape(bsz, seq, HG_HEADS, HG_DV)
    o = rmsnorm(o, norm_w.reshape(HG_HEADS, HG_DV)).reshape(bsz, seq, MIX_W)
    return (o * jax.nn.silu(g)).astype(z.dtype)


def rwkv7_recurrence(r, w, k, v, a, b):
    bsz, _, h, n = r.shape

    def step(state, inp):
        r_t, w_t, k_t, v_t, a_t, b_t = inp
        sa = jnp.einsum('bhvk,bhk->bhv', state, a_t)
        state = (state * w_t[:, :, None, :] + sa[..., None] * b_t[:, :, None, :]
                 + v_t[..., None] * k_t[:, :, None, :])
        return state, jnp.einsum('bhvk,bhk->bhv', state, r_t)

    xs = (jnp.moveaxis(r, 1, 0), jnp.moveaxis(w, 1, 0), jnp.moveaxis(k, 1, 0),
          jnp.moveaxis(v, 1, 0), jnp.moveaxis(a, 1, 0), jnp.moveaxis(b, 1, 0))
    _, y = lax.scan(step, jnp.zeros((bsz, h, n, n), jnp.float32), xs)
    return jnp.moveaxis(y, 0, 1)


def rwkv7_mixer(z, v_first, vres, mu, w0, w2, a0, a2, g2, k_k, k_a, r_k, ln_w, ln_b):
    bsz, seq, _ = z.shape
    zf = z.astype(jnp.float32)
    zs = zf + (causal_shift(zf) - zf) * mu
    r, k, v, xw, xa, xg = jnp.split(
        zs, [MIX_W, 2 * MIX_W, 3 * MIX_W, 3 * MIX_W + RW_DECAY_LORA,
             3 * MIX_W + RW_DECAY_LORA + RW_A_LORA], axis=-1)
    w_log = -jax.nn.softplus(-(w0 + jnp.tanh(xw) @ w2)) - 0.5
    decay = jnp.exp(-jnp.exp(w_log))
    if vres is None:
        v_first = v
    else:
        v0, v1, v2 = vres
        v = v + (v_first - v) * jax.nn.sigmoid(v0 + (v @ v1) @ v2)
    a = jax.nn.sigmoid(a0 + xa @ a2)
    g = jax.nn.sigmoid(xg) @ g2

    def heads(t):
        return t.reshape(bsz, seq, RW_HEADS, RW_HEAD)

    kk = heads(k * k_k)
    kk = kk / jnp.maximum(jnp.sqrt(jnp.sum(kk * kk, axis=-1, keepdims=True)), 1e-12)
    k = heads(k * (1.0 + (a - 1.0) * k_a))
    r, vh, ah = heads(r), heads(v), heads(a)
    y = rwkv7_recurrence(r, heads(decay), k, vh, -kk, kk * ah)
    mean = jnp.mean(y, axis=-1, keepdims=True)
    var = jnp.mean(jnp.square(y - mean), axis=-1, keepdims=True)
    y = ((y - mean) * lax.rsqrt(var + RW_LN_EPS) * ln_w.reshape(RW_HEADS, RW_HEAD)
         + ln_b.reshape(RW_HEADS, RW_HEAD))
    y = y + jnp.sum(r * k * r_k, axis=-1, keepdims=True) * vh
    y = y.reshape(bsz, seq, MIX_W) * g
    return y.astype(z.dtype), v_first


def complex_affine_combine(e1, e2):
    a1r, a1i, b1r, b1i = e1
    a2r, a2i, b2r, b2i = e2
    return (a2r * a1r - a2i * a1i, a2r * a1i + a2i * a1r,
            a2r * b1r - a2i * b1i + b2r, a2r * b1i + a2i * b1r + b2i)


def s5_mixer(z, a_re, a_im, b_re, b_im, c_re, c_im, d_skip, log_dt, w_glu, b_glu):
    bsz, seq, _ = z.shape
    u = z.astype(jnp.float32)
    ug = u.reshape(bsz, seq, S5_GROUPS, S5_GROUP)
    dt = jnp.exp(log_dt)[:, None]
    mag = jnp.exp(dt * a_re)
    lam_re, lam_im = mag * jnp.cos(dt * a_im), mag * jnp.sin(dt * a_im)
    den = a_re * a_re + a_im * a_im
    coef_re = ((lam_re - 1.0) * a_re + lam_im * a_im) / den
    coef_im = (lam_im * a_re - (lam_re - 1.0) * a_im) / den
    bb_re = coef_re[..., None] * b_re - coef_im[..., None] * b_im
    bb_im = coef_re[..., None] * b_im + coef_im[..., None] * b_re
    bu_re = jnp.einsum('bsgc,gnc->bsgn', ug, bb_re)
    bu_im = jnp.einsum('bsgc,gnc->bsgn', ug, bb_im)
    elems = (jnp.broadcast_to(lam_re, bu_re.shape), jnp.broadcast_to(lam_im, bu_re.shape), bu_re, bu_im)
    _, _, h_re, h_im = lax.associative_scan(complex_affine_combine, elems, axis=1)
    y = jnp.einsum('bsgn,gcn->bsgc', h_re, c_re) - jnp.einsum('bsgn,gcn->bsgc', h_im, c_im)
    y = y.reshape(bsz, seq, MIX_W) + d_skip * u
    y = jax.nn.gelu(y)
    y = y * jax.nn.sigmoid(y @ w_glu + b_glu)
    return y.astype(z.dtype)


def ssd_chunked(x, a, b, c):
    bsz, seq, h, p = x.shape
    nc = seq // MB_CHUNK
    x = x.reshape(bsz, nc, MB_CHUNK, h, p)
    b = b.reshape(bsz, nc, MB_CHUNK, h, -1)
    c = c.reshape(bsz, nc, MB_CHUNK, h, -1)
    a = a.reshape(bsz, nc, MB_CHUNK, h).transpose(0, 3, 1, 2)
    a_cum = jnp.cumsum(a, axis=-1)
    scores = jnp.einsum('bclhn,bcshn->bhcls', c, b) * segsum_exp(a)
    y_diag = jnp.einsum('bhcls,bcshp->bclhp', scores, x)
    decay_states = jnp.exp(a_cum[..., -1:] - a_cum)
    states = jnp.einsum('bclhn,bhcl,bclhp->bchpn', b, decay_states, x)
    states = jnp.concatenate([jnp.zeros_like(states[:, :1]), states], axis=1)
    decay_chunk = segsum_exp(jnp.pad(a_cum[..., -1], ((0, 0), (0, 0), (1, 0))))
    states = jnp.einsum('bhzc,bchpn->bzhpn', decay_chunk, states)[:, :-1]
    y_off = jnp.einsum('bclhn,bchpn,bhcl->bclhp', c, states, jnp.exp(a_cum))
    return (y_diag + y_off).reshape(bsz, seq, h, p)


def mamba2_mixer(z, conv_w, conv_b, dt_bias, a_log, d_skip, norm_w):
    bsz, seq, _ = z.shape
    zf = z.astype(jnp.float32)
    gate, xbc, dt_raw = jnp.split(zf, [MIX_W, MIX_W + MB_XBC], axis=-1)
    xp = jnp.pad(xbc, ((0, 0), (MB_CONV - 1, 0), (0, 0)))
    conv = conv_b
    for j in range(MB_CONV):
        conv = conv + xp[:, j:j + seq] * conv_w[j]
    xbc = jax.nn.silu(conv)
    xs, bm, cm = jnp.split(xbc, [MIX_W, MIX_W + MB_GROUPS * MB_STATE], axis=-1)
    rep = MB_HEADS // MB_GROUPS
    bm = jnp.repeat(bm.reshape(bsz, seq, MB_GROUPS, MB_STATE), rep, axis=2)
    cm = jnp.repeat(cm.reshape(bsz, seq, MB_GROUPS, MB_STATE), rep, axis=2)
    xs = xs.reshape(bsz, seq, MB_HEADS, MB_HEADDIM)
    dt = jax.nn.softplus(dt_raw + dt_bias)
    a = -jnp.exp(a_log.astype(jnp.float32))
    y = ssd_chunked(xs * dt[..., None], a * dt, bm, cm) + d_skip[:, None] * xs
    y = rmsnorm(y.reshape(bsz, seq, MIX_W) * jax.nn.silu(gate), norm_w)
    return y.astype(z.dtype)


def swiglu(h, w_in, w_out):
    gate, up = jnp.split(h @ w_in, 2, axis=-1)
    return (jax.nn.silu(gate) * up) @ w_out


def setup_inputs(seed: int = 0) -> dict:
    key = jax.random.key(seed)
    keys = jax.random.split(key, 48)
    counter = [0]

    def nk():
        counter[0] += 1
        return keys[counter[0] - 1]

    def nrm(shape, scale):
        return scale * jax.random.normal(nk(), shape, jnp.float32)

    def unif(shape, lo, hi):
        return jax.random.uniform(nk(), shape, jnp.float32, lo, hi)

    L = DEPTH
    Lv = DEPTH - 1
    dt_mb = jnp.exp(unif((L, MB_HEADS), math.log(1e-3), math.log(1e-1)))
    return {
        'x': nrm((BATCH, SEQ, D_MODEL), 1.0),
        'norm_mix_w': 1.0 + nrm((L, D_MODEL), 0.02),
        'w_in': nrm((L, D_MODEL, IN_COLS), D_MODEL ** -0.5),
        'w_branch': nrm((L, N_BRANCH, MIX_W, D_MODEL), MIX_W ** -0.5),
        'w_out': nrm((L, D_MODEL, D_MODEL), D_MODEL ** -0.5),
        'norm_ffn_w': 1.0 + nrm((L, D_MODEL), 0.02),
        'w_ffn_in': nrm((L, D_MODEL, 2 * FF_HIDDEN), D_MODEL ** -0.5),
        'w_ffn_out': nrm((L, FF_HIDDEN, D_MODEL), FF_HIDDEN ** -0.5),
        'norm_final_w': 1.0 + nrm((D_MODEL,), 0.02),
        'hgrn_lower_bounds': nrm((L, MIX_W), 0.5),
        'hgrn_norm_w': 1.0 + nrm((L, MIX_W), 0.02),
        'rwkv_mu': unif((L, RW_COLS), 0.0, 1.0),
        'rwkv_w0': unif((L, MIX_W), -6.0, 1.0),
        'rwkv_w2': nrm((L, RW_DECAY_LORA, MIX_W), 0.1 * RW_DECAY_LORA ** -0.5),
        'rwkv_a0': nrm((L, MIX_W), 0.1),
        'rwkv_a2': nrm((L, RW_A_LORA, MIX_W), 0.1 * RW_A_LORA ** -0.5),
        'rwkv_g2': nrm((L, RW_G_LORA, MIX_W), RW_G_LORA ** -0.5),
        'rwkv_k_k': 0.85 + nrm((L, MIX_W), 0.05),
        'rwkv_k_a': 1.0 + nrm((L, MIX_W), 0.05),
        'rwkv_r_k': nrm((L, RW_HEADS, RW_HEAD), 0.1),
        'rwkv_ln_w': 1.0 + nrm((L, MIX_W), 0.02),
        'rwkv_ln_b': nrm((L, MIX_W), 0.02),
        'rwkv_v0': 1.0 + nrm((Lv, MIX_W), 0.1),
        'rwkv_v1': nrm((Lv, MIX_W, RW_V_LORA), MIX_W ** -0.5),
        'rwkv_v2': nrm((Lv, RW_V_LORA, MIX_W), 0.1 * RW_V_LORA ** -0.5),
        's5_a_re': -0.5 + nrm((L, S5_GROUPS, S5_STATE), 0.01),
        's5_a_im': math.pi * jnp.arange(S5_STATE, dtype=jnp.float32) + nrm((L, S5_GROUPS, S5_STATE), 0.01),
        's5_b_re': nrm((L, S5_GROUPS, S5_STATE, S5_GROUP), (2 * S5_GROUP) ** -0.5),
        's5_b_im': nrm((L, S5_GROUPS, S5_STATE, S5_GROUP), (2 * S5_GROUP) ** -0.5),
        's5_c_re': nrm((L, S5_GROUPS, S5_GROUP, S5_STATE), S5_STATE ** -0.5),
        's5_c_im': nrm((L, S5_GROUPS, S5_GROUP, S5_STATE), S5_STATE ** -0.5),
        's5_d': 1.0 + nrm((L, MIX_W), 0.1),
        's5_log_dt': unif((L, S5_GROUPS), math.log(1e-3), math.log(1e-1)),
        's5_w_glu': nrm((L, MIX_W, MIX_W), MIX_W ** -0.5),
        's5_b_glu': nrm((L, MIX_W), 0.02),
        'mamba_conv_w': nrm((L, MB_CONV, MB_XBC), MB_CONV ** -0.5),
        'mamba_conv_b': nrm((L, MB_XBC), 0.02),
        'mamba_dt_bias': dt_mb + jnp.log(-jnp.expm1(-dt_mb)),
        'mamba_a_log': jnp.log(unif((L, MB_HEADS), 1.0, 16.0)),
        'mamba_d': 1.0 + nrm((L, MB_HEADS), 0.1),
        'mamba_norm_w': 1.0 + nrm((L, MIX_W), 0.02),
    }


def reference(x, norm_mix_w, w_in, w_branch, w_out, norm_ffn_w, w_ffn_in, w_ffn_out, norm_final_w,
              hgrn_lower_bounds, hgrn_norm_w,
              rwkv_mu, rwkv_w0, rwkv_w2, rwkv_a0, rwkv_a2, rwkv_g2, rwkv_k_k, rwkv_k_a, rwkv_r_k,
              rwkv_ln_w, rwkv_ln_b, rwkv_v0, rwkv_v1, rwkv_v2,
              s5_a_re, s5_a_im, s5_b_re, s5_b_im, s5_c_re, s5_c_im, s5_d, s5_log_dt, s5_w_glu, s5_b_glu,
              mamba_conv_w, mamba_conv_b, mamba_dt_bias, mamba_a_log, mamba_d, mamba_norm_w):
    bsz, seq, _ = x.shape
    p = jax.nn.softmax(hgrn_lower_bounds.astype(jnp.float32), axis=0)
    lower_bounds = jnp.cumsum(p, axis=0) - p[0]
    offs = [GATE_COLS, GATE_COLS + HG_COLS, GATE_COLS + HG_COLS + RW_COLS,
            GATE_COLS + HG_COLS + RW_COLS + S5_COLS]
    v_first = None
    for l in range(DEPTH):
        u = rmsnorm(x, norm_mix_w[l])
        z_gate, z_hg, z_rw, z_s5, z_mb = jnp.split(u @ w_in[l], offs, axis=-1)
        y_hg = hgrn2_mixer(z_hg, lower_bounds[l], hgrn_norm_w[l])
        vres = None if l == 0 else (rwkv_v0[l - 1], rwkv_v1[l - 1], rwkv_v2[l - 1])
        y_rw, v_first = rwkv7_mixer(z_rw, v_first, vres, rwkv_mu[l], rwkv_w0[l], rwkv_w2[l],
                                    rwkv_a0[l], rwkv_a2[l], rwkv_g2[l], rwkv_k_k[l], rwkv_k_a[l],
                                    rwkv_r_k[l], rwkv_ln_w[l], rwkv_ln_b[l])
        y_s5 = s5_mixer(z_s5, s5_a_re[l], s5_a_im[l], s5_b_re[l], s5_b_im[l], s5_c_re[l], s5_c_im[l],
                        s5_d[l], s5_log_dt[l], s5_w_glu[l], s5_b_glu[l])
        y_mb = mamba2_mixer(z_mb, mamba_conv_w[l], mamba_conv_b[l], mamba_dt_bias[l], mamba_a_log[l],
                            mamba_d[l], mamba_norm_w[l])
        branches = jnp.stack([y_hg, y_rw, y_s5, y_mb], axis=2)
        proj = jnp.einsum('bskw,kwd->bskd', branches, w_branch[l])
        gates = jax.nn.sigmoid(z_gate).reshape(bsz, seq, N_BRANCH, D_MODEL)
        x = x + jnp.sum(gates * proj, axis=2) @ w_out[l]
        x = x + swiglu(rmsnorm(x, norm_ffn_w[l]), w_ffn_in[l], w_ffn_out[l])
    return rmsnorm(x, norm_final_w)
```

```python
import functools
import math

import jax
import jax.numpy as jnp
from jax import lax
from jax.experimental import pallas as pl
from jax.experimental.pallas import tpu as pltpu

F32 = jnp.float32
BF16 = jnp.bfloat16

V7X_LANES = 128
V7X_SUBLANES = 8
V7X_VMEM_LIMIT_BYTES = 56 * 1024 * 1024

RMS_EPS = 1e-6
N_BRANCH = 4
HG_DK = 128
HG_TINY = 1e-30
RW_HEAD = 64
RW_DECAY_LORA = 64
RW_A_LORA = 64
RW_G_LORA = 128
RW_LN_EPS = 64e-5
S5_GROUP = 16
S5_STATE = 64
MB_HEADDIM = 64
MB_GROUPS = 2
MB_STATE = 128
MB_CONV = 4
MB_CHUNK = 128


def _cparams(*sem):
    return pltpu.CompilerParams(dimension_semantics=sem, vmem_limit_bytes=V7X_VMEM_LIMIT_BYTES)


def _bdot(a, b):
    return jnp.dot(a.astype(BF16), b.astype(BF16), preferred_element_type=F32)


def _split3(x):
    x1 = x.astype(BF16)
    r1 = x - x1.astype(F32)
    x2 = r1.astype(BF16)
    x3 = (r1 - x2.astype(F32)).astype(BF16)
    return x1, x2, x3


def _dot_exact_rhs(x, m):
    x1, x2, x3 = _split3(x)
    d = functools.partial(jnp.dot, preferred_element_type=F32)
    return d(x1, m) + d(x2, m) + d(x3, m)


def _dot_exact_lhs(m, x):
    x1, x2, x3 = _split3(x)
    d = functools.partial(jnp.dot, preferred_element_type=F32)
    return d(m, x1) + d(m, x2) + d(m, x3)


def _dot_exact_nt(m, x):
    x1, x2, x3 = _split3(x)
    d = functools.partial(lax.dot_general, dimension_numbers=(((1,), (1,)), ((), ())),
                          preferred_element_type=F32)
    return d(m, x1) + d(m, x2) + d(m, x3)


def _softplus(x):
    return jnp.maximum(x, 0.0) + jnp.log(1.0 + jnp.exp(-jnp.abs(x)))


def _silu(x):
    return x * jax.nn.sigmoid(x)


def _rms(x, w):
    return x * lax.rsqrt(jnp.mean(x * x, axis=-1, keepdims=True) + RMS_EPS) * w


def _inproj_kernel(x_ref, nw_ref, w_ref, o_ref, u_ref):
    @pl.when(pl.program_id(1) == 0)
    def _():
        u_ref[...] = _rms(x_ref[...], nw_ref[...]).astype(BF16)

    o_ref[...] = jnp.dot(u_ref[...], w_ref[...], preferred_element_type=F32)


def _inproj(x2d, norm_w, w_bf16, tm, tn):
    t, d = x2d.shape
    n = w_bf16.shape[1]
    return pl.pallas_call(
        _inproj_kernel,
        grid=(t // tm, n // tn),
        in_specs=[pl.BlockSpec((tm, d), lambda i, j: (i, 0)),
                  pl.BlockSpec((1, d), lambda i, j: (0, 0)),
                  pl.BlockSpec((d, tn), lambda i, j: (0, j))],
        out_specs=pl.BlockSpec((tm, tn), lambda i, j: (i, j)),
        out_shape=jax.ShapeDtypeStruct((t, n), F32),
        scratch_shapes=[pltpu.VMEM((tm, d), BF16)],
        compiler_params=_cparams("parallel", "arbitrary"),
        name="inproj",
    )(x2d, norm_w.reshape(1, d), w_bf16)


def _ffn_in_kernel(x_ref, nw_ref, wg_ref, wu_ref, o_ref, u_ref):
    @pl.when(pl.program_id(1) == 0)
    def _():
        u_ref[...] = _rms(x_ref[...], nw_ref[...]).astype(BF16)

    u = u_ref[...]
    gate = jnp.dot(u, wg_ref[...], preferred_element_type=F32)
    up = jnp.dot(u, wu_ref[...], preferred_element_type=F32)
    o_ref[...] = (_silu(gate) * up).astype(BF16)


def _ffn_in(x2d, norm_w, w_bf16, tm, tn):
    t, d = x2d.shape
    ff = w_bf16.shape[1] // 2
    nj = ff // tn
    return pl.pallas_call(
        _ffn_in_kernel,
        grid=(t // tm, nj),
        in_specs=[pl.BlockSpec((tm, d), lambda i, j: (i, 0)),
                  pl.BlockSpec((1, d), lambda i, j: (0, 0)),
                  pl.BlockSpec((d, tn), lambda i, j: (0, j)),
                  pl.BlockSpec((d, tn), lambda i, j: (0, j + nj))],
        out_specs=pl.BlockSpec((tm, tn), lambda i, j: (i, j)),
        out_shape=jax.ShapeDtypeStruct((t, ff), BF16),
        scratch_shapes=[pltpu.VMEM((tm, d), BF16)],
        compiler_params=_cparams("parallel", "arbitrary"),
        name="ffn_in",
    )(x2d, norm_w.reshape(1, d), w_bf16, w_bf16)


def _ffn_out_kernel(a_ref, w_ref, x_ref, fw_ref, o_ref, *, final_norm):
    y = x_ref[...] + jnp.dot(a_ref[...], w_ref[...], preferred_element_type=F32)
    if final_norm:
        y = _rms(y, fw_ref[...])
    o_ref[...] = y


def _ffn_out(act, w_bf16, x2d, final_w, final_norm, tm):
    t, d = x2d.shape
    ff = act.shape[1]
    return pl.pallas_call(
        functools.partial(_ffn_out_kernel, final_norm=final_norm),
        grid=(t // tm,),
        in_specs=[pl.BlockSpec((tm, ff), lambda i: (i, 0)),
                  pl.BlockSpec((ff, d), lambda i: (0, 0)),
                  pl.BlockSpec((tm, d), lambda i: (i, 0)),
                  pl.BlockSpec((1, d), lambda i: (0, 0))],
        out_specs=pl.BlockSpec((tm, d), lambda i: (i, 0)),
        out_shape=jax.ShapeDtypeStruct((t, d), F32),
        compiler_params=_cparams("parallel"),
        name="ffn_out",
    )(act, w_bf16, x2d, final_w.reshape(1, d))


def _merge_kernel(yh_ref, yr_ref, ys_ref, ym_ref, zg_ref, wb_ref, wo_ref, x_ref, o_ref):
    d = x_ref.shape[1]
    acc = jnp.zeros(x_ref.shape, F32)
    for k, y_ref in enumerate((yh_ref, yr_ref, ys_ref, ym_ref)):
        proj = jnp.dot(y_ref[...].astype(BF16), wb_ref[k], preferred_element_type=F32)
        acc = acc + jax.nn.sigmoid(zg_ref[:, k * d:(k + 1) * d]) * proj
    o_ref[...] = x_ref[...] + jnp.dot(acc.astype(BF16), wo_ref[...], preferred_element_type=F32)


def _merge(ys, z2d, wb_bf16, wo_bf16, x2d, tm):
    t, d = x2d.shape
    w = ys[0].shape[1]
    yspec = pl.BlockSpec((tm, w), lambda i: (i, 0))
    return pl.pallas_call(
        _merge_kernel,
        grid=(t // tm,),
        in_specs=[yspec, yspec, yspec, yspec,
                  pl.BlockSpec((tm, N_BRANCH * d), lambda i: (i, 0)),
                  pl.BlockSpec((N_BRANCH, w, d), lambda i: (0, 0, 0)),
                  pl.BlockSpec((d, d), lambda i: (0, 0)),
                  pl.BlockSpec((tm, d), lambda i: (i, 0))],
        out_specs=pl.BlockSpec((tm, d), lambda i: (i, 0)),
        out_shape=jax.ShapeDtypeStruct((t, d), F32),
        compiler_params=_cparams("parallel"),
        name="merge",
    )(*ys, z2d, wb_bf16, wo_bf16, x2d)


def _hg_bounds_kernel(h_ref, o_ref):
    h = h_ref[...]
    depth = h.shape[0]
    m = jnp.max(h, axis=0, keepdims=True)
    e = jnp.exp(h - m)
    p = e / jnp.sum(e, axis=0, keepdims=True)
    run = jnp.zeros_like(p[0:1])
    rows = []
    for l in range(depth):
        run = run + p[l:l + 1]
        rows.append(run - p[0:1])
    o_ref[...] = jnp.concatenate(rows, axis=0)


def _hg_bounds(hgrn_lower_bounds):
    return pl.pallas_call(
        _hg_bounds_kernel,
        out_shape=jax.ShapeDtypeStruct(hgrn_lower_bounds.shape, F32),
        name="hg_bounds",
    )(hgrn_lower_bounds)


def _hg_prep_kernel(q_ref, f_ref, lb_ref, qo_ref, ko_ref, do_ref):
    lb = lb_ref[...]
    f = f_ref[...]
    qo_ref[...] = _silu(q_ref[...])
    ko_ref[...] = (1.0 - lb) * jax.nn.sigmoid(-f)
    do_ref[...] = jnp.maximum(lb + (1.0 - lb) * jax.nn.sigmoid(f), HG_TINY)


def _hg_prep(z2d, lb, col0, w, tm):
    t = z2d.shape[0]
    cb = col0 // w
    ospec = pl.BlockSpec((tm, w), lambda i: (i, 0))
    oshape = jax.ShapeDtypeStruct((t, w), F32)
    return pl.pallas_call(
        _hg_prep_kernel,
        grid=(t // tm,),
        in_specs=[pl.BlockSpec((tm, w), lambda i: (i, cb)),
                  pl.BlockSpec((tm, w), lambda i: (i, cb + 1)),
                  pl.BlockSpec((1, w), lambda i: (0, 0))],
        out_specs=[ospec, ospec, ospec],
        out_shape=[oshape, oshape, oshape],
        compiler_params=_cparams("parallel"),
        name="hg_prep",
    )(z2d, z2d, lb.reshape(1, w))


def _hg_scan_kernel(q_ref, k_ref, d_ref, v_ref, o_ref, s_ref, *, tc, dk, nvb):
    sub = V7X_SUBLANES
    lanes = V7X_LANES

    @pl.when(pl.program_id(0) == 0)
    def _():
        s_ref[...] = jnp.zeros(s_ref.shape, F32)

    def step(t, carry):
        vb = [v_ref[t, pl.ds(sub * j, sub), :] for j in range(nvb)]
        acc = [jnp.zeros((sub, lanes), F32) for _ in range(nvb)]
        for kk in range(dk):
            qrow = jnp.broadcast_to(q_ref[t, pl.ds(kk, 1), :], (sub, lanes))
            krow = jnp.broadcast_to(k_ref[t, pl.ds(kk, 1), :], (sub, lanes))
            drow = jnp.broadcast_to(d_ref[t, pl.ds(kk, 1), :], (sub, lanes))
            for j in range(nvb):
                s = s_ref[kk, pl.ds(sub * j, sub), :] * drow + krow * vb[j]
                s_ref[kk, pl.ds(sub * j, sub), :] = s
                acc[j] = acc[j] + qrow * s
        for j in range(nvb):
            o_ref[t, pl.ds(sub * j, sub), :] = acc[j]
        return carry

    lax.fori_loop(0, tc, step, 0)


def _hg_scan(q_l, k_l, d_l, v_l, tc):
    s, dk, lanes = q_l.shape
    vl = v_l.shape[1]
    kspec = pl.BlockSpec((tc, dk, lanes), lambda i: (i, 0, 0))
    vspec = pl.BlockSpec((tc, vl, lanes), lambda i: (i, 0, 0))
    return pl.pallas_call(
        functools.partial(_hg_scan_kernel, tc=tc, dk=dk, nvb=vl // V7X_SUBLANES),
        grid=(s // tc,),
        in_specs=[kspec, kspec, kspec, vspec],
        out_specs=vspec,
        out_shape=jax.ShapeDtypeStruct((s, vl, lanes), F32),
        scratch_shapes=[pltpu.VMEM((dk, vl, lanes), F32)],
        compiler_params=_cparams("arbitrary"),
        name="hg_scan",
    )(q_l, k_l, d_l, v_l)


def _hg_post_kernel(o_ref, g_ref, nw_ref, y_ref, *, heads, dv):
    o = o_ref[...]
    parts = []
    for h in range(heads):
        oh = o[:, h * dv:(h + 1) * dv]
        parts.append(oh * lax.rsqrt(jnp.mean(oh * oh, axis=-1, keepdims=True) + RMS_EPS))
    y = jnp.concatenate(parts, axis=-1) * nw_ref[...]
    y_ref[...] = y * _silu(g_ref[...])


def _hg_post(o2d, z2d, norm_w, gcol0, tm):
    t, w = o2d.shape
    gb = gcol0 // w
    return pl.pallas_call(
        functools.partial(_hg_post_kernel, heads=w // HG_DK, dv=HG_DK),
        grid=(t // tm,),
        in_specs=[pl.BlockSpec((tm, w), lambda i: (i, 0)),
                  pl.BlockSpec((tm, w), lambda i: (i, gb)),
                  pl.BlockSpec((1, w), lambda i: (0, 0))],
        out_specs=pl.BlockSpec((tm, w), lambda i: (i, 0)),
        out_shape=jax.ShapeDtypeStruct((t, w), F32),
        compiler_params=_cparams("parallel"),
        name="hg_post",
    )(o2d, z2d, norm_w.reshape(1, w))


def _chains_to_lanes_k(a, bsz, seq, heads, dk, nq):
    a = a.reshape(bsz, seq, heads, dk).transpose(1, 3, 0, 2).reshape(seq, dk, 1, bsz * heads)
    return jnp.broadcast_to(a, (seq, dk, nq, bsz * heads)).reshape(seq, dk, nq * bsz * heads)


def _chains_to_lanes_v(a, bsz, seq, heads, dv, nq):
    a = a.reshape(bsz, seq, heads, nq, dv // nq).transpose(1, 4, 3, 0, 2)
    return a.reshape(seq, dv // nq, nq * bsz * heads)


def _lanes_to_chains_v(a, bsz, seq, heads, dv, nq):
    a = a.reshape(seq, dv // nq, nq, bsz, heads).transpose(3, 0, 4, 2, 1)
    return a.reshape(bsz * seq, heads * dv)


def _hgrn2(z2d, lb, norm_w, bsz, seq, w, col0, tm, tc):
    heads = w // HG_DK
    nq = V7X_LANES // (bsz * heads)
    q, k, dec = _hg_prep(z2d, lb, col0, w, tm)
    v = lax.slice_in_dim(z2d, col0 + 2 * w, col0 + 3 * w, axis=1)
    q_l = _chains_to_lanes_k(q, bsz, seq, heads, HG_DK, nq)
    k_l = _chains_to_lanes_k(k, bsz, seq, heads, HG_DK, nq)
    d_l = _chains_to_lanes_k(dec, bsz, seq, heads, HG_DK, nq)
    v_l = _chains_to_lanes_v(v, bsz, seq, heads, HG_DK, nq)
    o_l = _hg_scan(q_l, k_l, d_l, v_l, tc)
    o = _lanes_to_chains_v(o_l, bsz, seq, heads, HG_DK, nq)
    return _hg_post(o, z2d, norm_w, col0 + 3 * w, tm)


def _shift_rows(cur, prev_last, first):
    rolled = pltpu.roll(cur, 1, axis=0)
    row0 = jnp.where(first, 0.0, prev_last)
    rid = lax.broadcasted_iota(jnp.int32, cur.shape, 0)
    return jnp.where(rid == 0, jnp.broadcast_to(row0, cur.shape), rolled)


def _rw_prep_kernel(zm_ref, zmp_ref, zl_ref, zlp_ref, vf_ref,
                    mum_ref, mul_ref, w0_ref, w2_ref, a0_ref, a2_ref, g2_ref,
                    kk_ref, ka_ref, rk_ref, v0_ref, v1_ref, v2_ref, ones_ref,
                    r_out, w_out, k_out, v_out, a_out, b_out, g_out, bonus_out,
                    *, w, tiles_per_seq, has_vres):
    first = (pl.program_id(0) % tiles_per_seq) == 0
    sub = V7X_SUBLANES
    zm = zm_ref[...]
    zl = zl_ref[...]
    zms = zm + (_shift_rows(zm, zmp_ref[sub - 1:sub, :], first) - zm) * mum_ref[...]
    zls = zl + (_shift_rows(zl, zlp_ref[sub - 1:sub, :], first) - zl) * mul_ref[...]
    r = zms[:, 0:w]
    k = zms[:, w:2 * w]
    v = zms[:, 2 * w:3 * w]
    w_log = -_softplus(-(w0_ref[...] + _bdot(jnp.tanh(zls), w2_ref[...]))) - 0.5
    decay = jnp.exp(-jnp.exp(w_log))
    if has_vres:
        mix = jax.nn.sigmoid(v0_ref[...] + _bdot(_bdot(v, v1_ref[...]), v2_ref[...]))
        v = v + (vf_ref[...] - v) * mix
    a = jax.nn.sigmoid(a0_ref[...] + _bdot(zls, a2_ref[...]))
    g = _bdot(jax.nn.sigmoid(zls), g2_ref[...])
    ones = ones_ref[...]
    kk = k * kk_ref[...]
    ss = _dot_exact_rhs(kk * kk, ones)
    kk = kk / jnp.maximum(jnp.sqrt(ss), 1e-12)
    k2 = k * (1.0 + (a - 1.0) * ka_ref[...])
    r_out[...] = r
    w_out[...] = decay
    k_out[...] = k2
    v_out[...] = v
    a_out[...] = -kk
    b_out[...] = kk * a
    g_out[...] = g
    bonus_out[...] = _dot_exact_rhs(r * k2 * rk_ref[...], ones) * v


def _rw_prep(z2d, v_first, p, col_main, col_lora, w, lw, seq, tm, has_vres):
    t = z2d.shape[0]
    sub = V7X_SUBLANES
    mb = col_main // (3 * w)
    lbk = col_lora // lw
    rows8 = tm // sub

    def prev_idx(i):
        return jnp.maximum(i * rows8 - 1, 0)

    row = lambda n: pl.BlockSpec((1, n), lambda i: (0, 0))
    full = lambda a: pl.BlockSpec(a.shape, lambda i: (0,) * a.ndim)
    ospec = pl.BlockSpec((tm, w), lambda i: (i, 0))
    oshape = jax.ShapeDtypeStruct((t, w), F32)
    args = [z2d, z2d, z2d, z2d, v_first,
            p["mu_main"], p["mu_lora"], p["w0"], p["w2p"], p["a0"], p["a2p"], p["g2p"],
            p["k_k"], p["k_a"], p["r_k"], p["v0"], p["v1p"], p["v2p"], p["ones"]]
    in_specs = [pl.BlockSpec((tm, 3 * w), lambda i: (i, mb)),
                pl.BlockSpec((sub, 3 * w), lambda i: (prev_idx(i), mb)),
                pl.BlockSpec((tm, lw), lambda i: (i, lbk)),
                pl.BlockSpec((sub, lw), lambda i: (prev_idx(i), lbk)),
                ospec,
                row(3 * w), row(lw), row(w), full(p["w2p"]), row(w), full(p["a2p"]), full(p["g2p"]),
                row(w), row(w), row(w), row(w), full(p["v1p"]), full(p["v2p"]), full(p["ones"])]
    return pl.pallas_call(
        functools.partial(_rw_prep_kernel, w=w, tiles_per_seq=seq // tm, has_vres=has_vres),
        grid=(t // tm,),
        in_specs=in_specs,
        out_specs=[ospec] * 8,
        out_shape=[oshape] * 8,
        compiler_params=_cparams("parallel"),
        name="rw_prep",
    )(*args)


def _rw_scan_kernel(r_ref, w_ref, k_ref, a_ref, b_ref, v_ref, y_ref, s_ref, *, tc, dk, nvb):
    sub = V7X_SUBLANES
    lanes = V7X_LANES

    @pl.when(pl.program_id(0) == 0)
    def _():
        s_ref[...] = jnp.zeros(s_ref.shape, F32)

    def bc(ref, t, kk):
        return jnp.broadcast_to(ref[t, pl.ds(kk, 1), :], (sub, lanes))

    sa0 = [jnp.zeros((sub, lanes), F32) for _ in range(nvb)]
    for kk in range(dk):
        arow = bc(a_ref, 0, kk)
        for j in range(nvb):
            sa0[j] = sa0[j] + s_ref[kk, pl.ds(sub * j, sub), :] * arow

    def step(t, sa):
        tn = jnp.minimum(t + 1, tc - 1)
        vb = [v_ref[t, pl.ds(sub * j, sub), :] for j in range(nvb)]
        yacc = [jnp.zeros((sub, lanes), F32) for _ in range(nvb)]
        sacc = [jnp.zeros((sub, lanes), F32) for _ in range(nvb)]
        for kk in range(dk):
            wrow = bc(w_ref, t, kk)
            brow = bc(b_ref, t, kk)
            krow = bc(k_ref, t, kk)
            rrow = bc(r_ref, t, kk)
            anext = bc(a_ref, tn, kk)
            for j in range(nvb):
                s = s_ref[kk, pl.ds(sub * j, sub), :] * wrow + sa[j] * brow + vb[j] * krow
                s_ref[kk, pl.ds(sub * j, sub), :] = s
                yacc[j] = yacc[j] + s * rrow
                sacc[j] = sacc[j] + s * anext
        for j in range(nvb):
            y_ref[t, pl.ds(sub * j, sub), :] = yacc[j]
        return tuple(sacc)

    lax.fori_loop(0, tc, step, tuple(sa0))


def _rw_scan(r_l, w_l, k_l, a_l, b_l, v_l, tc):
    s, dk, lanes = r_l.shape
    vl = v_l.shape[1]
    kspec = pl.BlockSpec((tc, dk, lanes), lambda i: (i, 0, 0))
    vspec = pl.BlockSpec((tc, vl, lanes), lambda i: (i, 0, 0))
    return pl.pallas_call(
        functools.partial(_rw_scan_kernel, tc=tc, dk=dk, nvb=vl // V7X_SUBLANES),
        grid=(s // tc,),
        in_specs=[kspec] * 5 + [vspec],
        out_specs=vspec,
        out_shape=jax.ShapeDtypeStruct((s, vl, lanes), F32),
        scratch_shapes=[pltpu.VMEM((dk, vl, lanes), F32)],
        compiler_params=_cparams("arbitrary"),
        name="rw_scan",
    )(r_l, w_l, k_l, a_l, b_l, v_l)


def _rw_post_kernel(y_ref, g_ref, bonus_ref, lnw_ref, lnb_ref, ones_ref, o_ref):
    y = y_ref[...]
    ones = ones_ref[...]
    inv_n = 1.0 / RW_HEAD
    mean = _dot_exact_rhs(y, ones) * inv_n
    yc = y - mean
    var = _dot_exact_rhs(yc * yc, ones) * inv_n
    y = yc * lax.rsqrt(var + RW_LN_EPS) * lnw_ref[...] + lnb_ref[...]
    o_ref[...] = (y + bonus_ref[...]) * g_ref[...]


def _rw_post(y2d, g, bonus, ln_w, ln_b, ones, tm):
    t, w = y2d.shape
    tspec = pl.BlockSpec((tm, w), lambda i: (i, 0))
    row = pl.BlockSpec((1, w), lambda i: (0, 0))
    return pl.pallas_call(
        _rw_post_kernel,
        grid=(t // tm,),
        in_specs=[tspec, tspec, tspec, row, row, pl.BlockSpec((w, w), lambda i: (0, 0))],
        out_specs=tspec,
        out_shape=jax.ShapeDtypeStruct((t, w), F32),
        compiler_params=_cparams("parallel"),
        name="rw_post",
    )(y2d, g, bonus, ln_w.reshape(1, w), ln_b.reshape(1, w), ones)


def _rwkv7(z2d, v_first, p, bsz, seq, w, col_main, col_lora, lw, tm, tc, has_vres):
    heads = w // RW_HEAD
    nq = V7X_LANES // (bsz * heads)
    r, dec, k2, v, a, b, g, bonus = _rw_prep(z2d, v_first, p, col_main, col_lora, w, lw, seq, tm,
                                             has_vres)
    to_k = functools.partial(_chains_to_lanes_k, bsz=bsz, seq=seq, heads=heads, dk=RW_HEAD, nq=nq)
    v_l = _chains_to_lanes_v(v, bsz, seq, heads, RW_HEAD, nq)
    y_l = _rw_scan(to_k(r), to_k(dec), to_k(k2), to_k(a), to_k(b), v_l, tc)
    y = _lanes_to_chains_v(y_l, bsz, seq, heads, RW_HEAD, nq)
    return _rw_post(y, g, bonus, p["ln_w"], p["ln_b"], p["ones"], tm), v


def _s5_disc_kernel(are_ref, aim_ref, dt_ref, bre_ref, bim_ref, lre_ref, lim_ref, bbre_ref, bbim_ref):
    a_re = are_ref[...]
    a_im = aim_ref[...]
    dt = jnp.exp(dt_ref[...])
    mag = jnp.exp(dt * a_re)
    lam_re = mag * jnp.cos(dt * a_im)
    lam_im = mag * jnp.sin(dt * a_im)
    den = a_re * a_re + a_im * a_im
    coef_re = ((lam_re - 1.0) * a_re + lam_im * a_im) / den
    coef_im = (lam_im * a_re - (lam_re - 1.0) * a_im) / den
    b_re = bre_ref[...]
    b_im = bim_ref[...]
    lre_ref[...] = lam_re
    lim_ref[...] = lam_im
    bbre_ref[...] = coef_re * b_re - coef_im * b_im
    bbim_ref[...] = coef_re * b_im + coef_im * b_re


def _s5_discretize(a_re, a_im, log_dt, b_re, b_im):
    g, n, c = b_re.shape
    shp = (g, n * c)
    bc = lambda a: jnp.broadcast_to(a[..., None], (g, n, c)).reshape(shp)
    dtb = jnp.broadcast_to(log_dt[:, None], shp)
    o = jax.ShapeDtypeStruct(shp, F32)
    lre, lim, bbre, bbim = pl.pallas_call(
        _s5_disc_kernel, out_shape=[o, o, o, o], name="s5_disc",
    )(bc(a_re), bc(a_im), dtb, b_re.reshape(shp), b_im.reshape(shp))
    un = lambda a: a.reshape(g, n, c)
    return un(lre)[..., 0], un(lim)[..., 0], un(bbre), un(bbim)


def _s5_kernel(u_ref, bblk_ref, cblk_ref, lre_ref, lim_ref, d_ref, wg_ref, bg_ref, y_ref,
               h_ref, hr_ref, hi_ref, *, tc, bsz, ns, lane_chunk):
    @pl.when(pl.program_id(0) == 0)
    def _():
        hr_ref[...] = jnp.zeros(hr_ref.shape, F32)
        hi_ref[...] = jnp.zeros(hi_ref.shape, F32)

    u = u_ref[...]
    h_ref[...] = _bdot(u, bblk_ref[...])
    for c0 in range(0, ns, lane_chunk):
        lr = jnp.broadcast_to(lre_ref[:, c0:c0 + lane_chunk], (bsz, lane_chunk))
        li = jnp.broadcast_to(lim_ref[:, c0:c0 + lane_chunk], (bsz, lane_chunk))

        def step(t, carry, c0=c0, lr=lr, li=li):
            hr, hi = carry
            rows = pl.ds(pl.multiple_of(t * bsz, bsz), bsz)
            nr = lr * hr - li * hi + h_ref[rows, c0:c0 + lane_chunk]
            ni = lr * hi + li * hr + h_ref[rows, ns + c0:ns + c0 + lane_chunk]
            h_ref[rows, c0:c0 + lane_chunk] = nr
            h_ref[rows, ns + c0:ns + c0 + lane_chunk] = ni
            return nr, ni

        hr, hi = lax.fori_loop(0, tc, step,
                               (hr_ref[:, c0:c0 + lane_chunk], hi_ref[:, c0:c0 + lane_chunk]))
        hr_ref[:, c0:c0 + lane_chunk] = hr
        hi_ref[:, c0:c0 + lane_chunk] = hi
    y = _bdot(h_ref[...], cblk_ref[...]) + d_ref[...] * u
    y = jax.nn.gelu(y)
    y_ref[...] = y * jax.nn.sigmoid(_bdot(y, wg_ref[...]) + bg_ref[...])


def _s5(u_tm, p, bsz, tc):
    rows, w = u_tm.shape
    ns = p["lam_re"].shape[1]
    blk = tc * bsz
    full = lambda a: pl.BlockSpec(a.shape, lambda i: (0,) * a.ndim)
    return pl.pallas_call(
        functools.partial(_s5_kernel, tc=tc, bsz=bsz, ns=ns, lane_chunk=min(ns, 512)),
        grid=(rows // blk,),
        in_specs=[pl.BlockSpec((blk, w), lambda i: (i, 0)),
                  full(p["bblk"]), full(p["cblk"]), full(p["lam_re"]), full(p["lam_im"]),
                  full(p["d"]), full(p["w_glu"]), full(p["b_glu"])],
        out_specs=pl.BlockSpec((blk, w), lambda i: (i, 0)),
        out_shape=jax.ShapeDtypeStruct((rows, w), F32),
        scratch_shapes=[pltpu.VMEM((blk, 2 * ns), F32),
                        pltpu.VMEM((bsz, ns), F32), pltpu.VMEM((bsz, ns), F32)],
        compiler_params=_cparams("arbitrary"),
        name="s5",
    )(u_tm, p["bblk"], p["cblk"], p["lam_re"], p["lam_im"], p["d"], p["w_glu"], p["b_glu"])


def _mb_kernel(gate_ref, x_ref, bc_ref, dt_ref, cw_ref, cb_ref, dtb_ref, alog_ref, dsk_ref, nw_ref,
               expand_ref, tril_ref, y_ref, prev_ref, st_ref, *, q, w, heads, groups, nstate):
    sub = V7X_SUBLANES
    lanes = V7X_LANES
    hd = w // heads
    gw = w // groups
    hpg = heads // groups

    @pl.when(pl.program_id(1) == 0)
    def _():
        prev_ref[...] = jnp.zeros(prev_ref.shape, F32)
        st_ref[...] = jnp.zeros(st_ref.shape, F32)

    xbc = jnp.concatenate([x_ref[0], bc_ref[0]], axis=-1)
    full = jnp.concatenate([prev_ref[...], xbc], axis=0)
    conv = jnp.broadcast_to(cb_ref[...], xbc.shape)
    for j in range(MB_CONV):
        shift = MB_CONV - 1 - j
        src = full if shift == 0 else pltpu.roll(full, shift, axis=0)
        conv = conv + src[sub:sub + q, :] * cw_ref[j:j + 1, :]
    prev_ref[...] = xbc[q - sub:q, :]
    act = _silu(conv)
    xs = act[:, 0:w]
    bmat = act[:, w:w + groups * nstate]
    cmat = act[:, w + groups * nstate:w + 2 * groups * nstate]

    dt = _softplus(dt_ref[0] + dtb_ref[...])
    a = -jnp.exp(alog_ref[...]) * dt
    a_cum = _dot_exact_lhs(tril_ref[...], a)
    expand = expand_ref[...]
    dt_e = _dot_exact_rhs(dt, expand)
    acum_e = _dot_exact_rhs(a_cum, expand)
    alast_e = acum_e[q - 1:q, :]
    xdt = xs * dt_e
    xdec = xdt * jnp.exp(alast_e - acum_e)

    rid = lax.broadcasted_iota(jnp.int32, (q, q), 0)
    cid = lax.broadcasted_iota(jnp.int32, (q, q), 1)
    causal = rid >= cid
    sel_r = lax.broadcasted_iota(jnp.int32, (lanes, q), 0)
    sel_c = lax.broadcasted_iota(jnp.int32, (q, lanes), 1)
    lane_w = lax.broadcasted_iota(jnp.int32, (q, gw), 1)

    y_parts = []
    for g in range(groups):
        bg = bmat[:, g * nstate:(g + 1) * nstate]
        cg = cmat[:, g * nstate:(g + 1) * nstate]
        scores = lax.dot_general(cg.astype(BF16), bg.astype(BF16), (((1,), (1,)), ((), ())),
                                 preferred_element_type=F32)
        xg = xdt[:, g * gw:(g + 1) * gw]
        yg = _bdot(cg, st_ref[g]) * jnp.exp(acum_e[:, g * gw:(g + 1) * gw])
        for hh in range(hpg):
            h = g * hpg + hh
            col = _dot_exact_rhs(a_cum, (sel_r == h).astype(BF16))
            rowv = _dot_exact_nt((sel_c == h).astype(BF16), a_cum)
            decay = jnp.where(causal, jnp.exp(col - rowv), 0.0)
            xh = jnp.where((lane_w >= hh * hd) & (lane_w < (hh + 1) * hd), xg, 0.0)
            yg = yg + _bdot(scores * decay, xh)
        y_parts.append(yg)
        upd = _bdot(bg.T, xdec[:, g * gw:(g + 1) * gw])
        st_ref[g] = st_ref[g] * jnp.exp(alast_e[:, g * gw:(g + 1) * gw]) + upd
    y = jnp.concatenate(y_parts, axis=-1) + dsk_ref[...] * xs
    y_ref[0] = _rms(y * _silu(gate_ref[0]), nw_ref[...])


def _mamba2(z3d, p, w, cols, q):
    bsz, seq, _ = z3d.shape
    heads = w // MB_HEADDIM
    lanes = V7X_LANES
    cg, cx, cbc, cdt = cols
    blk = lambda width, col: pl.BlockSpec((1, q, width), lambda b, c: (b, c, col // width))
    full = lambda a: pl.BlockSpec(a.shape, lambda b, c: (0,) * a.ndim)
    consts = [p["conv_w"], p["conv_b"], p["dt_bias"], p["a_log"], p["d"], p["norm_w"],
              p["expand"], p["tril"]]
    return pl.pallas_call(
        functools.partial(_mb_kernel, q=q, w=w, heads=heads, groups=MB_GROUPS, nstate=MB_STATE),
        grid=(bsz, seq // q),
        in_specs=[blk(w, cg), blk(w, cx), blk(w, cbc), blk(lanes, cdt)] + [full(a) for a in consts],
        out_specs=pl.BlockSpec((1, q, w), lambda b, c: (b, c, 0)),
        out_shape=jax.ShapeDtypeStruct((bsz, seq, w), F32),
        scratch_shapes=[pltpu.VMEM((V7X_SUBLANES, 2 * w), F32),
                        pltpu.VMEM((MB_GROUPS, MB_STATE, w // MB_GROUPS), F32)],
        compiler_params=_cparams("parallel", "arbitrary"),
        name="mamba2",
    )(z3d, z3d, z3d, z3d, *consts)


def _pad_rows(a, rows, at):
    out = jnp.zeros((rows, a.shape[1]), a.dtype)
    return lax.dynamic_update_slice(out, a, (at, 0))


def _block_ones(w, head):
    idx = jnp.arange(w) // head
    return (idx[:, None] == idx[None, :]).astype(BF16)


def _pick_tile(n, target):
    t = min(n, target)
    while n % t:
        t //= 2
    return t


def kernel(x, norm_mix_w, w_in, w_branch, w_out, norm_ffn_w, w_ffn_in, w_ffn_out, norm_final_w, hgrn_lower_bounds, hgrn_norm_w, rwkv_mu, rwkv_w0, rwkv_w2, rwkv_a0, rwkv_a2, rwkv_g2, rwkv_k_k, rwkv_k_a, rwkv_r_k, rwkv_ln_w, rwkv_ln_b, rwkv_v0, rwkv_v1, rwkv_v2, s5_a_re, s5_a_im, s5_b_re, s5_b_im, s5_c_re, s5_c_im, s5_d, s5_log_dt, s5_w_glu, s5_b_glu, mamba_conv_w, mamba_conv_b, mamba_dt_bias, mamba_a_log, mamba_d, mamba_norm_w):
    bsz, seq, d = x.shape
    depth = w_in.shape[0]
    w = d // 2
    lanes = V7X_LANES
    t = bsz * seq
    mb_heads = w // MB_HEADDIM
    mb_bc = 2 * MB_GROUPS * MB_STATE
    lw = RW_DECAY_LORA + RW_A_LORA + RW_G_LORA
    assert mb_bc == w and 3 * w % lw == 0

    o_gate = 0
    o_hg = o_gate + N_BRANCH * d
    o_rw = o_hg + 4 * w
    o_rwl = o_rw + 3 * w
    o_s5 = o_rwl + lw
    o_mbg = o_s5 + w
    o_mbx = o_mbg + w
    o_mbbc = o_mbx + w
    o_mbdt = o_mbbc + mb_bc
    c_gate, c_hg, c_rw = 0, N_BRANCH * d, N_BRANCH * d + 4 * w
    c_s5 = c_rw + 3 * w
    c_mbg = c_s5 + w
    c_mbx = c_mbg + w
    c_mbbc = c_mbx + w
    c_rwl = c_mbbc + mb_bc
    c_mbdt = c_rwl + lw
    n_cols = c_mbdt + lanes
    tn = 512
    n_pad = -(-n_cols // tn) * tn

    def permute_cols(wl):
        sl = lambda a, b: wl[:, a:b]
        pieces = [sl(o_gate, o_rwl), sl(o_s5, o_mbg), sl(o_mbg, o_mbx), sl(o_mbx, o_mbbc),
                  sl(o_mbbc, o_mbdt), sl(o_rwl, o_s5), sl(o_mbdt, o_mbdt + mb_heads),
                  jnp.zeros((d, n_pad - c_mbdt - mb_heads), wl.dtype)]
        return jnp.concatenate(pieces, axis=1).astype(BF16)

    tm = _pick_tile(seq, 256)
    tm_in = _pick_tile(t, 1024)
    tc_hg = _pick_tile(seq, 16)
    tc_rw = _pick_tile(seq, 32)
    tc_s5 = _pick_tile(seq, 64)
    q_mb = min(MB_CHUNK, seq)
    ff = w_ffn_out.shape[1]
    tn_ff = 256 if ff % 256 == 0 else lanes

    lower_bounds = _hg_bounds(hgrn_lower_bounds)
    ones_rw = _block_ones(w, RW_HEAD)
    eye_g = jnp.eye(w // S5_GROUP, dtype=F32)
    expand = (jnp.arange(lanes)[:, None] == (jnp.arange(w) // MB_HEADDIM)[None, :]).astype(BF16)
    tril = (jnp.arange(q_mb)[:, None] >= jnp.arange(q_mb)[None, :]).astype(BF16)

    x2d = x.reshape(t, d)
    v_first = jnp.zeros((t, w), F32)
    for l in range(depth):
        z2d = _inproj(x2d, norm_mix_w[l], permute_cols(w_in[l]), tm_in, tn)
        z3d = z2d.reshape(bsz, seq, n_pad)

        y_hg = _hgrn2(z2d, lower_bounds[l], hgrn_norm_w[l], bsz, seq, w, c_hg, tm, tc_hg)

        mu = rwkv_mu[l]
        has_vres = l > 0
        lv = max(l - 1, 0)
        rp = {
            "mu_main": mu[None, :3 * w], "mu_lora": mu[None, 3 * w:],
            "w0": rwkv_w0[l][None], "a0": rwkv_a0[l][None],
            "w2p": _pad_rows(rwkv_w2[l], lw, 0).astype(BF16),
            "a2p": _pad_rows(rwkv_a2[l], lw, RW_DECAY_LORA).astype(BF16),
            "g2p": _pad_rows(rwkv_g2[l], lw, RW_DECAY_LORA + RW_A_LORA).astype(BF16),
            "k_k": rwkv_k_k[l][None], "k_a": rwkv_k_a[l][None], "r_k": rwkv_r_k[l].reshape(1, w),
            "v0": rwkv_v0[lv][None],
            "v1p": jnp.pad(rwkv_v1[lv], ((0, 0), (0, lanes - rwkv_v1.shape[2]))).astype(BF16),
            "v2p": _pad_rows(rwkv_v2[lv], lanes, 0).astype(BF16),
            "ones": ones_rw, "ln_w": rwkv_ln_w[l], "ln_b": rwkv_ln_b[l],
        }
        y_rw, v_l = _rwkv7(z2d, v_first, rp, bsz, seq, w, c_rw, c_rwl, lw, tm, tc_rw, has_vres)
        if l == 0:
            v_first = v_l

        lam_re, lam_im, bb_re, bb_im = _s5_discretize(s5_a_re[l], s5_a_im[l], s5_log_dt[l],
                                                      s5_b_re[l], s5_b_im[l])
        ns = lam_re.size
        blk_in = lambda bb: jnp.einsum("gnc,gh->gchn", bb, eye_g).reshape(w, ns)
        blk_out = lambda cc: jnp.einsum("gcn,gh->gnhc", cc, eye_g).reshape(ns, w)
        sp = {
            "bblk": jnp.concatenate([blk_in(bb_re), blk_in(bb_im)], axis=1).astype(BF16),
            "cblk": jnp.concatenate([blk_out(s5_c_re[l]), -blk_out(s5_c_im[l])], axis=0).astype(BF16),
            "lam_re": lam_re.reshape(1, ns), "lam_im": lam_im.reshape(1, ns),
            "d": s5_d[l][None], "w_glu": s5_w_glu[l].astype(BF16), "b_glu": s5_b_glu[l][None],
        }
        u_tm = z3d[:, :, c_s5:c_s5 + w].transpose(1, 0, 2).reshape(t, w)
        y_s5 = _s5(u_tm, sp, bsz, tc_s5).reshape(seq, bsz, w).transpose(1, 0, 2).reshape(t, w)

        mp = {
            "conv_w": mamba_conv_w[l], "conv_b": mamba_conv_b[l][None],
            "dt_bias": jnp.pad(mamba_dt_bias[l], (0, lanes - mb_heads))[None],
            "a_log": jnp.pad(mamba_a_log[l], (0, lanes - mb_heads))[None],
            "d": jnp.repeat(mamba_d[l], MB_HEADDIM)[None], "norm_w": mamba_norm_w[l][None],
            "expand": expand, "tril": tril,
        }
        y_mb = _mamba2(z3d, mp, w, (c_mbg, c_mbx, c_mbbc, c_mbdt), q_mb).reshape(t, w)

        x2d = _merge((y_hg, y_rw, y_s5, y_mb), z2d, w_branch[l].astype(BF16), w_out[l].astype(BF16),
                     x2d, tm)
        act = _ffn_in(x2d, norm_ffn_w[l], w_ffn_in[l].astype(BF16), tm_in, tn_ff)
        x2d = _ffn_out(act, w_ffn_out[l].astype(BF16), x2d, norm_final_w, l == depth - 1, tm)
    return x2d.reshape(bsz, seq, d)
```

```python
import functools
import math

import jax
import jax.numpy as jnp
from jax import lax
from jax.experimental import pallas as pl
from jax.experimental.pallas import tpu as pltpu

F32 = jnp.float32
BF16 = jnp.bfloat16

V7X_LANES = 128
V7X_SUBLANES = 8
V7X_VMEM_LIMIT_BYTES = 56 * 1024 * 1024

RMS_EPS = 1e-6
N_BRANCH = 4
HG_DK = 128
HG_TINY = 1e-30
HG_CHUNK = 64
HG_SUBBLOCK = 8
RW_HEAD = 64
RW_DECAY_LORA = 64
RW_A_LORA = 64
RW_G_LORA = 128
RW_LN_EPS = 64e-5
RW_NVEC = 5
S5_GROUP = 16
S5_STATE = 64
MB_HEADDIM = 64
MB_GROUPS = 2
MB_STATE = 128
MB_CONV = 4
MB_CHUNK = 128


def _cparams(*sem):
    return pltpu.CompilerParams(dimension_semantics=sem, vmem_limit_bytes=V7X_VMEM_LIMIT_BYTES)


def _bdot(a, b):
    return jnp.dot(a.astype(BF16), b.astype(BF16), preferred_element_type=F32)


def _split3(x):
    x1 = x.astype(BF16)
    r1 = x - x1.astype(F32)
    x2 = r1.astype(BF16)
    x3 = (r1 - x2.astype(F32)).astype(BF16)
    return x1, x2, x3


def _dot_exact_rhs(x, m):
    x1, x2, x3 = _split3(x)
    d = functools.partial(jnp.dot, preferred_element_type=F32)
    return d(x1, m) + d(x2, m) + d(x3, m)


def _dot_exact_lhs(m, x):
    x1, x2, x3 = _split3(x)
    d = functools.partial(jnp.dot, preferred_element_type=F32)
    return d(m, x1) + d(m, x2) + d(m, x3)


def _dot_exact_nt(m, x):
    x1, x2, x3 = _split3(x)
    d = functools.partial(lax.dot_general, dimension_numbers=(((1,), (1,)), ((), ())),
                          preferred_element_type=F32)
    return d(m, x1) + d(m, x2) + d(m, x3)


def _softplus(x):
    return jnp.maximum(x, 0.0) + jnp.log(1.0 + jnp.exp(-jnp.abs(x)))


def _silu(x):
    return x * jax.nn.sigmoid(x)


def _rms(x, w):
    return x * lax.rsqrt(jnp.mean(x * x, axis=-1, keepdims=True) + RMS_EPS) * w


def _inproj_kernel(x_ref, nw_ref, w_ref, o_ref, u_ref, *, gate):
    @pl.when(pl.program_id(1) == 0)
    def _():
        u_ref[...] = _rms(x_ref[...], nw_ref[...]).astype(BF16)

    z = jnp.dot(u_ref[...], w_ref[...], preferred_element_type=F32)
    o_ref[...] = jax.nn.sigmoid(z).astype(o_ref.dtype) if gate else z


def _inproj(x2d, norm_w, w_bf16, tm, tn, gate):
    t, d = x2d.shape
    n = w_bf16.shape[1]
    return pl.pallas_call(
        functools.partial(_inproj_kernel, gate=gate),
        grid=(t // tm, n // tn),
        in_specs=[pl.BlockSpec((tm, d), lambda i, j: (i, 0)),
                  pl.BlockSpec((1, d), lambda i, j: (0, 0)),
                  pl.BlockSpec((d, tn), lambda i, j: (0, j))],
        out_specs=pl.BlockSpec((tm, tn), lambda i, j: (i, j)),
        out_shape=jax.ShapeDtypeStruct((t, n), BF16 if gate else F32),
        scratch_shapes=[pltpu.VMEM((tm, d), BF16)],
        compiler_params=_cparams("parallel", "arbitrary"),
        name="inproj_gate" if gate else "inproj",
    )(x2d, norm_w.reshape(1, d), w_bf16)


def _ffn_in_kernel(x_ref, nw_ref, wg_ref, wu_ref, o_ref, u_ref):
    @pl.when(pl.program_id(1) == 0)
    def _():
        u_ref[...] = _rms(x_ref[...], nw_ref[...]).astype(BF16)

    u = u_ref[...]
    gate = jnp.dot(u, wg_ref[...], preferred_element_type=F32)
    up = jnp.dot(u, wu_ref[...], preferred_element_type=F32)
    o_ref[...] = (_silu(gate) * up).astype(BF16)


def _ffn_in(x2d, norm_w, w_bf16, tm, tn):
    t, d = x2d.shape
    ff = w_bf16.shape[1] // 2
    nj = ff // tn
    return pl.pallas_call(
        _ffn_in_kernel,
        grid=(t // tm, nj),
        in_specs=[pl.BlockSpec((tm, d), lambda i, j: (i, 0)),
                  pl.BlockSpec((1, d), lambda i, j: (0, 0)),
                  pl.BlockSpec((d, tn), lambda i, j: (0, j)),
                  pl.BlockSpec((d, tn), lambda i, j: (0, j + nj))],
        out_specs=pl.BlockSpec((tm, tn), lambda i, j: (i, j)),
        out_shape=jax.ShapeDtypeStruct((t, ff), BF16),
        scratch_shapes=[pltpu.VMEM((tm, d), BF16)],
        compiler_params=_cparams("parallel", "arbitrary"),
        name="ffn_in",
    )(x2d, norm_w.reshape(1, d), w_bf16, w_bf16)


def _ffn_out_kernel(a_ref, w_ref, x_ref, fw_ref, o_ref, *, final_norm):
    y = x_ref[...] + jnp.dot(a_ref[...], w_ref[...], preferred_element_type=F32)
    if final_norm:
        y = _rms(y, fw_ref[...])
    o_ref[...] = y


def _ffn_out(act, w_bf16, x2d, final_w, final_norm, tm):
    t, d = x2d.shape
    ff = act.shape[1]
    return pl.pallas_call(
        functools.partial(_ffn_out_kernel, final_norm=final_norm),
        grid=(t // tm,),
        in_specs=[pl.BlockSpec((tm, ff), lambda i: (i, 0)),
                  pl.BlockSpec((ff, d), lambda i: (0, 0)),
                  pl.BlockSpec((tm, d), lambda i: (i, 0)),
                  pl.BlockSpec((1, d), lambda i: (0, 0))],
        out_specs=pl.BlockSpec((tm, d), lambda i: (i, 0)),
        out_shape=jax.ShapeDtypeStruct((t, d), F32),
        compiler_params=_cparams("parallel"),
        name="ffn_out",
    )(act, w_bf16, x2d, final_w.reshape(1, d))


def _merge_kernel(yh_ref, yr_ref, ys_ref, ym_ref, zg_ref, wb_ref, wo_ref, x_ref, o_ref):
    d = x_ref.shape[1]
    acc = jnp.zeros(x_ref.shape, F32)
    for k, y_ref in enumerate((yh_ref, yr_ref, ys_ref, ym_ref)):
        proj = jnp.dot(y_ref[...].astype(BF16), wb_ref[k], preferred_element_type=F32)
        acc = acc + zg_ref[:, k * d:(k + 1) * d].astype(F32) * proj
    o_ref[...] = x_ref[...] + jnp.dot(acc.astype(BF16), wo_ref[...], preferred_element_type=F32)


def _merge(ys, z2d, wb_bf16, wo_bf16, x2d, tm):
    t, d = x2d.shape
    w = ys[0].shape[1]
    yspec = pl.BlockSpec((tm, w), lambda i: (i, 0))
    return pl.pallas_call(
        _merge_kernel,
        grid=(t // tm,),
        in_specs=[yspec, yspec, yspec, yspec,
                  pl.BlockSpec((tm, N_BRANCH * d), lambda i: (i, 0)),
                  pl.BlockSpec((N_BRANCH, w, d), lambda i: (0, 0, 0)),
                  pl.BlockSpec((d, d), lambda i: (0, 0)),
                  pl.BlockSpec((tm, d), lambda i: (i, 0))],
        out_specs=pl.BlockSpec((tm, d), lambda i: (i, 0)),
        out_shape=jax.ShapeDtypeStruct((t, d), F32),
        compiler_params=_cparams("parallel"),
        name="merge",
    )(*ys, z2d, wb_bf16, wo_bf16, x2d)


def _hg_bounds_kernel(h_ref, o_ref):
    h = h_ref[...]
    depth = h.shape[0]
    m = jnp.max(h, axis=0, keepdims=True)
    e = jnp.exp(h - m)
    p = e / jnp.sum(e, axis=0, keepdims=True)
    run = jnp.zeros_like(p[0:1])
    rows = []
    for l in range(depth):
        run = run + p[l:l + 1]
        rows.append(run - p[0:1])
    o_ref[...] = jnp.concatenate(rows, axis=0)


def _hg_bounds(hgrn_lower_bounds):
    return pl.pallas_call(
        _hg_bounds_kernel,
        out_shape=jax.ShapeDtypeStruct(hgrn_lower_bounds.shape, F32),
        name="hg_bounds",
    )(hgrn_lower_bounds)


def _hg_chunk_kernel(q_ref, f_ref, i_ref, g_ref, lb_ref, nw_ref, tril_ref, y_ref,
                     st_ref, kbuf, bbuf, vbuf, *, tb, c, cs, heads, dk):
    sub = V7X_SUBLANES
    nb = c // cs
    nt = (((1,), (1,)), ((), ()))

    @pl.when(pl.program_id(1) == 0)
    def _():
        st_ref[...] = jnp.zeros(st_ref.shape, F32)

    kbuf[0:sub, :] = jnp.zeros((sub, kbuf.shape[1]), F32)
    bbuf[0:sub, :] = jnp.zeros((sub, bbuf.shape[1]), F32)
    vbuf[0:sub, :] = jnp.zeros((sub, vbuf.shape[1]), F32)
    lb = lb_ref[...]
    tril = tril_ref[...]
    rid = lax.broadcasted_iota(jnp.int32, (c, dk), 0)
    rid1 = lax.broadcasted_iota(jnp.int32, (c, 1), 0)

    def chunk(ci, carry):
        rows = pl.ds(pl.multiple_of(ci * c, c), c)
        ff = f_ref[0, rows, :]
        q = _silu(q_ref[0, rows, :])
        dec = jnp.maximum(lb + (1.0 - lb) * jax.nn.sigmoid(ff), HG_TINY)
        k = (1.0 - lb) * jax.nn.sigmoid(-ff)
        v = i_ref[0, rows, :]
        b = _dot_exact_lhs(tril, jnp.log(dec))
        kbuf[sub:sub + c, :] = k
        bbuf[sub:sub + c, :] = b
        vbuf[sub:sub + c, :] = v
        outs = []
        for h in range(heads):
            hs = slice(h * dk, (h + 1) * dk)
            qh, kh, bh, vh = q[:, hs], k[:, hs], b[:, hs], v[:, hs]
            blast = bh[c - 1:c, :]
            st = st_ref[h]
            o = lax.dot_general((qh * jnp.exp(bh)).astype(BF16), st.astype(BF16), nt,
                                preferred_element_type=F32)
            att_rows = [jnp.zeros((cs, c), F32)]
            for i in range(1, nb):
                beta = bh[i * cs - 1:i * cs, :]
                kt = jnp.where(rid < i * cs, kh * jnp.exp(jnp.minimum(beta - bh, 0.0)), 0.0)
                qt = qh[i * cs:(i + 1) * cs, :] * jnp.exp(bh[i * cs:(i + 1) * cs, :] - beta)
                att_rows.append(lax.dot_general(qt.astype(BF16), kt.astype(BF16), nt,
                                                preferred_element_type=F32))
            o = o + _bdot(jnp.concatenate(att_rows, axis=0), vh)
            for dlt in range(cs):
                if dlt == 0:
                    a = jnp.sum(qh * kh, axis=-1, keepdims=True)
                    o = o + a * vh
                else:
                    win = pl.ds(sub - dlt, c)
                    e = jnp.exp(jnp.minimum(bh - bbuf[win, hs], 0.0))
                    a = jnp.sum(qh * kbuf[win, hs] * e, axis=-1, keepdims=True)
                    a = jnp.where((rid1 % cs) >= dlt, a, 0.0)
                    o = o + a * vbuf[win, hs]
            kd = kh * jnp.exp(blast - bh)
            st_ref[h] = st * jnp.exp(blast) + _bdot(vh.T, kd)
            outs.append(o * lax.rsqrt(jnp.mean(o * o, axis=-1, keepdims=True) + RMS_EPS))
        y = jnp.concatenate(outs, axis=-1) * nw_ref[...]
        y_ref[0, rows, :] = y * _silu(g_ref[0, rows, :])
        return carry

    lax.fori_loop(0, tb // c, chunk, 0)


def _hgrn2(z3d, lb, norm_w, w, col0, tb, c, cs):
    bsz, seq, _ = z3d.shape
    heads = w // HG_DK
    cb = col0 // w
    tril = (jnp.arange(c)[:, None] >= jnp.arange(c)[None, :]).astype(BF16)
    zspec = lambda j: pl.BlockSpec((1, tb, w), lambda b, i: (b, i, cb + j))
    row = pl.BlockSpec((1, w), lambda b, i: (0, 0))
    return pl.pallas_call(
        functools.partial(_hg_chunk_kernel, tb=tb, c=c, cs=cs, heads=heads, dk=HG_DK),
        grid=(bsz, seq // tb),
        in_specs=[zspec(0), zspec(1), zspec(2), zspec(3), row, row,
                  pl.BlockSpec((c, c), lambda b, i: (0, 0))],
        out_specs=pl.BlockSpec((1, tb, w), lambda b, i: (b, i, 0)),
        out_shape=jax.ShapeDtypeStruct((bsz, seq, w), F32),
        scratch_shapes=[pltpu.VMEM((heads, HG_DK, HG_DK), F32)]
        + [pltpu.VMEM((c + V7X_SUBLANES, w), F32)] * 3,
        compiler_params=_cparams("parallel", "arbitrary"),
        name="hgrn2",
    )(z3d, z3d, z3d, z3d, lb.reshape(1, w), norm_w.reshape(1, w), tril)


def _chains_to_lanes_k(a, bsz, seq, heads, dk, nq):
    nvec = a.shape[1] // (heads * dk)
    a = jnp.broadcast_to(a.reshape(1, bsz, seq, nvec, heads, dk), (nq, bsz, seq, nvec, heads, dk))
    return a.transpose(2, 3, 5, 0, 1, 4).reshape(seq, nvec, dk, nq * bsz * heads)


def _chains_to_lanes_v(a, bsz, seq, heads, dv, nq):
    a = a.reshape(bsz, seq, heads, nq, dv // nq).transpose(1, 4, 3, 0, 2)
    return a.reshape(seq, dv // nq, nq * bsz * heads)


def _lanes_to_chains_v(a, bsz, seq, heads, dv, nq):
    a = a.reshape(seq, dv // nq, nq, bsz, heads).transpose(3, 0, 4, 2, 1)
    return a.reshape(bsz * seq, heads * dv)


def _shift_rows(cur, prev_last, first):
    rolled = pltpu.roll(cur, 1, axis=0)
    row0 = jnp.where(first, 0.0, prev_last)
    rid = lax.broadcasted_iota(jnp.int32, cur.shape, 0)
    return jnp.where(rid == 0, jnp.broadcast_to(row0, cur.shape), rolled)


def _rw_prep_kernel(zm_ref, zmp_ref, zl_ref, zlp_ref, vf_ref,
                    mum_ref, mul_ref, w0_ref, w2_ref, a0_ref, a2_ref, g2_ref,
                    kk_ref, ka_ref, rk_ref, v0_ref, v1_ref, v2_ref, ones_ref,
                    kvec_out, v_out, g_out, bonus_out,
                    *, w, tiles_per_seq, has_vres):
    first = (pl.program_id(0) % tiles_per_seq) == 0
    sub = V7X_SUBLANES
    zm = zm_ref[...]
    zl = zl_ref[...]
    zms = zm + (_shift_rows(zm, zmp_ref[sub - 1:sub, :], first) - zm) * mum_ref[...]
    zls = zl + (_shift_rows(zl, zlp_ref[sub - 1:sub, :], first) - zl) * mul_ref[...]
    r = zms[:, 0:w]
    k = zms[:, w:2 * w]
    v = zms[:, 2 * w:3 * w]
    w_log = -_softplus(-(w0_ref[...] + _bdot(jnp.tanh(zls), w2_ref[...]))) - 0.5
    decay = jnp.exp(-jnp.exp(w_log))
    if has_vres:
        mix = jax.nn.sigmoid(v0_ref[...] + _bdot(_bdot(v, v1_ref[...]), v2_ref[...]))
        v = v + (vf_ref[...] - v) * mix
    a = jax.nn.sigmoid(a0_ref[...] + _bdot(zls, a2_ref[...]))
    g = _bdot(jax.nn.sigmoid(zls), g2_ref[...])
    ones = ones_ref[...]
    kk = k * kk_ref[...]
    ss = _dot_exact_rhs(kk * kk, ones)
    kk = kk / jnp.maximum(jnp.sqrt(ss), 1e-12)
    k2 = k * (1.0 + (a - 1.0) * ka_ref[...])
    for j, vec in enumerate((r, decay, k2, -kk, kk * a)):
        kvec_out[:, j * w:(j + 1) * w] = vec
    v_out[...] = v
    g_out[...] = g
    bonus_out[...] = _dot_exact_rhs(r * k2 * rk_ref[...], ones) * v


def _rw_prep(z2d, v_first, p, col_main, col_lora, w, lw, seq, tm, has_vres):
    t = z2d.shape[0]
    sub = V7X_SUBLANES
    mb = col_main // (3 * w)
    lbk = col_lora // lw
    rows8 = tm // sub

    def prev_idx(i):
        return jnp.maximum(i * rows8 - 1, 0)

    row = lambda n: pl.BlockSpec((1, n), lambda i: (0, 0))
    full = lambda a: pl.BlockSpec(a.shape, lambda i: (0,) * a.ndim)
    ospec = pl.BlockSpec((tm, w), lambda i: (i, 0))
    oshape = jax.ShapeDtypeStruct((t, w), F32)
    args = [z2d, z2d, z2d, z2d, v_first,
            p["mu_main"], p["mu_lora"], p["w0"], p["w2p"], p["a0"], p["a2p"], p["g2p"],
            p["k_k"], p["k_a"], p["r_k"], p["v0"], p["v1p"], p["v2p"], p["ones"]]
    in_specs = [pl.BlockSpec((tm, 3 * w), lambda i: (i, mb)),
                pl.BlockSpec((sub, 3 * w), lambda i: (prev_idx(i), mb)),
                pl.BlockSpec((tm, lw), lambda i: (i, lbk)),
                pl.BlockSpec((sub, lw), lambda i: (prev_idx(i), lbk)),
                ospec,
                row(3 * w), row(lw), row(w), full(p["w2p"]), row(w), full(p["a2p"]), full(p["g2p"]),
                row(w), row(w), row(w), row(w), full(p["v1p"]), full(p["v2p"]), full(p["ones"])]
    return pl.pallas_call(
        functools.partial(_rw_prep_kernel, w=w, tiles_per_seq=seq // tm, has_vres=has_vres),
        grid=(t // tm,),
        in_specs=in_specs,
        out_specs=[pl.BlockSpec((tm, RW_NVEC * w), lambda i: (i, 0)), ospec, ospec, ospec],
        out_shape=[jax.ShapeDtypeStruct((t, RW_NVEC * w), F32), oshape, oshape, oshape],
        compiler_params=_cparams("parallel"),
        name="rw_prep",
    )(*args)


def _rw_scan_kernel(x_ref, v_ref, y_ref, s_ref, *, tc, dk, nvb):
    sub = V7X_SUBLANES
    lanes = V7X_LANES
    jr, jw, jk, ja, jb = range(RW_NVEC)

    @pl.when(pl.program_id(0) == 0)
    def _():
        s_ref[...] = jnp.zeros(s_ref.shape, F32)

    def bc(j, t, kk):
        return jnp.broadcast_to(x_ref[t, j, pl.ds(kk, 1), :], (sub, lanes))

    sa0 = [jnp.zeros((sub, lanes), F32) for _ in range(nvb)]
    for kk in range(dk):
        arow = bc(ja, 0, kk)
        for j in range(nvb):
            sa0[j] = sa0[j] + s_ref[kk, pl.ds(sub * j, sub), :] * arow

    def step(t, sa):
        tn = jnp.minimum(t + 1, tc - 1)
        vb = [v_ref[t, pl.ds(sub * j, sub), :] for j in range(nvb)]
        yacc = [jnp.zeros((sub, lanes), F32) for _ in range(nvb)]
        sacc = [jnp.zeros((sub, lanes), F32) for _ in range(nvb)]
        for kk in range(dk):
            wrow = bc(jw, t, kk)
            brow = bc(jb, t, kk)
            krow = bc(jk, t, kk)
            rrow = bc(jr, t, kk)
            anext = bc(ja, tn, kk)
            for j in range(nvb):
                s = s_ref[kk, pl.ds(sub * j, sub), :] * wrow + sa[j] * brow + vb[j] * krow
                s_ref[kk, pl.ds(sub * j, sub), :] = s
                yacc[j] = yacc[j] + s * rrow
                sacc[j] = sacc[j] + s * anext
        for j in range(nvb):
            y_ref[t, pl.ds(sub * j, sub), :] = yacc[j]
        return tuple(sacc)

    lax.fori_loop(0, tc, step, tuple(sa0))


def _rw_scan(x_l, v_l, tc):
    s, nvec, dk, lanes = x_l.shape
    vl = v_l.shape[1]
    kspec = pl.BlockSpec((tc, nvec, dk, lanes), lambda i: (i, 0, 0, 0))
    vspec = pl.BlockSpec((tc, vl, lanes), lambda i: (i, 0, 0))
    return pl.pallas_call(
        functools.partial(_rw_scan_kernel, tc=tc, dk=dk, nvb=vl // V7X_SUBLANES),
        grid=(s // tc,),
        in_specs=[kspec, vspec],
        out_specs=vspec,
        out_shape=jax.ShapeDtypeStruct((s, vl, lanes), F32),
        scratch_shapes=[pltpu.VMEM((dk, vl, lanes), F32)],
        compiler_params=_cparams("arbitrary"),
        name="rw_scan",
    )(x_l, v_l)


def _rw_post_kernel(y_ref, g_ref, bonus_ref, lnw_ref, lnb_ref, ones_ref, o_ref):
    y = y_ref[...]
    ones = ones_ref[...]
    inv_n = 1.0 / RW_HEAD
    mean = _dot_exact_rhs(y, ones) * inv_n
    yc = y - mean
    var = _dot_exact_rhs(yc * yc, ones) * inv_n
    y = yc * lax.rsqrt(var + RW_LN_EPS) * lnw_ref[...] + lnb_ref[...]
    o_ref[...] = (y + bonus_ref[...]) * g_ref[...]


def _rw_post(y2d, g, bonus, ln_w, ln_b, ones, tm):
    t, w = y2d.shape
    tspec = pl.BlockSpec((tm, w), lambda i: (i, 0))
    row = pl.BlockSpec((1, w), lambda i: (0, 0))
    return pl.pallas_call(
        _rw_post_kernel,
        grid=(t // tm,),
        in_specs=[tspec, tspec, tspec, row, row, pl.BlockSpec((w, w), lambda i: (0, 0))],
        out_specs=tspec,
        out_shape=jax.ShapeDtypeStruct((t, w), F32),
        compiler_params=_cparams("parallel"),
        name="rw_post",
    )(y2d, g, bonus, ln_w.reshape(1, w), ln_b.reshape(1, w), ones)


def _rwkv7(z2d, v_first, p, bsz, seq, w, col_main, col_lora, lw, tm, tc, has_vres):
    heads = w // RW_HEAD
    nq = V7X_LANES // (bsz * heads)
    kvec, v, g, bonus = _rw_prep(z2d, v_first, p, col_main, col_lora, w, lw, seq, tm, has_vres)
    x_l = _chains_to_lanes_k(kvec, bsz, seq, heads, RW_HEAD, nq)
    v_l = _chains_to_lanes_v(v, bsz, seq, heads, RW_HEAD, nq)
    y_l = _rw_scan(x_l, v_l, tc)
    y = _lanes_to_chains_v(y_l, bsz, seq, heads, RW_HEAD, nq)
    return _rw_post(y, g, bonus, p["ln_w"], p["ln_b"], p["ones"], tm), v


def _s5_disc_kernel(are_ref, aim_ref, dt_ref, bre_ref, bim_ref, lre_ref, lim_ref, bbre_ref, bbim_ref):
    a_re = are_ref[...]
    a_im = aim_ref[...]
    dt = jnp.exp(dt_ref[...])
    mag = jnp.exp(dt * a_re)
    lam_re = mag * jnp.cos(dt * a_im)
    lam_im = mag * jnp.sin(dt * a_im)
    den = a_re * a_re + a_im * a_im
    coef_re = ((lam_re - 1.0) * a_re + lam_im * a_im) / den
    coef_im = (lam_im * a_re - (lam_re - 1.0) * a_im) / den
    b_re = bre_ref[...]
    b_im = bim_ref[...]
    lre_ref[...] = lam_re
    lim_ref[...] = lam_im
    bbre_ref[...] = coef_re * b_re - coef_im * b_im
    bbim_ref[...] = coef_re * b_im + coef_im * b_re


def _s5_discretize(a_re, a_im, log_dt, b_re, b_im):
    g, n, c = b_re.shape
    shp = (g, n * c)
    bc = lambda a: jnp.broadcast_to(a[..., None], (g, n, c)).reshape(shp)
    dtb = jnp.broadcast_to(log_dt[:, None], shp)
    o = jax.ShapeDtypeStruct(shp, F32)
    lre, lim, bbre, bbim = pl.pallas_call(
        _s5_disc_kernel, out_shape=[o, o, o, o], name="s5_disc",
    )(bc(a_re), bc(a_im), dtb, b_re.reshape(shp), b_im.reshape(shp))
    un = lambda a: a.reshape(g, n, c)
    return un(lre)[..., 0], un(lim)[..., 0], un(bbre), un(bbim)


def _s5_kernel(u_ref, bblk_ref, cblk_ref, lre_ref, lim_ref, d_ref, wg_ref, bg_ref, y_ref,
               h_ref, hr_ref, hi_ref, *, tc, bsz, ns, lane_chunk):
    @pl.when(pl.program_id(0) == 0)
    def _():
        hr_ref[...] = jnp.zeros(hr_ref.shape, F32)
        hi_ref[...] = jnp.zeros(hi_ref.shape, F32)

    u = u_ref[...]
    h_ref[...] = _bdot(u, bblk_ref[...])
    for c0 in range(0, ns, lane_chunk):
        lr = jnp.broadcast_to(lre_ref[:, c0:c0 + lane_chunk], (bsz, lane_chunk))
        li = jnp.broadcast_to(lim_ref[:, c0:c0 + lane_chunk], (bsz, lane_chunk))

        def step(t, carry, c0=c0, lr=lr, li=li):
            hr, hi = carry
            rows = pl.ds(pl.multiple_of(t * bsz, bsz), bsz)
            nr = lr * hr - li * hi + h_ref[rows, c0:c0 + lane_chunk]
            ni = lr * hi + li * hr + h_ref[rows, ns + c0:ns + c0 + lane_chunk]
            h_ref[rows, c0:c0 + lane_chunk] = nr
            h_ref[rows, ns + c0:ns + c0 + lane_chunk] = ni
            return nr, ni

        hr, hi = lax.fori_loop(0, tc, step,
                               (hr_ref[:, c0:c0 + lane_chunk], hi_ref[:, c0:c0 + lane_chunk]))
        hr_ref[:, c0:c0 + lane_chunk] = hr
        hi_ref[:, c0:c0 + lane_chunk] = hi
    y = _bdot(h_ref[...], cblk_ref[...]) + d_ref[...] * u
    y = jax.nn.gelu(y)
    y_ref[...] = y * jax.nn.sigmoid(_bdot(y, wg_ref[...]) + bg_ref[...])


def _s5(u_tm, p, bsz, tc):
    rows, w = u_tm.shape
    ns = p["lam_re"].shape[1]
    blk = tc * bsz
    full = lambda a: pl.BlockSpec(a.shape, lambda i: (0,) * a.ndim)
    return pl.pallas_call(
        functools.partial(_s5_kernel, tc=tc, bsz=bsz, ns=ns, lane_chunk=min(ns, 512)),
        grid=(rows // blk,),
        in_specs=[pl.BlockSpec((blk, w), lambda i: (i, 0)),
                  full(p["bblk"]), full(p["cblk"]), full(p["lam_re"]), full(p["lam_im"]),
                  full(p["d"]), full(p["w_glu"]), full(p["b_glu"])],
        out_specs=pl.BlockSpec((blk, w), lambda i: (i, 0)),
        out_shape=jax.ShapeDtypeStruct((rows, w), F32),
        scratch_shapes=[pltpu.VMEM((blk, 2 * ns), F32),
                        pltpu.VMEM((bsz, ns), F32), pltpu.VMEM((bsz, ns), F32)],
        compiler_params=_cparams("arbitrary"),
        name="s5",
    )(u_tm, p["bblk"], p["cblk"], p["lam_re"], p["lam_im"], p["d"], p["w_glu"], p["b_glu"])


def _mb_kernel(gate_ref, x_ref, bc_ref, dt_ref, cw_ref, cb_ref, dtb_ref, alog_ref, dsk_ref, nw_ref,
               expand_ref, tril_ref, y_ref, prev_ref, st_ref, *, q, w, heads, groups, nstate):
    sub = V7X_SUBLANES
    lanes = V7X_LANES
    hd = w // heads
    gw = w // groups
    hpg = heads // groups

    @pl.when(pl.program_id(1) == 0)
    def _():
        prev_ref[...] = jnp.zeros(prev_ref.shape, F32)
        st_ref[...] = jnp.zeros(st_ref.shape, F32)

    xbc = jnp.concatenate([x_ref[0], bc_ref[0]], axis=-1)
    full = jnp.concatenate([prev_ref[...], xbc], axis=0)
    conv = jnp.broadcast_to(cb_ref[...], xbc.shape)
    for j in range(MB_CONV):
        shift = MB_CONV - 1 - j
        src = full if shift == 0 else pltpu.roll(full, shift, axis=0)
        conv = conv + src[sub:sub + q, :] * cw_ref[j:j + 1, :]
    prev_ref[...] = xbc[q - sub:q, :]
    act = _silu(conv)
    xs = act[:, 0:w]
    bmat = act[:, w:w + groups * nstate]
    cmat = act[:, w + groups * nstate:w + 2 * groups * nstate]

    dt = _softplus(dt_ref[0] + dtb_ref[...])
    a = -jnp.exp(alog_ref[...]) * dt
    a_cum = _dot_exact_lhs(tril_ref[...], a)
    expand = expand_ref[...]
    dt_e = _dot_exact_rhs(dt, expand)
    acum_e = _dot_exact_rhs(a_cum, expand)
    alast_e = acum_e[q - 1:q, :]
    xdt = xs * dt_e
    xdec = xdt * jnp.exp(alast_e - acum_e)

    rid = lax.broadcasted_iota(jnp.int32, (q, q), 0)
    cid = lax.broadcasted_iota(jnp.int32, (q, q), 1)
    causal = rid >= cid
    sel_r = lax.broadcasted_iota(jnp.int32, (lanes, q), 0)
    sel_c = lax.broadcasted_iota(jnp.int32, (q, lanes), 1)
    lane_w = lax.broadcasted_iota(jnp.int32, (q, gw), 1)

    y_parts = []
    for g in range(groups):
        bg = bmat[:, g * nstate:(g + 1) * nstate]
        cg = cmat[:, g * nstate:(g + 1) * nstate]
        scores = lax.dot_general(cg.astype(BF16), bg.astype(BF16), (((1,), (1,)), ((), ())),
                                 preferred_element_type=F32)
        xg = xdt[:, g * gw:(g + 1) * gw]
        yg = _bdot(cg, st_ref[g]) * jnp.exp(acum_e[:, g * gw:(g + 1) * gw])
        for hh in range(hpg):
            h = g * hpg + hh
            col = _dot_exact_rhs(a_cum, (sel_r == h).astype(BF16))
            rowv = _dot_exact_nt((sel_c == h).astype(BF16), a_cum)
            decay = jnp.where(causal, jnp.exp(col - rowv), 0.0)
            xh = jnp.where((lane_w >= hh * hd) & (lane_w < (hh + 1) * hd), xg, 0.0)
            yg = yg + _bdot(scores * decay, xh)
        y_parts.append(yg)
        upd = _bdot(bg.T, xdec[:, g * gw:(g + 1) * gw])
        st_ref[g] = st_ref[g] * jnp.exp(alast_e[:, g * gw:(g + 1) * gw]) + upd
    y = jnp.concatenate(y_parts, axis=-1) + dsk_ref[...] * xs
    y_ref[0] = _rms(y * _silu(gate_ref[0]), nw_ref[...])


def _mamba2(z3d, p, w, cols, q):
    bsz, seq, _ = z3d.shape
    heads = w // MB_HEADDIM
    lanes = V7X_LANES
    cg, cx, cbc, cdt = cols
    blk = lambda width, col: pl.BlockSpec((1, q, width), lambda b, c: (b, c, col // width))
    full = lambda a: pl.BlockSpec(a.shape, lambda b, c: (0,) * a.ndim)
    consts = [p["conv_w"], p["conv_b"], p["dt_bias"], p["a_log"], p["d"], p["norm_w"],
              p["expand"], p["tril"]]
    return pl.pallas_call(
        functools.partial(_mb_kernel, q=q, w=w, heads=heads, groups=MB_GROUPS, nstate=MB_STATE),
        grid=(bsz, seq // q),
        in_specs=[blk(w, cg), blk(w, cx), blk(w, cbc), blk(lanes, cdt)] + [full(a) for a in consts],
        out_specs=pl.BlockSpec((1, q, w), lambda b, c: (b, c, 0)),
        out_shape=jax.ShapeDtypeStruct((bsz, seq, w), F32),
        scratch_shapes=[pltpu.VMEM((V7X_SUBLANES, 2 * w), F32),
                        pltpu.VMEM((MB_GROUPS, MB_STATE, w // MB_GROUPS), F32)],
        compiler_params=_cparams("parallel", "arbitrary"),
        name="mamba2",
    )(z3d, z3d, z3d, z3d, *consts)


def _pad_rows(a, rows, at):
    out = jnp.zeros((rows, a.shape[1]), a.dtype)
    return lax.dynamic_update_slice(out, a, (at, 0))


def _block_ones(w, head):
    idx = jnp.arange(w) // head
    return (idx[:, None] == idx[None, :]).astype(BF16)


def _pick_tile(n, target):
    t = min(n, target)
    while n % t:
        t //= 2
    return t


def kernel(x, norm_mix_w, w_in, w_branch, w_out, norm_ffn_w, w_ffn_in, w_ffn_out, norm_final_w, hgrn_lower_bounds, hgrn_norm_w, rwkv_mu, rwkv_w0, rwkv_w2, rwkv_a0, rwkv_a2, rwkv_g2, rwkv_k_k, rwkv_k_a, rwkv_r_k, rwkv_ln_w, rwkv_ln_b, rwkv_v0, rwkv_v1, rwkv_v2, s5_a_re, s5_a_im, s5_b_re, s5_b_im, s5_c_re, s5_c_im, s5_d, s5_log_dt, s5_w_glu, s5_b_glu, mamba_conv_w, mamba_conv_b, mamba_dt_bias, mamba_a_log, mamba_d, mamba_norm_w):
    bsz, seq, d = x.shape
    depth = w_in.shape[0]
    w = d // 2
    lanes = V7X_LANES
    t = bsz * seq
    mb_heads = w // MB_HEADDIM
    mb_bc = 2 * MB_GROUPS * MB_STATE
    lw = RW_DECAY_LORA + RW_A_LORA + RW_G_LORA
    assert mb_bc == w and 3 * w % lw == 0

    o_gate = 0
    o_hg = o_gate + N_BRANCH * d
    o_rw = o_hg + 4 * w
    o_rwl = o_rw + 3 * w
    o_s5 = o_rwl + lw
    o_mbg = o_s5 + w
    o_mbx = o_mbg + w
    o_mbbc = o_mbx + w
    o_mbdt = o_mbbc + mb_bc
    c_rw = 0
    c_hg = c_rw + 3 * w
    c_s5 = c_hg + 4 * w
    c_mbg = c_s5 + w
    c_mbx = c_mbg + w
    c_mbbc = c_mbx + w
    c_rwl = c_mbbc + mb_bc
    c_mbdt = c_rwl + lw
    n_cols = c_mbdt + lanes
    tn = 512
    n_pad = -(-n_cols // tn) * tn

    def mixer_cols(wl):
        sl = lambda a, b: wl[:, a:b]
        pieces = [sl(o_rw, o_rwl), sl(o_hg, o_rw), sl(o_s5, o_mbdt), sl(o_rwl, o_s5),
                  sl(o_mbdt, o_mbdt + mb_heads),
                  jnp.zeros((d, n_pad - c_mbdt - mb_heads), wl.dtype)]
        return jnp.concatenate(pieces, axis=1).astype(BF16)

    tm = _pick_tile(seq, 256)
    tm_in = _pick_tile(t, 1024)
    tb_hg = _pick_tile(seq, 256)
    c_hg_chunk = min(HG_CHUNK, tb_hg)
    tc_rw = _pick_tile(seq, 32)
    tc_s5 = _pick_tile(seq, 64)
    q_mb = min(MB_CHUNK, seq)
    ff = w_ffn_out.shape[1]
    tn_ff = 256 if ff % 256 == 0 else lanes

    lower_bounds = _hg_bounds(hgrn_lower_bounds)
    ones_rw = _block_ones(w, RW_HEAD)
    eye_g = jnp.eye(w // S5_GROUP, dtype=F32)
    expand = (jnp.arange(lanes)[:, None] == (jnp.arange(w) // MB_HEADDIM)[None, :]).astype(BF16)
    tril = (jnp.arange(q_mb)[:, None] >= jnp.arange(q_mb)[None, :]).astype(BF16)

    x2d = x.reshape(t, d)
    v_first = jnp.zeros((t, w), F32)
    for l in range(depth):
        gates = _inproj(x2d, norm_mix_w[l], w_in[l][:, o_gate:o_hg].astype(BF16), tm_in, tn, True)
        z2d = _inproj(x2d, norm_mix_w[l], mixer_cols(w_in[l]), tm_in, tn, False)
        z3d = z2d.reshape(bsz, seq, n_pad)

        y_hg = _hgrn2(z3d, lower_bounds[l], hgrn_norm_w[l], w, c_hg, tb_hg, c_hg_chunk,
                      HG_SUBBLOCK).reshape(t, w)

        mu = rwkv_mu[l]
        has_vres = l > 0
        lv = max(l - 1, 0)
        rp = {
            "mu_main": mu[None, :3 * w], "mu_lora": mu[None, 3 * w:],
            "w0": rwkv_w0[l][None], "a0": rwkv_a0[l][None],
            "w2p": _pad_rows(rwkv_w2[l], lw, 0).astype(BF16),
            "a2p": _pad_rows(rwkv_a2[l], lw, RW_DECAY_LORA).astype(BF16),
            "g2p": _pad_rows(rwkv_g2[l], lw, RW_DECAY_LORA + RW_A_LORA).astype(BF16),
            "k_k": rwkv_k_k[l][None], "k_a": rwkv_k_a[l][None], "r_k": rwkv_r_k[l].reshape(1, w),
            "v0": rwkv_v0[lv][None],
            "v1p": jnp.pad(rwkv_v1[lv], ((0, 0), (0, lanes - rwkv_v1.shape[2]))).astype(BF16),
            "v2p": _pad_rows(rwkv_v2[lv], lanes, 0).astype(BF16),
            "ones": ones_rw, "ln_w": rwkv_ln_w[l], "ln_b": rwkv_ln_b[l],
        }
        y_rw, v_l = _rwkv7(z2d, v_first, rp, bsz, seq, w, c_rw, c_rwl, lw, tm, tc_rw, has_vres)
        if l == 0:
            v_first = v_l

        lam_re, lam_im, bb_re, bb_im = _s5_discretize(s5_a_re[l], s5_a_im[l], s5_log_dt[l],
                                                      s5_b_re[l], s5_b_im[l])
        ns = lam_re.size
        blk_in = lambda bb: jnp.einsum("gnc,gh->gchn", bb, eye_g).reshape(w, ns)
        blk_out = lambda cc: jnp.einsum("gcn,gh->gnhc", cc, eye_g).reshape(ns, w)
        sp = {
            "bblk": jnp.concatenate([blk_in(bb_re), blk_in(bb_im)], axis=1).astype(BF16),
            "cblk": jnp.concatenate([blk_out(s5_c_re[l]), -blk_out(s5_c_im[l])], axis=0).astype(BF16),
            "lam_re": lam_re.reshape(1, ns), "lam_im": lam_im.reshape(1, ns),
            "d": s5_d[l][None], "w_glu": s5_w_glu[l].astype(BF16), "b_glu": s5_b_glu[l][None],
        }
        u_tm = z3d[:, :, c_s5:c_s5 + w].transpose(1, 0, 2).reshape(t, w)
        y_s5 = _s5(u_tm, sp, bsz, tc_s5).reshape(seq, bsz, w).transpose(1, 0, 2).reshape(t, w)

        mp = {
            "conv_w": mamba_conv_w[l], "conv_b": mamba_conv_b[l][None],
            "dt_bias": jnp.pad(mamba_dt_bias[l], (0, lanes - mb_heads))[None],
            "a_log": jnp.pad(mamba_a_log[l], (0, lanes - mb_heads))[None],
            "d": jnp.repeat(mamba_d[l], MB_HEADDIM)[None], "norm_w": mamba_norm_w[l][None],
            "expand": expand, "tril": tril,
        }
        y_mb = _mamba2(z3d, mp, w, (c_mbg, c_mbx, c_mbbc, c_mbdt), q_mb).reshape(t, w)

        x2d = _merge((y_hg, y_rw, y_s5, y_mb), gates, w_branch[l].astype(BF16), w_out[l].astype(BF16),
                     x2d, tm)
        act = _ffn_in(x2d, norm_ffn_w[l], w_ffn_in[l].astype(BF16), tm_in, tn_ff)
        x2d = _ffn_out(act, w_ffn_out[l].astype(BF16), x2d, norm_final_w, l == depth - 1, tm)
    return x2d.reshape(bsz, seq, d)
```

```python
import functools
import math

import jax
import jax.numpy as jnp
from jax import lax
from jax.experimental import pallas as pl
from jax.experimental.pallas import tpu as pltpu

F32 = jnp.float32
BF16 = jnp.bfloat16

V7X_LANES = 128
V7X_SUBLANES = 8
V7X_VMEM_LIMIT_BYTES = 56 * 1024 * 1024

RMS_EPS = 1e-6
N_BRANCH = 4
HG_DK = 128
HG_TINY = 1e-30
HG_CHUNK = 64
HG_SUBBLOCK = 8
RW_HEAD = 64
RW_DECAY_LORA = 64
RW_A_LORA = 64
RW_G_LORA = 128
RW_LN_EPS = 64e-5
RW_NVEC = 5
S5_GROUP = 16
S5_STATE = 64
MB_HEADDIM = 64
MB_GROUPS = 2
MB_STATE = 128
MB_CONV = 4
MB_CHUNK = 128


def _cparams(*sem):
    return pltpu.CompilerParams(dimension_semantics=sem, vmem_limit_bytes=V7X_VMEM_LIMIT_BYTES)


def _bdot(a, b):
    return jnp.dot(a.astype(BF16), b.astype(BF16), preferred_element_type=F32)


def _split3(x):
    x1 = x.astype(BF16)
    r1 = x - x1.astype(F32)
    x2 = r1.astype(BF16)
    x3 = (r1 - x2.astype(F32)).astype(BF16)
    return x1, x2, x3


def _dot_exact_rhs(x, m):
    x1, x2, x3 = _split3(x)
    d = functools.partial(jnp.dot, preferred_element_type=F32)
    return d(x1, m) + d(x2, m) + d(x3, m)


def _dot_exact_lhs(m, x):
    x1, x2, x3 = _split3(x)
    d = functools.partial(jnp.dot, preferred_element_type=F32)
    return d(m, x1) + d(m, x2) + d(m, x3)


def _softplus(x):
    return jnp.maximum(x, 0.0) + jnp.log(1.0 + jnp.exp(-jnp.abs(x)))


def _silu(x):
    return x * jax.nn.sigmoid(x)


def _rms(x, w):
    return x * lax.rsqrt(jnp.mean(x * x, axis=-1, keepdims=True) + RMS_EPS) * w


def _resident(shape):
    return pl.BlockSpec(shape, lambda i: (0,) * len(shape), pipeline_mode=pl.Buffered(1))


def _inproj_kernel(x_ref, nw_ref, w_ref, o_ref, *, gate, tn):
    u = _rms(x_ref[...], nw_ref[...]).astype(BF16)
    for n0 in range(0, o_ref.shape[1], tn):
        z = jnp.dot(u, w_ref[:, n0:n0 + tn], preferred_element_type=F32)
        o_ref[:, n0:n0 + tn] = jax.nn.sigmoid(z).astype(o_ref.dtype) if gate else z


def _inproj(x2d, norm_w, w_bf16, tm, tn, gate):
    t, d = x2d.shape
    n = w_bf16.shape[1]
    return pl.pallas_call(
        functools.partial(_inproj_kernel, gate=gate, tn=tn),
        grid=(t // tm,),
        in_specs=[pl.BlockSpec((tm, d), lambda i: (i, 0)),
                  pl.BlockSpec((1, d), lambda i: (0, 0)),
                  _resident((d, n))],
        out_specs=pl.BlockSpec((tm, n), lambda i: (i, 0)),
        out_shape=jax.ShapeDtypeStruct((t, n), BF16 if gate else F32),
        compiler_params=_cparams("parallel"),
        name="inproj_gate" if gate else "inproj",
    )(x2d, norm_w.reshape(1, d), w_bf16)


def _ffn_in_kernel(x_ref, nw_ref, w_ref, o_ref, *, tn):
    ff = o_ref.shape[1]
    u = _rms(x_ref[...], nw_ref[...]).astype(BF16)
    for n0 in range(0, ff, tn):
        gate = jnp.dot(u, w_ref[:, n0:n0 + tn], preferred_element_type=F32)
        up = jnp.dot(u, w_ref[:, ff + n0:ff + n0 + tn], preferred_element_type=F32)
        o_ref[:, n0:n0 + tn] = (_silu(gate) * up).astype(BF16)


def _ffn_in(x2d, norm_w, w_bf16, tm, tn):
    t, d = x2d.shape
    ff = w_bf16.shape[1] // 2
    return pl.pallas_call(
        functools.partial(_ffn_in_kernel, tn=tn),
        grid=(t // tm,),
        in_specs=[pl.BlockSpec((tm, d), lambda i: (i, 0)),
                  pl.BlockSpec((1, d), lambda i: (0, 0)),
                  _resident((d, 2 * ff))],
        out_specs=pl.BlockSpec((tm, ff), lambda i: (i, 0)),
        out_shape=jax.ShapeDtypeStruct((t, ff), BF16),
        compiler_params=_cparams("parallel"),
        name="ffn_in",
    )(x2d, norm_w.reshape(1, d), w_bf16)


def _ffn_out_kernel(a_ref, w_ref, x_ref, fw_ref, o_ref, *, final_norm):
    y = x_ref[...] + jnp.dot(a_ref[...], w_ref[...], preferred_element_type=F32)
    if final_norm:
        y = _rms(y, fw_ref[...])
    o_ref[...] = y


def _ffn_out(act, w_bf16, x2d, final_w, final_norm, tm):
    t, d = x2d.shape
    ff = act.shape[1]
    return pl.pallas_call(
        functools.partial(_ffn_out_kernel, final_norm=final_norm),
        grid=(t // tm,),
        in_specs=[pl.BlockSpec((tm, ff), lambda i: (i, 0)),
                  _resident((ff, d)),
                  pl.BlockSpec((tm, d), lambda i: (i, 0)),
                  pl.BlockSpec((1, d), lambda i: (0, 0))],
        out_specs=pl.BlockSpec((tm, d), lambda i: (i, 0)),
        out_shape=jax.ShapeDtypeStruct((t, d), F32),
        compiler_params=_cparams("parallel"),
        name="ffn_out",
    )(act, w_bf16, x2d, final_w.reshape(1, d))


def _merge_kernel(yh_ref, yr_ref, ys_ref, ym_ref, zg_ref, wb_ref, wo_ref, x_ref, o_ref):
    d = x_ref.shape[1]
    acc = jnp.zeros(x_ref.shape, F32)
    for k, y_ref in enumerate((yh_ref, yr_ref, ys_ref, ym_ref)):
        proj = jnp.dot(y_ref[...].astype(BF16), wb_ref[k], preferred_element_type=F32)
        acc = acc + zg_ref[:, k * d:(k + 1) * d].astype(F32) * proj
    o_ref[...] = x_ref[...] + jnp.dot(acc.astype(BF16), wo_ref[...], preferred_element_type=F32)


def _merge(ys, z2d, wb_bf16, wo_bf16, x2d, tm):
    t, d = x2d.shape
    w = ys[0].shape[1]
    yspec = pl.BlockSpec((tm, w), lambda i: (i, 0))
    return pl.pallas_call(
        _merge_kernel,
        grid=(t // tm,),
        in_specs=[yspec, yspec, yspec, yspec,
                  pl.BlockSpec((tm, N_BRANCH * d), lambda i: (i, 0)),
                  _resident((N_BRANCH, w, d)),
                  _resident((d, d)),
                  pl.BlockSpec((tm, d), lambda i: (i, 0))],
        out_specs=pl.BlockSpec((tm, d), lambda i: (i, 0)),
        out_shape=jax.ShapeDtypeStruct((t, d), F32),
        compiler_params=_cparams("parallel"),
        name="merge",
    )(*ys, z2d, wb_bf16, wo_bf16, x2d)


def _hg_bounds_kernel(h_ref, o_ref):
    h = h_ref[...]
    depth = h.shape[0]
    m = jnp.max(h, axis=0, keepdims=True)
    e = jnp.exp(h - m)
    p = e / jnp.sum(e, axis=0, keepdims=True)
    run = jnp.zeros_like(p[0:1])
    rows = []
    for l in range(depth):
        run = run + p[l:l + 1]
        rows.append(run - p[0:1])
    o_ref[...] = jnp.concatenate(rows, axis=0)


def _hg_bounds(hgrn_lower_bounds):
    return pl.pallas_call(
        _hg_bounds_kernel,
        out_shape=jax.ShapeDtypeStruct(hgrn_lower_bounds.shape, F32),
        name="hg_bounds",
    )(hgrn_lower_bounds)


def _hg_chunk_kernel(q_ref, f_ref, i_ref, g_ref, lb_ref, nw_ref, tril_ref, y_ref,
                     st_ref, kbuf, bbuf, vbuf, *, tb, c, cs, heads, dk):
    sub = V7X_SUBLANES
    nb = c // cs
    nt = (((1,), (1,)), ((), ()))

    @pl.when(pl.program_id(1) == 0)
    def _():
        st_ref[...] = jnp.zeros(st_ref.shape, F32)

    kbuf[0:sub, :] = jnp.zeros((sub, kbuf.shape[1]), F32)
    bbuf[0:sub, :] = jnp.zeros((sub, bbuf.shape[1]), F32)
    vbuf[0:sub, :] = jnp.zeros((sub, vbuf.shape[1]), F32)
    lb = lb_ref[...]
    tril = tril_ref[...]
    rid = lax.broadcasted_iota(jnp.int32, (c, dk), 0)
    rid1 = lax.broadcasted_iota(jnp.int32, (c, 1), 0)

    def chunk(ci, carry):
        rows = pl.ds(pl.multiple_of(ci * c, c), c)
        ff = f_ref[0, rows, :]
        q = _silu(q_ref[0, rows, :])
        dec = jnp.maximum(lb + (1.0 - lb) * jax.nn.sigmoid(ff), HG_TINY)
        k = (1.0 - lb) * jax.nn.sigmoid(-ff)
        v = i_ref[0, rows, :]
        b = _dot_exact_lhs(tril, jnp.log(dec))
        kbuf[sub:sub + c, :] = k
        bbuf[sub:sub + c, :] = b
        vbuf[sub:sub + c, :] = v
        outs = []
        for h in range(heads):
            hs = slice(h * dk, (h + 1) * dk)
            qh, kh, bh, vh = q[:, hs], k[:, hs], b[:, hs], v[:, hs]
            blast = bh[c - 1:c, :]
            st = st_ref[h]
            o = lax.dot_general((qh * jnp.exp(bh)).astype(BF16), st.astype(BF16), nt,
                                preferred_element_type=F32)
            att_rows = [jnp.zeros((cs, c), F32)]
            for i in range(1, nb):
                beta = bh[i * cs - 1:i * cs, :]
                kt = jnp.where(rid < i * cs, kh * jnp.exp(jnp.minimum(beta - bh, 0.0)), 0.0)
                qt = qh[i * cs:(i + 1) * cs, :] * jnp.exp(bh[i * cs:(i + 1) * cs, :] - beta)
                att_rows.append(lax.dot_general(qt.astype(BF16), kt.astype(BF16), nt,
                                                preferred_element_type=F32))
            o = o + _bdot(jnp.concatenate(att_rows, axis=0), vh)
            for dlt in range(cs):
                if dlt == 0:
                    a = jnp.sum(qh * kh, axis=-1, keepdims=True)
                    o = o + a * vh
                else:
                    win = pl.ds(sub - dlt, c)
                    e = jnp.exp(jnp.minimum(bh - bbuf[win, hs], 0.0))
                    a = jnp.sum(qh * kbuf[win, hs] * e, axis=-1, keepdims=True)
                    a = jnp.where((rid1 % cs) >= dlt, a, 0.0)
                    o = o + a * vbuf[win, hs]
            kd = kh * jnp.exp(blast - bh)
            st_ref[h] = st * jnp.exp(blast) + _bdot(vh.T, kd)
            outs.append(o * lax.rsqrt(jnp.mean(o * o, axis=-1, keepdims=True) + RMS_EPS))
        y = jnp.concatenate(outs, axis=-1) * nw_ref[...]
        y_ref[0, rows, :] = y * _silu(g_ref[0, rows, :])
        return carry

    lax.fori_loop(0, tb // c, chunk, 0)


def _hgrn2(z3d, lb, norm_w, w, col0, tb, c, cs):
    bsz, seq, _ = z3d.shape
    heads = w // HG_DK
    cb = col0 // w
    tril = (jnp.arange(c)[:, None] >= jnp.arange(c)[None, :]).astype(BF16)
    zspec = lambda j: pl.BlockSpec((1, tb, w), lambda b, i: (b, i, cb + j))
    row = pl.BlockSpec((1, w), lambda b, i: (0, 0))
    return pl.pallas_call(
        functools.partial(_hg_chunk_kernel, tb=tb, c=c, cs=cs, heads=heads, dk=HG_DK),
        grid=(bsz, seq // tb),
        in_specs=[zspec(0), zspec(1), zspec(2), zspec(3), row, row,
                  pl.BlockSpec((c, c), lambda b, i: (0, 0))],
        out_specs=pl.BlockSpec((1, tb, w), lambda b, i: (b, i, 0)),
        out_shape=jax.ShapeDtypeStruct((bsz, seq, w), F32),
        scratch_shapes=[pltpu.VMEM((heads, HG_DK, HG_DK), F32)]
        + [pltpu.VMEM((c + V7X_SUBLANES, w), F32)] * 3,
        compiler_params=_cparams("parallel", "arbitrary"),
        name="hgrn2",
    )(z3d, z3d, z3d, z3d, lb.reshape(1, w), norm_w.reshape(1, w), tril)


def _chains_to_lanes_k(a, bsz, seq, heads, dk, nq):
    nvec = a.shape[1] // (heads * dk)
    a = jnp.broadcast_to(a.reshape(1, bsz, seq, nvec, heads, dk), (nq, bsz, seq, nvec, heads, dk))
    return a.transpose(2, 3, 5, 0, 1, 4).reshape(seq, nvec, dk, nq * bsz * heads)


def _chains_to_lanes_v(a, bsz, seq, heads, dv, nq):
    a = a.reshape(bsz, seq, heads, nq, dv // nq).transpose(1, 4, 3, 0, 2)
    return a.reshape(seq, dv // nq, nq * bsz * heads)


def _lanes_to_chains_v(a, bsz, seq, heads, dv, nq):
    a = a.reshape(seq, dv // nq, nq, bsz, heads).transpose(3, 0, 4, 2, 1)
    return a.reshape(bsz * seq, heads * dv)


def _shift_rows(cur, prev_last, first):
    rolled = pltpu.roll(cur, 1, axis=0)
    row0 = jnp.where(first, 0.0, prev_last)
    rid = lax.broadcasted_iota(jnp.int32, cur.shape, 0)
    return jnp.where(rid == 0, jnp.broadcast_to(row0, cur.shape), rolled)


def _rw_prep_kernel(zm_ref, zmp_ref, zl_ref, zlp_ref, vf_ref,
                    mum_ref, mul_ref, w0_ref, w2_ref, a0_ref, a2_ref, g2_ref,
                    kk_ref, ka_ref, rk_ref, v0_ref, v1_ref, v2_ref, ones_ref,
                    kvec_out, v_out, g_out, bonus_out,
                    *, w, tiles_per_seq, has_vres):
    first = (pl.program_id(0) % tiles_per_seq) == 0
    sub = V7X_SUBLANES
    zm = zm_ref[...]
    zl = zl_ref[...]
    zms = zm + (_shift_rows(zm, zmp_ref[sub - 1:sub, :], first) - zm) * mum_ref[...]
    zls = zl + (_shift_rows(zl, zlp_ref[sub - 1:sub, :], first) - zl) * mul_ref[...]
    r = zms[:, 0:w]
    k = zms[:, w:2 * w]
    v = zms[:, 2 * w:3 * w]
    w_log = -_softplus(-(w0_ref[...] + _bdot(jnp.tanh(zls), w2_ref[...]))) - 0.5
    decay = jnp.exp(-jnp.exp(w_log))
    if has_vres:
        mix = jax.nn.sigmoid(v0_ref[...] + _bdot(_bdot(v, v1_ref[...]), v2_ref[...]))
        v = v + (vf_ref[...] - v) * mix
    a = jax.nn.sigmoid(a0_ref[...] + _bdot(zls, a2_ref[...]))
    g = _bdot(jax.nn.sigmoid(zls), g2_ref[...])
    ones = ones_ref[...]
    kk = k * kk_ref[...]
    ss = _dot_exact_rhs(kk * kk, ones)
    kk = kk / jnp.maximum(jnp.sqrt(ss), 1e-12)
    k2 = k * (1.0 + (a - 1.0) * ka_ref[...])
    for j, vec in enumerate((r, decay, k2, -kk, kk * a)):
        kvec_out[:, j * w:(j + 1) * w] = vec
    v_out[...] = v
    g_out[...] = g
    bonus_out[...] = _dot_exact_rhs(r * k2 * rk_ref[...], ones) * v


def _rw_prep(z2d, v_first, p, col_main, col_lora, w, lw, seq, tm, has_vres):
    t = z2d.shape[0]
    sub = V7X_SUBLANES
    mb = col_main // (3 * w)
    lbk = col_lora // lw
    rows8 = tm // sub

    def prev_idx(i):
        return jnp.maximum(i * rows8 - 1, 0)

    row = lambda n: pl.BlockSpec((1, n), lambda i: (0, 0))
    full = lambda a: pl.BlockSpec(a.shape, lambda i: (0,) * a.ndim)
    ospec = pl.BlockSpec((tm, w), lambda i: (i, 0))
    oshape = jax.ShapeDtypeStruct((t, w), F32)
    args = [z2d, z2d, z2d, z2d, v_first,
            p["mu_main"], p["mu_lora"], p["w0"], p["w2p"], p["a0"], p["a2p"], p["g2p"],
            p["k_k"], p["k_a"], p["r_k"], p["v0"], p["v1p"], p["v2p"], p["ones"]]
    in_specs = [pl.BlockSpec((tm, 3 * w), lambda i: (i, mb)),
                pl.BlockSpec((sub, 3 * w), lambda i: (prev_idx(i), mb)),
                pl.BlockSpec((tm, lw), lambda i: (i, lbk)),
                pl.BlockSpec((sub, lw), lambda i: (prev_idx(i), lbk)),
                ospec,
                row(3 * w), row(lw), row(w), full(p["w2p"]), row(w), full(p["a2p"]), full(p["g2p"]),
                row(w), row(w), row(w), row(w), full(p["v1p"]), full(p["v2p"]), full(p["ones"])]
    return pl.pallas_call(
        functools.partial(_rw_prep_kernel, w=w, tiles_per_seq=seq // tm, has_vres=has_vres),
        grid=(t // tm,),
        in_specs=in_specs,
        out_specs=[pl.BlockSpec((tm, RW_NVEC * w), lambda i: (i, 0)), ospec, ospec, ospec],
        out_shape=[jax.ShapeDtypeStruct((t, RW_NVEC * w), F32), oshape, oshape, oshape],
        compiler_params=_cparams("parallel"),
        name="rw_prep",
    )(*args)


def _rw_scan_kernel(x_ref, v_ref, y_ref, s_ref, *, tc, dk, nvb):
    sub = V7X_SUBLANES
    lanes = V7X_LANES
    jr, jw, jk, ja, jb = range(RW_NVEC)

    @pl.when(pl.program_id(0) == 0)
    def _():
        s_ref[...] = jnp.zeros(s_ref.shape, F32)

    def bc(j, t, kk):
        return jnp.broadcast_to(x_ref[t, j, pl.ds(kk, 1), :], (sub, lanes))

    sa0 = [jnp.zeros((sub, lanes), F32) for _ in range(nvb)]
    for kk in range(dk):
        arow = bc(ja, 0, kk)
        for j in range(nvb):
            sa0[j] = sa0[j] + s_ref[kk, pl.ds(sub * j, sub), :] * arow

    def step(t, sa):
        tn = jnp.minimum(t + 1, tc - 1)
        vb = [v_ref[t, pl.ds(sub * j, sub), :] for j in range(nvb)]
        yacc = [jnp.zeros((sub, lanes), F32) for _ in range(nvb)]
        sacc = [jnp.zeros((sub, lanes), F32) for _ in range(nvb)]
        for kk in range(dk):
            wrow = bc(jw, t, kk)
            brow = bc(jb, t, kk)
            krow = bc(jk, t, kk)
            rrow = bc(jr, t, kk)
            anext = bc(ja, tn, kk)
            for j in range(nvb):
                s = s_ref[kk, pl.ds(sub * j, sub), :] * wrow + sa[j] * brow + vb[j] * krow
                s_ref[kk, pl.ds(sub * j, sub), :] = s
                yacc[j] = yacc[j] + s * rrow
                sacc[j] = sacc[j] + s * anext
        for j in range(nvb):
            y_ref[t, pl.ds(sub * j, sub), :] = yacc[j]
        return tuple(sacc)

    lax.fori_loop(0, tc, step, tuple(sa0))


def _rw_scan(x_l, v_l, tc):
    s, nvec, dk, lanes = x_l.shape
    vl = v_l.shape[1]
    kspec = pl.BlockSpec((tc, nvec, dk, lanes), lambda i: (i, 0, 0, 0))
    vspec = pl.BlockSpec((tc, vl, lanes), lambda i: (i, 0, 0))
    return pl.pallas_call(
        functools.partial(_rw_scan_kernel, tc=tc, dk=dk, nvb=vl // V7X_SUBLANES),
        grid=(s // tc,),
        in_specs=[kspec, vspec],
        out_specs=vspec,
        out_shape=jax.ShapeDtypeStruct((s, vl, lanes), F32),
        scratch_shapes=[pltpu.VMEM((dk, vl, lanes), F32)],
        compiler_params=_cparams("arbitrary"),
        name="rw_scan",
    )(x_l, v_l)


def _rw_post_kernel(y_ref, g_ref, bonus_ref, lnw_ref, lnb_ref, ones_ref, o_ref):
    y = y_ref[...]
    ones = ones_ref[...]
    inv_n = 1.0 / RW_HEAD
    mean = _dot_exact_rhs(y, ones) * inv_n
    yc = y - mean
    var = _dot_exact_rhs(yc * yc, ones) * inv_n
    y = yc * lax.rsqrt(var + RW_LN_EPS) * lnw_ref[...] + lnb_ref[...]
    o_ref[...] = (y + bonus_ref[...]) * g_ref[...]


def _rw_post(y2d, g, bonus, ln_w, ln_b, ones, tm):
    t, w = y2d.shape
    tspec = pl.BlockSpec((tm, w), lambda i: (i, 0))
    row = pl.BlockSpec((1, w), lambda i: (0, 0))
    return pl.pallas_call(
        _rw_post_kernel,
        grid=(t // tm,),
        in_specs=[tspec, tspec, tspec, row, row, pl.BlockSpec((w, w), lambda i: (0, 0))],
        out_specs=tspec,
        out_shape=jax.ShapeDtypeStruct((t, w), F32),
        compiler_params=_cparams("parallel"),
        name="rw_post",
    )(y2d, g, bonus, ln_w.reshape(1, w), ln_b.reshape(1, w), ones)


def _rwkv7(z2d, v_first, p, bsz, seq, w, col_main, col_lora, lw, tm, tc, has_vres):
    heads = w // RW_HEAD
    nq = V7X_LANES // (bsz * heads)
    kvec, v, g, bonus = _rw_prep(z2d, v_first, p, col_main, col_lora, w, lw, seq, tm, has_vres)
    x_l = _chains_to_lanes_k(kvec, bsz, seq, heads, RW_HEAD, nq)
    v_l = _chains_to_lanes_v(v, bsz, seq, heads, RW_HEAD, nq)
    y_l = _rw_scan(x_l, v_l, tc)
    y = _lanes_to_chains_v(y_l, bsz, seq, heads, RW_HEAD, nq)
    return _rw_post(y, g, bonus, p["ln_w"], p["ln_b"], p["ones"], tm), v


def _s5_disc_kernel(are_ref, aim_ref, dt_ref, bre_ref, bim_ref, lre_ref, lim_ref, bbre_ref, bbim_ref):
    a_re = are_ref[...]
    a_im = aim_ref[...]
    dt = jnp.exp(dt_ref[...])
    mag = jnp.exp(dt * a_re)
    lam_re = mag * jnp.cos(dt * a_im)
    lam_im = mag * jnp.sin(dt * a_im)
    den = a_re * a_re + a_im * a_im
    coef_re = ((lam_re - 1.0) * a_re + lam_im * a_im) / den
    coef_im = (lam_im * a_re - (lam_re - 1.0) * a_im) / den
    b_re = bre_ref[...]
    b_im = bim_ref[...]
    lre_ref[...] = lam_re
    lim_ref[...] = lam_im
    bbre_ref[...] = coef_re * b_re - coef_im * b_im
    bbim_ref[...] = coef_re * b_im + coef_im * b_re


def _s5_discretize(a_re, a_im, log_dt, b_re, b_im):
    g, n, c = b_re.shape
    shp = (g, n * c)
    bc = lambda a: jnp.broadcast_to(a[..., None], (g, n, c)).reshape(shp)
    dtb = jnp.broadcast_to(log_dt[:, None], shp)
    o = jax.ShapeDtypeStruct(shp, F32)
    lre, lim, bbre, bbim = pl.pallas_call(
        _s5_disc_kernel, out_shape=[o, o, o, o], name="s5_disc",
    )(bc(a_re), bc(a_im), dtb, b_re.reshape(shp), b_im.reshape(shp))
    un = lambda a: a.reshape(g, n, c)
    return un(lre)[..., 0], un(lim)[..., 0], un(bbre), un(bbim)


def _s5_kernel(u_ref, bblk_ref, cblk_ref, lre_ref, lim_ref, d_ref, wg_ref, bg_ref, y_ref,
               h_ref, hr_ref, hi_ref, *, tc, bsz, ns, lane_chunk):
    @pl.when(pl.program_id(0) == 0)
    def _():
        hr_ref[...] = jnp.zeros(hr_ref.shape, F32)
        hi_ref[...] = jnp.zeros(hi_ref.shape, F32)

    u = u_ref[...]
    nsg = bblk_ref.shape[0]
    lanes = V7X_LANES
    for sg in range(nsg):
        c0 = sg * lane_chunk
        drive = _bdot(u[:, sg * lanes:(sg + 1) * lanes], bblk_ref[sg])
        h_ref[:, c0:c0 + lane_chunk] = drive[:, 0:lane_chunk]
        h_ref[:, ns + c0:ns + c0 + lane_chunk] = drive[:, lane_chunk:2 * lane_chunk]
    for c0 in range(0, ns, lane_chunk):
        lr = jnp.broadcast_to(lre_ref[:, c0:c0 + lane_chunk], (bsz, lane_chunk))
        li = jnp.broadcast_to(lim_ref[:, c0:c0 + lane_chunk], (bsz, lane_chunk))

        def step(t, carry, c0=c0, lr=lr, li=li):
            hr, hi = carry
            rows = pl.ds(pl.multiple_of(t * bsz, bsz), bsz)
            nr = lr * hr - li * hi + h_ref[rows, c0:c0 + lane_chunk]
            ni = lr * hi + li * hr + h_ref[rows, ns + c0:ns + c0 + lane_chunk]
            h_ref[rows, c0:c0 + lane_chunk] = nr
            h_ref[rows, ns + c0:ns + c0 + lane_chunk] = ni
            return nr, ni

        hr, hi = lax.fori_loop(0, tc, step,
                               (hr_ref[:, c0:c0 + lane_chunk], hi_ref[:, c0:c0 + lane_chunk]))
        hr_ref[:, c0:c0 + lane_chunk] = hr
        hi_ref[:, c0:c0 + lane_chunk] = hi
    outs = []
    for sg in range(nsg):
        c0 = sg * lane_chunk
        outs.append(_bdot(h_ref[:, c0:c0 + lane_chunk], cblk_ref[sg, 0:lane_chunk, :])
                    + _bdot(h_ref[:, ns + c0:ns + c0 + lane_chunk], cblk_ref[sg, lane_chunk:, :]))
    y = jnp.concatenate(outs, axis=-1) + d_ref[...] * u
    y = jax.nn.gelu(y)
    y_ref[...] = y * jax.nn.sigmoid(_bdot(y, wg_ref[...]) + bg_ref[...])


def _s5(u_tm, p, bsz, tc):
    rows, w = u_tm.shape
    ns = p["lam_re"].shape[1]
    blk = tc * bsz
    full = lambda a: pl.BlockSpec(a.shape, lambda i: (0,) * a.ndim)
    return pl.pallas_call(
        functools.partial(_s5_kernel, tc=tc, bsz=bsz, ns=ns, lane_chunk=ns // p["bblk"].shape[0]),
        grid=(rows // blk,),
        in_specs=[pl.BlockSpec((blk, w), lambda i: (i, 0)),
                  full(p["bblk"]), full(p["cblk"]), full(p["lam_re"]), full(p["lam_im"]),
                  full(p["d"]), full(p["w_glu"]), full(p["b_glu"])],
        out_specs=pl.BlockSpec((blk, w), lambda i: (i, 0)),
        out_shape=jax.ShapeDtypeStruct((rows, w), F32),
        scratch_shapes=[pltpu.VMEM((blk, 2 * ns), F32),
                        pltpu.VMEM((bsz, ns), F32), pltpu.VMEM((bsz, ns), F32)],
        compiler_params=_cparams("arbitrary"),
        name="s5",
    )(u_tm, p["bblk"], p["cblk"], p["lam_re"], p["lam_im"], p["d"], p["w_glu"], p["b_glu"])


def _mb_kernel(gate_ref, x_ref, bc_ref, dt_ref, cw_ref, cb_ref, dtb_ref, alog_ref, dsk_ref, nw_ref,
               expand_ref, tril_ref, y_ref, prev_ref, st_ref, *, q, w, heads, groups, nstate):
    sub = V7X_SUBLANES
    lanes = V7X_LANES
    hd = w // heads
    gw = w // groups
    hpg = heads // groups

    @pl.when(pl.program_id(1) == 0)
    def _():
        prev_ref[...] = jnp.zeros(prev_ref.shape, F32)
        st_ref[...] = jnp.zeros(st_ref.shape, F32)

    xbc = jnp.concatenate([x_ref[0], bc_ref[0]], axis=-1)
    full = jnp.concatenate([prev_ref[...], xbc], axis=0)
    conv = jnp.broadcast_to(cb_ref[...], xbc.shape)
    for j in range(MB_CONV):
        shift = MB_CONV - 1 - j
        src = full if shift == 0 else pltpu.roll(full, shift, axis=0)
        conv = conv + src[sub:sub + q, :] * cw_ref[j:j + 1, :]
    prev_ref[...] = xbc[q - sub:q, :]
    act = _silu(conv)
    xs = act[:, 0:w]
    bmat = act[:, w:w + groups * nstate]
    cmat = act[:, w + groups * nstate:w + 2 * groups * nstate]

    dt = _softplus(dt_ref[0] + dtb_ref[...])
    a = -jnp.exp(alog_ref[...]) * dt
    a_cum = _dot_exact_lhs(tril_ref[...], a)
    expand = expand_ref[...]
    dt_e = _dot_exact_rhs(dt, expand)
    acum_e = _dot_exact_rhs(a_cum, expand)
    alast_e = acum_e[q - 1:q, :]
    xdt = xs * dt_e
    xdec = xdt * jnp.exp(alast_e - acum_e)

    rid = lax.broadcasted_iota(jnp.int32, (q, q), 0)
    cid = lax.broadcasted_iota(jnp.int32, (q, q), 1)
    causal = rid >= cid
    a_cum_t = a_cum.T
    lane_w = lax.broadcasted_iota(jnp.int32, (q, gw), 1)

    y_parts = []
    for g in range(groups):
        bg = bmat[:, g * nstate:(g + 1) * nstate]
        cg = cmat[:, g * nstate:(g + 1) * nstate]
        scores = lax.dot_general(cg.astype(BF16), bg.astype(BF16), (((1,), (1,)), ((), ())),
                                 preferred_element_type=F32)
        xg = xdt[:, g * gw:(g + 1) * gw]
        yg = _bdot(cg, st_ref[g]) * jnp.exp(acum_e[:, g * gw:(g + 1) * gw])
        for hh in range(hpg):
            h = g * hpg + hh
            col = jnp.broadcast_to(a_cum[:, h:h + 1], (q, q))
            rowv = jnp.broadcast_to(a_cum_t[h:h + 1, :], (q, q))
            decay = jnp.where(causal, jnp.exp(col - rowv), 0.0)
            xh = jnp.where((lane_w >= hh * hd) & (lane_w < (hh + 1) * hd), xg, 0.0)
            yg = yg + _bdot(scores * decay, xh)
        y_parts.append(yg)
        upd = _bdot(bg.T, xdec[:, g * gw:(g + 1) * gw])
        st_ref[g] = st_ref[g] * jnp.exp(alast_e[:, g * gw:(g + 1) * gw]) + upd
    y = jnp.concatenate(y_parts, axis=-1) + dsk_ref[...] * xs
    y_ref[0] = _rms(y * _silu(gate_ref[0]), nw_ref[...])


def _mamba2(z3d, p, w, cols, q):
    bsz, seq, _ = z3d.shape
    heads = w // MB_HEADDIM
    lanes = V7X_LANES
    cg, cx, cbc, cdt = cols
    blk = lambda width, col: pl.BlockSpec((1, q, width), lambda b, c: (b, c, col // width))
    full = lambda a: pl.BlockSpec(a.shape, lambda b, c: (0,) * a.ndim)
    consts = [p["conv_w"], p["conv_b"], p["dt_bias"], p["a_log"], p["d"], p["norm_w"],
              p["expand"], p["tril"]]
    return pl.pallas_call(
        functools.partial(_mb_kernel, q=q, w=w, heads=heads, groups=MB_GROUPS, nstate=MB_STATE),
        grid=(bsz, seq // q),
        in_specs=[blk(w, cg), blk(w, cx), blk(w, cbc), blk(lanes, cdt)] + [full(a) for a in consts],
        out_specs=pl.BlockSpec((1, q, w), lambda b, c: (b, c, 0)),
        out_shape=jax.ShapeDtypeStruct((bsz, seq, w), F32),
        scratch_shapes=[pltpu.VMEM((V7X_SUBLANES, 2 * w), F32),
                        pltpu.VMEM((MB_GROUPS, MB_STATE, w // MB_GROUPS), F32)],
        compiler_params=_cparams("parallel", "arbitrary"),
        name="mamba2",
    )(z3d, z3d, z3d, z3d, *consts)


def _pad_rows(a, rows, at):
    out = jnp.zeros((rows, a.shape[1]), a.dtype)
    return lax.dynamic_update_slice(out, a, (at, 0))


def _block_ones(w, head):
    idx = jnp.arange(w) // head
    return (idx[:, None] == idx[None, :]).astype(BF16)


def _pick_tile(n, target):
    t = min(n, target)
    while n % t:
        t //= 2
    return t


def kernel(x, norm_mix_w, w_in, w_branch, w_out, norm_ffn_w, w_ffn_in, w_ffn_out, norm_final_w, hgrn_lower_bounds, hgrn_norm_w, rwkv_mu, rwkv_w0, rwkv_w2, rwkv_a0, rwkv_a2, rwkv_g2, rwkv_k_k, rwkv_k_a, rwkv_r_k, rwkv_ln_w, rwkv_ln_b, rwkv_v0, rwkv_v1, rwkv_v2, s5_a_re, s5_a_im, s5_b_re, s5_b_im, s5_c_re, s5_c_im, s5_d, s5_log_dt, s5_w_glu, s5_b_glu, mamba_conv_w, mamba_conv_b, mamba_dt_bias, mamba_a_log, mamba_d, mamba_norm_w):
    bsz, seq, d = x.shape
    depth = w_in.shape[0]
    w = d // 2
    lanes = V7X_LANES
    t = bsz * seq
    mb_heads = w // MB_HEADDIM
    mb_bc = 2 * MB_GROUPS * MB_STATE
    lw = RW_DECAY_LORA + RW_A_LORA + RW_G_LORA
    assert mb_bc == w and 3 * w % lw == 0

    o_gate = 0
    o_hg = o_gate + N_BRANCH * d
    o_rw = o_hg + 4 * w
    o_rwl = o_rw + 3 * w
    o_s5 = o_rwl + lw
    o_mbg = o_s5 + w
    o_mbx = o_mbg + w
    o_mbbc = o_mbx + w
    o_mbdt = o_mbbc + mb_bc
    c_rw = 0
    c_hg = c_rw + 3 * w
    c_s5 = c_hg + 4 * w
    c_mbg = c_s5 + w
    c_mbx = c_mbg + w
    c_mbbc = c_mbx + w
    c_rwl = c_mbbc + mb_bc
    c_mbdt = c_rwl + lw
    n_cols = c_mbdt + lanes
    tn = 512
    n_pad = -(-n_cols // tn) * tn

    def mixer_cols(wl):
        sl = lambda a, b: wl[:, a:b]
        pieces = [sl(o_rw, o_rwl), sl(o_hg, o_rw), sl(o_s5, o_mbdt), sl(o_rwl, o_s5),
                  sl(o_mbdt, o_mbdt + mb_heads),
                  jnp.zeros((d, n_pad - c_mbdt - mb_heads), wl.dtype)]
        return jnp.concatenate(pieces, axis=1).astype(BF16)

    tm = _pick_tile(seq, 256)
    tm_in = _pick_tile(seq, 512)
    tb_hg = _pick_tile(seq, 256)
    c_hg_chunk = min(HG_CHUNK, tb_hg)
    tc_rw = _pick_tile(seq, 32)
    tc_s5 = _pick_tile(seq, 64)
    q_mb = min(MB_CHUNK, seq)
    ff = w_ffn_out.shape[1]
    tn_ff = 256 if ff % 256 == 0 else lanes

    lower_bounds = _hg_bounds(hgrn_lower_bounds)
    ones_rw = _block_ones(w, RW_HEAD)
    eye_g = jnp.eye(lanes // S5_GROUP, dtype=F32)
    expand = (jnp.arange(lanes)[:, None] == (jnp.arange(w) // MB_HEADDIM)[None, :]).astype(BF16)
    tril = (jnp.arange(q_mb)[:, None] >= jnp.arange(q_mb)[None, :]).astype(BF16)

    x2d = x.reshape(t, d)
    v_first = jnp.zeros((t, w), F32)
    for l in range(depth):
        gates = _inproj(x2d, norm_mix_w[l], w_in[l][:, o_gate:o_hg].astype(BF16), tm_in, tn, True)
        z2d = _inproj(x2d, norm_mix_w[l], mixer_cols(w_in[l]), tm_in, tn, False)
        z3d = z2d.reshape(bsz, seq, n_pad)

        y_hg = _hgrn2(z3d, lower_bounds[l], hgrn_norm_w[l], w, c_hg, tb_hg, c_hg_chunk,
                      HG_SUBBLOCK).reshape(t, w)

        mu = rwkv_mu[l]
        has_vres = l > 0
        lv = max(l - 1, 0)
        rp = {
            "mu_main": mu[None, :3 * w], "mu_lora": mu[None, 3 * w:],
            "w0": rwkv_w0[l][None], "a0": rwkv_a0[l][None],
            "w2p": _pad_rows(rwkv_w2[l], lw, 0).astype(BF16),
            "a2p": _pad_rows(rwkv_a2[l], lw, RW_DECAY_LORA).astype(BF16),
            "g2p": _pad_rows(rwkv_g2[l], lw, RW_DECAY_LORA + RW_A_LORA).astype(BF16),
            "k_k": rwkv_k_k[l][None], "k_a": rwkv_k_a[l][None], "r_k": rwkv_r_k[l].reshape(1, w),
            "v0": rwkv_v0[lv][None],
            "v1p": jnp.pad(rwkv_v1[lv], ((0, 0), (0, lanes - rwkv_v1.shape[2]))).astype(BF16),
            "v2p": _pad_rows(rwkv_v2[lv], lanes, 0).astype(BF16),
            "ones": ones_rw, "ln_w": rwkv_ln_w[l], "ln_b": rwkv_ln_b[l],
        }
        y_rw, v_l = _rwkv7(z2d, v_first, rp, bsz, seq, w, c_rw, c_rwl, lw, tm, tc_rw, has_vres)
        if l == 0:
            v_first = v_l

        lam_re, lam_im, bb_re, bb_im = _s5_discretize(s5_a_re[l], s5_a_im[l], s5_log_dt[l],
                                                      s5_b_re[l], s5_b_im[l])
        ns = lam_re.size
        nsg = w // lanes
        gps = lanes // S5_GROUP
        blk_in = lambda bb: jnp.einsum(
            "sgnc,gh->sgchn", bb.reshape(nsg, gps, S5_STATE, S5_GROUP), eye_g
        ).reshape(nsg, lanes, gps * S5_STATE)
        blk_out = lambda cc: jnp.einsum(
            "sgcn,gh->sgnhc", cc.reshape(nsg, gps, S5_GROUP, S5_STATE), eye_g
        ).reshape(nsg, gps * S5_STATE, lanes)
        sp = {
            "bblk": jnp.concatenate([blk_in(bb_re), blk_in(bb_im)], axis=2).astype(BF16),
            "cblk": jnp.concatenate([blk_out(s5_c_re[l]), -blk_out(s5_c_im[l])], axis=1).astype(BF16),
            "lam_re": lam_re.reshape(1, ns), "lam_im": lam_im.reshape(1, ns),
            "d": s5_d[l][None], "w_glu": s5_w_glu[l].astype(BF16), "b_glu": s5_b_glu[l][None],
        }
        u_tm = z3d[:, :, c_s5:c_s5 + w].transpose(1, 0, 2).reshape(t, w)
        y_s5 = _s5(u_tm, sp, bsz, tc_s5).reshape(seq, bsz, w).transpose(1, 0, 2).reshape(t, w)

        mp = {
            "conv_w": mamba_conv_w[l], "conv_b": mamba_conv_b[l][None],
            "dt_bias": jnp.pad(mamba_dt_bias[l], (0, lanes - mb_heads))[None],
            "a_log": jnp.pad(mamba_a_log[l], (0, lanes - mb_heads))[None],
            "d": jnp.repeat(mamba_d[l], MB_HEADDIM)[None], "norm_w": mamba_norm_w[l][None],
            "expand": expand, "tril": tril,
        }
        y_mb = _mamba2(z3d, mp, w, (c_mbg, c_mbx, c_mbbc, c_mbdt), q_mb).reshape(t, w)

        x2d = _merge((y_hg, y_rw, y_s5, y_mb), gates, w_branch[l].astype(BF16), w_out[l].astype(BF16),
                     x2d, tm)
        act = _ffn_in(x2d, norm_ffn_w[l], w_ffn_in[l].astype(BF16), tm_in, tn_ff)
        x2d = _ffn_out(act, w_ffn_out[l].astype(BF16), x2d, norm_final_w, l == depth - 1, tm)
    return x2d.reshape(bsz, seq, d)
```

```python
import functools
import math

import jax
import jax.numpy as jnp
from jax import lax
from jax.experimental import pallas as pl
from jax.experimental.pallas import tpu as pltpu

F32 = jnp.float32
BF16 = jnp.bfloat16

V7X_LANES = 128
V7X_SUBLANES = 8
V7X_VMEM_LIMIT_BYTES = 56 * 1024 * 1024

RMS_EPS = 1e-6
N_BRANCH = 4
HG_DK = 128
HG_TINY = 1e-30
HG_CHUNK = 64
HG_SUBBLOCK = 8
RW_HEAD = 64
RW_DECAY_LORA = 64
RW_A_LORA = 64
RW_G_LORA = 128
RW_LN_EPS = 64e-5
RW_NVEC = 6
S5_GROUP = 16
S5_STATE = 64
MB_HEADDIM = 64
MB_GROUPS = 2
MB_STATE = 128
MB_CONV = 4
MB_CHUNK = 128


def _cparams(*sem):
    return pltpu.CompilerParams(dimension_semantics=sem, vmem_limit_bytes=V7X_VMEM_LIMIT_BYTES)


def _bdot(a, b):
    return jnp.dot(a.astype(BF16), b.astype(BF16), preferred_element_type=F32)


def _split3(x):
    x1 = x.astype(BF16)
    r1 = x - x1.astype(F32)
    x2 = r1.astype(BF16)
    x3 = (r1 - x2.astype(F32)).astype(BF16)
    return x1, x2, x3


def _dot_exact_rhs(x, m):
    x1, x2, x3 = _split3(x)
    d = functools.partial(jnp.dot, preferred_element_type=F32)
    return d(x1, m) + d(x2, m) + d(x3, m)


def _dot_exact_lhs(m, x):
    x1, x2, x3 = _split3(x)
    d = functools.partial(jnp.dot, preferred_element_type=F32)
    return d(m, x1) + d(m, x2) + d(m, x3)


def _softplus(x):
    return jnp.maximum(x, 0.0) + jnp.log(1.0 + jnp.exp(-jnp.abs(x)))


def _silu(x):
    return x * jax.nn.sigmoid(x)


def _rms(x, w):
    return x * lax.rsqrt(jnp.mean(x * x, axis=-1, keepdims=True) + RMS_EPS) * w


def _resident(shape):
    return pl.BlockSpec(shape, lambda i: (0,) * len(shape), pipeline_mode=pl.Buffered(1))


def _inproj_kernel(x_ref, nw_ref, w_ref, o_ref, *, gate, tn):
    u = _rms(x_ref[...], nw_ref[...]).astype(BF16)
    for n0 in range(0, o_ref.shape[1], tn):
        z = jnp.dot(u, w_ref[:, n0:n0 + tn], preferred_element_type=F32)
        o_ref[:, n0:n0 + tn] = jax.nn.sigmoid(z).astype(o_ref.dtype) if gate else z


def _inproj(x2d, norm_w, w_bf16, tm, tn, gate):
    t, d = x2d.shape
    n = w_bf16.shape[1]
    return pl.pallas_call(
        functools.partial(_inproj_kernel, gate=gate, tn=tn),
        grid=(t // tm,),
        in_specs=[pl.BlockSpec((tm, d), lambda i: (i, 0)),
                  pl.BlockSpec((1, d), lambda i: (0, 0)),
                  _resident((d, n))],
        out_specs=pl.BlockSpec((tm, n), lambda i: (i, 0)),
        out_shape=jax.ShapeDtypeStruct((t, n), BF16 if gate else F32),
        compiler_params=_cparams("parallel"),
        name="inproj_gate" if gate else "inproj",
    )(x2d, norm_w.reshape(1, d), w_bf16)


def _ffn_in_kernel(x_ref, nw_ref, w_ref, o_ref, *, tn):
    ff = o_ref.shape[1]
    u = _rms(x_ref[...], nw_ref[...]).astype(BF16)
    for n0 in range(0, ff, tn):
        gate = jnp.dot(u, w_ref[:, n0:n0 + tn], preferred_element_type=F32)
        up = jnp.dot(u, w_ref[:, ff + n0:ff + n0 + tn], preferred_element_type=F32)
        o_ref[:, n0:n0 + tn] = (_silu(gate) * up).astype(BF16)


def _ffn_in(x2d, norm_w, w_bf16, tm, tn):
    t, d = x2d.shape
    ff = w_bf16.shape[1] // 2
    return pl.pallas_call(
        functools.partial(_ffn_in_kernel, tn=tn),
        grid=(t // tm,),
        in_specs=[pl.BlockSpec((tm, d), lambda i: (i, 0)),
                  pl.BlockSpec((1, d), lambda i: (0, 0)),
                  _resident((d, 2 * ff))],
        out_specs=pl.BlockSpec((tm, ff), lambda i: (i, 0)),
        out_shape=jax.ShapeDtypeStruct((t, ff), BF16),
        compiler_params=_cparams("parallel"),
        name="ffn_in",
    )(x2d, norm_w.reshape(1, d), w_bf16)


def _ffn_out_kernel(a_ref, w_ref, x_ref, fw_ref, o_ref, *, final_norm):
    y = x_ref[...] + jnp.dot(a_ref[...], w_ref[...], preferred_element_type=F32)
    if final_norm:
        y = _rms(y, fw_ref[...])
    o_ref[...] = y


def _ffn_out(act, w_bf16, x2d, final_w, final_norm, tm):
    t, d = x2d.shape
    ff = act.shape[1]
    return pl.pallas_call(
        functools.partial(_ffn_out_kernel, final_norm=final_norm),
        grid=(t // tm,),
        in_specs=[pl.BlockSpec((tm, ff), lambda i: (i, 0)),
                  _resident((ff, d)),
                  pl.BlockSpec((tm, d), lambda i: (i, 0)),
                  pl.BlockSpec((1, d), lambda i: (0, 0))],
        out_specs=pl.BlockSpec((tm, d), lambda i: (i, 0)),
        out_shape=jax.ShapeDtypeStruct((t, d), F32),
        compiler_params=_cparams("parallel"),
        name="ffn_out",
    )(act, w_bf16, x2d, final_w.reshape(1, d))


def _merge_kernel(yh_ref, yr_ref, ys_ref, ym_ref, zg_ref, wb_ref, wo_ref, x_ref, o_ref):
    d = x_ref.shape[1]
    acc = jnp.zeros(x_ref.shape, F32)
    for k, y_ref in enumerate((yh_ref, yr_ref, ys_ref, ym_ref)):
        proj = jnp.dot(y_ref[...].astype(BF16), wb_ref[k], preferred_element_type=F32)
        acc = acc + zg_ref[:, k * d:(k + 1) * d].astype(F32) * proj
    o_ref[...] = x_ref[...] + jnp.dot(acc.astype(BF16), wo_ref[...], preferred_element_type=F32)


def _merge(ys, z2d, wb_bf16, wo_bf16, x2d, tm):
    t, d = x2d.shape
    w = ys[0].shape[1]
    yspec = pl.BlockSpec((tm, w), lambda i: (i, 0))
    return pl.pallas_call(
        _merge_kernel,
        grid=(t // tm,),
        in_specs=[yspec, yspec, yspec, yspec,
                  pl.BlockSpec((tm, N_BRANCH * d), lambda i: (i, 0)),
                  _resident((N_BRANCH, w, d)),
                  _resident((d, d)),
                  pl.BlockSpec((tm, d), lambda i: (i, 0))],
        out_specs=pl.BlockSpec((tm, d), lambda i: (i, 0)),
        out_shape=jax.ShapeDtypeStruct((t, d), F32),
        compiler_params=_cparams("parallel"),
        name="merge",
    )(*ys, z2d, wb_bf16, wo_bf16, x2d)


def _hg_bounds_kernel(h_ref, o_ref):
    h = h_ref[...]
    depth = h.shape[0]
    m = jnp.max(h, axis=0, keepdims=True)
    e = jnp.exp(h - m)
    p = e / jnp.sum(e, axis=0, keepdims=True)
    run = jnp.zeros_like(p[0:1])
    rows = []
    for l in range(depth):
        run = run + p[l:l + 1]
        rows.append(run - p[0:1])
    o_ref[...] = jnp.concatenate(rows, axis=0)


def _hg_bounds(hgrn_lower_bounds):
    return pl.pallas_call(
        _hg_bounds_kernel,
        out_shape=jax.ShapeDtypeStruct(hgrn_lower_bounds.shape, F32),
        name="hg_bounds",
    )(hgrn_lower_bounds)


def _hg_chunk_kernel(q_ref, f_ref, i_ref, g_ref, lb_ref, nw_ref, tril_ref, y_ref,
                     st_ref, kbuf, bbuf, vbuf, *, tb, c, cs, heads, dk):
    sub = V7X_SUBLANES
    nb = c // cs
    nt = (((1,), (1,)), ((), ()))

    @pl.when(pl.program_id(1) == 0)
    def _():
        st_ref[...] = jnp.zeros(st_ref.shape, F32)

    kbuf[0:sub, :] = jnp.zeros((sub, kbuf.shape[1]), F32)
    bbuf[0:sub, :] = jnp.zeros((sub, bbuf.shape[1]), F32)
    vbuf[0:sub, :] = jnp.zeros((sub, vbuf.shape[1]), F32)
    lb = lb_ref[...]
    tril = tril_ref[...]
    rid = lax.broadcasted_iota(jnp.int32, (c, dk), 0)
    rid1 = lax.broadcasted_iota(jnp.int32, (c, 1), 0)

    def chunk(ci, carry):
        rows = pl.ds(pl.multiple_of(ci * c, c), c)
        ff = f_ref[0, rows, :]
        q = _silu(q_ref[0, rows, :])
        dec = jnp.maximum(lb + (1.0 - lb) * jax.nn.sigmoid(ff), HG_TINY)
        k = (1.0 - lb) * jax.nn.sigmoid(-ff)
        v = i_ref[0, rows, :]
        b = _dot_exact_lhs(tril, jnp.log(dec))
        kbuf[sub:sub + c, :] = k
        bbuf[sub:sub + c, :] = b
        vbuf[sub:sub + c, :] = v
        outs = []
        for h in range(heads):
            hs = slice(h * dk, (h + 1) * dk)
            qh, kh, bh, vh = q[:, hs], k[:, hs], b[:, hs], v[:, hs]
            blast = bh[c - 1:c, :]
            st = st_ref[h]
            o = lax.dot_general((qh * jnp.exp(bh)).astype(BF16), st.astype(BF16), nt,
                                preferred_element_type=F32)
            att_rows = [jnp.zeros((cs, c), F32)]
            for i in range(1, nb):
                beta = bh[i * cs - 1:i * cs, :]
                kt = jnp.where(rid < i * cs, kh * jnp.exp(jnp.minimum(beta - bh, 0.0)), 0.0)
                qt = qh[i * cs:(i + 1) * cs, :] * jnp.exp(bh[i * cs:(i + 1) * cs, :] - beta)
                att_rows.append(lax.dot_general(qt.astype(BF16), kt.astype(BF16), nt,
                                                preferred_element_type=F32))
            o = o + _bdot(jnp.concatenate(att_rows, axis=0), vh)
            for dlt in range(cs):
                if dlt == 0:
                    a = jnp.sum(qh * kh, axis=-1, keepdims=True)
                    o = o + a * vh
                else:
                    win = pl.ds(sub - dlt, c)
                    e = jnp.exp(jnp.minimum(bh - bbuf[win, hs], 0.0))
                    a = jnp.sum(qh * kbuf[win, hs] * e, axis=-1, keepdims=True)
                    a = jnp.where((rid1 % cs) >= dlt, a, 0.0)
                    o = o + a * vbuf[win, hs]
            kd = kh * jnp.exp(blast - bh)
            st_ref[h] = st * jnp.exp(blast) + _bdot(vh.T, kd)
            outs.append(o * lax.rsqrt(jnp.mean(o * o, axis=-1, keepdims=True) + RMS_EPS))
        y = jnp.concatenate(outs, axis=-1) * nw_ref[...]
        y_ref[0, rows, :] = y * _silu(g_ref[0, rows, :])
        return carry

    lax.fori_loop(0, tb // c, chunk, 0)


def _hgrn2(z3d, lb, norm_w, w, col0, tb, c, cs):
    bsz, seq, _ = z3d.shape
    heads = w // HG_DK
    cb = col0 // w
    tril = (jnp.arange(c)[:, None] >= jnp.arange(c)[None, :]).astype(BF16)
    zspec = lambda j: pl.BlockSpec((1, tb, w), lambda b, i: (b, i, cb + j))
    row = pl.BlockSpec((1, w), lambda b, i: (0, 0))
    return pl.pallas_call(
        functools.partial(_hg_chunk_kernel, tb=tb, c=c, cs=cs, heads=heads, dk=HG_DK),
        grid=(bsz, seq // tb),
        in_specs=[zspec(0), zspec(1), zspec(2), zspec(3), row, row,
                  pl.BlockSpec((c, c), lambda b, i: (0, 0))],
        out_specs=pl.BlockSpec((1, tb, w), lambda b, i: (b, i, 0)),
        out_shape=jax.ShapeDtypeStruct((bsz, seq, w), F32),
        scratch_shapes=[pltpu.VMEM((heads, HG_DK, HG_DK), F32)]
        + [pltpu.VMEM((c + V7X_SUBLANES, w), F32)] * 3,
        compiler_params=_cparams("parallel", "arbitrary"),
        name="hgrn2",
    )(z3d, z3d, z3d, z3d, lb.reshape(1, w), norm_w.reshape(1, w), tril)


def _lanes_to_chains_v(a, bsz, seq, heads, dv, nq):
    a = a.reshape(seq, dv // nq, nq, bsz, heads).transpose(3, 0, 4, 2, 1)
    return a.reshape(bsz * seq, heads * dv)


def _shift_rows(cur, prev_last, first):
    rolled = pltpu.roll(cur, 1, axis=0)
    row0 = jnp.where(first, 0.0, prev_last)
    rid = lax.broadcasted_iota(jnp.int32, cur.shape, 0)
    return jnp.where(rid == 0, jnp.broadcast_to(row0, cur.shape), rolled)


def _rw_prep_kernel(zm_ref, zmp_ref, zl_ref, zlp_ref, vf_ref,
                    mum_ref, mul_ref, w0_ref, w2_ref, a0_ref, a2_ref, g2_ref,
                    kk_ref, ka_ref, rk_ref, v0_ref, v1_ref, v2_ref, ones_ref,
                    kvec_out, v_out, g_out, bonus_out,
                    *, w, tiles_per_seq, has_vres):
    first = (pl.program_id(0) % tiles_per_seq) == 0
    sub = V7X_SUBLANES
    zm = zm_ref[...]
    zl = zl_ref[...]
    zms = zm + (_shift_rows(zm, zmp_ref[sub - 1:sub, :], first) - zm) * mum_ref[...]
    zls = zl + (_shift_rows(zl, zlp_ref[sub - 1:sub, :], first) - zl) * mul_ref[...]
    r = zms[:, 0:w]
    k = zms[:, w:2 * w]
    v = zms[:, 2 * w:3 * w]
    w_log = -_softplus(-(w0_ref[...] + _bdot(jnp.tanh(zls), w2_ref[...]))) - 0.5
    decay = jnp.exp(-jnp.exp(w_log))
    if has_vres:
        mix = jax.nn.sigmoid(v0_ref[...] + _bdot(_bdot(v, v1_ref[...]), v2_ref[...]))
        v = v + (vf_ref[...] - v) * mix
    a = jax.nn.sigmoid(a0_ref[...] + _bdot(zls, a2_ref[...]))
    g = _bdot(jax.nn.sigmoid(zls), g2_ref[...])
    ones = ones_ref[...]
    kk = k * kk_ref[...]
    ss = _dot_exact_rhs(kk * kk, ones)
    kk = kk / jnp.maximum(jnp.sqrt(ss), 1e-12)
    k2 = k * (1.0 + (a - 1.0) * ka_ref[...])
    vecs = (r, decay, k2, -kk, kk * a, v)
    for h in range(w // RW_HEAD):
        hs = slice(h * RW_HEAD, (h + 1) * RW_HEAD)
        kvec_out[0, h] = jnp.concatenate([x[:, hs] for x in vecs], axis=1)
    v_out[...] = v
    g_out[...] = g
    bonus_out[...] = _dot_exact_rhs(r * k2 * rk_ref[...], ones) * v


def _rw_prep(z2d, v_first, p, col_main, col_lora, w, lw, seq, tm, has_vres):
    t = z2d.shape[0]
    sub = V7X_SUBLANES
    mb = col_main // (3 * w)
    lbk = col_lora // lw
    rows8 = tm // sub
    heads = w // RW_HEAD
    tps = seq // tm

    def prev_idx(i):
        return jnp.maximum(i * rows8 - 1, 0)

    row = lambda n: pl.BlockSpec((1, n), lambda i: (0, 0))
    full = lambda a: pl.BlockSpec(a.shape, lambda i: (0,) * a.ndim)
    ospec = pl.BlockSpec((tm, w), lambda i: (i, 0))
    oshape = jax.ShapeDtypeStruct((t, w), F32)
    args = [z2d, z2d, z2d, z2d, v_first,
            p["mu_main"], p["mu_lora"], p["w0"], p["w2p"], p["a0"], p["a2p"], p["g2p"],
            p["k_k"], p["k_a"], p["r_k"], p["v0"], p["v1p"], p["v2p"], p["ones"]]
    in_specs = [pl.BlockSpec((tm, 3 * w), lambda i: (i, mb)),
                pl.BlockSpec((sub, 3 * w), lambda i: (prev_idx(i), mb)),
                pl.BlockSpec((tm, lw), lambda i: (i, lbk)),
                pl.BlockSpec((sub, lw), lambda i: (prev_idx(i), lbk)),
                ospec,
                row(3 * w), row(lw), row(w), full(p["w2p"]), row(w), full(p["a2p"]), full(p["g2p"]),
                row(w), row(w), row(w), row(w), full(p["v1p"]), full(p["v2p"]), full(p["ones"])]
    return pl.pallas_call(
        functools.partial(_rw_prep_kernel, w=w, tiles_per_seq=seq // tm, has_vres=has_vres),
        grid=(t // tm,),
        in_specs=in_specs,
        out_specs=[pl.BlockSpec((1, heads, tm, RW_NVEC * RW_HEAD),
                                lambda i: (i // tps, 0, i % tps, 0)), ospec, ospec, ospec],
        out_shape=[jax.ShapeDtypeStruct((t // seq, heads, seq, RW_NVEC * RW_HEAD), F32),
                   oshape, oshape, oshape],
        compiler_params=_cparams("parallel"),
        name="rw_prep",
    )(*args)


def _rw_scan_kernel(x_ref, y_ref, s_ref, *, tc, dk, nvb, nq):
    sub = V7X_SUBLANES
    lanes = V7X_LANES
    jr, jw, jk, ja, jb, jv = range(RW_NVEC)
    vl = nvb * sub
    qid = lax.broadcasted_iota(jnp.int32, (sub, lanes), 1) // (lanes // nq)

    @pl.when(pl.program_id(0) == 0)
    def _():
        s_ref[...] = jnp.zeros(s_ref.shape, F32)

    def bc(j, t, kk):
        return jnp.broadcast_to(x_ref[t, j, pl.ds(kk, 1), :], (sub, lanes))

    sa0 = [jnp.zeros((sub, lanes), F32) for _ in range(nvb)]
    for kk in range(dk):
        arow = bc(ja, 0, kk)
        for j in range(nvb):
            sa0[j] = sa0[j] + s_ref[kk, pl.ds(sub * j, sub), :] * arow

    def step(t, sa):
        tn = jnp.minimum(t + 1, tc - 1)
        vb = []
        for j in range(nvb):
            vj = x_ref[t, jv, pl.ds(sub * j, sub), :]
            for q in range(1, nq):
                vj = jnp.where(qid == q, x_ref[t, jv, pl.ds(q * vl + sub * j, sub), :], vj)
            vb.append(vj)
        yacc = [jnp.zeros((sub, lanes), F32) for _ in range(nvb)]
        sacc = [jnp.zeros((sub, lanes), F32) for _ in range(nvb)]
        for kk in range(dk):
            wrow = bc(jw, t, kk)
            brow = bc(jb, t, kk)
            krow = bc(jk, t, kk)
            rrow = bc(jr, t, kk)
            anext = bc(ja, tn, kk)
            for j in range(nvb):
                s = s_ref[kk, pl.ds(sub * j, sub), :] * wrow + sa[j] * brow + vb[j] * krow
                s_ref[kk, pl.ds(sub * j, sub), :] = s
                yacc[j] = yacc[j] + s * rrow
                sacc[j] = sacc[j] + s * anext
        for j in range(nvb):
            y_ref[t, pl.ds(sub * j, sub), :] = yacc[j]
        return tuple(sacc)

    lax.fori_loop(0, tc, step, tuple(sa0))


def _rw_scan(x_l, nq, tc):
    s, nvec, dk, lanes = x_l.shape
    vl = dk // nq
    kspec = pl.BlockSpec((tc, nvec, dk, lanes), lambda i: (i, 0, 0, 0))
    vspec = pl.BlockSpec((tc, vl, lanes), lambda i: (i, 0, 0))
    return pl.pallas_call(
        functools.partial(_rw_scan_kernel, tc=tc, dk=dk, nvb=vl // V7X_SUBLANES, nq=nq),
        grid=(s // tc,),
        in_specs=[kspec],
        out_specs=vspec,
        out_shape=jax.ShapeDtypeStruct((s, vl, lanes), F32),
        scratch_shapes=[pltpu.VMEM((dk, vl, lanes), F32)],
        compiler_params=_cparams("arbitrary"),
        name="rw_scan",
    )(x_l)


def _rw_post_kernel(y_ref, g_ref, bonus_ref, lnw_ref, lnb_ref, ones_ref, o_ref):
    y = y_ref[...]
    ones = ones_ref[...]
    inv_n = 1.0 / RW_HEAD
    mean = _dot_exact_rhs(y, ones) * inv_n
    yc = y - mean
    var = _dot_exact_rhs(yc * yc, ones) * inv_n
    y = yc * lax.rsqrt(var + RW_LN_EPS) * lnw_ref[...] + lnb_ref[...]
    o_ref[...] = (y + bonus_ref[...]) * g_ref[...]


def _rw_post(y2d, g, bonus, ln_w, ln_b, ones, tm):
    t, w = y2d.shape
    tspec = pl.BlockSpec((tm, w), lambda i: (i, 0))
    row = pl.BlockSpec((1, w), lambda i: (0, 0))
    return pl.pallas_call(
        _rw_post_kernel,
        grid=(t // tm,),
        in_specs=[tspec, tspec, tspec, row, row, pl.BlockSpec((w, w), lambda i: (0, 0))],
        out_specs=tspec,
        out_shape=jax.ShapeDtypeStruct((t, w), F32),
        compiler_params=_cparams("parallel"),
        name="rw_post",
    )(y2d, g, bonus, ln_w.reshape(1, w), ln_b.reshape(1, w), ones)


def _rwkv7(z2d, v_first, p, bsz, seq, w, col_main, col_lora, lw, tm, tc, has_vres):
    heads = w // RW_HEAD
    nq = V7X_LANES // (bsz * heads)
    kvec, v, g, bonus = _rw_prep(z2d, v_first, p, col_main, col_lora, w, lw, seq, tm, has_vres)
    chains = bsz * heads
    x_l = jnp.broadcast_to(kvec.reshape(1, chains, seq, RW_NVEC * RW_HEAD),
                           (nq, chains, seq, RW_NVEC * RW_HEAD))
    x_l = x_l.reshape(nq * chains, seq, RW_NVEC * RW_HEAD).transpose(1, 2, 0)
    y_l = _rw_scan(x_l.reshape(seq, RW_NVEC, RW_HEAD, nq * chains), nq, tc)
    y = _lanes_to_chains_v(y_l, bsz, seq, heads, RW_HEAD, nq)
    return _rw_post(y, g, bonus, p["ln_w"], p["ln_b"], p["ones"], tm), v


def _s5_disc_kernel(are_ref, aim_ref, dt_ref, bre_ref, bim_ref, lre_ref, lim_ref, bbre_ref, bbim_ref):
    a_re = are_ref[...]
    a_im = aim_ref[...]
    dt = jnp.exp(dt_ref[...])
    mag = jnp.exp(dt * a_re)
    lam_re = mag * jnp.cos(dt * a_im)
    lam_im = mag * jnp.sin(dt * a_im)
    den = a_re * a_re + a_im * a_im
    coef_re = ((lam_re - 1.0) * a_re + lam_im * a_im) / den
    coef_im = (lam_im * a_re - (lam_re - 1.0) * a_im) / den
    b_re = bre_ref[...]
    b_im = bim_ref[...]
    lre_ref[...] = lam_re
    lim_ref[...] = lam_im
    bbre_ref[...] = coef_re * b_re - coef_im * b_im
    bbim_ref[...] = coef_re * b_im + coef_im * b_re


def _s5_discretize(a_re, a_im, log_dt, b_re, b_im):
    g, n, c = b_re.shape
    shp = (g, n * c)
    bc = lambda a: jnp.broadcast_to(a[..., None], (g, n, c)).reshape(shp)
    dtb = jnp.broadcast_to(log_dt[:, None], shp)
    o = jax.ShapeDtypeStruct(shp, F32)
    lre, lim, bbre, bbim = pl.pallas_call(
        _s5_disc_kernel, out_shape=[o, o, o, o], name="s5_disc",
    )(bc(a_re), bc(a_im), dtb, b_re.reshape(shp), b_im.reshape(shp))
    un = lambda a: a.reshape(g, n, c)
    return un(lre)[..., 0], un(lim)[..., 0], un(bbre), un(bbim)


def _s5_kernel(u_ref, bblk_ref, cblk_ref, lre_ref, lim_ref, d_ref, wg_ref, bg_ref, y_ref,
               h_ref, hr_ref, hi_ref, *, tc, bsz, ns, lane_chunk):
    @pl.when(pl.program_id(0) == 0)
    def _():
        hr_ref[...] = jnp.zeros(hr_ref.shape, F32)
        hi_ref[...] = jnp.zeros(hi_ref.shape, F32)

    u = u_ref[...]
    nsg = bblk_ref.shape[0]
    lanes = V7X_LANES
    for sg in range(nsg):
        c0 = sg * lane_chunk
        drive = _bdot(u[:, sg * lanes:(sg + 1) * lanes], bblk_ref[sg])
        h_ref[:, c0:c0 + lane_chunk] = drive[:, 0:lane_chunk]
        h_ref[:, ns + c0:ns + c0 + lane_chunk] = drive[:, lane_chunk:2 * lane_chunk]
    for c0 in range(0, ns, lane_chunk):
        lr = jnp.broadcast_to(lre_ref[:, c0:c0 + lane_chunk], (bsz, lane_chunk))
        li = jnp.broadcast_to(lim_ref[:, c0:c0 + lane_chunk], (bsz, lane_chunk))

        def step(t, carry, c0=c0, lr=lr, li=li):
            hr, hi = carry
            rows = pl.ds(pl.multiple_of(t * bsz, bsz), bsz)
            nr = lr * hr - li * hi + h_ref[rows, c0:c0 + lane_chunk]
            ni = lr * hi + li * hr + h_ref[rows, ns + c0:ns + c0 + lane_chunk]
            h_ref[rows, c0:c0 + lane_chunk] = nr
            h_ref[rows, ns + c0:ns + c0 + lane_chunk] = ni
            return nr, ni

        hr, hi = lax.fori_loop(0, tc, step,
                               (hr_ref[:, c0:c0 + lane_chunk], hi_ref[:, c0:c0 + lane_chunk]))
        hr_ref[:, c0:c0 + lane_chunk] = hr
        hi_ref[:, c0:c0 + lane_chunk] = hi
    outs = []
    for sg in range(nsg):
        c0 = sg * lane_chunk
        outs.append(_bdot(h_ref[:, c0:c0 + lane_chunk], cblk_ref[sg, 0:lane_chunk, :])
                    + _bdot(h_ref[:, ns + c0:ns + c0 + lane_chunk], cblk_ref[sg, lane_chunk:, :]))
    y = jnp.concatenate(outs, axis=-1) + d_ref[...] * u
    y = jax.nn.gelu(y)
    y_ref[...] = y * jax.nn.sigmoid(_bdot(y, wg_ref[...]) + bg_ref[...])


def _s5(u_tm, p, bsz, tc):
    rows, w = u_tm.shape
    ns = p["lam_re"].shape[1]
    blk = tc * bsz
    full = lambda a: pl.BlockSpec(a.shape, lambda i: (0,) * a.ndim)
    return pl.pallas_call(
        functools.partial(_s5_kernel, tc=tc, bsz=bsz, ns=ns, lane_chunk=ns // p["bblk"].shape[0]),
        grid=(rows // blk,),
        in_specs=[pl.BlockSpec((blk, w), lambda i: (i, 0)),
                  full(p["bblk"]), full(p["cblk"]), full(p["lam_re"]), full(p["lam_im"]),
                  full(p["d"]), full(p["w_glu"]), full(p["b_glu"])],
        out_specs=pl.BlockSpec((blk, w), lambda i: (i, 0)),
        out_shape=jax.ShapeDtypeStruct((rows, w), F32),
        scratch_shapes=[pltpu.VMEM((blk, 2 * ns), F32),
                        pltpu.VMEM((bsz, ns), F32), pltpu.VMEM((bsz, ns), F32)],
        compiler_params=_cparams("arbitrary"),
        name="s5",
    )(u_tm, p["bblk"], p["cblk"], p["lam_re"], p["lam_im"], p["d"], p["w_glu"], p["b_glu"])


def _mb_kernel(gate_ref, x_ref, bc_ref, dt_ref, cw_ref, cb_ref, dtb_ref, alog_ref, dsk_ref, nw_ref,
               expand_ref, tril_ref, y_ref, prev_ref, st_ref, *, q, w, heads, groups, nstate):
    sub = V7X_SUBLANES
    lanes = V7X_LANES
    hd = w // heads
    gw = w // groups
    hpg = heads // groups

    @pl.when(pl.program_id(1) == 0)
    def _():
        prev_ref[...] = jnp.zeros(prev_ref.shape, F32)
        st_ref[...] = jnp.zeros(st_ref.shape, F32)

    xbc = jnp.concatenate([x_ref[0], bc_ref[0]], axis=-1)
    full = jnp.concatenate([prev_ref[...], xbc], axis=0)
    conv = jnp.broadcast_to(cb_ref[...], xbc.shape)
    for j in range(MB_CONV):
        shift = MB_CONV - 1 - j
        src = full if shift == 0 else pltpu.roll(full, shift, axis=0)
        conv = conv + src[sub:sub + q, :] * cw_ref[j:j + 1, :]
    prev_ref[...] = xbc[q - sub:q, :]
    act = _silu(conv)
    xs = act[:, 0:w]
    bmat = act[:, w:w + groups * nstate]
    cmat = act[:, w + groups * nstate:w + 2 * groups * nstate]

    dt = _softplus(dt_ref[0] + dtb_ref[...])
    a = -jnp.exp(alog_ref[...]) * dt
    a_cum = _dot_exact_lhs(tril_ref[...], a)
    expand = expand_ref[...]
    dt_e = _dot_exact_rhs(dt, expand)
    acum_e = _dot_exact_rhs(a_cum, expand)
    alast_e = acum_e[q - 1:q, :]
    xdt = xs * dt_e
    xdec = xdt * jnp.exp(alast_e - acum_e)

    rid = lax.broadcasted_iota(jnp.int32, (q, q), 0)
    cid = lax.broadcasted_iota(jnp.int32, (q, q), 1)
    causal = rid >= cid
    a_cum_t = a_cum.T
    lane_w = lax.broadcasted_iota(jnp.int32, (q, gw), 1)

    y_parts = []
    for g in range(groups):
        bg = bmat[:, g * nstate:(g + 1) * nstate]
        cg = cmat[:, g * nstate:(g + 1) * nstate]
        scores = lax.dot_general(cg.astype(BF16), bg.astype(BF16), (((1,), (1,)), ((), ())),
                                 preferred_element_type=F32)
        xg = xdt[:, g * gw:(g + 1) * gw]
        yg = _bdot(cg, st_ref[g]) * jnp.exp(acum_e[:, g * gw:(g + 1) * gw])
        for hh in range(hpg):
            h = g * hpg + hh
            col = jnp.broadcast_to(a_cum[:, h:h + 1], (q, q))
            rowv = jnp.broadcast_to(a_cum_t[h:h + 1, :], (q, q))
            decay = jnp.where(causal, jnp.exp(col - rowv), 0.0)
            xh = jnp.where((lane_w >= hh * hd) & (lane_w < (hh + 1) * hd), xg, 0.0)
            yg = yg + _bdot(scores * decay, xh)
        y_parts.append(yg)
        upd = _bdot(bg.T, xdec[:, g * gw:(g + 1) * gw])
        st_ref[g] = st_ref[g] * jnp.exp(alast_e[:, g * gw:(g + 1) * gw]) + upd
    y = jnp.concatenate(y_parts, axis=-1) + dsk_ref[...] * xs
    y_ref[0] = _rms(y * _silu(gate_ref[0]), nw_ref[...])


def _mamba2(z3d, p, w, cols, q):
    bsz, seq, _ = z3d.shape
    heads = w // MB_HEADDIM
    lanes = V7X_LANES
    cg, cx, cbc, cdt = cols
    blk = lambda width, col: pl.BlockSpec((1, q, width), lambda b, c: (b, c, col // width))
    full = lambda a: pl.BlockSpec(a.shape, lambda b, c: (0,) * a.ndim)
    consts = [p["conv_w"], p["conv_b"], p["dt_bias"], p["a_log"], p["d"], p["norm_w"],
              p["expand"], p["tril"]]
    return pl.pallas_call(
        functools.partial(_mb_kernel, q=q, w=w, heads=heads, groups=MB_GROUPS, nstate=MB_STATE),
        grid=(bsz, seq // q),
        in_specs=[blk(w, cg), blk(w, cx), blk(w, cbc), blk(lanes, cdt)] + [full(a) for a in consts],
        out_specs=pl.BlockSpec((1, q, w), lambda b, c: (b, c, 0)),
        out_shape=jax.ShapeDtypeStruct((bsz, seq, w), F32),
        scratch_shapes=[pltpu.VMEM((V7X_SUBLANES, 2 * w), F32),
                        pltpu.VMEM((MB_GROUPS, MB_STATE, w // MB_GROUPS), F32)],
        compiler_params=_cparams("parallel", "arbitrary"),
        name="mamba2",
    )(z3d, z3d, z3d, z3d, *consts)


def _pad_rows(a, rows, at):
    out = jnp.zeros((rows, a.shape[1]), a.dtype)
    return lax.dynamic_update_slice(out, a, (at, 0))


def _block_ones(w, head):
    idx = jnp.arange(w) // head
    return (idx[:, None] == idx[None, :]).astype(BF16)


def _pick_tile(n, target):
    t = min(n, target)
    while n % t:
        t //= 2
    return t


def kernel(x, norm_mix_w, w_in, w_branch, w_out, norm_ffn_w, w_ffn_in, w_ffn_out, norm_final_w, hgrn_lower_bounds, hgrn_norm_w, rwkv_mu, rwkv_w0, rwkv_w2, rwkv_a0, rwkv_a2, rwkv_g2, rwkv_k_k, rwkv_k_a, rwkv_r_k, rwkv_ln_w, rwkv_ln_b, rwkv_v0, rwkv_v1, rwkv_v2, s5_a_re, s5_a_im, s5_b_re, s5_b_im, s5_c_re, s5_c_im, s5_d, s5_log_dt, s5_w_glu, s5_b_glu, mamba_conv_w, mamba_conv_b, mamba_dt_bias, mamba_a_log, mamba_d, mamba_norm_w):
    bsz, seq, d = x.shape
    depth = w_in.shape[0]
    w = d // 2
    lanes = V7X_LANES
    t = bsz * seq
    mb_heads = w // MB_HEADDIM
    mb_bc = 2 * MB_GROUPS * MB_STATE
    lw = RW_DECAY_LORA + RW_A_LORA + RW_G_LORA
    assert mb_bc == w and 3 * w % lw == 0

    o_gate = 0
    o_hg = o_gate + N_BRANCH * d
    o_rw = o_hg + 4 * w
    o_rwl = o_rw + 3 * w
    o_s5 = o_rwl + lw
    o_mbg = o_s5 + w
    o_mbx = o_mbg + w
    o_mbbc = o_mbx + w
    o_mbdt = o_mbbc + mb_bc
    c_rw = 0
    c_hg = c_rw + 3 * w
    c_s5 = c_hg + 4 * w
    c_mbg = c_s5 + w
    c_mbx = c_mbg + w
    c_mbbc = c_mbx + w
    c_rwl = c_mbbc + mb_bc
    c_mbdt = c_rwl + lw
    n_cols = c_mbdt + lanes
    tn = 512
    n_pad = -(-n_cols // tn) * tn

    def mixer_cols(wl):
        sl = lambda a, b: wl[:, a:b]
        pieces = [sl(o_rw, o_rwl), sl(o_hg, o_rw), sl(o_s5, o_mbdt), sl(o_rwl, o_s5),
                  sl(o_mbdt, o_mbdt + mb_heads),
                  jnp.zeros((d, n_pad - c_mbdt - mb_heads), wl.dtype)]
        return jnp.concatenate(pieces, axis=1).astype(BF16)

    tm = _pick_tile(seq, 256)
    tm_in = _pick_tile(seq, 512)
    tb_hg = _pick_tile(seq, 256)
    c_hg_chunk = min(HG_CHUNK, tb_hg)
    tc_rw = _pick_tile(seq, 32)
    tc_s5 = _pick_tile(seq, 64)
    q_mb = min(MB_CHUNK, seq)
    ff = w_ffn_out.shape[1]
    tn_ff = 256 if ff % 256 == 0 else lanes

    lower_bounds = _hg_bounds(hgrn_lower_bounds)
    ones_rw = _block_ones(w, RW_HEAD)
    eye_g = jnp.eye(lanes // S5_GROUP, dtype=F32)
    expand = (jnp.arange(lanes)[:, None] == (jnp.arange(w) // MB_HEADDIM)[None, :]).astype(BF16)
    tril = (jnp.arange(q_mb)[:, None] >= jnp.arange(q_mb)[None, :]).astype(BF16)

    x2d = x.reshape(t, d)
    v_first = jnp.zeros((t, w), F32)
    for l in range(depth):
        gates = _inproj(x2d, norm_mix_w[l], w_in[l][:, o_gate:o_hg].astype(BF16), tm_in, tn, True)
        z2d = _inproj(x2d, norm_mix_w[l], mixer_cols(w_in[l]), tm_in, tn, False)
        z3d = z2d.reshape(bsz, seq, n_pad)

        y_hg = _hgrn2(z3d, lower_bounds[l], hgrn_norm_w[l], w, c_hg, tb_hg, c_hg_chunk,
                      HG_SUBBLOCK).reshape(t, w)

        mu = rwkv_mu[l]
        has_vres = l > 0
        lv = max(l - 1, 0)
        rp = {
            "mu_main": mu[None, :3 * w], "mu_lora": mu[None, 3 * w:],
            "w0": rwkv_w0[l][None], "a0": rwkv_a0[l][None],
            "w2p": _pad_rows(rwkv_w2[l], lw, 0).astype(BF16),
            "a2p": _pad_rows(rwkv_a2[l], lw, RW_DECAY_LORA).astype(BF16),
            "g2p": _pad_rows(rwkv_g2[l], lw, RW_DECAY_LORA + RW_A_LORA).astype(BF16),
            "k_k": rwkv_k_k[l][None], "k_a": rwkv_k_a[l][None], "r_k": rwkv_r_k[l].reshape(1, w),
            "v0": rwkv_v0[lv][None],
            "v1p": jnp.pad(rwkv_v1[lv], ((0, 0), (0, lanes - rwkv_v1.shape[2]))).astype(BF16),
            "v2p": _pad_rows(rwkv_v2[lv], lanes, 0).astype(BF16),
            "ones": ones_rw, "ln_w": rwkv_ln_w[l], "ln_b": rwkv_ln_b[l],
        }
        y_rw, v_l = _rwkv7(z2d, v_first, rp, bsz, seq, w, c_rw, c_rwl, lw, tm, tc_rw, has_vres)
        if l == 0:
            v_first = v_l

        lam_re, lam_im, bb_re, bb_im = _s5_discretize(s5_a_re[l], s5_a_im[l], s5_log_dt[l],
                                                      s5_b_re[l], s5_b_im[l])
        ns = lam_re.size
        nsg = w // lanes
        gps = lanes // S5_GROUP
        blk_in = lambda bb: jnp.einsum(
            "sgnc,gh->sgchn", bb.reshape(nsg, gps, S5_STATE, S5_GROUP), eye_g
        ).reshape(nsg, lanes, gps * S5_STATE)
        blk_out = lambda cc: jnp.einsum(
            "sgcn,gh->sgnhc", cc.reshape(nsg, gps, S5_GROUP, S5_STATE), eye_g
        ).reshape(nsg, gps * S5_STATE, lanes)
        sp = {
            "bblk": jnp.concatenate([blk_in(bb_re), blk_in(bb_im)], axis=2).astype(BF16),
            "cblk": jnp.concatenate([blk_out(s5_c_re[l]), -blk_out(s5_c_im[l])], axis=1).astype(BF16),
            "lam_re": lam_re.reshape(1, ns), "lam_im": lam_im.reshape(1, ns),
            "d": s5_d[l][None], "w_glu": s5_w_glu[l].astype(BF16), "b_glu": s5_b_glu[l][None],
        }
        u_tm = z3d[:, :, c_s5:c_s5 + w].transpose(1, 0, 2).reshape(t, w)
        y_s5 = _s5(u_tm, sp, bsz, tc_s5).reshape(seq, bsz, w).transpose(1, 0, 2).reshape(t, w)

        mp = {
            "conv_w": mamba_conv_w[l], "conv_b": mamba_conv_b[l][None],
            "dt_bias": jnp.pad(mamba_dt_bias[l], (0, lanes - mb_heads))[None],
            "a_log": jnp.pad(mamba_a_log[l], (0, lanes - mb_heads))[None],
            "d": jnp.repeat(mamba_d[l], MB_HEADDIM)[None], "norm_w": mamba_norm_w[l][None],
            "expand": expand, "tril": tril,
        }
        y_mb = _mamba2(z3d, mp, w, (c_mbg, c_mbx, c_mbbc, c_mbdt), q_mb).reshape(t, w)

        x2d = _merge((y_hg, y_rw, y_s5, y_mb), gates, w_branch[l].astype(BF16), w_out[l].astype(BF16),
                     x2d, tm)
        act = _ffn_in(x2d, norm_ffn_w[l], w_ffn_in[l].astype(BF16), tm_in, tn_ff)
        x2d = _ffn_out(act, w_ffn_out[l].astype(BF16), x2d, norm_final_w, l == depth - 1, tm)
    return x2d.reshape(bsz, seq, d)
```

```python
import functools
import math

import jax
import jax.numpy as jnp
from jax import lax
from jax.experimental import pallas as pl
from jax.experimental.pallas import tpu as pltpu

F32 = jnp.float32
BF16 = jnp.bfloat16

V7X_LANES = 128
V7X_SUBLANES = 8
V7X_VMEM_LIMIT_BYTES = 56 * 1024 * 1024

RMS_EPS = 1e-6
N_BRANCH = 4
HG_DK = 128
HG_TINY = 1e-30
HG_CHUNK = 64
HG_SUBBLOCK = 8
RW_HEAD = 64
RW_DECAY_LORA = 64
RW_A_LORA = 64
RW_G_LORA = 128
RW_LN_EPS = 64e-5
RW_NVEC = 6
S5_GROUP = 16
S5_STATE = 64
MB_HEADDIM = 64
MB_GROUPS = 2
MB_STATE = 128
MB_CONV = 4
MB_CHUNK = 128


def _cparams(*sem):
    return pltpu.CompilerParams(dimension_semantics=sem, vmem_limit_bytes=V7X_VMEM_LIMIT_BYTES)


def _bdot(a, b):
    return jnp.dot(a.astype(BF16), b.astype(BF16), preferred_element_type=F32)


def _split3(x):
    x1 = x.astype(BF16)
    r1 = x - x1.astype(F32)
    x2 = r1.astype(BF16)
    x3 = (r1 - x2.astype(F32)).astype(BF16)
    return x1, x2, x3


def _dot_exact_rhs(x, m):
    x1, x2, x3 = _split3(x)
    d = functools.partial(jnp.dot, preferred_element_type=F32)
    return d(x1, m) + d(x2, m) + d(x3, m)


def _dot_exact_lhs(m, x):
    x1, x2, x3 = _split3(x)
    d = functools.partial(jnp.dot, preferred_element_type=F32)
    return d(m, x1) + d(m, x2) + d(m, x3)


def _softplus(x):
    return jnp.maximum(x, 0.0) + jnp.log(1.0 + jnp.exp(-jnp.abs(x)))


def _silu(x):
    return x * jax.nn.sigmoid(x)


def _rms(x, w):
    return x * lax.rsqrt(jnp.mean(x * x, axis=-1, keepdims=True) + RMS_EPS) * w


def _resident(shape):
    return pl.BlockSpec(shape, lambda i: (0,) * len(shape), pipeline_mode=pl.Buffered(1))


def _inproj_kernel(x_ref, nw_ref, w_ref, o_ref, *, gate, tn):
    u = _rms(x_ref[...], nw_ref[...]).astype(BF16)
    for n0 in range(0, o_ref.shape[1], tn):
        z = jnp.dot(u, w_ref[:, n0:n0 + tn], preferred_element_type=F32)
        o_ref[:, n0:n0 + tn] = jax.nn.sigmoid(z).astype(o_ref.dtype) if gate else z


def _inproj(x2d, norm_w, w_bf16, tm, tn, gate):
    t, d = x2d.shape
    n = w_bf16.shape[1]
    return pl.pallas_call(
        functools.partial(_inproj_kernel, gate=gate, tn=tn),
        grid=(t // tm,),
        in_specs=[pl.BlockSpec((tm, d), lambda i: (i, 0)),
                  pl.BlockSpec((1, d), lambda i: (0, 0)),
                  _resident((d, n))],
        out_specs=pl.BlockSpec((tm, n), lambda i: (i, 0)),
        out_shape=jax.ShapeDtypeStruct((t, n), BF16 if gate else F32),
        compiler_params=_cparams("parallel"),
        name="inproj_gate" if gate else "inproj",
    )(x2d, norm_w.reshape(1, d), w_bf16)


def _ffn_in_kernel(x_ref, nw_ref, w_ref, o_ref, *, tn):
    ff = o_ref.shape[1]
    u = _rms(x_ref[...], nw_ref[...]).astype(BF16)
    for n0 in range(0, ff, tn):
        gate = jnp.dot(u, w_ref[:, n0:n0 + tn], preferred_element_type=F32)
        up = jnp.dot(u, w_ref[:, ff + n0:ff + n0 + tn], preferred_element_type=F32)
        o_ref[:, n0:n0 + tn] = (_silu(gate) * up).astype(BF16)


def _ffn_in(x2d, norm_w, w_bf16, tm, tn):
    t, d = x2d.shape
    ff = w_bf16.shape[1] // 2
    return pl.pallas_call(
        functools.partial(_ffn_in_kernel, tn=tn),
        grid=(t // tm,),
        in_specs=[pl.BlockSpec((tm, d), lambda i: (i, 0)),
                  pl.BlockSpec((1, d), lambda i: (0, 0)),
                  _resident((d, 2 * ff))],
        out_specs=pl.BlockSpec((tm, ff), lambda i: (i, 0)),
        out_shape=jax.ShapeDtypeStruct((t, ff), BF16),
        compiler_params=_cparams("parallel"),
        name="ffn_in",
    )(x2d, norm_w.reshape(1, d), w_bf16)


def _ffn_out_kernel(a_ref, w_ref, x_ref, fw_ref, o_ref, *, final_norm):
    y = x_ref[...] + jnp.dot(a_ref[...], w_ref[...], preferred_element_type=F32)
    if final_norm:
        y = _rms(y, fw_ref[...])
    o_ref[...] = y


def _ffn_out(act, w_bf16, x2d, final_w, final_norm, tm):
    t, d = x2d.shape
    ff = act.shape[1]
    return pl.pallas_call(
        functools.partial(_ffn_out_kernel, final_norm=final_norm),
        grid=(t // tm,),
        in_specs=[pl.BlockSpec((tm, ff), lambda i: (i, 0)),
                  _resident((ff, d)),
                  pl.BlockSpec((tm, d), lambda i: (i, 0)),
                  pl.BlockSpec((1, d), lambda i: (0, 0))],
        out_specs=pl.BlockSpec((tm, d), lambda i: (i, 0)),
        out_shape=jax.ShapeDtypeStruct((t, d), F32),
        compiler_params=_cparams("parallel"),
        name="ffn_out",
    )(act, w_bf16, x2d, final_w.reshape(1, d))


def _merge_kernel(yh_ref, yr_ref, ys_ref, ym_ref, zg_ref, wb_ref, wo_ref, x_ref, o_ref):
    d = x_ref.shape[1]
    acc = jnp.zeros(x_ref.shape, F32)
    for k, y_ref in enumerate((yh_ref, yr_ref, ys_ref, ym_ref)):
        proj = jnp.dot(y_ref[...].astype(BF16), wb_ref[k], preferred_element_type=F32)
        acc = acc + zg_ref[:, k * d:(k + 1) * d].astype(F32) * proj
    o_ref[...] = x_ref[...] + jnp.dot(acc.astype(BF16), wo_ref[...], preferred_element_type=F32)


def _merge(ys, z2d, wb_bf16, wo_bf16, x2d, tm):
    t, d = x2d.shape
    w = ys[0].shape[1]
    yspec = pl.BlockSpec((tm, w), lambda i: (i, 0))
    return pl.pallas_call(
        _merge_kernel,
        grid=(t // tm,),
        in_specs=[yspec, yspec, yspec, yspec,
                  pl.BlockSpec((tm, N_BRANCH * d), lambda i: (i, 0)),
                  _resident((N_BRANCH, w, d)),
                  _resident((d, d)),
                  pl.BlockSpec((tm, d), lambda i: (i, 0))],
        out_specs=pl.BlockSpec((tm, d), lambda i: (i, 0)),
        out_shape=jax.ShapeDtypeStruct((t, d), F32),
        compiler_params=_cparams("parallel"),
        name="merge",
    )(*ys, z2d, wb_bf16, wo_bf16, x2d)


def _hg_bounds_kernel(h_ref, o_ref):
    h = h_ref[...]
    depth = h.shape[0]
    m = jnp.max(h, axis=0, keepdims=True)
    e = jnp.exp(h - m)
    p = e / jnp.sum(e, axis=0, keepdims=True)
    run = jnp.zeros_like(p[0:1])
    rows = []
    for l in range(depth):
        run = run + p[l:l + 1]
        rows.append(run - p[0:1])
    o_ref[...] = jnp.concatenate(rows, axis=0)


def _hg_bounds(hgrn_lower_bounds):
    return pl.pallas_call(
        _hg_bounds_kernel,
        out_shape=jax.ShapeDtypeStruct(hgrn_lower_bounds.shape, F32),
        name="hg_bounds",
    )(hgrn_lower_bounds)


def _hg_chunk_kernel(q_ref, f_ref, i_ref, g_ref, lb_ref, nw_ref, tril_ref, after_ref, y_ref,
                     st_ref, kbuf, bbuf, vbuf, *, tb, c, cs, heads, dk):
    sub = V7X_SUBLANES
    nb = c // cs
    nt = (((1,), (1,)), ((), ()))

    @pl.when(pl.program_id(1) == 0)
    def _():
        st_ref[...] = jnp.zeros(st_ref.shape, F32)

    kbuf[0:sub, :] = jnp.zeros((sub, kbuf.shape[1]), F32)
    bbuf[0:sub, :] = jnp.zeros((sub, bbuf.shape[1]), F32)
    vbuf[0:sub, :] = jnp.zeros((sub, vbuf.shape[1]), F32)
    lb = lb_ref[...]
    tril = tril_ref[...]
    rid = lax.broadcasted_iota(jnp.int32, (c, dk), 0)
    rid1 = lax.broadcasted_iota(jnp.int32, (c, 1), 0)

    def chunk(ci, carry):
        rows = pl.ds(pl.multiple_of(ci * c, c), c)
        ff = f_ref[0, rows, :]
        q = _silu(q_ref[0, rows, :])
        dec = jnp.maximum(lb + (1.0 - lb) * jax.nn.sigmoid(ff), HG_TINY)
        k = (1.0 - lb) * jax.nn.sigmoid(-ff)
        v = i_ref[0, rows, :]
        b = _dot_exact_lhs(tril, jnp.log(dec))
        kbuf[sub:sub + c, :] = k
        bbuf[sub:sub + c, :] = b
        vbuf[sub:sub + c, :] = v
        outs = []
        for h in range(heads):
            hs = slice(h * dk, (h + 1) * dk)
            qh, kh, bh, vh = q[:, hs], k[:, hs], b[:, hs], v[:, hs]
            blast = bh[c - 1:c, :]
            st = st_ref[h]
            o = lax.dot_general((qh * jnp.exp(bh)).astype(BF16), st.astype(BF16), nt,
                                preferred_element_type=F32)
            att_rows = [jnp.zeros((cs, c), F32)]
            for i in range(1, nb):
                beta = bh[i * cs - 1:i * cs, :]
                kt = jnp.where(rid < i * cs, kh * jnp.exp(jnp.minimum(beta - bh, 0.0)), 0.0)
                qt = qh[i * cs:(i + 1) * cs, :] * jnp.exp(bh[i * cs:(i + 1) * cs, :] - beta)
                att_rows.append(lax.dot_general(qt.astype(BF16), kt.astype(BF16), nt,
                                                preferred_element_type=F32))
            o = o + _bdot(jnp.concatenate(att_rows, axis=0), vh)
            for dlt in range(cs):
                if dlt == 0:
                    a = jnp.sum(qh * kh, axis=-1, keepdims=True)
                    o = o + a * vh
                else:
                    win = pl.ds(sub - dlt, c)
                    e = jnp.exp(jnp.minimum(bh - bbuf[win, hs], 0.0))
                    a = jnp.sum(qh * kbuf[win, hs] * e, axis=-1, keepdims=True)
                    a = jnp.where((rid1 % cs) >= dlt, a, 0.0)
                    o = o + a * vbuf[win, hs]
            kd = kh * jnp.exp(blast - bh)
            st_ref[h] = st * jnp.exp(blast) + _bdot(vh.T, kd)
            outs.append(o * lax.rsqrt(jnp.mean(o * o, axis=-1, keepdims=True) + RMS_EPS))
        y = jnp.concatenate(outs, axis=-1) * nw_ref[...]
        y_ref[0, rows, :] = y * _silu(g_ref[0, rows, :])
        return carry

    lax.fori_loop(0, tb // c, chunk, 0)


def _hgrn2(z3d, lb, norm_w, w, col0, tb, c, cs, after):
    bsz, seq, _ = z3d.shape
    heads = w // HG_DK
    cb = col0 // w
    tril = (jnp.arange(c)[:, None] >= jnp.arange(c)[None, :]).astype(BF16)
    zspec = lambda j: pl.BlockSpec((1, tb, w), lambda b, i: (b, i, cb + j))
    row = pl.BlockSpec((1, w), lambda b, i: (0, 0))
    return pl.pallas_call(
        functools.partial(_hg_chunk_kernel, tb=tb, c=c, cs=cs, heads=heads, dk=HG_DK),
        grid=(bsz, seq // tb),
        in_specs=[zspec(0), zspec(1), zspec(2), zspec(3), row, row,
                  pl.BlockSpec((c, c), lambda b, i: (0, 0)),
                  pl.BlockSpec((1, V7X_SUBLANES, V7X_LANES), lambda b, i: (0, 0, 0))],
        out_specs=pl.BlockSpec((1, tb, w), lambda b, i: (b, i, 0)),
        out_shape=jax.ShapeDtypeStruct((bsz, seq, w), F32),
        scratch_shapes=[pltpu.VMEM((heads, HG_DK, HG_DK), F32)]
        + [pltpu.VMEM((c + V7X_SUBLANES, w), F32)] * 3,
        compiler_params=_cparams("parallel", "arbitrary"),
        name="hgrn2",
    )(z3d, z3d, z3d, z3d, lb.reshape(1, w), norm_w.reshape(1, w), tril, after)


def _lanes_to_chains_v(a, bsz, seq, heads, dv, nq):
    a = a.reshape(seq, dv // nq, nq, bsz, heads).transpose(3, 0, 4, 2, 1)
    return a.reshape(bsz * seq, heads * dv)


def _shift_rows(cur, prev_last, first):
    rolled = pltpu.roll(cur, 1, axis=0)
    row0 = jnp.where(first, 0.0, prev_last)
    rid = lax.broadcasted_iota(jnp.int32, cur.shape, 0)
    return jnp.where(rid == 0, jnp.broadcast_to(row0, cur.shape), rolled)


def _rw_prep_kernel(zm_ref, zmp_ref, zl_ref, zlp_ref, vf_ref,
                    mum_ref, mul_ref, w0_ref, w2_ref, a0_ref, a2_ref, g2_ref,
                    kk_ref, ka_ref, rk_ref, v0_ref, v1_ref, v2_ref, ones_ref,
                    kvec_out, v_out, g_out, bonus_out,
                    *, w, tiles_per_seq, has_vres):
    first = (pl.program_id(0) % tiles_per_seq) == 0
    sub = V7X_SUBLANES
    zm = zm_ref[...]
    zl = zl_ref[...]
    zms = zm + (_shift_rows(zm, zmp_ref[sub - 1:sub, :], first) - zm) * mum_ref[...]
    zls = zl + (_shift_rows(zl, zlp_ref[sub - 1:sub, :], first) - zl) * mul_ref[...]
    r = zms[:, 0:w]
    k = zms[:, w:2 * w]
    v = zms[:, 2 * w:3 * w]
    w_log = -_softplus(-(w0_ref[...] + _bdot(jnp.tanh(zls), w2_ref[...]))) - 0.5
    decay = jnp.exp(-jnp.exp(w_log))
    if has_vres:
        mix = jax.nn.sigmoid(v0_ref[...] + _bdot(_bdot(v, v1_ref[...]), v2_ref[...]))
        v = v + (vf_ref[...] - v) * mix
    a = jax.nn.sigmoid(a0_ref[...] + _bdot(zls, a2_ref[...]))
    g = _bdot(jax.nn.sigmoid(zls), g2_ref[...])
    ones = ones_ref[...]
    kk = k * kk_ref[...]
    ss = _dot_exact_rhs(kk * kk, ones)
    kk = kk / jnp.maximum(jnp.sqrt(ss), 1e-12)
    k2 = k * (1.0 + (a - 1.0) * ka_ref[...])
    vecs = (r, decay, k2, -kk, kk * a, v)
    for h in range(w // RW_HEAD):
        hs = slice(h * RW_HEAD, (h + 1) * RW_HEAD)
        row = jnp.concatenate([x[:, hs] for x in vecs], axis=1)
        for q in range(kvec_out.shape[0]):
            kvec_out[q, 0, h] = row
    v_out[...] = v
    g_out[...] = g
    bonus_out[...] = _dot_exact_rhs(r * k2 * rk_ref[...], ones) * v


def _rw_prep(z2d, v_first, p, col_main, col_lora, w, lw, seq, tm, has_vres):
    t = z2d.shape[0]
    sub = V7X_SUBLANES
    mb = col_main // (3 * w)
    lbk = col_lora // lw
    rows8 = tm // sub
    heads = w // RW_HEAD
    tps = seq // tm
    nq = V7X_LANES // (t // seq * heads)

    def prev_idx(i):
        return jnp.maximum(i * rows8 - 1, 0)

    row = lambda n: pl.BlockSpec((1, n), lambda i: (0, 0))
    full = lambda a: pl.BlockSpec(a.shape, lambda i: (0,) * a.ndim)
    ospec = pl.BlockSpec((tm, w), lambda i: (i, 0))
    oshape = jax.ShapeDtypeStruct((t, w), F32)
    args = [z2d, z2d, z2d, z2d, v_first,
            p["mu_main"], p["mu_lora"], p["w0"], p["w2p"], p["a0"], p["a2p"], p["g2p"],
            p["k_k"], p["k_a"], p["r_k"], p["v0"], p["v1p"], p["v2p"], p["ones"]]
    in_specs = [pl.BlockSpec((tm, 3 * w), lambda i: (i, mb)),
                pl.BlockSpec((sub, 3 * w), lambda i: (prev_idx(i), mb)),
                pl.BlockSpec((tm, lw), lambda i: (i, lbk)),
                pl.BlockSpec((sub, lw), lambda i: (prev_idx(i), lbk)),
                ospec,
                row(3 * w), row(lw), row(w), full(p["w2p"]), row(w), full(p["a2p"]), full(p["g2p"]),
                row(w), row(w), row(w), row(w), full(p["v1p"]), full(p["v2p"]), full(p["ones"])]
    return pl.pallas_call(
        functools.partial(_rw_prep_kernel, w=w, tiles_per_seq=seq // tm, has_vres=has_vres),
        grid=(t // tm,),
        in_specs=in_specs,
        out_specs=[pl.BlockSpec((nq, 1, heads, tm, RW_NVEC * RW_HEAD),
                                lambda i: (0, i // tps, 0, i % tps, 0)), ospec, ospec, ospec],
        out_shape=[jax.ShapeDtypeStruct((nq, t // seq, heads, seq, RW_NVEC * RW_HEAD), F32),
                   oshape, oshape, oshape],
        compiler_params=_cparams("parallel"),
        name="rw_prep",
    )(*args)


def _rw_scan_kernel(x_ref, *rest, tc, dk, nvb, nq):
    y_ref, s_ref = rest[-2:]
    sub = V7X_SUBLANES
    lanes = V7X_LANES
    jr, jw, jk, ja, jb, jv = range(RW_NVEC)
    vl = nvb * sub
    qid = lax.broadcasted_iota(jnp.int32, (sub, lanes), 1) // (lanes // nq)

    @pl.when(pl.program_id(0) == 0)
    def _():
        s_ref[...] = jnp.zeros(s_ref.shape, F32)

    def bc(j, t, kk):
        return jnp.broadcast_to(x_ref[t, j, pl.ds(kk, 1), :], (sub, lanes))

    sa0 = [jnp.zeros((sub, lanes), F32) for _ in range(nvb)]
    for kk in range(dk):
        arow = bc(ja, 0, kk)
        for j in range(nvb):
            sa0[j] = sa0[j] + s_ref[kk, pl.ds(sub * j, sub), :] * arow

    def step(t, sa):
        tn = jnp.minimum(t + 1, tc - 1)
        vb = []
        for j in range(nvb):
            vj = x_ref[t, jv, pl.ds(sub * j, sub), :]
            for q in range(1, nq):
                vj = jnp.where(qid == q, x_ref[t, jv, pl.ds(q * vl + sub * j, sub), :], vj)
            vb.append(vj)
        yacc = [jnp.zeros((sub, lanes), F32) for _ in range(nvb)]
        sacc = [jnp.zeros((sub, lanes), F32) for _ in range(nvb)]
        for kk in range(dk):
            wrow = bc(jw, t, kk)
            brow = bc(jb, t, kk)
            krow = bc(jk, t, kk)
            rrow = bc(jr, t, kk)
            anext = bc(ja, tn, kk)
            for j in range(nvb):
                s = s_ref[kk, pl.ds(sub * j, sub), :] * wrow + sa[j] * brow + vb[j] * krow
                s_ref[kk, pl.ds(sub * j, sub), :] = s
                yacc[j] = yacc[j] + s * rrow
                sacc[j] = sacc[j] + s * anext
        for j in range(nvb):
            y_ref[t, pl.ds(sub * j, sub), :] = yacc[j]
        return tuple(sacc)

    lax.fori_loop(0, tc, step, tuple(sa0))


def _rw_scan(x_l, nq, tc, after):
    s, nvec, dk, lanes = x_l.shape
    vl = dk // nq
    kspec = pl.BlockSpec((tc, nvec, dk, lanes), lambda i: (i, 0, 0, 0))
    vspec = pl.BlockSpec((tc, vl, lanes), lambda i: (i, 0, 0))
    order = [pl.BlockSpec((16, lanes), lambda i: (0, 0))] * len(after)
    return pl.pallas_call(
        functools.partial(_rw_scan_kernel, tc=tc, dk=dk, nvb=vl // V7X_SUBLANES, nq=nq),
        grid=(s // tc,),
        in_specs=[kspec] + order,
        out_specs=vspec,
        out_shape=jax.ShapeDtypeStruct((s, vl, lanes), F32),
        scratch_shapes=[pltpu.VMEM((dk, vl, lanes), F32)],
        compiler_params=_cparams("arbitrary"),
        name="rw_scan",
    )(x_l, *after)


def _rw_post_kernel(y_ref, g_ref, bonus_ref, lnw_ref, lnb_ref, ones_ref, o_ref):
    y = y_ref[...]
    ones = ones_ref[...]
    inv_n = 1.0 / RW_HEAD
    mean = _dot_exact_rhs(y, ones) * inv_n
    yc = y - mean
    var = _dot_exact_rhs(yc * yc, ones) * inv_n
    y = yc * lax.rsqrt(var + RW_LN_EPS) * lnw_ref[...] + lnb_ref[...]
    o_ref[...] = (y + bonus_ref[...]) * g_ref[...]


def _rw_post(y2d, g, bonus, ln_w, ln_b, ones, tm):
    t, w = y2d.shape
    tspec = pl.BlockSpec((tm, w), lambda i: (i, 0))
    row = pl.BlockSpec((1, w), lambda i: (0, 0))
    return pl.pallas_call(
        _rw_post_kernel,
        grid=(t // tm,),
        in_specs=[tspec, tspec, tspec, row, row, pl.BlockSpec((w, w), lambda i: (0, 0))],
        out_specs=tspec,
        out_shape=jax.ShapeDtypeStruct((t, w), F32),
        compiler_params=_cparams("parallel"),
        name="rw_post",
    )(y2d, g, bonus, ln_w.reshape(1, w), ln_b.reshape(1, w), ones)


def _rwkv7_front(z2d, v_first, p, bsz, seq, w, col_main, col_lora, lw, tm, has_vres):
    kvec, v, g, bonus = _rw_prep(z2d, v_first, p, col_main, col_lora, w, lw, seq, tm, has_vres)
    lanes = kvec.shape[0] * kvec.shape[1] * kvec.shape[2]
    x_l = kvec.reshape(lanes, seq, RW_NVEC * RW_HEAD).transpose(1, 2, 0)
    return x_l.reshape(seq, RW_NVEC, RW_HEAD, lanes), v, g, bonus


def _rwkv7_back(y_l, g, bonus, p, bsz, seq, w, tm):
    heads = w // RW_HEAD
    nq = V7X_LANES // (bsz * heads)
    y = _lanes_to_chains_v(y_l, bsz, seq, heads, RW_HEAD, nq)
    return _rw_post(y, g, bonus, p["ln_w"], p["ln_b"], p["ones"], tm)


def _s5_disc_kernel(are_ref, aim_ref, dt_ref, bre_ref, bim_ref, lre_ref, lim_ref, bbre_ref, bbim_ref):
    a_re = are_ref[...]
    a_im = aim_ref[...]
    dt = jnp.exp(dt_ref[...])
    mag = jnp.exp(dt * a_re)
    lam_re = mag * jnp.cos(dt * a_im)
    lam_im = mag * jnp.sin(dt * a_im)
    den = a_re * a_re + a_im * a_im
    coef_re = ((lam_re - 1.0) * a_re + lam_im * a_im) / den
    coef_im = (lam_im * a_re - (lam_re - 1.0) * a_im) / den
    b_re = bre_ref[...]
    b_im = bim_ref[...]
    lre_ref[...] = lam_re
    lim_ref[...] = lam_im
    bbre_ref[...] = coef_re * b_re - coef_im * b_im
    bbim_ref[...] = coef_re * b_im + coef_im * b_re


def _s5_discretize(a_re, a_im, log_dt, b_re, b_im):
    g, n, c = b_re.shape
    shp = (g, n * c)
    bc = lambda a: jnp.broadcast_to(a[..., None], (g, n, c)).reshape(shp)
    dtb = jnp.broadcast_to(log_dt[:, None], shp)
    o = jax.ShapeDtypeStruct(shp, F32)
    lre, lim, bbre, bbim = pl.pallas_call(
        _s5_disc_kernel, out_shape=[o, o, o, o], name="s5_disc",
    )(bc(a_re), bc(a_im), dtb, b_re.reshape(shp), b_im.reshape(shp))
    un = lambda a: a.reshape(g, n, c)
    return un(lre)[..., 0], un(lim)[..., 0], un(bbre), un(bbim)


def _s5_kernel(u_ref, bblk_ref, cblk_ref, lre_ref, lim_ref, d_ref, wg_ref, bg_ref, y_ref,
               h_ref, hr_ref, hi_ref, *, tc, bsz, ns, lane_chunk):
    @pl.when(pl.program_id(0) == 0)
    def _():
        hr_ref[...] = jnp.zeros(hr_ref.shape, F32)
        hi_ref[...] = jnp.zeros(hi_ref.shape, F32)

    u = u_ref[...]
    nsg = bblk_ref.shape[0]
    lanes = V7X_LANES
    for sg in range(nsg):
        c0 = sg * lane_chunk
        drive = _bdot(u[:, sg * lanes:(sg + 1) * lanes], bblk_ref[sg])
        h_ref[:, c0:c0 + lane_chunk] = drive[:, 0:lane_chunk]
        h_ref[:, ns + c0:ns + c0 + lane_chunk] = drive[:, lane_chunk:2 * lane_chunk]
    for c0 in range(0, ns, lane_chunk):
        lr = jnp.broadcast_to(lre_ref[:, c0:c0 + lane_chunk], (bsz, lane_chunk))
        li = jnp.broadcast_to(lim_ref[:, c0:c0 + lane_chunk], (bsz, lane_chunk))

        def step(t, carry, c0=c0, lr=lr, li=li):
            hr, hi = carry
            rows = pl.ds(pl.multiple_of(t * bsz, bsz), bsz)
            nr = lr * hr - li * hi + h_ref[rows, c0:c0 + lane_chunk]
            ni = lr * hi + li * hr + h_ref[rows, ns + c0:ns + c0 + lane_chunk]
            h_ref[rows, c0:c0 + lane_chunk] = nr
            h_ref[rows, ns + c0:ns + c0 + lane_chunk] = ni
            return nr, ni

        hr, hi = lax.fori_loop(0, tc, step,
                               (hr_ref[:, c0:c0 + lane_chunk], hi_ref[:, c0:c0 + lane_chunk]))
        hr_ref[:, c0:c0 + lane_chunk] = hr
        hi_ref[:, c0:c0 + lane_chunk] = hi
    outs = []
    for sg in range(nsg):
        c0 = sg * lane_chunk
        outs.append(_bdot(h_ref[:, c0:c0 + lane_chunk], cblk_ref[sg, 0:lane_chunk, :])
                    + _bdot(h_ref[:, ns + c0:ns + c0 + lane_chunk], cblk_ref[sg, lane_chunk:, :]))
    y = jnp.concatenate(outs, axis=-1) + d_ref[...] * u
    y = jax.nn.gelu(y)
    y_ref[...] = y * jax.nn.sigmoid(_bdot(y, wg_ref[...]) + bg_ref[...])


def _s5(u_tm, p, bsz, tc):
    rows, w = u_tm.shape
    ns = p["lam_re"].shape[1]
    blk = tc * bsz
    full = lambda a: pl.BlockSpec(a.shape, lambda i: (0,) * a.ndim)
    return pl.pallas_call(
        functools.partial(_s5_kernel, tc=tc, bsz=bsz, ns=ns, lane_chunk=ns // p["bblk"].shape[0]),
        grid=(rows // blk,),
        in_specs=[pl.BlockSpec((blk, w), lambda i: (i, 0)),
                  full(p["bblk"]), full(p["cblk"]), full(p["lam_re"]), full(p["lam_im"]),
                  full(p["d"]), full(p["w_glu"]), full(p["b_glu"])],
        out_specs=pl.BlockSpec((blk, w), lambda i: (i, 0)),
        out_shape=jax.ShapeDtypeStruct((rows, w), F32),
        scratch_shapes=[pltpu.VMEM((blk, 2 * ns), F32),
                        pltpu.VMEM((bsz, ns), F32), pltpu.VMEM((bsz, ns), F32)],
        compiler_params=_cparams("arbitrary"),
        name="s5",
    )(u_tm, p["bblk"], p["cblk"], p["lam_re"], p["lam_im"], p["d"], p["w_glu"], p["b_glu"])


def _mb_kernel(gate_ref, x_ref, bc_ref, dt_ref, cw_ref, cb_ref, dtb_ref, alog_ref, dsk_ref, nw_ref,
               expand_ref, tril_ref, y_ref, prev_ref, st_ref, *, q, w, heads, groups, nstate):
    sub = V7X_SUBLANES
    lanes = V7X_LANES
    hd = w // heads
    gw = w // groups
    hpg = heads // groups

    @pl.when(pl.program_id(1) == 0)
    def _():
        prev_ref[...] = jnp.zeros(prev_ref.shape, F32)
        st_ref[...] = jnp.zeros(st_ref.shape, F32)

    xbc = jnp.concatenate([x_ref[0], bc_ref[0]], axis=-1)
    full = jnp.concatenate([prev_ref[...], xbc], axis=0)
    conv = jnp.broadcast_to(cb_ref[...], xbc.shape)
    for j in range(MB_CONV):
        shift = MB_CONV - 1 - j
        src = full if shift == 0 else pltpu.roll(full, shift, axis=0)
        conv = conv + src[sub:sub + q, :] * cw_ref[j:j + 1, :]
    prev_ref[...] = xbc[q - sub:q, :]
    act = _silu(conv)
    xs = act[:, 0:w]
    bmat = act[:, w:w + groups * nstate]
    cmat = act[:, w + groups * nstate:w + 2 * groups * nstate]

    dt = _softplus(dt_ref[0] + dtb_ref[...])
    a = -jnp.exp(alog_ref[...]) * dt
    a_cum = _dot_exact_lhs(tril_ref[...], a)
    expand = expand_ref[...]
    dt_e = _dot_exact_rhs(dt, expand)
    acum_e = _dot_exact_rhs(a_cum, expand)
    alast_e = acum_e[q - 1:q, :]
    xdt = xs * dt_e
    xdec = xdt * jnp.exp(alast_e - acum_e)

    rid = lax.broadcasted_iota(jnp.int32, (q, q), 0)
    cid = lax.broadcasted_iota(jnp.int32, (q, q), 1)
    causal = rid >= cid
    a_cum_t = a_cum.T
    lane_w = lax.broadcasted_iota(jnp.int32, (q, gw), 1)

    y_parts = []
    for g in range(groups):
        bg = bmat[:, g * nstate:(g + 1) * nstate]
        cg = cmat[:, g * nstate:(g + 1) * nstate]
        scores = lax.dot_general(cg.astype(BF16), bg.astype(BF16), (((1,), (1,)), ((), ())),
                                 preferred_element_type=F32)
        xg = xdt[:, g * gw:(g + 1) * gw]
        yg = _bdot(cg, st_ref[g]) * jnp.exp(acum_e[:, g * gw:(g + 1) * gw])
        for hh in range(hpg):
            h = g * hpg + hh
            col = jnp.broadcast_to(a_cum[:, h:h + 1], (q, q))
            rowv = jnp.broadcast_to(a_cum_t[h:h + 1, :], (q, q))
            decay = jnp.where(causal, jnp.exp(col - rowv), 0.0)
            xh = jnp.where((lane_w >= hh * hd) & (lane_w < (hh + 1) * hd), xg, 0.0)
            yg = yg + _bdot(scores * decay, xh)
        y_parts.append(yg)
        upd = _bdot(bg.T, xdec[:, g * gw:(g + 1) * gw])
        st_ref[g] = st_ref[g] * jnp.exp(alast_e[:, g * gw:(g + 1) * gw]) + upd
    y = jnp.concatenate(y_parts, axis=-1) + dsk_ref[...] * xs
    y_ref[0] = _rms(y * _silu(gate_ref[0]), nw_ref[...])


def _mamba2(z3d, p, w, cols, q):
    bsz, seq, _ = z3d.shape
    heads = w // MB_HEADDIM
    lanes = V7X_LANES
    cg, cx, cbc, cdt = cols
    blk = lambda width, col: pl.BlockSpec((1, q, width), lambda b, c: (b, c, col // width))
    full = lambda a: pl.BlockSpec(a.shape, lambda b, c: (0,) * a.ndim)
    consts = [p["conv_w"], p["conv_b"], p["dt_bias"], p["a_log"], p["d"], p["norm_w"],
              p["expand"], p["tril"]]
    return pl.pallas_call(
        functools.partial(_mb_kernel, q=q, w=w, heads=heads, groups=MB_GROUPS, nstate=MB_STATE),
        grid=(bsz, seq // q),
        in_specs=[blk(w, cg), blk(w, cx), blk(w, cbc), blk(lanes, cdt)] + [full(a) for a in consts],
        out_specs=pl.BlockSpec((1, q, w), lambda b, c: (b, c, 0)),
        out_shape=jax.ShapeDtypeStruct((bsz, seq, w), F32),
        scratch_shapes=[pltpu.VMEM((V7X_SUBLANES, 2 * w), F32),
                        pltpu.VMEM((MB_GROUPS, MB_STATE, w // MB_GROUPS), F32)],
        compiler_params=_cparams("parallel", "arbitrary"),
        name="mamba2",
    )(z3d, z3d, z3d, z3d, *consts)


def _pad_rows(a, rows, at):
    out = jnp.zeros((rows, a.shape[1]), a.dtype)
    return lax.dynamic_update_slice(out, a, (at, 0))


def _block_ones(w, head):
    idx = jnp.arange(w) // head
    return (idx[:, None] == idx[None, :]).astype(BF16)


def _pick_tile(n, target):
    t = min(n, target)
    while n % t:
        t //= 2
    return t


def kernel(x, norm_mix_w, w_in, w_branch, w_out, norm_ffn_w, w_ffn_in, w_ffn_out, norm_final_w, hgrn_lower_bounds, hgrn_norm_w, rwkv_mu, rwkv_w0, rwkv_w2, rwkv_a0, rwkv_a2, rwkv_g2, rwkv_k_k, rwkv_k_a, rwkv_r_k, rwkv_ln_w, rwkv_ln_b, rwkv_v0, rwkv_v1, rwkv_v2, s5_a_re, s5_a_im, s5_b_re, s5_b_im, s5_c_re, s5_c_im, s5_d, s5_log_dt, s5_w_glu, s5_b_glu, mamba_conv_w, mamba_conv_b, mamba_dt_bias, mamba_a_log, mamba_d, mamba_norm_w):
    bsz, seq, d = x.shape
    depth = w_in.shape[0]
    w = d // 2
    lanes = V7X_LANES
    t = bsz * seq
    mb_heads = w // MB_HEADDIM
    mb_bc = 2 * MB_GROUPS * MB_STATE
    lw = RW_DECAY_LORA + RW_A_LORA + RW_G_LORA
    assert mb_bc == w and 3 * w % lw == 0

    o_gate = 0
    o_hg = o_gate + N_BRANCH * d
    o_rw = o_hg + 4 * w
    o_rwl = o_rw + 3 * w
    o_s5 = o_rwl + lw
    o_mbg = o_s5 + w
    o_mbx = o_mbg + w
    o_mbbc = o_mbx + w
    o_mbdt = o_mbbc + mb_bc
    c_rw = 0
    c_hg = c_rw + 3 * w
    c_s5 = c_hg + 4 * w
    c_mbg = c_s5 + w
    c_mbx = c_mbg + w
    c_mbbc = c_mbx + w
    c_rwl = c_mbbc + mb_bc
    c_mbdt = c_rwl + lw
    n_cols = c_mbdt + lanes
    tn = 512
    n_pad = -(-n_cols // tn) * tn

    def mixer_cols(wl):
        sl = lambda a, b: wl[:, a:b]
        pieces = [sl(o_rw, o_rwl), sl(o_hg, o_rw), sl(o_s5, o_mbdt), sl(o_rwl, o_s5),
                  sl(o_mbdt, o_mbdt + mb_heads),
                  jnp.zeros((d, n_pad - c_mbdt - mb_heads), wl.dtype)]
        return jnp.concatenate(pieces, axis=1).astype(BF16)

    tm = _pick_tile(seq, 256)
    tm_in = _pick_tile(seq, 512)
    tb_hg = _pick_tile(seq, 256)
    c_hg_chunk = min(HG_CHUNK, tb_hg)
    tc_rw = _pick_tile(seq, 32)
    tc_s5 = _pick_tile(seq, 64)
    q_mb = min(MB_CHUNK, seq)
    ff = w_ffn_out.shape[1]
    tn_ff = 256 if ff % 256 == 0 else lanes

    lower_bounds = _hg_bounds(hgrn_lower_bounds)
    ones_rw = _block_ones(w, RW_HEAD)
    eye_g = jnp.eye(lanes // S5_GROUP, dtype=F32)
    expand = (jnp.arange(lanes)[:, None] == (jnp.arange(w) // MB_HEADDIM)[None, :]).astype(BF16)
    tril = (jnp.arange(q_mb)[:, None] >= jnp.arange(q_mb)[None, :]).astype(BF16)

    x2d = x.reshape(t, d)
    v_first = jnp.zeros((t, w), F32)
    for l in range(depth):
        gates = _inproj(x2d, norm_mix_w[l], w_in[l][:, o_gate:o_hg].astype(BF16), tm_in, tn, True)
        z2d = _inproj(x2d, norm_mix_w[l], mixer_cols(w_in[l]), tm_in, tn, False)
        z3d = z2d.reshape(bsz, seq, n_pad)

        mu = rwkv_mu[l]
        has_vres = l > 0
        lv = max(l - 1, 0)
        rp = {
            "mu_main": mu[None, :3 * w], "mu_lora": mu[None, 3 * w:],
            "w0": rwkv_w0[l][None], "a0": rwkv_a0[l][None],
            "w2p": _pad_rows(rwkv_w2[l], lw, 0).astype(BF16),
            "a2p": _pad_rows(rwkv_a2[l], lw, RW_DECAY_LORA).astype(BF16),
            "g2p": _pad_rows(rwkv_g2[l], lw, RW_DECAY_LORA + RW_A_LORA).astype(BF16),
            "k_k": rwkv_k_k[l][None], "k_a": rwkv_k_a[l][None], "r_k": rwkv_r_k[l].reshape(1, w),
            "v0": rwkv_v0[lv][None],
            "v1p": jnp.pad(rwkv_v1[lv], ((0, 0), (0, lanes - rwkv_v1.shape[2]))).astype(BF16),
            "v2p": _pad_rows(rwkv_v2[lv], lanes, 0).astype(BF16),
            "ones": ones_rw, "ln_w": rwkv_ln_w[l], "ln_b": rwkv_ln_b[l],
        }
        rw_x, v_l, rw_g, rw_bonus = _rwkv7_front(z2d, v_first, rp, bsz, seq, w, c_rw, c_rwl, lw, tm,
                                                 has_vres)
        if l == 0:
            v_first = v_l

        lam_re, lam_im, bb_re, bb_im = _s5_discretize(s5_a_re[l], s5_a_im[l], s5_log_dt[l],
                                                      s5_b_re[l], s5_b_im[l])
        ns = lam_re.size
        nsg = w // lanes
        gps = lanes // S5_GROUP
        blk_in = lambda bb: jnp.einsum(
            "sgnc,gh->sgchn", bb.reshape(nsg, gps, S5_STATE, S5_GROUP), eye_g
        ).reshape(nsg, lanes, gps * S5_STATE)
        blk_out = lambda cc: jnp.einsum(
            "sgcn,gh->sgnhc", cc.reshape(nsg, gps, S5_GROUP, S5_STATE), eye_g
        ).reshape(nsg, gps * S5_STATE, lanes)
        sp = {
            "bblk": jnp.concatenate([blk_in(bb_re), blk_in(bb_im)], axis=2).astype(BF16),
            "cblk": jnp.concatenate([blk_out(s5_c_re[l]), -blk_out(s5_c_im[l])], axis=1).astype(BF16),
            "lam_re": lam_re.reshape(1, ns), "lam_im": lam_im.reshape(1, ns),
            "d": s5_d[l][None], "w_glu": s5_w_glu[l].astype(BF16), "b_glu": s5_b_glu[l][None],
        }
        u_tm = z3d[:, :, c_s5:c_s5 + w].transpose(1, 0, 2).reshape(t, w)
        y_s5 = _s5(u_tm, sp, bsz, tc_s5).reshape(seq, bsz, w).transpose(1, 0, 2).reshape(t, w)

        mp = {
            "conv_w": mamba_conv_w[l], "conv_b": mamba_conv_b[l][None],
            "dt_bias": jnp.pad(mamba_dt_bias[l], (0, lanes - mb_heads))[None],
            "a_log": jnp.pad(mamba_a_log[l], (0, lanes - mb_heads))[None],
            "d": jnp.repeat(mamba_d[l], MB_HEADDIM)[None], "norm_w": mamba_norm_w[l][None],
            "expand": expand, "tril": tril,
        }
        y_mb = _mamba2(z3d, mp, w, (c_mbg, c_mbx, c_mbbc, c_mbdt), q_mb).reshape(t, w)

        rw_y = _rw_scan(rw_x, lanes // (bsz * (w // RW_HEAD)), tc_rw, (gates, y_s5, y_mb))
        y_hg = _hgrn2(z3d, lower_bounds[l], hgrn_norm_w[l], w, c_hg, tb_hg, c_hg_chunk,
                      HG_SUBBLOCK, rw_y).reshape(t, w)
        y_rw = _rwkv7_back(rw_y, rw_g, rw_bonus, rp, bsz, seq, w, tm)
        x2d = _merge((y_hg, y_rw, y_s5, y_mb), gates, w_branch[l].astype(BF16), w_out[l].astype(BF16),
                     x2d, tm)
        act = _ffn_in(x2d, norm_ffn_w[l], w_ffn_in[l].astype(BF16), tm_in, tn_ff)
        x2d = _ffn_out(act, w_ffn_out[l].astype(BF16), x2d, norm_final_w, l == depth - 1, tm)
    return x2d.reshape(bsz, seq, d)
```

```python
import functools
import math

import jax
import jax.numpy as jnp
from jax import lax
from jax.experimental import pallas as pl
from jax.experimental.pallas import tpu as pltpu

F32 = jnp.float32
BF16 = jnp.bfloat16

V7X_LANES = 128
V7X_SUBLANES = 8
V7X_VMEM_LIMIT_BYTES = 56 * 1024 * 1024

RMS_EPS = 1e-6
N_BRANCH = 4
HG_DK = 128
HG_TINY = 1e-30
HG_CHUNK = 64
HG_SUBBLOCK = 4
RW_HEAD = 64
RW_DECAY_LORA = 64
RW_A_LORA = 64
RW_G_LORA = 128
RW_LN_EPS = 64e-5
RW_NVEC = 6
S5_GROUP = 16
S5_STATE = 64
MB_HEADDIM = 64
MB_GROUPS = 2
MB_STATE = 128
MB_CONV = 4
MB_CHUNK = 128


def _cparams(*sem):
    return pltpu.CompilerParams(dimension_semantics=sem, vmem_limit_bytes=V7X_VMEM_LIMIT_BYTES)


def _bdot(a, b):
    return jnp.dot(a.astype(BF16), b.astype(BF16), preferred_element_type=F32)


def _split3(x):
    x1 = x.astype(BF16)
    r1 = x - x1.astype(F32)
    x2 = r1.astype(BF16)
    x3 = (r1 - x2.astype(F32)).astype(BF16)
    return x1, x2, x3


def _dot_exact_rhs(x, m):
    x1, x2, x3 = _split3(x)
    d = functools.partial(jnp.dot, preferred_element_type=F32)
    return d(x1, m) + d(x2, m) + d(x3, m)


def _dot_exact_lhs(m, x):
    x1, x2, x3 = _split3(x)
    d = functools.partial(jnp.dot, preferred_element_type=F32)
    return d(m, x1) + d(m, x2) + d(m, x3)


def _softplus(x):
    return jnp.maximum(x, 0.0) + jnp.log(1.0 + jnp.exp(-jnp.abs(x)))


def _silu(x):
    return x * jax.nn.sigmoid(x)


def _rms(x, w):
    return x * lax.rsqrt(jnp.mean(x * x, axis=-1, keepdims=True) + RMS_EPS) * w


def _resident(shape):
    return pl.BlockSpec(shape, lambda i: (0,) * len(shape), pipeline_mode=pl.Buffered(1))


def _inproj_kernel(x_ref, nw_ref, w_ref, o_ref, *, gate, tn):
    u = _rms(x_ref[...], nw_ref[...]).astype(BF16)
    for n0 in range(0, o_ref.shape[1], tn):
        z = jnp.dot(u, w_ref[:, n0:n0 + tn], preferred_element_type=F32)
        o_ref[:, n0:n0 + tn] = jax.nn.sigmoid(z).astype(o_ref.dtype) if gate else z


def _inproj(x2d, norm_w, w_bf16, tm, tn, gate):
    t, d = x2d.shape
    n = w_bf16.shape[1]
    return pl.pallas_call(
        functools.partial(_inproj_kernel, gate=gate, tn=tn),
        grid=(t // tm,),
        in_specs=[pl.BlockSpec((tm, d), lambda i: (i, 0)),
                  pl.BlockSpec((1, d), lambda i: (0, 0)),
                  _resident((d, n))],
        out_specs=pl.BlockSpec((tm, n), lambda i: (i, 0)),
        out_shape=jax.ShapeDtypeStruct((t, n), BF16 if gate else F32),
        compiler_params=_cparams("parallel"),
        name="inproj_gate" if gate else "inproj",
    )(x2d, norm_w.reshape(1, d), w_bf16)


def _ffn_in_kernel(x_ref, nw_ref, w_ref, o_ref, *, tn):
    ff = o_ref.shape[1]
    u = _rms(x_ref[...], nw_ref[...]).astype(BF16)
    for n0 in range(0, ff, tn):
        gate = jnp.dot(u, w_ref[:, n0:n0 + tn], preferred_element_type=F32)
        up = jnp.dot(u, w_ref[:, ff + n0:ff + n0 + tn], preferred_element_type=F32)
        o_ref[:, n0:n0 + tn] = (_silu(gate) * up).astype(BF16)


def _ffn_in(x2d, norm_w, w_bf16, tm, tn):
    t, d = x2d.shape
    ff = w_bf16.shape[1] // 2
    return pl.pallas_call(
        functools.partial(_ffn_in_kernel, tn=tn),
        grid=(t // tm,),
        in_specs=[pl.BlockSpec((tm, d), lambda i: (i, 0)),
                  pl.BlockSpec((1, d), lambda i: (0, 0)),
                  _resident((d, 2 * ff))],
        out_specs=pl.BlockSpec((tm, ff), lambda i: (i, 0)),
        out_shape=jax.ShapeDtypeStruct((t, ff), BF16),
        compiler_params=_cparams("parallel"),
        name="ffn_in",
    )(x2d, norm_w.reshape(1, d), w_bf16)


def _ffn_out_kernel(a_ref, w_ref, x_ref, fw_ref, o_ref, *, final_norm):
    y = x_ref[...] + jnp.dot(a_ref[...], w_ref[...], preferred_element_type=F32)
    if final_norm:
        y = _rms(y, fw_ref[...])
    o_ref[...] = y


def _ffn_out(act, w_bf16, x2d, final_w, final_norm, tm):
    t, d = x2d.shape
    ff = act.shape[1]
    return pl.pallas_call(
        functools.partial(_ffn_out_kernel, final_norm=final_norm),
        grid=(t // tm,),
        in_specs=[pl.BlockSpec((tm, ff), lambda i: (i, 0)),
                  _resident((ff, d)),
                  pl.BlockSpec((tm, d), lambda i: (i, 0)),
                  pl.BlockSpec((1, d), lambda i: (0, 0))],
        out_specs=pl.BlockSpec((tm, d), lambda i: (i, 0)),
        out_shape=jax.ShapeDtypeStruct((t, d), F32),
        compiler_params=_cparams("parallel"),
        name="ffn_out",
    )(act, w_bf16, x2d, final_w.reshape(1, d))


def _merge_kernel(yh_ref, yr_ref, ys_ref, ym_ref, zg_ref, wb_ref, wo_ref, x_ref, o_ref):
    d = x_ref.shape[1]
    acc = jnp.zeros(x_ref.shape, F32)
    for k, y_ref in enumerate((yh_ref, yr_ref, ys_ref, ym_ref)):
        proj = jnp.dot(y_ref[...].astype(BF16), wb_ref[k], preferred_element_type=F32)
        acc = acc + zg_ref[:, k * d:(k + 1) * d].astype(F32) * proj
    o_ref[...] = x_ref[...] + jnp.dot(acc.astype(BF16), wo_ref[...], preferred_element_type=F32)


def _merge(ys, z2d, wb_bf16, wo_bf16, x2d, tm):
    t, d = x2d.shape
    w = ys[0].shape[1]
    yspec = pl.BlockSpec((tm, w), lambda i: (i, 0))
    return pl.pallas_call(
        _merge_kernel,
        grid=(t // tm,),
        in_specs=[yspec, yspec, yspec, yspec,
                  pl.BlockSpec((tm, N_BRANCH * d), lambda i: (i, 0)),
                  _resident((N_BRANCH, w, d)),
                  _resident((d, d)),
                  pl.BlockSpec((tm, d), lambda i: (i, 0))],
        out_specs=pl.BlockSpec((tm, d), lambda i: (i, 0)),
        out_shape=jax.ShapeDtypeStruct((t, d), F32),
        compiler_params=_cparams("parallel"),
        name="merge",
    )(*ys, z2d, wb_bf16, wo_bf16, x2d)


def _hg_bounds_kernel(h_ref, o_ref):
    h = h_ref[...]
    depth = h.shape[0]
    m = jnp.max(h, axis=0, keepdims=True)
    e = jnp.exp(h - m)
    p = e / jnp.sum(e, axis=0, keepdims=True)
    run = jnp.zeros_like(p[0:1])
    rows = []
    for l in range(depth):
        run = run + p[l:l + 1]
        rows.append(run - p[0:1])
    o_ref[...] = jnp.concatenate(rows, axis=0)


def _hg_bounds(hgrn_lower_bounds):
    return pl.pallas_call(
        _hg_bounds_kernel,
        out_shape=jax.ShapeDtypeStruct(hgrn_lower_bounds.shape, F32),
        name="hg_bounds",
    )(hgrn_lower_bounds)


def _hg_chunk_kernel(q_ref, f_ref, i_ref, g_ref, lb_ref, nw_ref, tril_ref, after_ref, y_ref,
                     st_ref, kbuf, bbuf, vbuf, *, tb, c, cs, heads, dk):
    sub = V7X_SUBLANES
    nt = (((1,), (1,)), ((), ()))

    @pl.when(pl.program_id(1) == 0)
    def _():
        st_ref[...] = jnp.zeros(st_ref.shape, F32)

    kbuf[0:sub, :] = jnp.zeros((sub, kbuf.shape[1]), F32)
    bbuf[0:sub, :] = jnp.zeros((sub, bbuf.shape[1]), F32)
    vbuf[0:sub, :] = jnp.zeros((sub, vbuf.shape[1]), F32)
    lb = lb_ref[...]
    tril = tril_ref[...]
    rid = lax.broadcasted_iota(jnp.int32, (c, dk), 0)
    rid1 = lax.broadcasted_iota(jnp.int32, (c, 1), 0)
    pr = lax.broadcasted_iota(jnp.int32, (c, c), 0)
    pc = lax.broadcasted_iota(jnp.int32, (c, c), 1)

    def chunk(ci):
        rows = pl.ds(ci * c, c)
        ff = f_ref[0, rows, :]
        q = _silu(q_ref[0, rows, :])
        dec = jnp.maximum(lb + (1.0 - lb) * jax.nn.sigmoid(ff), HG_TINY)
        k = (1.0 - lb) * jax.nn.sigmoid(-ff)
        v = i_ref[0, rows, :]
        b = _dot_exact_lhs(tril, jnp.log(dec))
        base = sub + ci * c
        kbuf[base:base + c, :] = k
        bbuf[base:base + c, :] = b
        vbuf[base:base + c, :] = v
        outs = []
        for h in range(heads):
            hs = slice(h * dk, (h + 1) * dk)
            qh, kh, bh, vh = q[:, hs], k[:, hs], b[:, hs], v[:, hs]
            blast = bh[c - 1:c, :]
            st = st_ref[h]
            o = lax.dot_general((qh * jnp.exp(bh)).astype(BF16), st.astype(BF16), nt,
                                preferred_element_type=F32)
            att = jnp.zeros((c, c), F32)
            grp = 2 * cs
            while grp <= c:
                half = grp // 2
                qparts, kparts = [], []
                for r0 in range(0, c, grp):
                    d = bh[r0:r0 + grp, :] - bh[r0 + half - 1:r0 + half, :]
                    if half % sub == 0:
                        zero = jnp.zeros((half, dk), F32)
                        kparts += [kh[r0:r0 + half, :] * jnp.exp(-d[0:half, :]), zero]
                        qparts += [zero, qh[r0 + half:r0 + grp, :] * jnp.exp(d[half:grp, :])]
                    else:
                        upper = (rid[0:grp, :] % grp) >= half
                        e = jnp.exp(jnp.where(upper, d, -d))
                        kparts.append(jnp.where(upper, 0.0, kh[r0:r0 + grp, :] * e))
                        qparts.append(jnp.where(upper, qh[r0:r0 + grp, :] * e, 0.0))
                a = lax.dot_general(jnp.concatenate(qparts, axis=0).astype(BF16),
                                    jnp.concatenate(kparts, axis=0).astype(BF16), nt,
                                    preferred_element_type=F32)
                att = att + (a if grp == c else jnp.where(pr // grp == pc // grp, a, 0.0))
                grp *= 2
            o = o + _bdot(att, vh)
            for dlt in range(cs):
                if dlt == 0:
                    a = jnp.sum(qh * kh, axis=-1, keepdims=True)
                    o = o + a * vh
                else:
                    win = pl.ds(base - dlt, c)
                    e = jnp.exp(jnp.minimum(bh - bbuf[win, hs], 0.0))
                    a = jnp.sum(qh * kbuf[win, hs] * e, axis=-1, keepdims=True)
                    a = jnp.where((rid1 % cs) >= dlt, a, 0.0)
                    o = o + a * vbuf[win, hs]
            kd = kh * jnp.exp(blast - bh)
            st_ref[h] = st * jnp.exp(blast) + _bdot(vh.T, kd)
            outs.append(o * lax.rsqrt(jnp.mean(o * o, axis=-1, keepdims=True) + RMS_EPS))
        y = jnp.concatenate(outs, axis=-1) * nw_ref[...]
        y_ref[0, rows, :] = (y * _silu(g_ref[0, rows, :])).astype(y_ref.dtype)

    for ci in range(tb // c):
        chunk(ci)


def _hgrn2(z3d, lb, norm_w, w, col0, tb, c, cs, after):
    bsz, seq, _ = z3d.shape
    heads = w // HG_DK
    cb = col0 // w
    tril = (jnp.arange(c)[:, None] >= jnp.arange(c)[None, :]).astype(BF16)
    zspec = lambda j: pl.BlockSpec((1, tb, w), lambda b, i: (b, i, cb + j))
    row = pl.BlockSpec((1, w), lambda b, i: (0, 0))
    return pl.pallas_call(
        functools.partial(_hg_chunk_kernel, tb=tb, c=c, cs=cs, heads=heads, dk=HG_DK),
        grid=(bsz, seq // tb),
        in_specs=[zspec(0), zspec(1), zspec(2), zspec(3), row, row,
                  pl.BlockSpec((c, c), lambda b, i: (0, 0)),
                  pl.BlockSpec((1, V7X_SUBLANES, V7X_LANES), lambda b, i: (0, 0, 0))],
        out_specs=pl.BlockSpec((1, tb, w), lambda b, i: (b, i, 0)),
        out_shape=jax.ShapeDtypeStruct((bsz, seq, w), BF16),
        scratch_shapes=[pltpu.VMEM((heads, HG_DK, HG_DK), F32)]
        + [pltpu.VMEM((tb + V7X_SUBLANES, w), F32)] * 3,
        compiler_params=_cparams("parallel", "arbitrary"),
        name="hgrn2",
    )(z3d, z3d, z3d, z3d, lb.reshape(1, w), norm_w.reshape(1, w), tril, after)


def _lanes_to_chains_v(a, bsz, seq, heads, dv, nq):
    a = a.reshape(seq, dv // nq, nq, bsz, heads).transpose(3, 0, 4, 2, 1)
    return a.reshape(bsz * seq, heads * dv)


def _shift_rows(cur, prev_last, first):
    rolled = pltpu.roll(cur, 1, axis=0)
    row0 = jnp.where(first, 0.0, prev_last)
    rid = lax.broadcasted_iota(jnp.int32, cur.shape, 0)
    return jnp.where(rid == 0, jnp.broadcast_to(row0, cur.shape), rolled)


def _rw_prep_kernel(zm_ref, zmp_ref, zl_ref, zlp_ref, vf_ref,
                    mum_ref, mul_ref, w0_ref, w2_ref, a0_ref, a2_ref, g2_ref,
                    kk_ref, ka_ref, rk_ref, v0_ref, v1_ref, v2_ref, ones_ref,
                    kvec_out, v_out, g_out, bonus_out,
                    *, w, tiles_per_seq, has_vres):
    first = (pl.program_id(0) % tiles_per_seq) == 0
    sub = V7X_SUBLANES
    zm = zm_ref[...]
    zl = zl_ref[...]
    zms = zm + (_shift_rows(zm, zmp_ref[sub - 1:sub, :], first) - zm) * mum_ref[...]
    zls = zl + (_shift_rows(zl, zlp_ref[sub - 1:sub, :], first) - zl) * mul_ref[...]
    r = zms[:, 0:w]
    k = zms[:, w:2 * w]
    v = zms[:, 2 * w:3 * w]
    w_log = -_softplus(-(w0_ref[...] + _bdot(jnp.tanh(zls), w2_ref[...]))) - 0.5
    decay = jnp.exp(-jnp.exp(w_log))
    if has_vres:
        mix = jax.nn.sigmoid(v0_ref[...] + _bdot(_bdot(v, v1_ref[...]), v2_ref[...]))
        v = v + (vf_ref[...] - v) * mix
    a = jax.nn.sigmoid(a0_ref[...] + _bdot(zls, a2_ref[...]))
    g = _bdot(jax.nn.sigmoid(zls), g2_ref[...])
    ones = ones_ref[...]
    kk = k * kk_ref[...]
    ss = _dot_exact_rhs(kk * kk, ones)
    kk = kk / jnp.maximum(jnp.sqrt(ss), 1e-12)
    k2 = k * (1.0 + (a - 1.0) * ka_ref[...])
    vecs = (r, decay, k2, -kk, kk * a, v)
    for h in range(w // RW_HEAD):
        hs = slice(h * RW_HEAD, (h + 1) * RW_HEAD)
        row = jnp.concatenate([x[:, hs] for x in vecs], axis=1)
        for q in range(kvec_out.shape[0]):
            kvec_out[q, 0, h] = row
    v_out[...] = v
    g_out[...] = g
    bonus_out[...] = _dot_exact_rhs(r * k2 * rk_ref[...], ones) * v


def _rw_prep(z2d, v_first, p, col_main, col_lora, w, lw, seq, tm, has_vres):
    t = z2d.shape[0]
    sub = V7X_SUBLANES
    mb = col_main // (3 * w)
    lbk = col_lora // lw
    rows8 = tm // sub
    heads = w // RW_HEAD
    tps = seq // tm
    nq = V7X_LANES // (t // seq * heads)

    def prev_idx(i):
        return jnp.maximum(i * rows8 - 1, 0)

    row = lambda n: pl.BlockSpec((1, n), lambda i: (0, 0))
    full = lambda a: pl.BlockSpec(a.shape, lambda i: (0,) * a.ndim)
    ospec = pl.BlockSpec((tm, w), lambda i: (i, 0))
    oshape = jax.ShapeDtypeStruct((t, w), F32)
    args = [z2d, z2d, z2d, z2d, v_first,
            p["mu_main"], p["mu_lora"], p["w0"], p["w2p"], p["a0"], p["a2p"], p["g2p"],
            p["k_k"], p["k_a"], p["r_k"], p["v0"], p["v1p"], p["v2p"], p["ones"]]
    in_specs = [pl.BlockSpec((tm, 3 * w), lambda i: (i, mb)),
                pl.BlockSpec((sub, 3 * w), lambda i: (prev_idx(i), mb)),
                pl.BlockSpec((tm, lw), lambda i: (i, lbk)),
                pl.BlockSpec((sub, lw), lambda i: (prev_idx(i), lbk)),
                ospec,
                row(3 * w), row(lw), row(w), full(p["w2p"]), row(w), full(p["a2p"]), full(p["g2p"]),
                row(w), row(w), row(w), row(w), full(p["v1p"]), full(p["v2p"]), full(p["ones"])]
    return pl.pallas_call(
        functools.partial(_rw_prep_kernel, w=w, tiles_per_seq=seq // tm, has_vres=has_vres),
        grid=(t // tm,),
        in_specs=in_specs,
        out_specs=[pl.BlockSpec((nq, 1, heads, tm, RW_NVEC * RW_HEAD),
                                lambda i: (0, i // tps, 0, i % tps, 0)), ospec, ospec, ospec],
        out_shape=[jax.ShapeDtypeStruct((nq, t // seq, heads, seq, RW_NVEC * RW_HEAD), F32),
                   oshape, oshape, oshape],
        compiler_params=_cparams("parallel"),
        name="rw_prep",
    )(*args)


def _rw_scan_kernel(x_ref, *rest, tc, dk, nvb, nq):
    y_ref, s_ref = rest[-2:]
    sub = V7X_SUBLANES
    lanes = V7X_LANES
    jr, jw, jk, ja, jb, jv = range(RW_NVEC)
    vl = nvb * sub
    qid = lax.broadcasted_iota(jnp.int32, (sub, lanes), 1) // (lanes // nq)

    @pl.when(pl.program_id(0) == 0)
    def _():
        s_ref[...] = jnp.zeros(s_ref.shape, F32)

    def bc(j, t, kk):
        return jnp.broadcast_to(x_ref[t, j, pl.ds(kk, 1), :], (sub, lanes))

    sa0 = [jnp.zeros((sub, lanes), F32) for _ in range(nvb)]
    for kk in range(dk):
        arow = bc(ja, 0, kk)
        for j in range(nvb):
            sa0[j] = sa0[j] + s_ref[kk, pl.ds(sub * j, sub), :] * arow

    def step(t, sa):
        tn = jnp.minimum(t + 1, tc - 1)
        vb = []
        for j in range(nvb):
            vj = x_ref[t, jv, pl.ds(sub * j, sub), :]
            for q in range(1, nq):
                vj = jnp.where(qid == q, x_ref[t, jv, pl.ds(q * vl + sub * j, sub), :], vj)
            vb.append(vj)
        yacc = [jnp.zeros((sub, lanes), F32) for _ in range(nvb)]
        sacc = [jnp.zeros((sub, lanes), F32) for _ in range(nvb)]
        for kk in range(dk):
            wrow = bc(jw, t, kk)
            brow = bc(jb, t, kk)
            krow = bc(jk, t, kk)
            rrow = bc(jr, t, kk)
            anext = bc(ja, tn, kk)
            for j in range(nvb):
                s = s_ref[kk, pl.ds(sub * j, sub), :] * wrow + sa[j] * brow + vb[j] * krow
                s_ref[kk, pl.ds(sub * j, sub), :] = s
                yacc[j] = yacc[j] + s * rrow
                sacc[j] = sacc[j] + s * anext
        for j in range(nvb):
            y_ref[t, pl.ds(sub * j, sub), :] = yacc[j]
        return tuple(sacc)

    lax.fori_loop(0, tc, step, tuple(sa0))


def _rw_scan(x_l, nq, tc, after):
    s, nvec, dk, lanes = x_l.shape
    vl = dk // nq
    kspec = pl.BlockSpec((tc, nvec, dk, lanes), lambda i: (i, 0, 0, 0))
    vspec = pl.BlockSpec((tc, vl, lanes), lambda i: (i, 0, 0))
    order = [pl.BlockSpec((16, lanes), lambda i: (0, 0))] * len(after)
    return pl.pallas_call(
        functools.partial(_rw_scan_kernel, tc=tc, dk=dk, nvb=vl // V7X_SUBLANES, nq=nq),
        grid=(s // tc,),
        in_specs=[kspec] + order,
        out_specs=vspec,
        out_shape=jax.ShapeDtypeStruct((s, vl, lanes), F32),
        scratch_shapes=[pltpu.VMEM((dk, vl, lanes), F32)],
        compiler_params=_cparams("arbitrary"),
        name="rw_scan",
    )(x_l, *after)


def _rw_post_kernel(y_ref, g_ref, bonus_ref, lnw_ref, lnb_ref, ones_ref, o_ref):
    y = y_ref[...]
    ones = ones_ref[...]
    inv_n = 1.0 / RW_HEAD
    mean = _dot_exact_rhs(y, ones) * inv_n
    yc = y - mean
    var = _dot_exact_rhs(yc * yc, ones) * inv_n
    y = yc * lax.rsqrt(var + RW_LN_EPS) * lnw_ref[...] + lnb_ref[...]
    o_ref[...] = ((y + bonus_ref[...]) * g_ref[...]).astype(o_ref.dtype)


def _rw_post(y2d, g, bonus, ln_w, ln_b, ones, tm):
    t, w = y2d.shape
    tspec = pl.BlockSpec((tm, w), lambda i: (i, 0))
    row = pl.BlockSpec((1, w), lambda i: (0, 0))
    return pl.pallas_call(
        _rw_post_kernel,
        grid=(t // tm,),
        in_specs=[tspec, tspec, tspec, row, row, pl.BlockSpec((w, w), lambda i: (0, 0))],
        out_specs=tspec,
        out_shape=jax.ShapeDtypeStruct((t, w), BF16),
        compiler_params=_cparams("parallel"),
        name="rw_post",
    )(y2d, g, bonus, ln_w.reshape(1, w), ln_b.reshape(1, w), ones)


def _rwkv7_front(z2d, v_first, p, bsz, seq, w, col_main, col_lora, lw, tm, has_vres):
    kvec, v, g, bonus = _rw_prep(z2d, v_first, p, col_main, col_lora, w, lw, seq, tm, has_vres)
    lanes = kvec.shape[0] * kvec.shape[1] * kvec.shape[2]
    x_l = kvec.reshape(lanes, seq, RW_NVEC * RW_HEAD).transpose(1, 2, 0)
    return x_l.reshape(seq, RW_NVEC, RW_HEAD, lanes), v, g, bonus


def _rwkv7_back(y_l, g, bonus, p, bsz, seq, w, tm):
    heads = w // RW_HEAD
    nq = V7X_LANES // (bsz * heads)
    y = _lanes_to_chains_v(y_l, bsz, seq, heads, RW_HEAD, nq)
    return _rw_post(y, g, bonus, p["ln_w"], p["ln_b"], p["ones"], tm)


def _s5_disc_kernel(are_ref, aim_ref, dt_ref, bre_ref, bim_ref, lre_ref, lim_ref, bbre_ref, bbim_ref):
    a_re = are_ref[...]
    a_im = aim_ref[...]
    dt = jnp.exp(dt_ref[...])
    mag = jnp.exp(dt * a_re)
    lam_re = mag * jnp.cos(dt * a_im)
    lam_im = mag * jnp.sin(dt * a_im)
    den = a_re * a_re + a_im * a_im
    coef_re = ((lam_re - 1.0) * a_re + lam_im * a_im) / den
    coef_im = (lam_im * a_re - (lam_re - 1.0) * a_im) / den
    b_re = bre_ref[...]
    b_im = bim_ref[...]
    lre_ref[...] = lam_re
    lim_ref[...] = lam_im
    bbre_ref[...] = coef_re * b_re - coef_im * b_im
    bbim_ref[...] = coef_re * b_im + coef_im * b_re


def _s5_discretize(a_re, a_im, log_dt, b_re, b_im):
    g, n, c = b_re.shape
    shp = (g, n * c)
    bc = lambda a: jnp.broadcast_to(a[..., None], (g, n, c)).reshape(shp)
    dtb = jnp.broadcast_to(log_dt[:, None], shp)
    o = jax.ShapeDtypeStruct(shp, F32)
    lre, lim, bbre, bbim = pl.pallas_call(
        _s5_disc_kernel, out_shape=[o, o, o, o], name="s5_disc",
    )(bc(a_re), bc(a_im), dtb, b_re.reshape(shp), b_im.reshape(shp))
    un = lambda a: a.reshape(g, n, c)
    return un(lre)[..., 0], un(lim)[..., 0], un(bbre), un(bbim)


def _s5_kernel(u_ref, bblk_ref, cblk_ref, lre_ref, lim_ref, d_ref, wg_ref, bg_ref, y_ref,
               h_ref, hr_ref, hi_ref, *, tc, bsz, ns, lane_chunk):
    @pl.when(pl.program_id(0) == 0)
    def _():
        hr_ref[...] = jnp.zeros(hr_ref.shape, F32)
        hi_ref[...] = jnp.zeros(hi_ref.shape, F32)

    u = u_ref[...]
    nsg = bblk_ref.shape[0]
    lanes = V7X_LANES
    for sg in range(nsg):
        c0 = sg * lane_chunk
        drive = _bdot(u[:, sg * lanes:(sg + 1) * lanes], bblk_ref[sg])
        h_ref[:, c0:c0 + lane_chunk] = drive[:, 0:lane_chunk]
        h_ref[:, ns + c0:ns + c0 + lane_chunk] = drive[:, lane_chunk:2 * lane_chunk]
    for c0 in range(0, ns, lane_chunk):
        lr = jnp.broadcast_to(lre_ref[:, c0:c0 + lane_chunk], (bsz, lane_chunk))
        li = jnp.broadcast_to(lim_ref[:, c0:c0 + lane_chunk], (bsz, lane_chunk))

        def step(t, carry, c0=c0, lr=lr, li=li):
            hr, hi = carry
            rows = pl.ds(pl.multiple_of(t * bsz, bsz), bsz)
            nr = lr * hr - li * hi + h_ref[rows, c0:c0 + lane_chunk]
            ni = lr * hi + li * hr + h_ref[rows, ns + c0:ns + c0 + lane_chunk]
            h_ref[rows, c0:c0 + lane_chunk] = nr
            h_ref[rows, ns + c0:ns + c0 + lane_chunk] = ni
            return nr, ni

        hr, hi = lax.fori_loop(0, tc, step,
                               (hr_ref[:, c0:c0 + lane_chunk], hi_ref[:, c0:c0 + lane_chunk]))
        hr_ref[:, c0:c0 + lane_chunk] = hr
        hi_ref[:, c0:c0 + lane_chunk] = hi
    outs = []
    for sg in range(nsg):
        c0 = sg * lane_chunk
        outs.append(_bdot(h_ref[:, c0:c0 + lane_chunk], cblk_ref[sg, 0:lane_chunk, :])
                    + _bdot(h_ref[:, ns + c0:ns + c0 + lane_chunk], cblk_ref[sg, lane_chunk:, :]))
    y = jnp.concatenate(outs, axis=-1) + d_ref[...] * u
    y = jax.nn.gelu(y)
    y_ref[...] = (y * jax.nn.sigmoid(_bdot(y, wg_ref[...]) + bg_ref[...])).astype(y_ref.dtype)


def _s5(u_tm, p, bsz, tc):
    rows, w = u_tm.shape
    ns = p["lam_re"].shape[1]
    blk = tc * bsz
    full = lambda a: pl.BlockSpec(a.shape, lambda i: (0,) * a.ndim)
    return pl.pallas_call(
        functools.partial(_s5_kernel, tc=tc, bsz=bsz, ns=ns, lane_chunk=ns // p["bblk"].shape[0]),
        grid=(rows // blk,),
        in_specs=[pl.BlockSpec((blk, w), lambda i: (i, 0)),
                  full(p["bblk"]), full(p["cblk"]), full(p["lam_re"]), full(p["lam_im"]),
                  full(p["d"]), full(p["w_glu"]), full(p["b_glu"])],
        out_specs=pl.BlockSpec((blk, w), lambda i: (i, 0)),
        out_shape=jax.ShapeDtypeStruct((rows, w), BF16),
        scratch_shapes=[pltpu.VMEM((blk, 2 * ns), F32),
                        pltpu.VMEM((bsz, ns), F32), pltpu.VMEM((bsz, ns), F32)],
        compiler_params=_cparams("arbitrary"),
        name="s5",
    )(u_tm, p["bblk"], p["cblk"], p["lam_re"], p["lam_im"], p["d"], p["w_glu"], p["b_glu"])


def _mb_kernel(gate_ref, x_ref, bc_ref, dt_ref, cw_ref, cb_ref, dtb_ref, alog_ref, dsk_ref, nw_ref,
               expand_ref, tril_ref, y_ref, prev_ref, st_ref, *, q, w, heads, groups, nstate):
    sub = V7X_SUBLANES
    lanes = V7X_LANES
    hd = w // heads
    gw = w // groups
    hpg = heads // groups

    @pl.when(pl.program_id(1) == 0)
    def _():
        prev_ref[...] = jnp.zeros(prev_ref.shape, F32)
        st_ref[...] = jnp.zeros(st_ref.shape, F32)

    xbc = jnp.concatenate([x_ref[0], bc_ref[0]], axis=-1)
    full = jnp.concatenate([prev_ref[...], xbc], axis=0)
    conv = jnp.broadcast_to(cb_ref[...], xbc.shape)
    for j in range(MB_CONV):
        shift = MB_CONV - 1 - j
        src = full if shift == 0 else pltpu.roll(full, shift, axis=0)
        conv = conv + src[sub:sub + q, :] * cw_ref[j:j + 1, :]
    prev_ref[...] = xbc[q - sub:q, :]
    act = _silu(conv)
    xs = act[:, 0:w]
    bmat = act[:, w:w + groups * nstate]
    cmat = act[:, w + groups * nstate:w + 2 * groups * nstate]

    dt = _softplus(dt_ref[0] + dtb_ref[...])
    a = -jnp.exp(alog_ref[...]) * dt
    a_cum = _dot_exact_lhs(tril_ref[...], a)
    expand = expand_ref[...]
    dt_e = _dot_exact_rhs(dt, expand)
    acum_e = _dot_exact_rhs(a_cum, expand)
    alast_e = acum_e[q - 1:q, :]
    xdt = xs * dt_e
    xdec = xdt * jnp.exp(alast_e - acum_e)

    rid = lax.broadcasted_iota(jnp.int32, (q, q), 0)
    cid = lax.broadcasted_iota(jnp.int32, (q, q), 1)
    causal = rid >= cid
    a_cum_t = a_cum.T
    lane_w = lax.broadcasted_iota(jnp.int32, (q, gw), 1)

    y_parts = []
    for g in range(groups):
        bg = bmat[:, g * nstate:(g + 1) * nstate]
        cg = cmat[:, g * nstate:(g + 1) * nstate]
        scores = lax.dot_general(cg.astype(BF16), bg.astype(BF16), (((1,), (1,)), ((), ())),
                                 preferred_element_type=F32)
        xg = xdt[:, g * gw:(g + 1) * gw]
        yg = _bdot(cg, st_ref[g]) * jnp.exp(acum_e[:, g * gw:(g + 1) * gw])
        for hh in range(hpg):
            h = g * hpg + hh
            col = jnp.broadcast_to(a_cum[:, h:h + 1], (q, q))
            rowv = jnp.broadcast_to(a_cum_t[h:h + 1, :], (q, q))
            decay = jnp.where(causal, jnp.exp(col - rowv), 0.0)
            xh = jnp.where((lane_w >= hh * hd) & (lane_w < (hh + 1) * hd), xg, 0.0)
            yg = yg + _bdot(scores * decay, xh)
        y_parts.append(yg)
        upd = _bdot(bg.T, xdec[:, g * gw:(g + 1) * gw])
        st_ref[g] = st_ref[g] * jnp.exp(alast_e[:, g * gw:(g + 1) * gw]) + upd
    y = jnp.concatenate(y_parts, axis=-1) + dsk_ref[...] * xs
    y_ref[0] = _rms(y * _silu(gate_ref[0]), nw_ref[...]).astype(y_ref.dtype)


def _mamba2(z3d, p, w, cols, q):
    bsz, seq, _ = z3d.shape
    heads = w // MB_HEADDIM
    lanes = V7X_LANES
    cg, cx, cbc, cdt = cols
    blk = lambda width, col: pl.BlockSpec((1, q, width), lambda b, c: (b, c, col // width))
    full = lambda a: pl.BlockSpec(a.shape, lambda b, c: (0,) * a.ndim)
    consts = [p["conv_w"], p["conv_b"], p["dt_bias"], p["a_log"], p["d"], p["norm_w"],
              p["expand"], p["tril"]]
    return pl.pallas_call(
        functools.partial(_mb_kernel, q=q, w=w, heads=heads, groups=MB_GROUPS, nstate=MB_STATE),
        grid=(bsz, seq // q),
        in_specs=[blk(w, cg), blk(w, cx), blk(w, cbc), blk(lanes, cdt)] + [full(a) for a in consts],
        out_specs=pl.BlockSpec((1, q, w), lambda b, c: (b, c, 0)),
        out_shape=jax.ShapeDtypeStruct((bsz, seq, w), BF16),
        scratch_shapes=[pltpu.VMEM((V7X_SUBLANES, 2 * w), F32),
                        pltpu.VMEM((MB_GROUPS, MB_STATE, w // MB_GROUPS), F32)],
        compiler_params=_cparams("parallel", "arbitrary"),
        name="mamba2",
    )(z3d, z3d, z3d, z3d, *consts)


def _pad_rows(a, rows, at):
    out = jnp.zeros((rows, a.shape[1]), a.dtype)
    return lax.dynamic_update_slice(out, a, (at, 0))


def _block_ones(w, head):
    idx = jnp.arange(w) // head
    return (idx[:, None] == idx[None, :]).astype(BF16)


def _pick_tile(n, target):
    t = min(n, target)
    while n % t:
        t //= 2
    return t


def kernel(x, norm_mix_w, w_in, w_branch, w_out, norm_ffn_w, w_ffn_in, w_ffn_out, norm_final_w, hgrn_lower_bounds, hgrn_norm_w, rwkv_mu, rwkv_w0, rwkv_w2, rwkv_a0, rwkv_a2, rwkv_g2, rwkv_k_k, rwkv_k_a, rwkv_r_k, rwkv_ln_w, rwkv_ln_b, rwkv_v0, rwkv_v1, rwkv_v2, s5_a_re, s5_a_im, s5_b_re, s5_b_im, s5_c_re, s5_c_im, s5_d, s5_log_dt, s5_w_glu, s5_b_glu, mamba_conv_w, mamba_conv_b, mamba_dt_bias, mamba_a_log, mamba_d, mamba_norm_w):
    bsz, seq, d = x.shape
    depth = w_in.shape[0]
    w = d // 2
    lanes = V7X_LANES
    t = bsz * seq
    mb_heads = w // MB_HEADDIM
    mb_bc = 2 * MB_GROUPS * MB_STATE
    lw = RW_DECAY_LORA + RW_A_LORA + RW_G_LORA
    assert mb_bc == w and 3 * w % lw == 0

    o_gate = 0
    o_hg = o_gate + N_BRANCH * d
    o_rw = o_hg + 4 * w
    o_rwl = o_rw + 3 * w
    o_s5 = o_rwl + lw
    o_mbg = o_s5 + w
    o_mbx = o_mbg + w
    o_mbbc = o_mbx + w
    o_mbdt = o_mbbc + mb_bc
    c_rw = 0
    c_hg = c_rw + 3 * w
    c_s5 = c_hg + 4 * w
    c_mbg = c_s5 + w
    c_mbx = c_mbg + w
    c_mbbc = c_mbx + w
    c_rwl = c_mbbc + mb_bc
    c_mbdt = c_rwl + lw
    n_cols = c_mbdt + lanes
    tn = 512
    n_pad = -(-n_cols // tn) * tn

    def mixer_cols(wl):
        sl = lambda a, b: wl[:, a:b]
        pieces = [sl(o_rw, o_rwl), sl(o_hg, o_rw), sl(o_s5, o_mbdt), sl(o_rwl, o_s5),
                  sl(o_mbdt, o_mbdt + mb_heads),
                  jnp.zeros((d, n_pad - c_mbdt - mb_heads), wl.dtype)]
        return jnp.concatenate(pieces, axis=1).astype(BF16)

    tm = _pick_tile(seq, 256)
    tm_in = _pick_tile(seq, 512)
    tb_hg = _pick_tile(seq, 512)
    c_hg_chunk = min(HG_CHUNK, tb_hg)
    tc_rw = _pick_tile(seq, 32)
    tc_s5 = _pick_tile(seq, 64)
    q_mb = min(MB_CHUNK, seq)
    ff = w_ffn_out.shape[1]
    tn_ff = 256 if ff % 256 == 0 else lanes

    lower_bounds = _hg_bounds(hgrn_lower_bounds)
    ones_rw = _block_ones(w, RW_HEAD)
    eye_g = jnp.eye(lanes // S5_GROUP, dtype=F32)
    expand = (jnp.arange(lanes)[:, None] == (jnp.arange(w) // MB_HEADDIM)[None, :]).astype(BF16)
    tril = (jnp.arange(q_mb)[:, None] >= jnp.arange(q_mb)[None, :]).astype(BF16)

    x2d = x.reshape(t, d)
    v_first = jnp.zeros((t, w), F32)
    for l in range(depth):
        gates = _inproj(x2d, norm_mix_w[l], w_in[l][:, o_gate:o_hg].astype(BF16), tm_in, tn, True)
        z2d = _inproj(x2d, norm_mix_w[l], mixer_cols(w_in[l]), tm_in, tn, False)
        z3d = z2d.reshape(bsz, seq, n_pad)

        mu = rwkv_mu[l]
        has_vres = l > 0
        lv = max(l - 1, 0)
        rp = {
            "mu_main": mu[None, :3 * w], "mu_lora": mu[None, 3 * w:],
            "w0": rwkv_w0[l][None], "a0": rwkv_a0[l][None],
            "w2p": _pad_rows(rwkv_w2[l], lw, 0).astype(BF16),
            "a2p": _pad_rows(rwkv_a2[l], lw, RW_DECAY_LORA).astype(BF16),
            "g2p": _pad_rows(rwkv_g2[l], lw, RW_DECAY_LORA + RW_A_LORA).astype(BF16),
            "k_k": rwkv_k_k[l][None], "k_a": rwkv_k_a[l][None], "r_k": rwkv_r_k[l].reshape(1, w),
            "v0": rwkv_v0[lv][None],
            "v1p": jnp.pad(rwkv_v1[lv], ((0, 0), (0, lanes - rwkv_v1.shape[2]))).astype(BF16),
            "v2p": _pad_rows(rwkv_v2[lv], lanes, 0).astype(BF16),
            "ones": ones_rw, "ln_w": rwkv_ln_w[l], "ln_b": rwkv_ln_b[l],
        }
        rw_x, v_l, rw_g, rw_bonus = _rwkv7_front(z2d, v_first, rp, bsz, seq, w, c_rw, c_rwl, lw, tm,
                                                 has_vres)
        if l == 0:
            v_first = v_l

        lam_re, lam_im, bb_re, bb_im = _s5_discretize(s5_a_re[l], s5_a_im[l], s5_log_dt[l],
                                                      s5_b_re[l], s5_b_im[l])
        ns = lam_re.size
        nsg = w // lanes
        gps = lanes // S5_GROUP
        blk_in = lambda bb: jnp.einsum(
            "sgnc,gh->sgchn", bb.reshape(nsg, gps, S5_STATE, S5_GROUP), eye_g
        ).reshape(nsg, lanes, gps * S5_STATE)
        blk_out = lambda cc: jnp.einsum(
            "sgcn,gh->sgnhc", cc.reshape(nsg, gps, S5_GROUP, S5_STATE), eye_g
        ).reshape(nsg, gps * S5_STATE, lanes)
        sp = {
            "bblk": jnp.concatenate([blk_in(bb_re), blk_in(bb_im)], axis=2).astype(BF16),
            "cblk": jnp.concatenate([blk_out(s5_c_re[l]), -blk_out(s5_c_im[l])], axis=1).astype(BF16),
            "lam_re": lam_re.reshape(1, ns), "lam_im": lam_im.reshape(1, ns),
            "d": s5_d[l][None], "w_glu": s5_w_glu[l].astype(BF16), "b_glu": s5_b_glu[l][None],
        }
        u_tm = z3d[:, :, c_s5:c_s5 + w].transpose(1, 0, 2).reshape(t, w)
        y_s5 = _s5(u_tm, sp, bsz, tc_s5).reshape(seq, bsz, w).transpose(1, 0, 2).reshape(t, w)

        mp = {
            "conv_w": mamba_conv_w[l], "conv_b": mamba_conv_b[l][None],
            "dt_bias": jnp.pad(mamba_dt_bias[l], (0, lanes - mb_heads))[None],
            "a_log": jnp.pad(mamba_a_log[l], (0, lanes - mb_heads))[None],
            "d": jnp.repeat(mamba_d[l], MB_HEADDIM)[None], "norm_w": mamba_norm_w[l][None],
            "expand": expand, "tril": tril,
        }
        y_mb = _mamba2(z3d, mp, w, (c_mbg, c_mbx, c_mbbc, c_mbdt), q_mb).reshape(t, w)

        rw_y = _rw_scan(rw_x, lanes // (bsz * (w // RW_HEAD)), tc_rw, (gates, y_s5, y_mb))
        y_hg = _hgrn2(z3d, lower_bounds[l], hgrn_norm_w[l], w, c_hg, tb_hg, c_hg_chunk,
                      HG_SUBBLOCK, rw_y).reshape(t, w)
        y_rw = _rwkv7_back(rw_y, rw_g, rw_bonus, rp, bsz, seq, w, tm)
        x2d = _merge((y_hg, y_rw, y_s5, y_mb), gates, w_branch[l].astype(BF16), w_out[l].astype(BF16),
                     x2d, tm)
        act = _ffn_in(x2d, norm_ffn_w[l], w_ffn_in[l].astype(BF16), tm_in, tn_ff)
        x2d = _ffn_out(act, w_ffn_out[l].astype(BF16), x2d, norm_final_w, l == depth - 1, tm)
    return x2d.reshape(bsz, seq, d)
```

```python
import functools
import math

import jax
import jax.numpy as jnp
from jax import lax
from jax.experimental import pallas as pl
from jax.experimental.pallas import tpu as pltpu

F32 = jnp.float32
BF16 = jnp.bfloat16

V7X_LANES = 128
V7X_SUBLANES = 8
V7X_VMEM_LIMIT_BYTES = 56 * 1024 * 1024

RMS_EPS = 1e-6
N_BRANCH = 4
HG_DK = 128
HG_TINY = 1e-30
HG_CHUNK = 64
HG_SUBBLOCK = 4
RW_HEAD = 64
RW_DECAY_LORA = 64
RW_A_LORA = 64
RW_G_LORA = 128
RW_LN_EPS = 64e-5
RW_NVEC = 6
RW_GATE_STEPS = 2
S5_GROUP = 16
S5_STATE = 64
MB_HEADDIM = 64
MB_GROUPS = 2
MB_STATE = 128
MB_CONV = 4
MB_CHUNK = 128


def _cparams(*sem):
    return pltpu.CompilerParams(dimension_semantics=sem, vmem_limit_bytes=V7X_VMEM_LIMIT_BYTES)


def _bdot(a, b):
    return jnp.dot(a.astype(BF16), b.astype(BF16), preferred_element_type=F32)


def _split3(x):
    x1 = x.astype(BF16)
    r1 = x - x1.astype(F32)
    x2 = r1.astype(BF16)
    x3 = (r1 - x2.astype(F32)).astype(BF16)
    return x1, x2, x3


def _dot_exact_rhs(x, m):
    x1, x2, x3 = _split3(x)
    d = functools.partial(jnp.dot, preferred_element_type=F32)
    return d(x1, m) + d(x2, m) + d(x3, m)


def _dot_exact_lhs(m, x):
    x1, x2, x3 = _split3(x)
    d = functools.partial(jnp.dot, preferred_element_type=F32)
    return d(m, x1) + d(m, x2) + d(m, x3)


def _softplus(x):
    return jnp.maximum(x, 0.0) + jnp.log(1.0 + jnp.exp(-jnp.abs(x)))


def _silu(x):
    return x * jax.nn.sigmoid(x)


def _rms(x, w):
    return x * lax.rsqrt(jnp.mean(x * x, axis=-1, keepdims=True) + RMS_EPS) * w


def _resident(shape):
    return pl.BlockSpec(shape, lambda i: (0,) * len(shape), pipeline_mode=pl.Buffered(1))


def _inproj_kernel(x_ref, nw_ref, w_ref, o_ref, *, tn):
    u = _rms(x_ref[...], nw_ref[...]).astype(BF16)
    for n0 in range(0, o_ref.shape[1], tn):
        o_ref[:, n0:n0 + tn] = jnp.dot(u, w_ref[:, n0:n0 + tn], preferred_element_type=F32)


def _inproj(x2d, norm_w, w_bf16, tm, tn):
    t, d = x2d.shape
    n = w_bf16.shape[1]
    return pl.pallas_call(
        functools.partial(_inproj_kernel, tn=tn),
        grid=(t // tm,),
        in_specs=[pl.BlockSpec((tm, d), lambda i: (i, 0)),
                  pl.BlockSpec((1, d), lambda i: (0, 0)),
                  _resident((d, n))],
        out_specs=pl.BlockSpec((tm, n), lambda i: (i, 0)),
        out_shape=jax.ShapeDtypeStruct((t, n), F32),
        compiler_params=_cparams("parallel"),
        name="inproj",
    )(x2d, norm_w.reshape(1, d), w_bf16)


def _ffn_in_kernel(x_ref, nw_ref, w_ref, o_ref, *, tn):
    ff = o_ref.shape[1]
    u = _rms(x_ref[...], nw_ref[...]).astype(BF16)
    for n0 in range(0, ff, tn):
        gate = jnp.dot(u, w_ref[:, n0:n0 + tn], preferred_element_type=F32)
        up = jnp.dot(u, w_ref[:, ff + n0:ff + n0 + tn], preferred_element_type=F32)
        o_ref[:, n0:n0 + tn] = (_silu(gate) * up).astype(BF16)


def _ffn_in(x2d, norm_w, w_bf16, tm, tn):
    t, d = x2d.shape
    ff = w_bf16.shape[1] // 2
    return pl.pallas_call(
        functools.partial(_ffn_in_kernel, tn=tn),
        grid=(t // tm,),
        in_specs=[pl.BlockSpec((tm, d), lambda i: (i, 0)),
                  pl.BlockSpec((1, d), lambda i: (0, 0)),
                  _resident((d, 2 * ff))],
        out_specs=pl.BlockSpec((tm, ff), lambda i: (i, 0)),
        out_shape=jax.ShapeDtypeStruct((t, ff), BF16),
        compiler_params=_cparams("parallel"),
        name="ffn_in",
    )(x2d, norm_w.reshape(1, d), w_bf16)


def _ffn_out_kernel(a_ref, w_ref, x_ref, fw_ref, o_ref, *, final_norm):
    y = x_ref[...] + jnp.dot(a_ref[...], w_ref[...], preferred_element_type=F32)
    if final_norm:
        y = _rms(y, fw_ref[...])
    o_ref[...] = y


def _ffn_out(act, w_bf16, x2d, final_w, final_norm, tm):
    t, d = x2d.shape
    ff = act.shape[1]
    return pl.pallas_call(
        functools.partial(_ffn_out_kernel, final_norm=final_norm),
        grid=(t // tm,),
        in_specs=[pl.BlockSpec((tm, ff), lambda i: (i, 0)),
                  _resident((ff, d)),
                  pl.BlockSpec((tm, d), lambda i: (i, 0)),
                  pl.BlockSpec((1, d), lambda i: (0, 0))],
        out_specs=pl.BlockSpec((tm, d), lambda i: (i, 0)),
        out_shape=jax.ShapeDtypeStruct((t, d), F32),
        compiler_params=_cparams("parallel"),
        name="ffn_out",
    )(act, w_bf16, x2d, final_w.reshape(1, d))


def _merge_kernel(yh_ref, yr_ref, ys_ref, ym_ref, zg_ref, wb_ref, wo_ref, x_ref, o_ref):
    d = x_ref.shape[1]
    per = d // zg_ref.shape[3]
    acc = jnp.zeros(x_ref.shape, F32)
    for k, y_ref in enumerate((yh_ref, yr_ref, ys_ref, ym_ref)):
        proj = jnp.dot(y_ref[...].astype(BF16), wb_ref[k], preferred_element_type=F32)
        gate = jnp.concatenate([zg_ref[0, k * per + j] for j in range(per)], axis=1)
        acc = acc + gate.astype(F32) * proj
    o_ref[...] = x_ref[...] + jnp.dot(acc.astype(BF16), wo_ref[...], preferred_element_type=F32)


def _merge(ys, gates4, wb_bf16, wo_bf16, x2d):
    t, d = x2d.shape
    w = ys[0].shape[1]
    _, nsl, tm, gtn = gates4.shape
    yspec = pl.BlockSpec((tm, w), lambda i: (i, 0))
    return pl.pallas_call(
        _merge_kernel,
        grid=(t // tm,),
        in_specs=[yspec, yspec, yspec, yspec,
                  pl.BlockSpec((1, nsl, tm, gtn), lambda i: (i, 0, 0, 0)),
                  _resident((N_BRANCH, w, d)),
                  _resident((d, d)),
                  pl.BlockSpec((tm, d), lambda i: (i, 0))],
        out_specs=pl.BlockSpec((tm, d), lambda i: (i, 0)),
        out_shape=jax.ShapeDtypeStruct((t, d), F32),
        compiler_params=_cparams("parallel"),
        name="merge",
    )(*ys, gates4, wb_bf16, wo_bf16, x2d)


def _hg_bounds_kernel(h_ref, o_ref):
    h = h_ref[...]
    depth = h.shape[0]
    m = jnp.max(h, axis=0, keepdims=True)
    e = jnp.exp(h - m)
    p = e / jnp.sum(e, axis=0, keepdims=True)
    run = jnp.zeros_like(p[0:1])
    rows = []
    for l in range(depth):
        run = run + p[l:l + 1]
        rows.append(run - p[0:1])
    o_ref[...] = jnp.concatenate(rows, axis=0)


def _hg_bounds(hgrn_lower_bounds):
    return pl.pallas_call(
        _hg_bounds_kernel,
        out_shape=jax.ShapeDtypeStruct(hgrn_lower_bounds.shape, F32),
        name="hg_bounds",
    )(hgrn_lower_bounds)


def _hg_chunk_kernel(q_ref, f_ref, i_ref, g_ref, lb_ref, nw_ref, tril_ref, after_ref, y_ref,
                     st_ref, kbuf, bbuf, vbuf, *, tb, c, cs, heads, dk):
    sub = V7X_SUBLANES
    nt = (((1,), (1,)), ((), ()))

    @pl.when(pl.program_id(1) == 0)
    def _():
        st_ref[...] = jnp.zeros(st_ref.shape, F32)

    kbuf[0:sub, :] = jnp.zeros((sub, kbuf.shape[1]), F32)
    bbuf[0:sub, :] = jnp.zeros((sub, bbuf.shape[1]), F32)
    vbuf[0:sub, :] = jnp.zeros((sub, vbuf.shape[1]), F32)
    lb = lb_ref[...]
    tril = tril_ref[...]
    rid = lax.broadcasted_iota(jnp.int32, (c, dk), 0)
    rid1 = lax.broadcasted_iota(jnp.int32, (c, 1), 0)
    pr = lax.broadcasted_iota(jnp.int32, (c, c), 0)
    pc = lax.broadcasted_iota(jnp.int32, (c, c), 1)

    def chunk(ci):
        rows = pl.ds(ci * c, c)
        ff = f_ref[0, rows, :]
        q = _silu(q_ref[0, rows, :])
        dec = jnp.maximum(lb + (1.0 - lb) * jax.nn.sigmoid(ff), HG_TINY)
        k = (1.0 - lb) * jax.nn.sigmoid(-ff)
        v = i_ref[0, rows, :]
        b = _dot_exact_lhs(tril, jnp.log(dec))
        base = sub + ci * c
        kbuf[base:base + c, :] = k
        bbuf[base:base + c, :] = b
        vbuf[base:base + c, :] = v
        outs = []
        for h in range(heads):
            hs = slice(h * dk, (h + 1) * dk)
            qh, kh, bh, vh = q[:, hs], k[:, hs], b[:, hs], v[:, hs]
            blast = bh[c - 1:c, :]
            st = st_ref[h]
            o = lax.dot_general((qh * jnp.exp(bh)).astype(BF16), st.astype(BF16), nt,
                                preferred_element_type=F32)
            att = jnp.zeros((c, c), F32)
            grp = 2 * cs
            while grp <= c:
                half = grp // 2
                qparts, kparts = [], []
                for r0 in range(0, c, grp):
                    d = bh[r0:r0 + grp, :] - bh[r0 + half - 1:r0 + half, :]
                    if half % sub == 0:
                        zero = jnp.zeros((half, dk), F32)
                        kparts += [kh[r0:r0 + half, :] * jnp.exp(-d[0:half, :]), zero]
                        qparts += [zero, qh[r0 + half:r0 + grp, :] * jnp.exp(d[half:grp, :])]
                    else:
                        upper = (rid[0:grp, :] % grp) >= half
                        e = jnp.exp(jnp.where(upper, d, -d))
                        kparts.append(jnp.where(upper, 0.0, kh[r0:r0 + grp, :] * e))
                        qparts.append(jnp.where(upper, qh[r0:r0 + grp, :] * e, 0.0))
                a = lax.dot_general(jnp.concatenate(qparts, axis=0).astype(BF16),
                                    jnp.concatenate(kparts, axis=0).astype(BF16), nt,
                                    preferred_element_type=F32)
                att = att + (a if grp == c else jnp.where(pr // grp == pc // grp, a, 0.0))
                grp *= 2
            o = o + _bdot(att, vh)
            for dlt in range(cs):
                if dlt == 0:
                    a = jnp.sum(qh * kh, axis=-1, keepdims=True)
                    o = o + a * vh
                else:
                    win = pl.ds(base - dlt, c)
                    e = jnp.exp(jnp.minimum(bh - bbuf[win, hs], 0.0))
                    a = jnp.sum(qh * kbuf[win, hs] * e, axis=-1, keepdims=True)
                    a = jnp.where((rid1 % cs) >= dlt, a, 0.0)
                    o = o + a * vbuf[win, hs]
            kd = kh * jnp.exp(blast - bh)
            st_ref[h] = st * jnp.exp(blast) + _bdot(vh.T, kd)
            outs.append(o * lax.rsqrt(jnp.mean(o * o, axis=-1, keepdims=True) + RMS_EPS))
        y = jnp.concatenate(outs, axis=-1) * nw_ref[...]
        y_ref[0, rows, :] = (y * _silu(g_ref[0, rows, :])).astype(y_ref.dtype)

    for ci in range(tb // c):
        chunk(ci)


def _hgrn2(z3d, lb, norm_w, w, col0, tb, c, cs, after):
    bsz, seq, _ = z3d.shape
    heads = w // HG_DK
    cb = col0 // w
    tril = (jnp.arange(c)[:, None] >= jnp.arange(c)[None, :]).astype(BF16)
    zspec = lambda j: pl.BlockSpec((1, tb, w), lambda b, i: (b, i, cb + j))
    row = pl.BlockSpec((1, w), lambda b, i: (0, 0))
    return pl.pallas_call(
        functools.partial(_hg_chunk_kernel, tb=tb, c=c, cs=cs, heads=heads, dk=HG_DK),
        grid=(bsz, seq // tb),
        in_specs=[zspec(0), zspec(1), zspec(2), zspec(3), row, row,
                  pl.BlockSpec((c, c), lambda b, i: (0, 0)),
                  pl.BlockSpec((1, V7X_SUBLANES, V7X_LANES), lambda b, i: (0, 0, 0))],
        out_specs=pl.BlockSpec((1, tb, w), lambda b, i: (b, i, 0)),
        out_shape=jax.ShapeDtypeStruct((bsz, seq, w), BF16),
        scratch_shapes=[pltpu.VMEM((heads, HG_DK, HG_DK), F32)]
        + [pltpu.VMEM((tb + V7X_SUBLANES, w), F32)] * 3,
        compiler_params=_cparams("parallel", "arbitrary"),
        name="hgrn2",
    )(z3d, z3d, z3d, z3d, lb.reshape(1, w), norm_w.reshape(1, w), tril, after)


def _lanes_to_chains_v(a, bsz, seq, heads, dv, nq):
    a = a.reshape(seq, dv // nq, nq, bsz, heads).transpose(3, 0, 4, 2, 1)
    return a.reshape(bsz * seq, heads * dv)


def _shift_rows(cur, prev_last, first):
    rolled = pltpu.roll(cur, 1, axis=0)
    row0 = jnp.where(first, 0.0, prev_last)
    rid = lax.broadcasted_iota(jnp.int32, cur.shape, 0)
    return jnp.where(rid == 0, jnp.broadcast_to(row0, cur.shape), rolled)


def _rw_prep_kernel(zm_ref, zmp_ref, zl_ref, zlp_ref, vf_ref,
                    mum_ref, mul_ref, w0_ref, w2_ref, a0_ref, a2_ref, g2_ref,
                    kk_ref, ka_ref, rk_ref, v0_ref, v1_ref, v2_ref, ones_ref,
                    kvec_out, v_out, g_out, bonus_out,
                    *, w, tiles_per_seq, has_vres):
    first = (pl.program_id(0) % tiles_per_seq) == 0
    sub = V7X_SUBLANES
    zm = zm_ref[...]
    zl = zl_ref[...]
    zms = zm + (_shift_rows(zm, zmp_ref[sub - 1:sub, :], first) - zm) * mum_ref[...]
    zls = zl + (_shift_rows(zl, zlp_ref[sub - 1:sub, :], first) - zl) * mul_ref[...]
    r = zms[:, 0:w]
    k = zms[:, w:2 * w]
    v = zms[:, 2 * w:3 * w]
    w_log = -_softplus(-(w0_ref[...] + _bdot(jnp.tanh(zls), w2_ref[...]))) - 0.5
    decay = jnp.exp(-jnp.exp(w_log))
    if has_vres:
        mix = jax.nn.sigmoid(v0_ref[...] + _bdot(_bdot(v, v1_ref[...]), v2_ref[...]))
        v = v + (vf_ref[...] - v) * mix
    a = jax.nn.sigmoid(a0_ref[...] + _bdot(zls, a2_ref[...]))
    g = _bdot(jax.nn.sigmoid(zls), g2_ref[...])
    ones = ones_ref[...]
    kk = k * kk_ref[...]
    ss = _dot_exact_rhs(kk * kk, ones)
    kk = kk / jnp.maximum(jnp.sqrt(ss), 1e-12)
    k2 = k * (1.0 + (a - 1.0) * ka_ref[...])
    vecs = (r, decay, k2, -kk, kk * a, v)
    for h in range(w // RW_HEAD):
        hs = slice(h * RW_HEAD, (h + 1) * RW_HEAD)
        row = jnp.concatenate([x[:, hs] for x in vecs], axis=1)
        for q in range(kvec_out.shape[0]):
            kvec_out[q, 0, h] = row
    v_out[...] = v
    g_out[...] = g
    bonus_out[...] = _dot_exact_rhs(r * k2 * rk_ref[...], ones) * v


def _rw_prep(z2d, v_first, p, col_main, col_lora, w, lw, seq, tm, has_vres):
    t = z2d.shape[0]
    sub = V7X_SUBLANES
    mb = col_main // (3 * w)
    lbk = col_lora // lw
    rows8 = tm // sub
    heads = w // RW_HEAD
    tps = seq // tm
    nq = V7X_LANES // (t // seq * heads)

    def prev_idx(i):
        return jnp.maximum(i * rows8 - 1, 0)

    row = lambda n: pl.BlockSpec((1, n), lambda i: (0, 0))
    full = lambda a: pl.BlockSpec(a.shape, lambda i: (0,) * a.ndim)
    ospec = pl.BlockSpec((tm, w), lambda i: (i, 0))
    oshape = jax.ShapeDtypeStruct((t, w), F32)
    args = [z2d, z2d, z2d, z2d, v_first,
            p["mu_main"], p["mu_lora"], p["w0"], p["w2p"], p["a0"], p["a2p"], p["g2p"],
            p["k_k"], p["k_a"], p["r_k"], p["v0"], p["v1p"], p["v2p"], p["ones"]]
    in_specs = [pl.BlockSpec((tm, 3 * w), lambda i: (i, mb)),
                pl.BlockSpec((sub, 3 * w), lambda i: (prev_idx(i), mb)),
                pl.BlockSpec((tm, lw), lambda i: (i, lbk)),
                pl.BlockSpec((sub, lw), lambda i: (prev_idx(i), lbk)),
                ospec,
                row(3 * w), row(lw), row(w), full(p["w2p"]), row(w), full(p["a2p"]), full(p["g2p"]),
                row(w), row(w), row(w), row(w), full(p["v1p"]), full(p["v2p"]), full(p["ones"])]
    return pl.pallas_call(
        functools.partial(_rw_prep_kernel, w=w, tiles_per_seq=seq // tm, has_vres=has_vres),
        grid=(t // tm,),
        in_specs=in_specs,
        out_specs=[pl.BlockSpec((nq, 1, heads, tm, RW_NVEC * RW_HEAD),
                                lambda i: (0, i // tps, 0, i % tps, 0)), ospec, ospec, ospec],
        out_shape=[jax.ShapeDtypeStruct((nq, t // seq, heads, seq, RW_NVEC * RW_HEAD), F32),
                   oshape, oshape, oshape],
        compiler_params=_cparams("parallel"),
        name="rw_prep",
    )(*args)


def _rw_scan_kernel(x_ref, xt_ref, mixw_ref, wg_ref, *rest, tc, dk, nvb, nq, spc):
    y_ref, gate_ref, s_ref, u_ref = rest[-4:]
    u_ref[...] = _rms(xt_ref[...], mixw_ref[...]).astype(BF16)
    sub = V7X_SUBLANES
    lanes = V7X_LANES
    jr, jw, jk, ja, jb, jv = range(RW_NVEC)
    vl = nvb * sub
    qid = lax.broadcasted_iota(jnp.int32, (sub, lanes), 1) // (lanes // nq)

    @pl.when(pl.program_id(0) == 0)
    def _():
        s_ref[...] = jnp.zeros(s_ref.shape, F32)

    def bc(j, t, kk):
        return jnp.broadcast_to(x_ref[t, j, pl.ds(kk, 1), :], (sub, lanes))

    sa0 = [jnp.zeros((sub, lanes), F32) for _ in range(nvb)]
    for kk in range(dk):
        arow = bc(ja, 0, kk)
        for j in range(nvb):
            sa0[j] = sa0[j] + s_ref[kk, pl.ds(sub * j, sub), :] * arow

    def step(t, sa):
        tn = jnp.minimum(t + 1, tc - 1)
        vb = []
        for j in range(nvb):
            vj = x_ref[t, jv, pl.ds(sub * j, sub), :]
            for q in range(1, nq):
                vj = jnp.where(qid == q, x_ref[t, jv, pl.ds(q * vl + sub * j, sub), :], vj)
            vb.append(vj)
        yacc = [jnp.zeros((sub, lanes), F32) for _ in range(nvb)]
        sacc = [jnp.zeros((sub, lanes), F32) for _ in range(nvb)]
        for kk in range(dk):
            wrow = bc(jw, t, kk)
            brow = bc(jb, t, kk)
            krow = bc(jk, t, kk)
            rrow = bc(jr, t, kk)
            anext = bc(ja, tn, kk)
            for j in range(nvb):
                s = s_ref[kk, pl.ds(sub * j, sub), :] * wrow + sa[j] * brow + vb[j] * krow
                s_ref[kk, pl.ds(sub * j, sub), :] = s
                yacc[j] = yacc[j] + s * rrow
                sacc[j] = sacc[j] + s * anext
        for j in range(nvb):
            y_ref[t, pl.ds(sub * j, sub), :] = yacc[j]
        return tuple(sacc)

    def group(c, sa):
        for s in range(spc):
            sa = step(c * spc + s, sa)
        zg = jnp.dot(u_ref[...], wg_ref[c], preferred_element_type=F32)
        gate_ref[0, c] = jax.nn.sigmoid(zg).astype(gate_ref.dtype)
        return sa

    lax.fori_loop(0, tc // spc, group, tuple(sa0))


def _rw_scan(x_l, nq, tc, after, x2d, mix_w, w_gate, spc):
    s, nvec, dk, lanes = x_l.shape
    t, d = x2d.shape
    vl = dk // nq
    nsteps = s // tc
    slab = t // nsteps
    nsl = tc // spc
    gtn = w_gate.shape[1] // nsl
    w_gate3 = w_gate.reshape(d, nsl, gtn).transpose(1, 0, 2)
    kspec = pl.BlockSpec((tc, nvec, dk, lanes), lambda i: (i, 0, 0, 0))
    vspec = pl.BlockSpec((tc, vl, lanes), lambda i: (i, 0, 0))
    order = [pl.BlockSpec((16, lanes), lambda i: (0, 0))] * len(after)
    return pl.pallas_call(
        functools.partial(_rw_scan_kernel, tc=tc, dk=dk, nvb=vl // V7X_SUBLANES, nq=nq, spc=spc),
        grid=(nsteps,),
        in_specs=[kspec, pl.BlockSpec((slab, d), lambda i: (i, 0)),
                  pl.BlockSpec((1, d), lambda i: (0, 0)), _resident((nsl, d, gtn))] + order,
        out_specs=[vspec, pl.BlockSpec((1, nsl, slab, gtn), lambda i: (i, 0, 0, 0))],
        out_shape=[jax.ShapeDtypeStruct((s, vl, lanes), F32),
                   jax.ShapeDtypeStruct((nsteps, nsl, slab, gtn), BF16)],
        scratch_shapes=[pltpu.VMEM((dk, vl, lanes), F32), pltpu.VMEM((slab, d), BF16)],
        compiler_params=_cparams("arbitrary"),
        name="rw_scan",
    )(x_l, x2d, mix_w.reshape(1, d), w_gate3, *after)


def _rw_post_kernel(y_ref, g_ref, bonus_ref, lnw_ref, lnb_ref, ones_ref, o_ref):
    y = y_ref[...]
    ones = ones_ref[...]
    inv_n = 1.0 / RW_HEAD
    mean = _dot_exact_rhs(y, ones) * inv_n
    yc = y - mean
    var = _dot_exact_rhs(yc * yc, ones) * inv_n
    y = yc * lax.rsqrt(var + RW_LN_EPS) * lnw_ref[...] + lnb_ref[...]
    o_ref[...] = ((y + bonus_ref[...]) * g_ref[...]).astype(o_ref.dtype)


def _rw_post(y2d, g, bonus, ln_w, ln_b, ones, tm):
    t, w = y2d.shape
    tspec = pl.BlockSpec((tm, w), lambda i: (i, 0))
    row = pl.BlockSpec((1, w), lambda i: (0, 0))
    return pl.pallas_call(
        _rw_post_kernel,
        grid=(t // tm,),
        in_specs=[tspec, tspec, tspec, row, row, pl.BlockSpec((w, w), lambda i: (0, 0))],
        out_specs=tspec,
        out_shape=jax.ShapeDtypeStruct((t, w), BF16),
        compiler_params=_cparams("parallel"),
        name="rw_post",
    )(y2d, g, bonus, ln_w.reshape(1, w), ln_b.reshape(1, w), ones)


def _rwkv7_front(z2d, v_first, p, bsz, seq, w, col_main, col_lora, lw, tm, has_vres):
    kvec, v, g, bonus = _rw_prep(z2d, v_first, p, col_main, col_lora, w, lw, seq, tm, has_vres)
    lanes = kvec.shape[0] * kvec.shape[1] * kvec.shape[2]
    x_l = kvec.reshape(lanes, seq, RW_NVEC * RW_HEAD).transpose(1, 2, 0)
    return x_l.reshape(seq, RW_NVEC, RW_HEAD, lanes), v, g, bonus


def _rwkv7_back(y_l, g, bonus, p, bsz, seq, w, tm):
    heads = w // RW_HEAD
    nq = V7X_LANES // (bsz * heads)
    y = _lanes_to_chains_v(y_l, bsz, seq, heads, RW_HEAD, nq)
    return _rw_post(y, g, bonus, p["ln_w"], p["ln_b"], p["ones"], tm)


def _s5_disc_kernel(are_ref, aim_ref, dt_ref, bre_ref, bim_ref, lre_ref, lim_ref, bbre_ref, bbim_ref):
    a_re = are_ref[...]
    a_im = aim_ref[...]
    dt = jnp.exp(dt_ref[...])
    mag = jnp.exp(dt * a_re)
    lam_re = mag * jnp.cos(dt * a_im)
    lam_im = mag * jnp.sin(dt * a_im)
    den = a_re * a_re + a_im * a_im
    coef_re = ((lam_re - 1.0) * a_re + lam_im * a_im) / den
    coef_im = (lam_im * a_re - (lam_re - 1.0) * a_im) / den
    b_re = bre_ref[...]
    b_im = bim_ref[...]
    lre_ref[...] = lam_re
    lim_ref[...] = lam_im
    bbre_ref[...] = coef_re * b_re - coef_im * b_im
    bbim_ref[...] = coef_re * b_im + coef_im * b_re


def _s5_discretize(a_re, a_im, log_dt, b_re, b_im):
    g, n, c = b_re.shape
    shp = (g, n * c)
    bc = lambda a: jnp.broadcast_to(a[..., None], (g, n, c)).reshape(shp)
    dtb = jnp.broadcast_to(log_dt[:, None], shp)
    o = jax.ShapeDtypeStruct(shp, F32)
    lre, lim, bbre, bbim = pl.pallas_call(
        _s5_disc_kernel, out_shape=[o, o, o, o], name="s5_disc",
    )(bc(a_re), bc(a_im), dtb, b_re.reshape(shp), b_im.reshape(shp))
    un = lambda a: a.reshape(g, n, c)
    return un(lre)[..., 0], un(lim)[..., 0], un(bbre), un(bbim)


def _s5_kernel(u_ref, bblk_ref, cblk_ref, lre_ref, lim_ref, d_ref, wg_ref, bg_ref, y_ref,
               h_ref, hr_ref, hi_ref, *, tc, bsz, ns, lane_chunk):
    @pl.when(pl.program_id(0) == 0)
    def _():
        hr_ref[...] = jnp.zeros(hr_ref.shape, F32)
        hi_ref[...] = jnp.zeros(hi_ref.shape, F32)

    u = u_ref[...]
    nsg = bblk_ref.shape[0]
    lanes = V7X_LANES
    for sg in range(nsg):
        c0 = sg * lane_chunk
        drive = _bdot(u[:, sg * lanes:(sg + 1) * lanes], bblk_ref[sg])
        h_ref[:, c0:c0 + lane_chunk] = drive[:, 0:lane_chunk]
        h_ref[:, ns + c0:ns + c0 + lane_chunk] = drive[:, lane_chunk:2 * lane_chunk]
    for c0 in range(0, ns, lane_chunk):
        lr = jnp.broadcast_to(lre_ref[:, c0:c0 + lane_chunk], (bsz, lane_chunk))
        li = jnp.broadcast_to(lim_ref[:, c0:c0 + lane_chunk], (bsz, lane_chunk))

        def step(t, carry, c0=c0, lr=lr, li=li):
            hr, hi = carry
            rows = pl.ds(pl.multiple_of(t * bsz, bsz), bsz)
            nr = lr * hr - li * hi + h_ref[rows, c0:c0 + lane_chunk]
            ni = lr * hi + li * hr + h_ref[rows, ns + c0:ns + c0 + lane_chunk]
            h_ref[rows, c0:c0 + lane_chunk] = nr
            h_ref[rows, ns + c0:ns + c0 + lane_chunk] = ni
            return nr, ni

        hr, hi = lax.fori_loop(0, tc, step,
                               (hr_ref[:, c0:c0 + lane_chunk], hi_ref[:, c0:c0 + lane_chunk]))
        hr_ref[:, c0:c0 + lane_chunk] = hr
        hi_ref[:, c0:c0 + lane_chunk] = hi
    outs = []
    for sg in range(nsg):
        c0 = sg * lane_chunk
        outs.append(_bdot(h_ref[:, c0:c0 + lane_chunk], cblk_ref[sg, 0:lane_chunk, :])
                    + _bdot(h_ref[:, ns + c0:ns + c0 + lane_chunk], cblk_ref[sg, lane_chunk:, :]))
    y = jnp.concatenate(outs, axis=-1) + d_ref[...] * u
    y = jax.nn.gelu(y)
    y_ref[...] = (y * jax.nn.sigmoid(_bdot(y, wg_ref[...]) + bg_ref[...])).astype(y_ref.dtype)


def _s5(u_tm, p, bsz, tc):
    rows, w = u_tm.shape
    ns = p["lam_re"].shape[1]
    blk = tc * bsz
    full = lambda a: pl.BlockSpec(a.shape, lambda i: (0,) * a.ndim)
    return pl.pallas_call(
        functools.partial(_s5_kernel, tc=tc, bsz=bsz, ns=ns, lane_chunk=ns // p["bblk"].shape[0]),
        grid=(rows // blk,),
        in_specs=[pl.BlockSpec((blk, w), lambda i: (i, 0)),
                  full(p["bblk"]), full(p["cblk"]), full(p["lam_re"]), full(p["lam_im"]),
                  full(p["d"]), full(p["w_glu"]), full(p["b_glu"])],
        out_specs=pl.BlockSpec((blk, w), lambda i: (i, 0)),
        out_shape=jax.ShapeDtypeStruct((rows, w), BF16),
        scratch_shapes=[pltpu.VMEM((blk, 2 * ns), F32),
                        pltpu.VMEM((bsz, ns), F32), pltpu.VMEM((bsz, ns), F32)],
        compiler_params=_cparams("arbitrary"),
        name="s5",
    )(u_tm, p["bblk"], p["cblk"], p["lam_re"], p["lam_im"], p["d"], p["w_glu"], p["b_glu"])


def _mb_kernel(gate_ref, x_ref, bc_ref, dt_ref, cw_ref, cb_ref, dtb_ref, alog_ref, dsk_ref, nw_ref,
               expand_ref, tril_ref, y_ref, prev_ref, st_ref, *, q, w, heads, groups, nstate):
    sub = V7X_SUBLANES
    lanes = V7X_LANES
    hd = w // heads
    gw = w // groups
    hpg = heads // groups

    @pl.when(pl.program_id(1) == 0)
    def _():
        prev_ref[...] = jnp.zeros(prev_ref.shape, F32)
        st_ref[...] = jnp.zeros(st_ref.shape, F32)

    xbc = jnp.concatenate([x_ref[0], bc_ref[0]], axis=-1)
    full = jnp.concatenate([prev_ref[...], xbc], axis=0)
    conv = jnp.broadcast_to(cb_ref[...], xbc.shape)
    for j in range(MB_CONV):
        shift = MB_CONV - 1 - j
        src = full if shift == 0 else pltpu.roll(full, shift, axis=0)
        conv = conv + src[sub:sub + q, :] * cw_ref[j:j + 1, :]
    prev_ref[...] = xbc[q - sub:q, :]
    act = _silu(conv)
    xs = act[:, 0:w]
    bmat = act[:, w:w + groups * nstate]
    cmat = act[:, w + groups * nstate:w + 2 * groups * nstate]

    dt = _softplus(dt_ref[0] + dtb_ref[...])
    a = -jnp.exp(alog_ref[...]) * dt
    a_cum = _dot_exact_lhs(tril_ref[...], a)
    expand = expand_ref[...]
    dt_e = _dot_exact_rhs(dt, expand)
    acum_e = _dot_exact_rhs(a_cum, expand)
    alast_e = acum_e[q - 1:q, :]
    xdt = xs * dt_e
    xdec = xdt * jnp.exp(alast_e - acum_e)

    rid = lax.broadcasted_iota(jnp.int32, (q, q), 0)
    cid = lax.broadcasted_iota(jnp.int32, (q, q), 1)
    causal = rid >= cid
    a_cum_t = a_cum.T
    lane_w = lax.broadcasted_iota(jnp.int32, (q, gw), 1)

    y_parts = []
    for g in range(groups):
        bg = bmat[:, g * nstate:(g + 1) * nstate]
        cg = cmat[:, g * nstate:(g + 1) * nstate]
        scores = lax.dot_general(cg.astype(BF16), bg.astype(BF16), (((1,), (1,)), ((), ())),
                                 preferred_element_type=F32)
        xg = xdt[:, g * gw:(g + 1) * gw]
        yg = _bdot(cg, st_ref[g]) * jnp.exp(acum_e[:, g * gw:(g + 1) * gw])
        for hh in range(hpg):
            h = g * hpg + hh
            col = jnp.broadcast_to(a_cum[:, h:h + 1], (q, q))
            rowv = jnp.broadcast_to(a_cum_t[h:h + 1, :], (q, q))
            decay = jnp.where(causal, jnp.exp(col - rowv), 0.0)
            xh = jnp.where((lane_w >= hh * hd) & (lane_w < (hh + 1) * hd), xg, 0.0)
            yg = yg + _bdot(scores * decay, xh)
        y_parts.append(yg)
        upd = _bdot(bg.T, xdec[:, g * gw:(g + 1) * gw])
        st_ref[g] = st_ref[g] * jnp.exp(alast_e[:, g * gw:(g + 1) * gw]) + upd
    y = jnp.concatenate(y_parts, axis=-1) + dsk_ref[...] * xs
    y_ref[0] = _rms(y * _silu(gate_ref[0]), nw_ref[...]).astype(y_ref.dtype)


def _mamba2(z3d, p, w, cols, q):
    bsz, seq, _ = z3d.shape
    heads = w // MB_HEADDIM
    lanes = V7X_LANES
    cg, cx, cbc, cdt = cols
    blk = lambda width, col: pl.BlockSpec((1, q, width), lambda b, c: (b, c, col // width))
    full = lambda a: pl.BlockSpec(a.shape, lambda b, c: (0,) * a.ndim)
    consts = [p["conv_w"], p["conv_b"], p["dt_bias"], p["a_log"], p["d"], p["norm_w"],
              p["expand"], p["tril"]]
    return pl.pallas_call(
        functools.partial(_mb_kernel, q=q, w=w, heads=heads, groups=MB_GROUPS, nstate=MB_STATE),
        grid=(bsz, seq // q),
        in_specs=[blk(w, cg), blk(w, cx), blk(w, cbc), blk(lanes, cdt)] + [full(a) for a in consts],
        out_specs=pl.BlockSpec((1, q, w), lambda b, c: (b, c, 0)),
        out_shape=jax.ShapeDtypeStruct((bsz, seq, w), BF16),
        scratch_shapes=[pltpu.VMEM((V7X_SUBLANES, 2 * w), F32),
                        pltpu.VMEM((MB_GROUPS, MB_STATE, w // MB_GROUPS), F32)],
        compiler_params=_cparams("parallel", "arbitrary"),
        name="mamba2",
    )(z3d, z3d, z3d, z3d, *consts)


def _pad_rows(a, rows, at):
    out = jnp.zeros((rows, a.shape[1]), a.dtype)
    return lax.dynamic_update_slice(out, a, (at, 0))


def _block_ones(w, head):
    idx = jnp.arange(w) // head
    return (idx[:, None] == idx[None, :]).astype(BF16)


def _pick_tile(n, target):
    t = min(n, target)
    while n % t:
        t //= 2
    return t


def kernel(x, norm_mix_w, w_in, w_branch, w_out, norm_ffn_w, w_ffn_in, w_ffn_out, norm_final_w, hgrn_lower_bounds, hgrn_norm_w, rwkv_mu, rwkv_w0, rwkv_w2, rwkv_a0, rwkv_a2, rwkv_g2, rwkv_k_k, rwkv_k_a, rwkv_r_k, rwkv_ln_w, rwkv_ln_b, rwkv_v0, rwkv_v1, rwkv_v2, s5_a_re, s5_a_im, s5_b_re, s5_b_im, s5_c_re, s5_c_im, s5_d, s5_log_dt, s5_w_glu, s5_b_glu, mamba_conv_w, mamba_conv_b, mamba_dt_bias, mamba_a_log, mamba_d, mamba_norm_w):
    bsz, seq, d = x.shape
    depth = w_in.shape[0]
    w = d // 2
    lanes = V7X_LANES
    t = bsz * seq
    mb_heads = w // MB_HEADDIM
    mb_bc = 2 * MB_GROUPS * MB_STATE
    lw = RW_DECAY_LORA + RW_A_LORA + RW_G_LORA
    assert mb_bc == w and 3 * w % lw == 0

    o_gate = 0
    o_hg = o_gate + N_BRANCH * d
    o_rw = o_hg + 4 * w
    o_rwl = o_rw + 3 * w
    o_s5 = o_rwl + lw
    o_mbg = o_s5 + w
    o_mbx = o_mbg + w
    o_mbbc = o_mbx + w
    o_mbdt = o_mbbc + mb_bc
    c_rw = 0
    c_hg = c_rw + 3 * w
    c_s5 = c_hg + 4 * w
    c_mbg = c_s5 + w
    c_mbx = c_mbg + w
    c_mbbc = c_mbx + w
    c_rwl = c_mbbc + mb_bc
    c_mbdt = c_rwl + lw
    n_cols = c_mbdt + lanes
    tn = 512
    n_pad = -(-n_cols // tn) * tn

    def mixer_cols(wl):
        sl = lambda a, b: wl[:, a:b]
        pieces = [sl(o_rw, o_rwl), sl(o_hg, o_rw), sl(o_s5, o_mbdt), sl(o_rwl, o_s5),
                  sl(o_mbdt, o_mbdt + mb_heads),
                  jnp.zeros((d, n_pad - c_mbdt - mb_heads), wl.dtype)]
        return jnp.concatenate(pieces, axis=1).astype(BF16)

    tm = _pick_tile(seq, 256)
    tm_in = _pick_tile(seq, 512)
    tb_hg = _pick_tile(seq, 512)
    c_hg_chunk = min(HG_CHUNK, tb_hg)
    tc_rw = _pick_tile(seq, 32)
    tc_s5 = _pick_tile(seq, 64)
    q_mb = min(MB_CHUNK, seq)
    ff = w_ffn_out.shape[1]
    tn_ff = 256 if ff % 256 == 0 else lanes

    lower_bounds = _hg_bounds(hgrn_lower_bounds)
    ones_rw = _block_ones(w, RW_HEAD)
    eye_g = jnp.eye(lanes // S5_GROUP, dtype=F32)
    expand = (jnp.arange(lanes)[:, None] == (jnp.arange(w) // MB_HEADDIM)[None, :]).astype(BF16)
    tril = (jnp.arange(q_mb)[:, None] >= jnp.arange(q_mb)[None, :]).astype(BF16)

    x2d = x.reshape(t, d)
    v_first = jnp.zeros((t, w), F32)
    for l in range(depth):
        z2d = _inproj(x2d, norm_mix_w[l], mixer_cols(w_in[l]), tm_in, tn)
        z3d = z2d.reshape(bsz, seq, n_pad)

        mu = rwkv_mu[l]
        has_vres = l > 0
        lv = max(l - 1, 0)
        rp = {
            "mu_main": mu[None, :3 * w], "mu_lora": mu[None, 3 * w:],
            "w0": rwkv_w0[l][None], "a0": rwkv_a0[l][None],
            "w2p": _pad_rows(rwkv_w2[l], lw, 0).astype(BF16),
            "a2p": _pad_rows(rwkv_a2[l], lw, RW_DECAY_LORA).astype(BF16),
            "g2p": _pad_rows(rwkv_g2[l], lw, RW_DECAY_LORA + RW_A_LORA).astype(BF16),
            "k_k": rwkv_k_k[l][None], "k_a": rwkv_k_a[l][None], "r_k": rwkv_r_k[l].reshape(1, w),
            "v0": rwkv_v0[lv][None],
            "v1p": jnp.pad(rwkv_v1[lv], ((0, 0), (0, lanes - rwkv_v1.shape[2]))).astype(BF16),
            "v2p": _pad_rows(rwkv_v2[lv], lanes, 0).astype(BF16),
            "ones": ones_rw, "ln_w": rwkv_ln_w[l], "ln_b": rwkv_ln_b[l],
        }
        rw_x, v_l, rw_g, rw_bonus = _rwkv7_front(z2d, v_first, rp, bsz, seq, w, c_rw, c_rwl, lw, tm,
                                                 has_vres)
        if l == 0:
            v_first = v_l

        lam_re, lam_im, bb_re, bb_im = _s5_discretize(s5_a_re[l], s5_a_im[l], s5_log_dt[l],
                                                      s5_b_re[l], s5_b_im[l])
        ns = lam_re.size
        nsg = w // lanes
        gps = lanes // S5_GROUP
        blk_in = lambda bb: jnp.einsum(
            "sgnc,gh->sgchn", bb.reshape(nsg, gps, S5_STATE, S5_GROUP), eye_g
        ).reshape(nsg, lanes, gps * S5_STATE)
        blk_out = lambda cc: jnp.einsum(
            "sgcn,gh->sgnhc", cc.reshape(nsg, gps, S5_GROUP, S5_STATE), eye_g
        ).reshape(nsg, gps * S5_STATE, lanes)
        sp = {
            "bblk": jnp.concatenate([blk_in(bb_re), blk_in(bb_im)], axis=2).astype(BF16),
            "cblk": jnp.concatenate([blk_out(s5_c_re[l]), -blk_out(s5_c_im[l])], axis=1).astype(BF16),
            "lam_re": lam_re.reshape(1, ns), "lam_im": lam_im.reshape(1, ns),
            "d": s5_d[l][None], "w_glu": s5_w_glu[l].astype(BF16), "b_glu": s5_b_glu[l][None],
        }
        u_tm = z3d[:, :, c_s5:c_s5 + w].transpose(1, 0, 2).reshape(t, w)
        y_s5 = _s5(u_tm, sp, bsz, tc_s5).reshape(seq, bsz, w).transpose(1, 0, 2).reshape(t, w)

        mp = {
            "conv_w": mamba_conv_w[l], "conv_b": mamba_conv_b[l][None],
            "dt_bias": jnp.pad(mamba_dt_bias[l], (0, lanes - mb_heads))[None],
            "a_log": jnp.pad(mamba_a_log[l], (0, lanes - mb_heads))[None],
            "d": jnp.repeat(mamba_d[l], MB_HEADDIM)[None], "norm_w": mamba_norm_w[l][None],
            "expand": expand, "tril": tril,
        }
        y_mb = _mamba2(z3d, mp, w, (c_mbg, c_mbx, c_mbbc, c_mbdt), q_mb).reshape(t, w)

        rw_y, gates = _rw_scan(rw_x, lanes // (bsz * (w // RW_HEAD)), tc_rw, (y_s5, y_mb), x2d,
                               norm_mix_w[l], w_in[l][:, o_gate:o_hg].astype(BF16), RW_GATE_STEPS)
        y_hg = _hgrn2(z3d, lower_bounds[l], hgrn_norm_w[l], w, c_hg, tb_hg, c_hg_chunk,
                      HG_SUBBLOCK, rw_y).reshape(t, w)
        y_rw = _rwkv7_back(rw_y, rw_g, rw_bonus, rp, bsz, seq, w, tm)
        x2d = _merge((y_hg, y_rw, y_s5, y_mb), gates, w_branch[l].astype(BF16), w_out[l].astype(BF16),
                     x2d)
        act = _ffn_in(x2d, norm_ffn_w[l], w_ffn_in[l].astype(BF16), tm_in, tn_ff)
        x2d = _ffn_out(act, w_ffn_out[l].astype(BF16), x2d, norm_final_w, l == depth - 1, tm)
    return x2d.reshape(bsz, seq, d)
```

```python
import functools
import math

import jax
import jax.numpy as jnp
from jax import lax
from jax.experimental import pallas as pl
from jax.experimental.pallas import tpu as pltpu

F32 = jnp.float32
BF16 = jnp.bfloat16

V7X_LANES = 128
V7X_SUBLANES = 8
V7X_VMEM_LIMIT_BYTES = 56 * 1024 * 1024

RMS_EPS = 1e-6
N_BRANCH = 4
HG_DK = 128
HG_TINY = 1e-30
HG_CHUNK = 64
HG_SUBBLOCK = 4
RW_HEAD = 64
RW_DECAY_LORA = 64
RW_A_LORA = 64
RW_G_LORA = 128
RW_LN_EPS = 64e-5
RW_NVEC = 6
S5_GROUP = 16
S5_STATE = 64
MB_HEADDIM = 64
MB_GROUPS = 2
MB_STATE = 128
MB_CONV = 4
MB_CHUNK = 128


def _cparams(*sem):
    return pltpu.CompilerParams(dimension_semantics=sem, vmem_limit_bytes=V7X_VMEM_LIMIT_BYTES)


def _bdot(a, b):
    return jnp.dot(a.astype(BF16), b.astype(BF16), preferred_element_type=F32)


def _split3(x):
    x1 = x.astype(BF16)
    r1 = x - x1.astype(F32)
    x2 = r1.astype(BF16)
    x3 = (r1 - x2.astype(F32)).astype(BF16)
    return x1, x2, x3


def _dot_exact_rhs(x, m):
    x1, x2, x3 = _split3(x)
    d = functools.partial(jnp.dot, preferred_element_type=F32)
    return d(x1, m) + d(x2, m) + d(x3, m)


def _dot_exact_lhs(m, x):
    x1, x2, x3 = _split3(x)
    d = functools.partial(jnp.dot, preferred_element_type=F32)
    return d(m, x1) + d(m, x2) + d(m, x3)


def _softplus(x):
    return jnp.maximum(x, 0.0) + jnp.log(1.0 + jnp.exp(-jnp.abs(x)))


def _silu(x):
    return x * jax.nn.sigmoid(x)


def _rms(x, w):
    return x * lax.rsqrt(jnp.mean(x * x, axis=-1, keepdims=True) + RMS_EPS) * w


def _resident(shape):
    return pl.BlockSpec(shape, lambda i: (0,) * len(shape), pipeline_mode=pl.Buffered(1))


def _inproj_kernel(x_ref, nw_ref, w_ref, o_ref, *, gate, tn):
    u = _rms(x_ref[...], nw_ref[...]).astype(BF16)
    for n0 in range(0, o_ref.shape[1], tn):
        z = jnp.dot(u, w_ref[:, n0:n0 + tn], preferred_element_type=F32)
        o_ref[:, n0:n0 + tn] = jax.nn.sigmoid(z).astype(o_ref.dtype) if gate else z


def _inproj(x2d, norm_w, w_bf16, tm, tn, gate):
    t, d = x2d.shape
    n = w_bf16.shape[1]
    return pl.pallas_call(
        functools.partial(_inproj_kernel, gate=gate, tn=tn),
        grid=(t // tm,),
        in_specs=[pl.BlockSpec((tm, d), lambda i: (i, 0)),
                  pl.BlockSpec((1, d), lambda i: (0, 0)),
                  _resident((d, n))],
        out_specs=pl.BlockSpec((tm, n), lambda i: (i, 0)),
        out_shape=jax.ShapeDtypeStruct((t, n), BF16 if gate else F32),
        compiler_params=_cparams("parallel"),
        name="inproj_gate" if gate else "inproj",
    )(x2d, norm_w.reshape(1, d), w_bf16)


def _ffn_in_kernel(x_ref, nw_ref, w_ref, o_ref, *, tn):
    ff = o_ref.shape[1]
    u = _rms(x_ref[...], nw_ref[...]).astype(BF16)
    for n0 in range(0, ff, tn):
        gate = jnp.dot(u, w_ref[:, n0:n0 + tn], preferred_element_type=F32)
        up = jnp.dot(u, w_ref[:, ff + n0:ff + n0 + tn], preferred_element_type=F32)
        o_ref[:, n0:n0 + tn] = (_silu(gate) * up).astype(BF16)


def _ffn_in(x2d, norm_w, w_bf16, tm, tn):
    t, d = x2d.shape
    ff = w_bf16.shape[1] // 2
    return pl.pallas_call(
        functools.partial(_ffn_in_kernel, tn=tn),
        grid=(t // tm,),
        in_specs=[pl.BlockSpec((tm, d), lambda i: (i, 0)),
                  pl.BlockSpec((1, d), lambda i: (0, 0)),
                  _resident((d, 2 * ff))],
        out_specs=pl.BlockSpec((tm, ff), lambda i: (i, 0)),
        out_shape=jax.ShapeDtypeStruct((t, ff), BF16),
        compiler_params=_cparams("parallel"),
        name="ffn_in",
    )(x2d, norm_w.reshape(1, d), w_bf16)


def _ffn_out_kernel(a_ref, w_ref, x_ref, fw_ref, o_ref, *, final_norm):
    y = x_ref[...] + jnp.dot(a_ref[...], w_ref[...], preferred_element_type=F32)
    if final_norm:
        y = _rms(y, fw_ref[...])
    o_ref[...] = y


def _ffn_out(act, w_bf16, x2d, final_w, final_norm, tm):
    t, d = x2d.shape
    ff = act.shape[1]
    return pl.pallas_call(
        functools.partial(_ffn_out_kernel, final_norm=final_norm),
        grid=(t // tm,),
        in_specs=[pl.BlockSpec((tm, ff), lambda i: (i, 0)),
                  _resident((ff, d)),
                  pl.BlockSpec((tm, d), lambda i: (i, 0)),
                  pl.BlockSpec((1, d), lambda i: (0, 0))],
        out_specs=pl.BlockSpec((tm, d), lambda i: (i, 0)),
        out_shape=jax.ShapeDtypeStruct((t, d), F32),
        compiler_params=_cparams("parallel"),
        name="ffn_out",
    )(act, w_bf16, x2d, final_w.reshape(1, d))


def _merge_kernel(yh_ref, ys_ref, ym_ref, yr_ref, rg_ref, rb_ref, lnw_ref, lnb_ref, ones_ref,
                  zg_ref, wb_ref, wo_ref, x_ref, o_ref):
    d = x_ref.shape[1]
    y_rw = _rw_finish(yr_ref[...], rg_ref[...], rb_ref[...], lnw_ref[...], lnb_ref[...],
                      ones_ref[...])
    acc = jnp.zeros(x_ref.shape, F32)
    for k, y in enumerate((yh_ref[...], y_rw, ys_ref[...], ym_ref[...])):
        proj = jnp.dot(y.astype(BF16), wb_ref[k], preferred_element_type=F32)
        acc = acc + zg_ref[:, k * d:(k + 1) * d].astype(F32) * proj
    o_ref[...] = x_ref[...] + jnp.dot(acc.astype(BF16), wo_ref[...], preferred_element_type=F32)


def _merge(y_hg, y_s5, y_mb, rw, gates, wb_bf16, wo_bf16, x2d, tm):
    t, d = x2d.shape
    w = y_hg.shape[1]
    yspec = pl.BlockSpec((tm, w), lambda i: (i, 0))
    row = pl.BlockSpec((1, w), lambda i: (0, 0))
    y_raw, g, bonus, ln_w, ln_b, ones = rw
    return pl.pallas_call(
        _merge_kernel,
        grid=(t // tm,),
        in_specs=[yspec, yspec, yspec, yspec, yspec, yspec, row, row, _resident((w, w)),
                  pl.BlockSpec((tm, N_BRANCH * d), lambda i: (i, 0)),
                  _resident((N_BRANCH, w, d)),
                  _resident((d, d)),
                  pl.BlockSpec((tm, d), lambda i: (i, 0))],
        out_specs=pl.BlockSpec((tm, d), lambda i: (i, 0)),
        out_shape=jax.ShapeDtypeStruct((t, d), F32),
        compiler_params=_cparams("parallel"),
        name="merge",
    )(y_hg, y_s5, y_mb, y_raw, g, bonus, ln_w.reshape(1, w), ln_b.reshape(1, w), ones,
      gates, wb_bf16, wo_bf16, x2d)


def _hg_bounds_kernel(h_ref, o_ref):
    h = h_ref[...]
    depth = h.shape[0]
    m = jnp.max(h, axis=0, keepdims=True)
    e = jnp.exp(h - m)
    p = e / jnp.sum(e, axis=0, keepdims=True)
    run = jnp.zeros_like(p[0:1])
    rows = []
    for l in range(depth):
        run = run + p[l:l + 1]
        rows.append(run - p[0:1])
    o_ref[...] = jnp.concatenate(rows, axis=0)


def _hg_bounds(hgrn_lower_bounds):
    return pl.pallas_call(
        _hg_bounds_kernel,
        out_shape=jax.ShapeDtypeStruct(hgrn_lower_bounds.shape, F32),
        name="hg_bounds",
    )(hgrn_lower_bounds)


def _hg_chunk_kernel(q_ref, f_ref, i_ref, g_ref, lb_ref, nw_ref, tril_ref, after_ref, y_ref,
                     st_ref, kbuf, bbuf, vbuf, *, tb, c, cs, heads, dk):
    sub = V7X_SUBLANES
    nt = (((1,), (1,)), ((), ()))

    @pl.when(pl.program_id(1) == 0)
    def _():
        st_ref[...] = jnp.zeros(st_ref.shape, F32)

    kbuf[0:sub, :] = jnp.zeros((sub, kbuf.shape[1]), F32)
    bbuf[0:sub, :] = jnp.zeros((sub, bbuf.shape[1]), F32)
    vbuf[0:sub, :] = jnp.zeros((sub, vbuf.shape[1]), F32)
    lb = lb_ref[...]
    tril = tril_ref[...]
    rid = lax.broadcasted_iota(jnp.int32, (c, dk), 0)
    rid1 = lax.broadcasted_iota(jnp.int32, (c, 1), 0)
    pr = lax.broadcasted_iota(jnp.int32, (c, c), 0)
    pc = lax.broadcasted_iota(jnp.int32, (c, c), 1)

    def chunk(ci):
        rows = pl.ds(ci * c, c)
        ff = f_ref[0, rows, :]
        q = _silu(q_ref[0, rows, :])
        dec = jnp.maximum(lb + (1.0 - lb) * jax.nn.sigmoid(ff), HG_TINY)
        k = (1.0 - lb) * jax.nn.sigmoid(-ff)
        v = i_ref[0, rows, :]
        b = _dot_exact_lhs(tril, jnp.log(dec))
        base = sub + ci * c
        kbuf[base:base + c, :] = k
        bbuf[base:base + c, :] = b
        vbuf[base:base + c, :] = v
        outs = []
        for h in range(heads):
            hs = slice(h * dk, (h + 1) * dk)
            qh, kh, bh, vh = q[:, hs], k[:, hs], b[:, hs], v[:, hs]
            blast = bh[c - 1:c, :]
            st = st_ref[h]
            o = lax.dot_general((qh * jnp.exp(bh)).astype(BF16), st.astype(BF16), nt,
                                preferred_element_type=F32)
            att = jnp.zeros((c, c), F32)
            grp = 2 * cs
            while grp <= c:
                half = grp // 2
                qparts, kparts = [], []
                for r0 in range(0, c, grp):
                    d = bh[r0:r0 + grp, :] - bh[r0 + half - 1:r0 + half, :]
                    if half % sub == 0:
                        zero = jnp.zeros((half, dk), F32)
                        kparts += [kh[r0:r0 + half, :] * jnp.exp(-d[0:half, :]), zero]
                        qparts += [zero, qh[r0 + half:r0 + grp, :] * jnp.exp(d[half:grp, :])]
                    else:
                        upper = (rid[0:grp, :] % grp) >= half
                        e = jnp.exp(jnp.where(upper, d, -d))
                        kparts.append(jnp.where(upper, 0.0, kh[r0:r0 + grp, :] * e))
                        qparts.append(jnp.where(upper, qh[r0:r0 + grp, :] * e, 0.0))
                a = lax.dot_general(jnp.concatenate(qparts, axis=0).astype(BF16),
                                    jnp.concatenate(kparts, axis=0).astype(BF16), nt,
                                    preferred_element_type=F32)
                att = att + (a if grp == c else jnp.where(pr // grp == pc // grp, a, 0.0))
                grp *= 2
            o = o + _bdot(att, vh)
            for dlt in range(cs):
                if dlt == 0:
                    a = jnp.sum(qh * kh, axis=-1, keepdims=True)
                    o = o + a * vh
                else:
                    win = pl.ds(base - dlt, c)
                    e = jnp.exp(jnp.minimum(bh - bbuf[win, hs], 0.0))
                    a = jnp.sum(qh * kbuf[win, hs] * e, axis=-1, keepdims=True)
                    a = jnp.where((rid1 % cs) >= dlt, a, 0.0)
                    o = o + a * vbuf[win, hs]
            kd = kh * jnp.exp(blast - bh)
            st_ref[h] = st * jnp.exp(blast) + _bdot(vh.T, kd)
            outs.append(o * lax.rsqrt(jnp.mean(o * o, axis=-1, keepdims=True) + RMS_EPS))
        y = jnp.concatenate(outs, axis=-1) * nw_ref[...]
        y_ref[0, rows, :] = (y * _silu(g_ref[0, rows, :])).astype(y_ref.dtype)

    for ci in range(tb // c):
        chunk(ci)


def _hgrn2(z3d, lb, norm_w, w, col0, tb, c, cs, after):
    bsz, seq, _ = z3d.shape
    heads = w // HG_DK
    cb = col0 // w
    tril = (jnp.arange(c)[:, None] >= jnp.arange(c)[None, :]).astype(BF16)
    zspec = lambda j: pl.BlockSpec((1, tb, w), lambda b, i: (b, i, cb + j))
    row = pl.BlockSpec((1, w), lambda b, i: (0, 0))
    return pl.pallas_call(
        functools.partial(_hg_chunk_kernel, tb=tb, c=c, cs=cs, heads=heads, dk=HG_DK),
        grid=(bsz, seq // tb),
        in_specs=[zspec(0), zspec(1), zspec(2), zspec(3), row, row,
                  pl.BlockSpec((c, c), lambda b, i: (0, 0)),
                  pl.BlockSpec((1, V7X_SUBLANES, V7X_LANES), lambda b, i: (0, 0, 0))],
        out_specs=pl.BlockSpec((1, tb, w), lambda b, i: (b, i, 0)),
        out_shape=jax.ShapeDtypeStruct((bsz, seq, w), BF16),
        scratch_shapes=[pltpu.VMEM((heads, HG_DK, HG_DK), F32)]
        + [pltpu.VMEM((tb + V7X_SUBLANES, w), F32)] * 3,
        compiler_params=_cparams("parallel", "arbitrary"),
        name="hgrn2",
    )(z3d, z3d, z3d, z3d, lb.reshape(1, w), norm_w.reshape(1, w), tril, after)


def _lanes_to_chains_v(a, bsz, seq, heads, dv, nq):
    a = a.reshape(seq, dv // nq, nq, bsz, heads).transpose(3, 0, 4, 2, 1)
    return a.reshape(bsz * seq, heads * dv)


def _shift_rows(cur, prev_last, first):
    rolled = pltpu.roll(cur, 1, axis=0)
    row0 = jnp.where(first, 0.0, prev_last)
    rid = lax.broadcasted_iota(jnp.int32, cur.shape, 0)
    return jnp.where(rid == 0, jnp.broadcast_to(row0, cur.shape), rolled)


def _rw_prep_kernel(zm_ref, zmp_ref, zl_ref, zlp_ref, vf_ref,
                    mum_ref, mul_ref, w0_ref, w2_ref, a0_ref, a2_ref, g2_ref,
                    kk_ref, ka_ref, rk_ref, v0_ref, v1_ref, v2_ref, ones_ref,
                    kvec_out, v_out, g_out, bonus_out,
                    *, w, tiles_per_seq, has_vres):
    first = (pl.program_id(0) % tiles_per_seq) == 0
    sub = V7X_SUBLANES
    zm = zm_ref[...]
    zl = zl_ref[...]
    zms = zm + (_shift_rows(zm, zmp_ref[sub - 1:sub, :], first) - zm) * mum_ref[...]
    zls = zl + (_shift_rows(zl, zlp_ref[sub - 1:sub, :], first) - zl) * mul_ref[...]
    r = zms[:, 0:w]
    k = zms[:, w:2 * w]
    v = zms[:, 2 * w:3 * w]
    w_log = -_softplus(-(w0_ref[...] + _bdot(jnp.tanh(zls), w2_ref[...]))) - 0.5
    decay = jnp.exp(-jnp.exp(w_log))
    if has_vres:
        mix = jax.nn.sigmoid(v0_ref[...] + _bdot(_bdot(v, v1_ref[...]), v2_ref[...]))
        v = v + (vf_ref[...] - v) * mix
    a = jax.nn.sigmoid(a0_ref[...] + _bdot(zls, a2_ref[...]))
    g = _bdot(jax.nn.sigmoid(zls), g2_ref[...])
    ones = ones_ref[...]
    kk = k * kk_ref[...]
    ss = _dot_exact_rhs(kk * kk, ones)
    kk = kk / jnp.maximum(jnp.sqrt(ss), 1e-12)
    k2 = k * (1.0 + (a - 1.0) * ka_ref[...])
    vecs = (r, decay, k2, -kk, kk * a, v)
    for h in range(w // RW_HEAD):
        hs = slice(h * RW_HEAD, (h + 1) * RW_HEAD)
        row = jnp.concatenate([x[:, hs] for x in vecs], axis=1)
        for q in range(kvec_out.shape[0]):
            kvec_out[q, 0, h] = row
    v_out[...] = v
    g_out[...] = g
    bonus_out[...] = _dot_exact_rhs(r * k2 * rk_ref[...], ones) * v


def _rw_prep(z2d, v_first, p, col_main, col_lora, w, lw, seq, tm, has_vres):
    t = z2d.shape[0]
    sub = V7X_SUBLANES
    mb = col_main // (3 * w)
    lbk = col_lora // lw
    rows8 = tm // sub
    heads = w // RW_HEAD
    tps = seq // tm
    nq = V7X_LANES // (t // seq * heads)

    def prev_idx(i):
        return jnp.maximum(i * rows8 - 1, 0)

    row = lambda n: pl.BlockSpec((1, n), lambda i: (0, 0))
    full = lambda a: pl.BlockSpec(a.shape, lambda i: (0,) * a.ndim)
    ospec = pl.BlockSpec((tm, w), lambda i: (i, 0))
    oshape = jax.ShapeDtypeStruct((t, w), F32)
    args = [z2d, z2d, z2d, z2d, v_first,
            p["mu_main"], p["mu_lora"], p["w0"], p["w2p"], p["a0"], p["a2p"], p["g2p"],
            p["k_k"], p["k_a"], p["r_k"], p["v0"], p["v1p"], p["v2p"], p["ones"]]
    in_specs = [pl.BlockSpec((tm, 3 * w), lambda i: (i, mb)),
                pl.BlockSpec((sub, 3 * w), lambda i: (prev_idx(i), mb)),
                pl.BlockSpec((tm, lw), lambda i: (i, lbk)),
                pl.BlockSpec((sub, lw), lambda i: (prev_idx(i), lbk)),
                ospec,
                row(3 * w), row(lw), row(w), full(p["w2p"]), row(w), full(p["a2p"]), full(p["g2p"]),
                row(w), row(w), row(w), row(w), full(p["v1p"]), full(p["v2p"]), full(p["ones"])]
    return pl.pallas_call(
        functools.partial(_rw_prep_kernel, w=w, tiles_per_seq=seq // tm, has_vres=has_vres),
        grid=(t // tm,),
        in_specs=in_specs,
        out_specs=[pl.BlockSpec((nq, 1, heads, tm, RW_NVEC * RW_HEAD),
                                lambda i: (0, i // tps, 0, i % tps, 0)), ospec, ospec, ospec],
        out_shape=[jax.ShapeDtypeStruct((nq, t // seq, heads, seq, RW_NVEC * RW_HEAD), F32),
                   oshape, oshape, oshape],
        compiler_params=_cparams("parallel"),
        name="rw_prep",
    )(*args)


def _rw_scan_kernel(x_ref, *rest, tc, dk, nvb, nq):
    y_ref, s_ref = rest[-2:]
    sub = V7X_SUBLANES
    lanes = V7X_LANES
    jr, jw, jk, ja, jb, jv = range(RW_NVEC)
    vl = nvb * sub
    qid = lax.broadcasted_iota(jnp.int32, (sub, lanes), 1) // (lanes // nq)

    @pl.when(pl.program_id(0) == 0)
    def _():
        s_ref[...] = jnp.zeros(s_ref.shape, F32)

    def bc(j, t, kk):
        return jnp.broadcast_to(x_ref[t, j, pl.ds(kk, 1), :], (sub, lanes))

    sa0 = [jnp.zeros((sub, lanes), F32) for _ in range(nvb)]
    for kk in range(dk):
        arow = bc(ja, 0, kk)
        for j in range(nvb):
            sa0[j] = sa0[j] + s_ref[kk, pl.ds(sub * j, sub), :] * arow

    def step(t, sa):
        tn = jnp.minimum(t + 1, tc - 1)
        vb = []
        for j in range(nvb):
            vj = x_ref[t, jv, pl.ds(sub * j, sub), :]
            for q in range(1, nq):
                vj = jnp.where(qid == q, x_ref[t, jv, pl.ds(q * vl + sub * j, sub), :], vj)
            vb.append(vj)
        yacc = [jnp.zeros((sub, lanes), F32) for _ in range(nvb)]
        sacc = [jnp.zeros((sub, lanes), F32) for _ in range(nvb)]
        for kk in range(dk):
            wrow = bc(jw, t, kk)
            brow = bc(jb, t, kk)
            krow = bc(jk, t, kk)
            rrow = bc(jr, t, kk)
            anext = bc(ja, tn, kk)
            for j in range(nvb):
                s = s_ref[kk, pl.ds(sub * j, sub), :] * wrow + sa[j] * brow + vb[j] * krow
                s_ref[kk, pl.ds(sub * j, sub), :] = s
                yacc[j] = yacc[j] + s * rrow
                sacc[j] = sacc[j] + s * anext
        for j in range(nvb):
            y_ref[t, pl.ds(sub * j, sub), :] = yacc[j]
        return tuple(sacc)

    lax.fori_loop(0, tc, step, tuple(sa0))


def _rw_scan(x_l, nq, tc, after):
    s, nvec, dk, lanes = x_l.shape
    vl = dk // nq
    kspec = pl.BlockSpec((tc, nvec, dk, lanes), lambda i: (i, 0, 0, 0))
    vspec = pl.BlockSpec((tc, vl, lanes), lambda i: (i, 0, 0))
    order = [pl.BlockSpec((16, lanes), lambda i: (0, 0))] * len(after)
    return pl.pallas_call(
        functools.partial(_rw_scan_kernel, tc=tc, dk=dk, nvb=vl // V7X_SUBLANES, nq=nq),
        grid=(s // tc,),
        in_specs=[kspec] + order,
        out_specs=vspec,
        out_shape=jax.ShapeDtypeStruct((s, vl, lanes), F32),
        scratch_shapes=[pltpu.VMEM((dk, vl, lanes), F32)],
        compiler_params=_cparams("arbitrary"),
        name="rw_scan",
    )(x_l, *after)


def _rw_finish(y, g, bonus, ln_w, ln_b, ones):
    inv_n = 1.0 / RW_HEAD
    mean = _dot_exact_rhs(y, ones) * inv_n
    yc = y - mean
    var = _dot_exact_rhs(yc * yc, ones) * inv_n
    y = yc * lax.rsqrt(var + RW_LN_EPS) * ln_w + ln_b
    return (y + bonus) * g


def _rwkv7_front(z2d, v_first, p, bsz, seq, w, col_main, col_lora, lw, tm, has_vres):
    kvec, v, g, bonus = _rw_prep(z2d, v_first, p, col_main, col_lora, w, lw, seq, tm, has_vres)
    lanes = kvec.shape[0] * kvec.shape[1] * kvec.shape[2]
    x_l = kvec.reshape(lanes, seq, RW_NVEC * RW_HEAD).transpose(1, 2, 0)
    return x_l.reshape(seq, RW_NVEC, RW_HEAD, lanes), v, g, bonus


def _s5_disc_kernel(are_ref, aim_ref, dt_ref, bre_ref, bim_ref, lre_ref, lim_ref, bbre_ref, bbim_ref):
    a_re = are_ref[...]
    a_im = aim_ref[...]
    dt = jnp.exp(dt_ref[...])
    mag = jnp.exp(dt * a_re)
    lam_re = mag * jnp.cos(dt * a_im)
    lam_im = mag * jnp.sin(dt * a_im)
    den = a_re * a_re + a_im * a_im
    coef_re = ((lam_re - 1.0) * a_re + lam_im * a_im) / den
    coef_im = (lam_im * a_re - (lam_re - 1.0) * a_im) / den
    b_re = bre_ref[...]
    b_im = bim_ref[...]
    lre_ref[...] = lam_re
    lim_ref[...] = lam_im
    bbre_ref[...] = coef_re * b_re - coef_im * b_im
    bbim_ref[...] = coef_re * b_im + coef_im * b_re


def _s5_discretize(a_re, a_im, log_dt, b_re, b_im):
    g, n, c = b_re.shape
    shp = (g, n * c)
    bc = lambda a: jnp.broadcast_to(a[..., None], (g, n, c)).reshape(shp)
    dtb = jnp.broadcast_to(log_dt[:, None], shp)
    o = jax.ShapeDtypeStruct(shp, F32)
    lre, lim, bbre, bbim = pl.pallas_call(
        _s5_disc_kernel, out_shape=[o, o, o, o], name="s5_disc",
    )(bc(a_re), bc(a_im), dtb, b_re.reshape(shp), b_im.reshape(shp))
    un = lambda a: a.reshape(g, n, c)
    return un(lre)[..., 0], un(lim)[..., 0], un(bbre), un(bbim)


def _s5_kernel(u_ref, bblk_ref, cblk_ref, lre_ref, lim_ref, d_ref, wg_ref, bg_ref, y_ref,
               h_ref, hr_ref, hi_ref, *, tc, bsz, ns, lane_chunk):
    @pl.when(pl.program_id(0) == 0)
    def _():
        hr_ref[...] = jnp.zeros(hr_ref.shape, F32)
        hi_ref[...] = jnp.zeros(hi_ref.shape, F32)

    w = u_ref.shape[2]
    u = jnp.swapaxes(u_ref[...], 0, 1).reshape(tc * bsz, w)
    nsg = bblk_ref.shape[0]
    lanes = V7X_LANES
    for sg in range(nsg):
        c0 = sg * lane_chunk
        drive = _bdot(u[:, sg * lanes:(sg + 1) * lanes], bblk_ref[sg])
        h_ref[:, c0:c0 + lane_chunk] = drive[:, 0:lane_chunk]
        h_ref[:, ns + c0:ns + c0 + lane_chunk] = drive[:, lane_chunk:2 * lane_chunk]
    for c0 in range(0, ns, lane_chunk):
        lr = jnp.broadcast_to(lre_ref[:, c0:c0 + lane_chunk], (bsz, lane_chunk))
        li = jnp.broadcast_to(lim_ref[:, c0:c0 + lane_chunk], (bsz, lane_chunk))

        def step(t, carry, c0=c0, lr=lr, li=li):
            hr, hi = carry
            rows = pl.ds(pl.multiple_of(t * bsz, bsz), bsz)
            nr = lr * hr - li * hi + h_ref[rows, c0:c0 + lane_chunk]
            ni = lr * hi + li * hr + h_ref[rows, ns + c0:ns + c0 + lane_chunk]
            h_ref[rows, c0:c0 + lane_chunk] = nr
            h_ref[rows, ns + c0:ns + c0 + lane_chunk] = ni
            return nr, ni

        hr, hi = lax.fori_loop(0, tc, step,
                               (hr_ref[:, c0:c0 + lane_chunk], hi_ref[:, c0:c0 + lane_chunk]))
        hr_ref[:, c0:c0 + lane_chunk] = hr
        hi_ref[:, c0:c0 + lane_chunk] = hi
    outs = []
    for sg in range(nsg):
        c0 = sg * lane_chunk
        outs.append(_bdot(h_ref[:, c0:c0 + lane_chunk], cblk_ref[sg, 0:lane_chunk, :])
                    + _bdot(h_ref[:, ns + c0:ns + c0 + lane_chunk], cblk_ref[sg, lane_chunk:, :]))
    y = jnp.concatenate(outs, axis=-1) + d_ref[...] * u
    y = jax.nn.gelu(y)
    y = y * jax.nn.sigmoid(_bdot(y, wg_ref[...]) + bg_ref[...])
    y_ref[...] = jnp.swapaxes(y.reshape(tc, bsz, w), 0, 1).astype(y_ref.dtype)


def _s5(z3d, col0, w, p, tc):
    bsz, seq, _ = z3d.shape
    ns = p["lam_re"].shape[1]
    blk = tc * bsz
    full = lambda a: pl.BlockSpec(a.shape, lambda i: (0,) * a.ndim)
    return pl.pallas_call(
        functools.partial(_s5_kernel, tc=tc, bsz=bsz, ns=ns, lane_chunk=ns // p["bblk"].shape[0]),
        grid=(seq // tc,),
        in_specs=[pl.BlockSpec((bsz, tc, w), lambda i: (0, i, col0 // w)),
                  full(p["bblk"]), full(p["cblk"]), full(p["lam_re"]), full(p["lam_im"]),
                  full(p["d"]), full(p["w_glu"]), full(p["b_glu"])],
        out_specs=pl.BlockSpec((bsz, tc, w), lambda i: (0, i, 0)),
        out_shape=jax.ShapeDtypeStruct((bsz, seq, w), BF16),
        scratch_shapes=[pltpu.VMEM((blk, 2 * ns), F32),
                        pltpu.VMEM((bsz, ns), F32), pltpu.VMEM((bsz, ns), F32)],
        compiler_params=_cparams("arbitrary"),
        name="s5",
    )(z3d, p["bblk"], p["cblk"], p["lam_re"], p["lam_im"], p["d"], p["w_glu"], p["b_glu"])


def _mb_kernel(gate_ref, x_ref, bc_ref, dt_ref, cw_ref, cb_ref, dtb_ref, alog_ref, dsk_ref, nw_ref,
               expand_ref, tril_ref, y_ref, prev_ref, st_ref, *, q, w, heads, groups, nstate):
    sub = V7X_SUBLANES
    lanes = V7X_LANES
    hd = w // heads
    gw = w // groups
    hpg = heads // groups

    @pl.when(pl.program_id(1) == 0)
    def _():
        prev_ref[...] = jnp.zeros(prev_ref.shape, F32)
        st_ref[...] = jnp.zeros(st_ref.shape, F32)

    xbc = jnp.concatenate([x_ref[0], bc_ref[0]], axis=-1)
    full = jnp.concatenate([prev_ref[...], xbc], axis=0)
    conv = jnp.broadcast_to(cb_ref[...], xbc.shape)
    for j in range(MB_CONV):
        shift = MB_CONV - 1 - j
        src = full if shift == 0 else pltpu.roll(full, shift, axis=0)
        conv = conv + src[sub:sub + q, :] * cw_ref[j:j + 1, :]
    prev_ref[...] = xbc[q - sub:q, :]
    act = _silu(conv)
    xs = act[:, 0:w]
    bmat = act[:, w:w + groups * nstate]
    cmat = act[:, w + groups * nstate:w + 2 * groups * nstate]

    dt = _softplus(dt_ref[0] + dtb_ref[...])
    a = -jnp.exp(alog_ref[...]) * dt
    a_cum = _dot_exact_lhs(tril_ref[...], a)
    expand = expand_ref[...]
    dt_e = _dot_exact_rhs(dt, expand)
    acum_e = _dot_exact_rhs(a_cum, expand)
    alast_e = acum_e[q - 1:q, :]
    xdt = xs * dt_e
    xdec = xdt * jnp.exp(alast_e - acum_e)

    rid = lax.broadcasted_iota(jnp.int32, (q, q), 0)
    cid = lax.broadcasted_iota(jnp.int32, (q, q), 1)
    causal = rid >= cid
    a_cum_t = a_cum.T
    lane_w = lax.broadcasted_iota(jnp.int32, (q, gw), 1)

    y_parts = []
    for g in range(groups):
        bg = bmat[:, g * nstate:(g + 1) * nstate]
        cg = cmat[:, g * nstate:(g + 1) * nstate]
        scores = lax.dot_general(cg.astype(BF16), bg.astype(BF16), (((1,), (1,)), ((), ())),
                                 preferred_element_type=F32)
        xg = xdt[:, g * gw:(g + 1) * gw]
        yg = _bdot(cg, st_ref[g]) * jnp.exp(acum_e[:, g * gw:(g + 1) * gw])
        for hh in range(hpg):
            h = g * hpg + hh
            col = jnp.broadcast_to(a_cum[:, h:h + 1], (q, q))
            rowv = jnp.broadcast_to(a_cum_t[h:h + 1, :], (q, q))
            decay = jnp.where(causal, jnp.exp(col - rowv), 0.0)
            xh = jnp.where((lane_w >= hh * hd) & (lane_w < (hh + 1) * hd), xg, 0.0)
            yg = yg + _bdot(scores * decay, xh)
        y_parts.append(yg)
        upd = _bdot(bg.T, xdec[:, g * gw:(g + 1) * gw])
        st_ref[g] = st_ref[g] * jnp.exp(alast_e[:, g * gw:(g + 1) * gw]) + upd
    y = jnp.concatenate(y_parts, axis=-1) + dsk_ref[...] * xs
    y_ref[0] = _rms(y * _silu(gate_ref[0]), nw_ref[...]).astype(y_ref.dtype)


def _mamba2(z3d, p, w, cols, q):
    bsz, seq, _ = z3d.shape
    heads = w // MB_HEADDIM
    lanes = V7X_LANES
    cg, cx, cbc, cdt = cols
    blk = lambda width, col: pl.BlockSpec((1, q, width), lambda b, c: (b, c, col // width))
    full = lambda a: pl.BlockSpec(a.shape, lambda b, c: (0,) * a.ndim)
    consts = [p["conv_w"], p["conv_b"], p["dt_bias"], p["a_log"], p["d"], p["norm_w"],
              p["expand"], p["tril"]]
    return pl.pallas_call(
        functools.partial(_mb_kernel, q=q, w=w, heads=heads, groups=MB_GROUPS, nstate=MB_STATE),
        grid=(bsz, seq // q),
        in_specs=[blk(w, cg), blk(w, cx), blk(w, cbc), blk(lanes, cdt)] + [full(a) for a in consts],
        out_specs=pl.BlockSpec((1, q, w), lambda b, c: (b, c, 0)),
        out_shape=jax.ShapeDtypeStruct((bsz, seq, w), BF16),
        scratch_shapes=[pltpu.VMEM((V7X_SUBLANES, 2 * w), F32),
                        pltpu.VMEM((MB_GROUPS, MB_STATE, w // MB_GROUPS), F32)],
        compiler_params=_cparams("parallel", "arbitrary"),
        name="mamba2",
    )(z3d, z3d, z3d, z3d, *consts)


def _pad_rows(a, rows, at):
    out = jnp.zeros((rows, a.shape[1]), a.dtype)
    return lax.dynamic_update_slice(out, a, (at, 0))


def _block_ones(w, head):
    idx = jnp.arange(w) // head
    return (idx[:, None] == idx[None, :]).astype(BF16)


def _pick_tile(n, target):
    t = min(n, target)
    while n % t:
        t //= 2
    return t


def kernel(x, norm_mix_w, w_in, w_branch, w_out, norm_ffn_w, w_ffn_in, w_ffn_out, norm_final_w, hgrn_lower_bounds, hgrn_norm_w, rwkv_mu, rwkv_w0, rwkv_w2, rwkv_a0, rwkv_a2, rwkv_g2, rwkv_k_k, rwkv_k_a, rwkv_r_k, rwkv_ln_w, rwkv_ln_b, rwkv_v0, rwkv_v1, rwkv_v2, s5_a_re, s5_a_im, s5_b_re, s5_b_im, s5_c_re, s5_c_im, s5_d, s5_log_dt, s5_w_glu, s5_b_glu, mamba_conv_w, mamba_conv_b, mamba_dt_bias, mamba_a_log, mamba_d, mamba_norm_w):
    bsz, seq, d = x.shape
    depth = w_in.shape[0]
    w = d // 2
    lanes = V7X_LANES
    t = bsz * seq
    mb_heads = w // MB_HEADDIM
    mb_bc = 2 * MB_GROUPS * MB_STATE
    lw = RW_DECAY_LORA + RW_A_LORA + RW_G_LORA
    assert mb_bc == w and 3 * w % lw == 0

    o_gate = 0
    o_hg = o_gate + N_BRANCH * d
    o_rw = o_hg + 4 * w
    o_rwl = o_rw + 3 * w
    o_s5 = o_rwl + lw
    o_mbg = o_s5 + w
    o_mbx = o_mbg + w
    o_mbbc = o_mbx + w
    o_mbdt = o_mbbc + mb_bc
    c_rw = 0
    c_hg = c_rw + 3 * w
    c_s5 = c_hg + 4 * w
    c_mbg = c_s5 + w
    c_mbx = c_mbg + w
    c_mbbc = c_mbx + w
    c_rwl = c_mbbc + mb_bc
    c_mbdt = c_rwl + lw
    n_cols = c_mbdt + lanes
    tn = 512
    n_pad = -(-n_cols // tn) * tn

    def mixer_cols(wl):
        sl = lambda a, b: wl[:, a:b]
        pieces = [sl(o_rw, o_rwl), sl(o_hg, o_rw), sl(o_s5, o_mbdt), sl(o_rwl, o_s5),
                  sl(o_mbdt, o_mbdt + mb_heads),
                  jnp.zeros((d, n_pad - c_mbdt - mb_heads), wl.dtype)]
        return jnp.concatenate(pieces, axis=1).astype(BF16)

    tm = _pick_tile(seq, 256)
    tm_in = _pick_tile(seq, 512)
    tb_hg = _pick_tile(seq, 512)
    c_hg_chunk = min(HG_CHUNK, tb_hg)
    tc_rw = _pick_tile(seq, 32)
    tc_s5 = _pick_tile(seq, 64)
    q_mb = min(MB_CHUNK, seq)
    ff = w_ffn_out.shape[1]
    tn_ff = 256 if ff % 256 == 0 else lanes

    lower_bounds = _hg_bounds(hgrn_lower_bounds)
    ones_rw = _block_ones(w, RW_HEAD)
    eye_g = jnp.eye(lanes // S5_GROUP, dtype=F32)
    expand = (jnp.arange(lanes)[:, None] == (jnp.arange(w) // MB_HEADDIM)[None, :]).astype(BF16)
    tril = (jnp.arange(q_mb)[:, None] >= jnp.arange(q_mb)[None, :]).astype(BF16)

    x2d = x.reshape(t, d)
    v_first = jnp.zeros((t, w), F32)
    for l in range(depth):
        gates = _inproj(x2d, norm_mix_w[l], w_in[l][:, o_gate:o_hg].astype(BF16), tm_in, tn, True)
        z2d = _inproj(x2d, norm_mix_w[l], mixer_cols(w_in[l]), tm_in, tn, False)
        z3d = z2d.reshape(bsz, seq, n_pad)

        mu = rwkv_mu[l]
        has_vres = l > 0
        lv = max(l - 1, 0)
        rp = {
            "mu_main": mu[None, :3 * w], "mu_lora": mu[None, 3 * w:],
            "w0": rwkv_w0[l][None], "a0": rwkv_a0[l][None],
            "w2p": _pad_rows(rwkv_w2[l], lw, 0).astype(BF16),
            "a2p": _pad_rows(rwkv_a2[l], lw, RW_DECAY_LORA).astype(BF16),
            "g2p": _pad_rows(rwkv_g2[l], lw, RW_DECAY_LORA + RW_A_LORA).astype(BF16),
            "k_k": rwkv_k_k[l][None], "k_a": rwkv_k_a[l][None], "r_k": rwkv_r_k[l].reshape(1, w),
            "v0": rwkv_v0[lv][None],
            "v1p": jnp.pad(rwkv_v1[lv], ((0, 0), (0, lanes - rwkv_v1.shape[2]))).astype(BF16),
            "v2p": _pad_rows(rwkv_v2[lv], lanes, 0).astype(BF16),
            "ones": ones_rw, "ln_w": rwkv_ln_w[l], "ln_b": rwkv_ln_b[l],
        }
        rw_x, v_l, rw_g, rw_bonus = _rwkv7_front(z2d, v_first, rp, bsz, seq, w, c_rw, c_rwl, lw, tm,
                                                 has_vres)
        if l == 0:
            v_first = v_l

        lam_re, lam_im, bb_re, bb_im = _s5_discretize(s5_a_re[l], s5_a_im[l], s5_log_dt[l],
                                                      s5_b_re[l], s5_b_im[l])
        ns = lam_re.size
        nsg = w // lanes
        gps = lanes // S5_GROUP
        blk_in = lambda bb: jnp.einsum(
            "sgnc,gh->sgchn", bb.reshape(nsg, gps, S5_STATE, S5_GROUP), eye_g
        ).reshape(nsg, lanes, gps * S5_STATE)
        blk_out = lambda cc: jnp.einsum(
            "sgcn,gh->sgnhc", cc.reshape(nsg, gps, S5_GROUP, S5_STATE), eye_g
        ).reshape(nsg, gps * S5_STATE, lanes)
        sp = {
            "bblk": jnp.concatenate([blk_in(bb_re), blk_in(bb_im)], axis=2).astype(BF16),
            "cblk": jnp.concatenate([blk_out(s5_c_re[l]), -blk_out(s5_c_im[l])], axis=1).astype(BF16),
            "lam_re": lam_re.reshape(1, ns), "lam_im": lam_im.reshape(1, ns),
            "d": s5_d[l][None], "w_glu": s5_w_glu[l].astype(BF16), "b_glu": s5_b_glu[l][None],
        }
        y_s5 = _s5(z3d, c_s5, w, sp, tc_s5).reshape(t, w)

        mp = {
            "conv_w": mamba_conv_w[l], "conv_b": mamba_conv_b[l][None],
            "dt_bias": jnp.pad(mamba_dt_bias[l], (0, lanes - mb_heads))[None],
            "a_log": jnp.pad(mamba_a_log[l], (0, lanes - mb_heads))[None],
            "d": jnp.repeat(mamba_d[l], MB_HEADDIM)[None], "norm_w": mamba_norm_w[l][None],
            "expand": expand, "tril": tril,
        }
        y_mb = _mamba2(z3d, mp, w, (c_mbg, c_mbx, c_mbbc, c_mbdt), q_mb).reshape(t, w)

        rw_y = _rw_scan(rw_x, lanes // (bsz * (w // RW_HEAD)), tc_rw, (gates, y_s5, y_mb))
        y_hg = _hgrn2(z3d, lower_bounds[l], hgrn_norm_w[l], w, c_hg, tb_hg, c_hg_chunk,
                      HG_SUBBLOCK, rw_y).reshape(t, w)
        rw_heads = w // RW_HEAD
        rw_raw = _lanes_to_chains_v(rw_y, bsz, seq, rw_heads, RW_HEAD, lanes // (bsz * rw_heads))
        x2d = _merge(y_hg, y_s5, y_mb, (rw_raw, rw_g, rw_bonus, rp["ln_w"], rp["ln_b"], rp["ones"]),
                     gates, w_branch[l].astype(BF16), w_out[l].astype(BF16), x2d, tm)
        act = _ffn_in(x2d, norm_ffn_w[l], w_ffn_in[l].astype(BF16), tm_in, tn_ff)
        x2d = _ffn_out(act, w_ffn_out[l].astype(BF16), x2d, norm_final_w, l == depth - 1, tm)
    return x2d.reshape(bsz, seq, d)
```

```python
import functools
import math

import jax
import jax.numpy as jnp
from jax import lax
from jax.experimental import pallas as pl
from jax.experimental.pallas import tpu as pltpu

F32 = jnp.float32
BF16 = jnp.bfloat16

V7X_LANES = 128
V7X_SUBLANES = 8
V7X_VMEM_LIMIT_BYTES = 56 * 1024 * 1024

RMS_EPS = 1e-6
N_BRANCH = 4
HG_DK = 128
HG_TINY = 1e-30
HG_CHUNK = 64
HG_SUBBLOCK = 4
RW_HEAD = 64
RW_DECAY_LORA = 64
RW_A_LORA = 64
RW_G_LORA = 128
RW_LN_EPS = 64e-5
RW_NVEC = 6
S5_GROUP = 16
S5_STATE = 64
MB_HEADDIM = 64
MB_GROUPS = 2
MB_STATE = 128
MB_CONV = 4
MB_CHUNK = 128


def _cparams(*sem):
    return pltpu.CompilerParams(dimension_semantics=sem, vmem_limit_bytes=V7X_VMEM_LIMIT_BYTES)


def _bdot(a, b):
    return jnp.dot(a.astype(BF16), b.astype(BF16), preferred_element_type=F32)


def _split(x, parts):
    pieces, rest = [], x
    for i in range(parts):
        piece = rest.astype(BF16)
        pieces.append(piece)
        if i + 1 < parts:
            rest = rest - piece.astype(F32)
    return pieces


def _dot_exact_rhs(x, m, parts=3):
    return sum(jnp.dot(p, m, preferred_element_type=F32) for p in _split(x, parts))


def _dot_exact_lhs(m, x, parts=3):
    return sum(jnp.dot(m, p, preferred_element_type=F32) for p in _split(x, parts))


def _softplus(x):
    return jnp.maximum(x, 0.0) + jnp.log(1.0 + jnp.exp(-jnp.abs(x)))


def _silu(x):
    return x * jax.nn.sigmoid(x)


def _rms(x, w):
    return x * lax.rsqrt(jnp.mean(x * x, axis=-1, keepdims=True) + RMS_EPS) * w


def _resident(shape):
    return pl.BlockSpec(shape, lambda i: (0,) * len(shape), pipeline_mode=pl.Buffered(1))


def _inproj_kernel(x_ref, nw_ref, w_ref, after_ref, o_ref, *, gate, tn):
    u = _rms(x_ref[...], nw_ref[...]).astype(BF16)
    for n0 in range(0, o_ref.shape[1], tn):
        z = jnp.dot(u, w_ref[:, n0:n0 + tn], preferred_element_type=F32)
        o_ref[:, n0:n0 + tn] = jax.nn.sigmoid(z).astype(o_ref.dtype) if gate else z


def _inproj(x2d, norm_w, w_bf16, tm, tn, gate, after):
    t, d = x2d.shape
    n = w_bf16.shape[1]
    return pl.pallas_call(
        functools.partial(_inproj_kernel, gate=gate, tn=tn),
        grid=(t // tm,),
        in_specs=[pl.BlockSpec((tm, d), lambda i: (i, 0)),
                  pl.BlockSpec((1, d), lambda i: (0, 0)),
                  _resident((d, n)),
                  pl.BlockSpec((1, V7X_SUBLANES, V7X_LANES), lambda i: (0, 0, 0))],
        out_specs=pl.BlockSpec((tm, n), lambda i: (i, 0)),
        out_shape=jax.ShapeDtypeStruct((t, n), BF16 if gate else F32),
        compiler_params=_cparams("parallel"),
        name="inproj_gate" if gate else "inproj",
    )(x2d, norm_w.reshape(1, d), w_bf16, after)


def _ffn_in_kernel(x_ref, nw_ref, w_ref, o_ref, *, tn):
    ff = o_ref.shape[1]
    u = _rms(x_ref[...], nw_ref[...]).astype(BF16)
    for n0 in range(0, ff, tn):
        gate = jnp.dot(u, w_ref[:, n0:n0 + tn], preferred_element_type=F32)
        up = jnp.dot(u, w_ref[:, ff + n0:ff + n0 + tn], preferred_element_type=F32)
        o_ref[:, n0:n0 + tn] = (_silu(gate) * up).astype(BF16)


def _ffn_in(x2d, norm_w, w_bf16, tm, tn):
    t, d = x2d.shape
    ff = w_bf16.shape[1] // 2
    return pl.pallas_call(
        functools.partial(_ffn_in_kernel, tn=tn),
        grid=(t // tm,),
        in_specs=[pl.BlockSpec((tm, d), lambda i: (i, 0)),
                  pl.BlockSpec((1, d), lambda i: (0, 0)),
                  _resident((d, 2 * ff))],
        out_specs=pl.BlockSpec((tm, ff), lambda i: (i, 0)),
        out_shape=jax.ShapeDtypeStruct((t, ff), BF16),
        compiler_params=_cparams("parallel"),
        name="ffn_in",
    )(x2d, norm_w.reshape(1, d), w_bf16)


def _ffn_out_kernel(a_ref, w_ref, x_ref, fw_ref, o_ref, *, final_norm):
    y = x_ref[...] + jnp.dot(a_ref[...], w_ref[...], preferred_element_type=F32)
    if final_norm:
        y = _rms(y, fw_ref[...])
    o_ref[...] = y


def _ffn_out(act, w_bf16, x2d, final_w, final_norm, tm):
    t, d = x2d.shape
    ff = act.shape[1]
    return pl.pallas_call(
        functools.partial(_ffn_out_kernel, final_norm=final_norm),
        grid=(t // tm,),
        in_specs=[pl.BlockSpec((tm, ff), lambda i: (i, 0)),
                  _resident((ff, d)),
                  pl.BlockSpec((tm, d), lambda i: (i, 0)),
                  pl.BlockSpec((1, d), lambda i: (0, 0))],
        out_specs=pl.BlockSpec((tm, d), lambda i: (i, 0)),
        out_shape=jax.ShapeDtypeStruct((t, d), F32),
        compiler_params=_cparams("parallel"),
        name="ffn_out",
    )(act, w_bf16, x2d, final_w.reshape(1, d))


def _merge_kernel(yh_ref, ys_ref, ym_ref, yr_ref, rg_ref, rb_ref, lnw_ref, lnb_ref, ones_ref,
                  zg_ref, wb_ref, wo_ref, x_ref, o_ref):
    d = x_ref.shape[1]
    y_rw = _rw_finish(yr_ref[...], rg_ref[...], rb_ref[...], lnw_ref[...], lnb_ref[...],
                      ones_ref[...])
    acc = jnp.zeros(x_ref.shape, F32)
    for k, y in enumerate((yh_ref[...], y_rw, ys_ref[...], ym_ref[...])):
        proj = jnp.dot(y.astype(BF16), wb_ref[k], preferred_element_type=F32)
        acc = acc + zg_ref[:, k * d:(k + 1) * d].astype(F32) * proj
    o_ref[...] = x_ref[...] + jnp.dot(acc.astype(BF16), wo_ref[...], preferred_element_type=F32)


def _merge(y_hg, y_s5, y_mb, rw, gates, wb_bf16, wo_bf16, x2d, tm):
    t, d = x2d.shape
    w = y_hg.shape[1]
    yspec = pl.BlockSpec((tm, w), lambda i: (i, 0))
    row = pl.BlockSpec((1, w), lambda i: (0, 0))
    y_raw, g, bonus, ln_w, ln_b, ones = rw
    return pl.pallas_call(
        _merge_kernel,
        grid=(t // tm,),
        in_specs=[yspec, yspec, yspec, yspec, yspec, yspec, row, row, _resident((w, w)),
                  pl.BlockSpec((tm, N_BRANCH * d), lambda i: (i, 0)),
                  _resident((N_BRANCH, w, d)),
                  _resident((d, d)),
                  pl.BlockSpec((tm, d), lambda i: (i, 0))],
        out_specs=pl.BlockSpec((tm, d), lambda i: (i, 0)),
        out_shape=jax.ShapeDtypeStruct((t, d), F32),
        compiler_params=_cparams("parallel"),
        name="merge",
    )(y_hg, y_s5, y_mb, y_raw, g, bonus, ln_w.reshape(1, w), ln_b.reshape(1, w), ones,
      gates, wb_bf16, wo_bf16, x2d)


def _hg_bounds_kernel(h_ref, o_ref):
    h = h_ref[...]
    depth = h.shape[0]
    m = jnp.max(h, axis=0, keepdims=True)
    e = jnp.exp(h - m)
    p = e / jnp.sum(e, axis=0, keepdims=True)
    run = jnp.zeros_like(p[0:1])
    rows = []
    for l in range(depth):
        run = run + p[l:l + 1]
        rows.append(run - p[0:1])
    o_ref[...] = jnp.concatenate(rows, axis=0)


def _hg_bounds(hgrn_lower_bounds):
    return pl.pallas_call(
        _hg_bounds_kernel,
        out_shape=jax.ShapeDtypeStruct(hgrn_lower_bounds.shape, F32),
        name="hg_bounds",
    )(hgrn_lower_bounds)


def _hg_chunk_kernel(q_ref, f_ref, i_ref, g_ref, lb_ref, nw_ref, tril_ref, after_ref, y_ref,
                     st_ref, kbuf, bbuf, vbuf, *, tb, c, cs, heads, dk):
    sub = V7X_SUBLANES
    nt = (((1,), (1,)), ((), ()))

    @pl.when(pl.program_id(1) == 0)
    def _():
        st_ref[...] = jnp.zeros(st_ref.shape, F32)

    kbuf[0:sub, :] = jnp.zeros((sub, kbuf.shape[1]), F32)
    bbuf[0:sub, :] = jnp.zeros((sub, bbuf.shape[1]), F32)
    vbuf[0:sub, :] = jnp.zeros((sub, vbuf.shape[1]), F32)
    lb = lb_ref[...]
    tril = tril_ref[...]
    rid = lax.broadcasted_iota(jnp.int32, (c, dk), 0)
    rid1 = lax.broadcasted_iota(jnp.int32, (c, 1), 0)
    pr = lax.broadcasted_iota(jnp.int32, (c, c), 0)
    pc = lax.broadcasted_iota(jnp.int32, (c, c), 1)

    def chunk(ci):
        rows = pl.ds(ci * c, c)
        ff = f_ref[0, rows, :]
        q = _silu(q_ref[0, rows, :])
        dec = jnp.maximum(lb + (1.0 - lb) * jax.nn.sigmoid(ff), HG_TINY)
        k = (1.0 - lb) * jax.nn.sigmoid(-ff)
        v = i_ref[0, rows, :]
        b = _dot_exact_lhs(tril, jnp.log(dec))
        base = sub + ci * c
        kbuf[base:base + c, :] = k
        bbuf[base:base + c, :] = b
        vbuf[base:base + c, :] = v
        outs = []
        for h in range(heads):
            hs = slice(h * dk, (h + 1) * dk)
            qh, kh, bh, vh = q[:, hs], k[:, hs], b[:, hs], v[:, hs]
            blast = bh[c - 1:c, :]
            st = st_ref[h]
            o = lax.dot_general((qh * jnp.exp(bh)).astype(BF16), st.astype(BF16), nt,
                                preferred_element_type=F32)
            att = jnp.zeros((c, c), F32)
            grp = 2 * cs
            while grp <= c:
                half = grp // 2
                qparts, kparts = [], []
                for r0 in range(0, c, grp):
                    d = bh[r0:r0 + grp, :] - bh[r0 + half - 1:r0 + half, :]
                    if half % sub == 0:
                        zero = jnp.zeros((half, dk), F32)
                        kparts += [kh[r0:r0 + half, :] * jnp.exp(-d[0:half, :]), zero]
                        qparts += [zero, qh[r0 + half:r0 + grp, :] * jnp.exp(d[half:grp, :])]
                    else:
                        upper = (rid[0:grp, :] % grp) >= half
                        e = jnp.exp(jnp.where(upper, d, -d))
                        kparts.append(jnp.where(upper, 0.0, kh[r0:r0 + grp, :] * e))
                        qparts.append(jnp.where(upper, qh[r0:r0 + grp, :] * e, 0.0))
                a = lax.dot_general(jnp.concatenate(qparts, axis=0).astype(BF16),
                                    jnp.concatenate(kparts, axis=0).astype(BF16), nt,
                                    preferred_element_type=F32)
                att = att + (a if grp == c else jnp.where(pr // grp == pc // grp, a, 0.0))
                grp *= 2
            o = o + _bdot(att, vh)
            for dlt in range(cs):
                if dlt == 0:
                    a = jnp.sum(qh * kh, axis=-1, keepdims=True)
                    o = o + a * vh
                else:
                    win = pl.ds(base - dlt, c)
                    e = jnp.exp(jnp.minimum(bh - bbuf[win, hs], 0.0))
                    a = jnp.sum(qh * kbuf[win, hs] * e, axis=-1, keepdims=True)
                    a = jnp.where((rid1 % cs) >= dlt, a, 0.0)
                    o = o + a * vbuf[win, hs]
            kd = kh * jnp.exp(blast - bh)
            st_ref[h] = st * jnp.exp(blast) + _bdot(vh.T, kd)
            outs.append(o * lax.rsqrt(jnp.mean(o * o, axis=-1, keepdims=True) + RMS_EPS))
        y = jnp.concatenate(outs, axis=-1) * nw_ref[...]
        y_ref[0, rows, :] = (y * _silu(g_ref[0, rows, :])).astype(y_ref.dtype)

    for ci in range(tb // c):
        chunk(ci)


def _hgrn2(z3d, lb, norm_w, w, col0, tb, c, cs, after):
    bsz, seq, _ = z3d.shape
    heads = w // HG_DK
    cb = col0 // w
    tril = (jnp.arange(c)[:, None] >= jnp.arange(c)[None, :]).astype(BF16)
    zspec = lambda j: pl.BlockSpec((1, tb, w), lambda b, i: (b, i, cb + j))
    row = pl.BlockSpec((1, w), lambda b, i: (0, 0))
    return pl.pallas_call(
        functools.partial(_hg_chunk_kernel, tb=tb, c=c, cs=cs, heads=heads, dk=HG_DK),
        grid=(bsz, seq // tb),
        in_specs=[zspec(0), zspec(1), zspec(2), zspec(3), row, row,
                  pl.BlockSpec((c, c), lambda b, i: (0, 0)),
                  pl.BlockSpec((1, V7X_SUBLANES, V7X_LANES), lambda b, i: (0, 0, 0))],
        out_specs=pl.BlockSpec((1, tb, w), lambda b, i: (b, i, 0)),
        out_shape=jax.ShapeDtypeStruct((bsz, seq, w), BF16),
        scratch_shapes=[pltpu.VMEM((heads, HG_DK, HG_DK), F32)]
        + [pltpu.VMEM((tb + V7X_SUBLANES, w), F32)] * 3,
        compiler_params=_cparams("parallel", "arbitrary"),
        name="hgrn2",
    )(z3d, z3d, z3d, z3d, lb.reshape(1, w), norm_w.reshape(1, w), tril, after)


def _lanes_to_chains_v(a, bsz, seq, heads, dv, nq):
    a = a.reshape(seq, dv // nq, nq, bsz, heads).transpose(3, 0, 4, 2, 1)
    return a.reshape(bsz * seq, heads * dv)


def _shift_rows(cur, prev_last, first):
    rolled = pltpu.roll(cur, 1, axis=0)
    row0 = jnp.where(first, 0.0, prev_last)
    rid = lax.broadcasted_iota(jnp.int32, cur.shape, 0)
    return jnp.where(rid == 0, jnp.broadcast_to(row0, cur.shape), rolled)


def _rw_prep_kernel(zm_ref, zmp_ref, zl_ref, zlp_ref, vf_ref,
                    mum_ref, mul_ref, w0_ref, w2_ref, a0_ref, a2_ref, g2_ref,
                    kk_ref, ka_ref, rk_ref, v0_ref, v1_ref, v2_ref, ones_ref,
                    kvec_out, v_out, g_out, bonus_out,
                    *, w, tiles_per_seq, has_vres):
    first = (pl.program_id(0) % tiles_per_seq) == 0
    sub = V7X_SUBLANES
    zm = zm_ref[...]
    zl = zl_ref[...]
    zms = zm + (_shift_rows(zm, zmp_ref[sub - 1:sub, :], first) - zm) * mum_ref[...]
    zls = zl + (_shift_rows(zl, zlp_ref[sub - 1:sub, :], first) - zl) * mul_ref[...]
    r = zms[:, 0:w]
    k = zms[:, w:2 * w]
    v = zms[:, 2 * w:3 * w]
    w_log = -_softplus(-(w0_ref[...] + _bdot(jnp.tanh(zls), w2_ref[...]))) - 0.5
    decay = jnp.exp(-jnp.exp(w_log))
    if has_vres:
        mix = jax.nn.sigmoid(v0_ref[...] + _bdot(_bdot(v, v1_ref[...]), v2_ref[...]))
        v = v + (vf_ref[...] - v) * mix
    a = jax.nn.sigmoid(a0_ref[...] + _bdot(zls, a2_ref[...]))
    g = _bdot(jax.nn.sigmoid(zls), g2_ref[...])
    ones = ones_ref[...]
    kk = k * kk_ref[...]
    ss = _dot_exact_rhs(kk * kk, ones)
    kk = kk / jnp.maximum(jnp.sqrt(ss), 1e-12)
    k2 = k * (1.0 + (a - 1.0) * ka_ref[...])
    vecs = (r, decay, k2, -kk, kk * a, v)
    for h in range(w // RW_HEAD):
        hs = slice(h * RW_HEAD, (h + 1) * RW_HEAD)
        row = jnp.concatenate([x[:, hs] for x in vecs], axis=1)
        for q in range(kvec_out.shape[0]):
            kvec_out[q, 0, h] = row
    v_out[...] = v
    g_out[...] = g.astype(g_out.dtype)
    bonus_out[...] = (_dot_exact_rhs(r * k2 * rk_ref[...], ones) * v).astype(bonus_out.dtype)


def _rw_prep(z2d, v_first, p, col_main, col_lora, w, lw, seq, tm, has_vres):
    t = z2d.shape[0]
    sub = V7X_SUBLANES
    mb = col_main // (3 * w)
    lbk = col_lora // lw
    rows8 = tm // sub
    heads = w // RW_HEAD
    tps = seq // tm
    nq = V7X_LANES // (t // seq * heads)

    def prev_idx(i):
        return jnp.maximum(i * rows8 - 1, 0)

    row = lambda n: pl.BlockSpec((1, n), lambda i: (0, 0))
    full = lambda a: pl.BlockSpec(a.shape, lambda i: (0,) * a.ndim)
    ospec = pl.BlockSpec((tm, w), lambda i: (i, 0))
    oshape = jax.ShapeDtypeStruct((t, w), F32)
    args = [z2d, z2d, z2d, z2d, v_first,
            p["mu_main"], p["mu_lora"], p["w0"], p["w2p"], p["a0"], p["a2p"], p["g2p"],
            p["k_k"], p["k_a"], p["r_k"], p["v0"], p["v1p"], p["v2p"], p["ones"]]
    in_specs = [pl.BlockSpec((tm, 3 * w), lambda i: (i, mb)),
                pl.BlockSpec((sub, 3 * w), lambda i: (prev_idx(i), mb)),
                pl.BlockSpec((tm, lw), lambda i: (i, lbk)),
                pl.BlockSpec((sub, lw), lambda i: (prev_idx(i), lbk)),
                ospec,
                row(3 * w), row(lw), row(w), full(p["w2p"]), row(w), full(p["a2p"]), full(p["g2p"]),
                row(w), row(w), row(w), row(w), full(p["v1p"]), full(p["v2p"]), full(p["ones"])]
    return pl.pallas_call(
        functools.partial(_rw_prep_kernel, w=w, tiles_per_seq=seq // tm, has_vres=has_vres),
        grid=(t // tm,),
        in_specs=in_specs,
        out_specs=[pl.BlockSpec((nq, 1, heads, tm, RW_NVEC * RW_HEAD),
                                lambda i: (0, i // tps, 0, i % tps, 0)), ospec, ospec, ospec],
        out_shape=[jax.ShapeDtypeStruct((nq, t // seq, heads, seq, RW_NVEC * RW_HEAD), F32),
                   oshape, jax.ShapeDtypeStruct((t, w), BF16), jax.ShapeDtypeStruct((t, w), BF16)],
        compiler_params=_cparams("parallel"),
        name="rw_prep",
    )(*args)


def _rw_scan_kernel(x_ref, *rest, tc, dk, nvb, nq):
    y_ref, s_ref = rest[-2:]
    sub = V7X_SUBLANES
    lanes = V7X_LANES
    jr, jw, jk, ja, jb, jv = range(RW_NVEC)
    vl = nvb * sub
    qid = lax.broadcasted_iota(jnp.int32, (sub, lanes), 1) // (lanes // nq)

    @pl.when(pl.program_id(0) == 0)
    def _():
        s_ref[...] = jnp.zeros(s_ref.shape, F32)

    def bc(j, t, kk):
        return jnp.broadcast_to(x_ref[t, j, pl.ds(kk, 1), :], (sub, lanes))

    sa0 = [jnp.zeros((sub, lanes), F32) for _ in range(nvb)]
    for kk in range(dk):
        arow = bc(ja, 0, kk)
        for j in range(nvb):
            sa0[j] = sa0[j] + s_ref[kk, pl.ds(sub * j, sub), :] * arow

    def step(t, sa):
        tn = jnp.minimum(t + 1, tc - 1)
        vb = []
        for j in range(nvb):
            vj = x_ref[t, jv, pl.ds(sub * j, sub), :]
            for q in range(1, nq):
                vj = jnp.where(qid == q, x_ref[t, jv, pl.ds(q * vl + sub * j, sub), :], vj)
            vb.append(vj)
        yacc = [jnp.zeros((sub, lanes), F32) for _ in range(nvb)]
        sacc = [jnp.zeros((sub, lanes), F32) for _ in range(nvb)]
        for kk in range(dk):
            wrow = bc(jw, t, kk)
            brow = bc(jb, t, kk)
            krow = bc(jk, t, kk)
            rrow = bc(jr, t, kk)
            anext = bc(ja, tn, kk)
            for j in range(nvb):
                s = s_ref[kk, pl.ds(sub * j, sub), :] * wrow + sa[j] * brow + vb[j] * krow
                s_ref[kk, pl.ds(sub * j, sub), :] = s
                yacc[j] = yacc[j] + s * rrow
                sacc[j] = sacc[j] + s * anext
        for j in range(nvb):
            y_ref[t, pl.ds(sub * j, sub), :] = yacc[j]
        return tuple(sacc)

    lax.fori_loop(0, tc, step, tuple(sa0))


def _rw_scan(x_l, nq, tc, after):
    s, nvec, dk, lanes = x_l.shape
    vl = dk // nq
    kspec = pl.BlockSpec((tc, nvec, dk, lanes), lambda i: (i, 0, 0, 0))
    vspec = pl.BlockSpec((tc, vl, lanes), lambda i: (i, 0, 0))
    order = [pl.BlockSpec((16, lanes), lambda i: (0, 0))] * len(after)
    return pl.pallas_call(
        functools.partial(_rw_scan_kernel, tc=tc, dk=dk, nvb=vl // V7X_SUBLANES, nq=nq),
        grid=(s // tc,),
        in_specs=[kspec] + order,
        out_specs=vspec,
        out_shape=jax.ShapeDtypeStruct((s, vl, lanes), F32),
        scratch_shapes=[pltpu.VMEM((dk, vl, lanes), F32)],
        compiler_params=_cparams("arbitrary"),
        name="rw_scan",
    )(x_l, *after)


def _rw_finish(y, g, bonus, ln_w, ln_b, ones):
    inv_n = 1.0 / RW_HEAD
    mean = _dot_exact_rhs(y, ones, parts=2) * inv_n
    yc = y - mean
    var = _dot_exact_rhs(yc * yc, ones, parts=2) * inv_n
    y = yc * lax.rsqrt(var + RW_LN_EPS) * ln_w + ln_b
    return (y + bonus) * g


def _rwkv7_front(z2d, v_first, p, bsz, seq, w, col_main, col_lora, lw, tm, has_vres):
    kvec, v, g, bonus = _rw_prep(z2d, v_first, p, col_main, col_lora, w, lw, seq, tm, has_vres)
    lanes = kvec.shape[0] * kvec.shape[1] * kvec.shape[2]
    x_l = kvec.reshape(lanes, seq, RW_NVEC * RW_HEAD).transpose(1, 2, 0)
    return x_l.reshape(seq, RW_NVEC, RW_HEAD, lanes), v, g, bonus


def _s5_disc_kernel(are_ref, aim_ref, dt_ref, bre_ref, bim_ref, lre_ref, lim_ref, bbre_ref, bbim_ref):
    a_re = are_ref[...]
    a_im = aim_ref[...]
    dt = jnp.exp(dt_ref[...])
    mag = jnp.exp(dt * a_re)
    lam_re = mag * jnp.cos(dt * a_im)
    lam_im = mag * jnp.sin(dt * a_im)
    den = a_re * a_re + a_im * a_im
    coef_re = ((lam_re - 1.0) * a_re + lam_im * a_im) / den
    coef_im = (lam_im * a_re - (lam_re - 1.0) * a_im) / den
    b_re = bre_ref[...]
    b_im = bim_ref[...]
    lre_ref[...] = lam_re
    lim_ref[...] = lam_im
    bbre_ref[...] = coef_re * b_re - coef_im * b_im
    bbim_ref[...] = coef_re * b_im + coef_im * b_re


def _s5_discretize(a_re, a_im, log_dt, b_re, b_im):
    g, n, c = b_re.shape
    shp = (g, n * c)
    bc = lambda a: jnp.broadcast_to(a[..., None], (g, n, c)).reshape(shp)
    dtb = jnp.broadcast_to(log_dt[:, None], shp)
    o = jax.ShapeDtypeStruct(shp, F32)
    lre, lim, bbre, bbim = pl.pallas_call(
        _s5_disc_kernel, out_shape=[o, o, o, o], name="s5_disc",
    )(bc(a_re), bc(a_im), dtb, b_re.reshape(shp), b_im.reshape(shp))
    un = lambda a: a.reshape(g, n, c)
    return un(lre)[..., 0], un(lim)[..., 0], un(bbre), un(bbim)


def _s5_kernel(u_ref, bblk_ref, cblk_ref, lre_ref, lim_ref, d_ref, wg_ref, bg_ref, y_ref,
               h_ref, hr_ref, hi_ref, *, tc, bsz, ns, lane_chunk):
    @pl.when(pl.program_id(0) == 0)
    def _():
        hr_ref[...] = jnp.zeros(hr_ref.shape, F32)
        hi_ref[...] = jnp.zeros(hi_ref.shape, F32)

    w = u_ref.shape[2]
    u = jnp.swapaxes(u_ref[...], 0, 1).reshape(tc * bsz, w)
    nsg = bblk_ref.shape[0]
    lanes = V7X_LANES
    for sg in range(nsg):
        c0 = sg * lane_chunk
        drive = _bdot(u[:, sg * lanes:(sg + 1) * lanes], bblk_ref[sg])
        h_ref[:, c0:c0 + lane_chunk] = drive[:, 0:lane_chunk]
        h_ref[:, ns + c0:ns + c0 + lane_chunk] = drive[:, lane_chunk:2 * lane_chunk]
    for c0 in range(0, ns, lane_chunk):
        lr = jnp.broadcast_to(lre_ref[:, c0:c0 + lane_chunk], (bsz, lane_chunk))
        li = jnp.broadcast_to(lim_ref[:, c0:c0 + lane_chunk], (bsz, lane_chunk))

        def step(t, carry, c0=c0, lr=lr, li=li):
            hr, hi = carry
            rows = pl.ds(pl.multiple_of(t * bsz, bsz), bsz)
            nr = lr * hr - li * hi + h_ref[rows, c0:c0 + lane_chunk]
            ni = lr * hi + li * hr + h_ref[rows, ns + c0:ns + c0 + lane_chunk]
            h_ref[rows, c0:c0 + lane_chunk] = nr
            h_ref[rows, ns + c0:ns + c0 + lane_chunk] = ni
            return nr, ni

        hr, hi = lax.fori_loop(0, tc, step,
                               (hr_ref[:, c0:c0 + lane_chunk], hi_ref[:, c0:c0 + lane_chunk]))
        hr_ref[:, c0:c0 + lane_chunk] = hr
        hi_ref[:, c0:c0 + lane_chunk] = hi
    outs = []
    for sg in range(nsg):
        c0 = sg * lane_chunk
        outs.append(_bdot(h_ref[:, c0:c0 + lane_chunk], cblk_ref[sg, 0:lane_chunk, :])
                    + _bdot(h_ref[:, ns + c0:ns + c0 + lane_chunk], cblk_ref[sg, lane_chunk:, :]))
    y = jnp.concatenate(outs, axis=-1) + d_ref[...] * u
    y = jax.nn.gelu(y)
    y = y * jax.nn.sigmoid(_bdot(y, wg_ref[...]) + bg_ref[...])
    y_ref[...] = jnp.swapaxes(y.reshape(tc, bsz, w), 0, 1).astype(y_ref.dtype)


def _s5(z3d, col0, w, p, tc):
    bsz, seq, _ = z3d.shape
    ns = p["lam_re"].shape[1]
    blk = tc * bsz
    full = lambda a: pl.BlockSpec(a.shape, lambda i: (0,) * a.ndim)
    return pl.pallas_call(
        functools.partial(_s5_kernel, tc=tc, bsz=bsz, ns=ns, lane_chunk=ns // p["bblk"].shape[0]),
        grid=(seq // tc,),
        in_specs=[pl.BlockSpec((bsz, tc, w), lambda i: (0, i, col0 // w)),
                  full(p["bblk"]), full(p["cblk"]), full(p["lam_re"]), full(p["lam_im"]),
                  full(p["d"]), full(p["w_glu"]), full(p["b_glu"])],
        out_specs=pl.BlockSpec((bsz, tc, w), lambda i: (0, i, 0)),
        out_shape=jax.ShapeDtypeStruct((bsz, seq, w), BF16),
        scratch_shapes=[pltpu.VMEM((blk, 2 * ns), F32),
                        pltpu.VMEM((bsz, ns), F32), pltpu.VMEM((bsz, ns), F32)],
        compiler_params=_cparams("arbitrary"),
        name="s5",
    )(z3d, p["bblk"], p["cblk"], p["lam_re"], p["lam_im"], p["d"], p["w_glu"], p["b_glu"])


def _mb_kernel(gate_ref, x_ref, bc_ref, dt_ref, cw_ref, cb_ref, dtb_ref, alog_ref, dsk_ref, nw_ref,
               expand_ref, tril_ref, after_ref, y_ref, prev_ref, st_ref,
               *, q, w, heads, groups, nstate):
    sub = V7X_SUBLANES
    lanes = V7X_LANES
    hd = w // heads
    gw = w // groups
    hpg = heads // groups

    @pl.when(pl.program_id(1) == 0)
    def _():
        prev_ref[...] = jnp.zeros(prev_ref.shape, F32)
        st_ref[...] = jnp.zeros(st_ref.shape, F32)

    xbc = jnp.concatenate([x_ref[0], bc_ref[0]], axis=-1)
    full = jnp.concatenate([prev_ref[...], xbc], axis=0)
    conv = jnp.broadcast_to(cb_ref[...], xbc.shape)
    for j in range(MB_CONV):
        shift = MB_CONV - 1 - j
        src = full if shift == 0 else pltpu.roll(full, shift, axis=0)
        conv = conv + src[sub:sub + q, :] * cw_ref[j:j + 1, :]
    prev_ref[...] = xbc[q - sub:q, :]
    act = _silu(conv)
    xs = act[:, 0:w]
    bmat = act[:, w:w + groups * nstate]
    cmat = act[:, w + groups * nstate:w + 2 * groups * nstate]

    dt = _softplus(dt_ref[0] + dtb_ref[...])
    a = -jnp.exp(alog_ref[...]) * dt
    a_cum = _dot_exact_lhs(tril_ref[...], a)
    expand = expand_ref[...]
    dt_e = _dot_exact_rhs(dt, expand)
    acum_e = _dot_exact_rhs(a_cum, expand)
    alast_e = acum_e[q - 1:q, :]
    xdt = xs * dt_e
    xdec = xdt * jnp.exp(alast_e - acum_e)

    rid = lax.broadcasted_iota(jnp.int32, (q, q), 0)
    cid = lax.broadcasted_iota(jnp.int32, (q, q), 1)
    causal = rid >= cid
    a_cum_t = a_cum.T
    lane_w = lax.broadcasted_iota(jnp.int32, (q, gw), 1)

    y_parts = []
    for g in range(groups):
        bg = bmat[:, g * nstate:(g + 1) * nstate]
        cg = cmat[:, g * nstate:(g + 1) * nstate]
        scores = lax.dot_general(cg.astype(BF16), bg.astype(BF16), (((1,), (1,)), ((), ())),
                                 preferred_element_type=F32)
        xg = xdt[:, g * gw:(g + 1) * gw]
        yg = _bdot(cg, st_ref[g]) * jnp.exp(acum_e[:, g * gw:(g + 1) * gw])
        for hh in range(hpg):
            h = g * hpg + hh
            col = jnp.broadcast_to(a_cum[:, h:h + 1], (q, q))
            rowv = jnp.broadcast_to(a_cum_t[h:h + 1, :], (q, q))
            decay = jnp.where(causal, jnp.exp(col - rowv), 0.0)
            xh = jnp.where((lane_w >= hh * hd) & (lane_w < (hh + 1) * hd), xg, 0.0)
            yg = yg + _bdot(scores * decay, xh)
        y_parts.append(yg)
        upd = _bdot(bg.T, xdec[:, g * gw:(g + 1) * gw])
        st_ref[g] = st_ref[g] * jnp.exp(alast_e[:, g * gw:(g + 1) * gw]) + upd
    y = jnp.concatenate(y_parts, axis=-1) + dsk_ref[...] * xs
    y_ref[0] = _rms(y * _silu(gate_ref[0]), nw_ref[...]).astype(y_ref.dtype)


def _mamba2(z3d, p, w, cols, q, after):
    bsz, seq, _ = z3d.shape
    heads = w // MB_HEADDIM
    lanes = V7X_LANES
    cg, cx, cbc, cdt = cols
    blk = lambda width, col: pl.BlockSpec((1, q, width), lambda b, c: (b, c, col // width))
    full = lambda a: pl.BlockSpec(a.shape, lambda b, c: (0,) * a.ndim)
    consts = [p["conv_w"], p["conv_b"], p["dt_bias"], p["a_log"], p["d"], p["norm_w"],
              p["expand"], p["tril"]]
    return pl.pallas_call(
        functools.partial(_mb_kernel, q=q, w=w, heads=heads, groups=MB_GROUPS, nstate=MB_STATE),
        grid=(bsz, seq // q),
        in_specs=[blk(w, cg), blk(w, cx), blk(w, cbc), blk(lanes, cdt)] + [full(a) for a in consts]
        + [pl.BlockSpec((1, V7X_SUBLANES, lanes), lambda b, c: (0, 0, 0))],
        out_specs=pl.BlockSpec((1, q, w), lambda b, c: (b, c, 0)),
        out_shape=jax.ShapeDtypeStruct((bsz, seq, w), BF16),
        scratch_shapes=[pltpu.VMEM((V7X_SUBLANES, 2 * w), F32),
                        pltpu.VMEM((MB_GROUPS, MB_STATE, w // MB_GROUPS), F32)],
        compiler_params=_cparams("parallel", "arbitrary"),
        name="mamba2",
    )(z3d, z3d, z3d, z3d, *consts, after)


def _pad_rows(a, rows, at):
    out = jnp.zeros((rows, a.shape[1]), a.dtype)
    return lax.dynamic_update_slice(out, a, (at, 0))


def _block_ones(w, head):
    idx = jnp.arange(w) // head
    return (idx[:, None] == idx[None, :]).astype(BF16)


def _pick_tile(n, target):
    t = min(n, target)
    while n % t:
        t //= 2
    return t


def kernel(x, norm_mix_w, w_in, w_branch, w_out, norm_ffn_w, w_ffn_in, w_ffn_out, norm_final_w, hgrn_lower_bounds, hgrn_norm_w, rwkv_mu, rwkv_w0, rwkv_w2, rwkv_a0, rwkv_a2, rwkv_g2, rwkv_k_k, rwkv_k_a, rwkv_r_k, rwkv_ln_w, rwkv_ln_b, rwkv_v0, rwkv_v1, rwkv_v2, s5_a_re, s5_a_im, s5_b_re, s5_b_im, s5_c_re, s5_c_im, s5_d, s5_log_dt, s5_w_glu, s5_b_glu, mamba_conv_w, mamba_conv_b, mamba_dt_bias, mamba_a_log, mamba_d, mamba_norm_w):
    bsz, seq, d = x.shape
    depth = w_in.shape[0]
    w = d // 2
    lanes = V7X_LANES
    t = bsz * seq
    mb_heads = w // MB_HEADDIM
    mb_bc = 2 * MB_GROUPS * MB_STATE
    lw = RW_DECAY_LORA + RW_A_LORA + RW_G_LORA
    assert mb_bc == w and 3 * w % lw == 0

    o_gate = 0
    o_hg = o_gate + N_BRANCH * d
    o_rw = o_hg + 4 * w
    o_rwl = o_rw + 3 * w
    o_s5 = o_rwl + lw
    o_mbg = o_s5 + w
    o_mbx = o_mbg + w
    o_mbbc = o_mbx + w
    o_mbdt = o_mbbc + mb_bc
    c_rw = 0
    c_hg = c_rw + 3 * w
    c_s5 = c_hg + 4 * w
    c_mbg = c_s5 + w
    c_mbx = c_mbg + w
    c_mbbc = c_mbx + w
    c_rwl = c_mbbc + mb_bc
    c_mbdt = c_rwl + lw
    n_cols = c_mbdt + lanes
    tn = 512
    n_pad = -(-n_cols // tn) * tn

    def mixer_cols(wl):
        sl = lambda a, b: wl[:, a:b]
        pieces = [sl(o_rw, o_rwl), sl(o_hg, o_rw), sl(o_s5, o_mbdt), sl(o_rwl, o_s5),
                  sl(o_mbdt, o_mbdt + mb_heads),
                  jnp.zeros((d, n_pad - c_mbdt - mb_heads), wl.dtype)]
        return jnp.concatenate(pieces, axis=1).astype(BF16)

    tm = _pick_tile(seq, 256)
    tm_in = _pick_tile(seq, 512)
    tb_hg = _pick_tile(seq, 512)
    c_hg_chunk = min(HG_CHUNK, tb_hg)
    tc_rw = _pick_tile(seq, 64)
    tc_s5 = _pick_tile(seq, 64)
    q_mb = min(MB_CHUNK, seq)
    ff = w_ffn_out.shape[1]
    tn_ff = 256 if ff % 256 == 0 else lanes

    lower_bounds = _hg_bounds(hgrn_lower_bounds)
    ones_rw = _block_ones(w, RW_HEAD)
    eye_g = jnp.eye(lanes // S5_GROUP, dtype=F32)
    expand = (jnp.arange(lanes)[:, None] == (jnp.arange(w) // MB_HEADDIM)[None, :]).astype(BF16)
    tril = (jnp.arange(q_mb)[:, None] >= jnp.arange(q_mb)[None, :]).astype(BF16)

    x2d = x.reshape(t, d)
    v_first = jnp.zeros((t, w), F32)
    for l in range(depth):
        x3d = x2d.reshape(bsz, seq, d)
        z2d = _inproj(x2d, norm_mix_w[l], mixer_cols(w_in[l]), tm_in, tn, False, x3d)
        z3d = z2d.reshape(bsz, seq, n_pad)

        mu = rwkv_mu[l]
        has_vres = l > 0
        lv = max(l - 1, 0)
        rp = {
            "mu_main": mu[None, :3 * w], "mu_lora": mu[None, 3 * w:],
            "w0": rwkv_w0[l][None], "a0": rwkv_a0[l][None],
            "w2p": _pad_rows(rwkv_w2[l], lw, 0).astype(BF16),
            "a2p": _pad_rows(rwkv_a2[l], lw, RW_DECAY_LORA).astype(BF16),
            "g2p": _pad_rows(rwkv_g2[l], lw, RW_DECAY_LORA + RW_A_LORA).astype(BF16),
            "k_k": rwkv_k_k[l][None], "k_a": rwkv_k_a[l][None], "r_k": rwkv_r_k[l].reshape(1, w),
            "v0": rwkv_v0[lv][None],
            "v1p": jnp.pad(rwkv_v1[lv], ((0, 0), (0, lanes - rwkv_v1.shape[2]))).astype(BF16),
            "v2p": _pad_rows(rwkv_v2[lv], lanes, 0).astype(BF16),
            "ones": ones_rw, "ln_w": rwkv_ln_w[l], "ln_b": rwkv_ln_b[l],
        }
        rw_x, v_l, rw_g, rw_bonus = _rwkv7_front(z2d, v_first, rp, bsz, seq, w, c_rw, c_rwl, lw, tm,
                                                 has_vres)
        if l == 0:
            v_first = v_l

        lam_re, lam_im, bb_re, bb_im = _s5_discretize(s5_a_re[l], s5_a_im[l], s5_log_dt[l],
                                                      s5_b_re[l], s5_b_im[l])
        ns = lam_re.size
        nsg = w // lanes
        gps = lanes // S5_GROUP
        blk_in = lambda bb: jnp.einsum(
            "sgnc,gh->sgchn", bb.reshape(nsg, gps, S5_STATE, S5_GROUP), eye_g
        ).reshape(nsg, lanes, gps * S5_STATE)
        blk_out = lambda cc: jnp.einsum(
            "sgcn,gh->sgnhc", cc.reshape(nsg, gps, S5_GROUP, S5_STATE), eye_g
        ).reshape(nsg, gps * S5_STATE, lanes)
        sp = {
            "bblk": jnp.concatenate([blk_in(bb_re), blk_in(bb_im)], axis=2).astype(BF16),
            "cblk": jnp.concatenate([blk_out(s5_c_re[l]), -blk_out(s5_c_im[l])], axis=1).astype(BF16),
            "lam_re": lam_re.reshape(1, ns), "lam_im": lam_im.reshape(1, ns),
            "d": s5_d[l][None], "w_glu": s5_w_glu[l].astype(BF16), "b_glu": s5_b_glu[l][None],
        }
        y_s5 = _s5(z3d, c_s5, w, sp, tc_s5).reshape(t, w)

        mp = {
            "conv_w": mamba_conv_w[l], "conv_b": mamba_conv_b[l][None],
            "dt_bias": jnp.pad(mamba_dt_bias[l], (0, lanes - mb_heads))[None],
            "a_log": jnp.pad(mamba_a_log[l], (0, lanes - mb_heads))[None],
            "d": jnp.repeat(mamba_d[l], MB_HEADDIM)[None], "norm_w": mamba_norm_w[l][None],
            "expand": expand, "tril": tril,
        }
        y_hg = _hgrn2(z3d, lower_bounds[l], hgrn_norm_w[l], w, c_hg, tb_hg, c_hg_chunk,
                      HG_SUBBLOCK, z3d).reshape(t, w)
        rw_y = _rw_scan(rw_x, lanes // (bsz * (w // RW_HEAD)), tc_rw, (y_hg, y_s5))
        gates = _inproj(x2d, norm_mix_w[l], w_in[l][:, o_gate:o_hg].astype(BF16), tm_in, tn, True,
                        rw_y)
        y_mb = _mamba2(z3d, mp, w, (c_mbg, c_mbx, c_mbbc, c_mbdt), q_mb, rw_y).reshape(t, w)
        rw_heads = w // RW_HEAD
        rw_raw = _lanes_to_chains_v(rw_y, bsz, seq, rw_heads, RW_HEAD, lanes // (bsz * rw_heads))
        x2d = _merge(y_hg, y_s5, y_mb, (rw_raw, rw_g, rw_bonus, rp["ln_w"], rp["ln_b"], rp["ones"]),
                     gates, w_branch[l].astype(BF16), w_out[l].astype(BF16), x2d, tm)
        act = _ffn_in(x2d, norm_ffn_w[l], w_ffn_in[l].astype(BF16), tm_in, tn_ff)
        x2d = _ffn_out(act, w_ffn_out[l].astype(BF16), x2d, norm_final_w, l == depth - 1, tm)
    return x2d.reshape(bsz, seq, d)
```

```python
import functools
import math

import jax
import jax.numpy as jnp
from jax import lax
from jax.experimental import pallas as pl
from jax.experimental.pallas import tpu as pltpu

F32 = jnp.float32
BF16 = jnp.bfloat16

V7X_LANES = 128
V7X_SUBLANES = 8
V7X_VMEM_LIMIT_BYTES = 56 * 1024 * 1024

RMS_EPS = 1e-6
N_BRANCH = 4
HG_DK = 128
HG_TINY = 1e-30
HG_CHUNK = 64
HG_SUBBLOCK = 4
RW_HEAD = 64
RW_DECAY_LORA = 64
RW_A_LORA = 64
RW_G_LORA = 128
RW_LN_EPS = 64e-5
RW_NVEC = 6
S5_GROUP = 16
S5_STATE = 64
MB_HEADDIM = 64
MB_GROUPS = 2
MB_STATE = 128
MB_CONV = 4
MB_CHUNK = 128


def _cparams(*sem):
    return pltpu.CompilerParams(dimension_semantics=sem, vmem_limit_bytes=V7X_VMEM_LIMIT_BYTES)


def _bdot(a, b):
    return jnp.dot(a.astype(BF16), b.astype(BF16), preferred_element_type=F32)


def _split(x, parts):
    pieces, rest = [], x
    for i in range(parts):
        piece = rest.astype(BF16)
        pieces.append(piece)
        if i + 1 < parts:
            rest = rest - piece.astype(F32)
    return pieces


def _dot_exact_rhs(x, m, parts=3):
    return sum(jnp.dot(p, m, preferred_element_type=F32) for p in _split(x, parts))


def _dot_exact_lhs(m, x, parts=3):
    return sum(jnp.dot(m, p, preferred_element_type=F32) for p in _split(x, parts))


def _softplus(x):
    return jnp.maximum(x, 0.0) + jnp.log(1.0 + jnp.exp(-jnp.abs(x)))


def _silu(x):
    return x * jax.nn.sigmoid(x)


def _rms(x, w):
    return x * lax.rsqrt(jnp.mean(x * x, axis=-1, keepdims=True) + RMS_EPS) * w


def _resident(shape):
    return pl.BlockSpec(shape, lambda i: (0,) * len(shape), pipeline_mode=pl.Buffered(1))


def _inproj_kernel(x_ref, nw_ref, w_ref, after_ref, o_ref, *, gate, tn):
    u = _rms(x_ref[...], nw_ref[...]).astype(BF16)
    for n0 in range(0, o_ref.shape[1], tn):
        z = jnp.dot(u, w_ref[:, n0:n0 + tn], preferred_element_type=F32)
        o_ref[:, n0:n0 + tn] = jax.nn.sigmoid(z).astype(o_ref.dtype) if gate else z


def _inproj(x2d, norm_w, w_bf16, tm, tn, gate, after):
    t, d = x2d.shape
    n = w_bf16.shape[1]
    return pl.pallas_call(
        functools.partial(_inproj_kernel, gate=gate, tn=tn),
        grid=(t // tm,),
        in_specs=[pl.BlockSpec((tm, d), lambda i: (i, 0)),
                  pl.BlockSpec((1, d), lambda i: (0, 0)),
                  _resident((d, n)),
                  pl.BlockSpec((1, V7X_SUBLANES, V7X_LANES), lambda i: (0, 0, 0))],
        out_specs=pl.BlockSpec((tm, n), lambda i: (i, 0)),
        out_shape=jax.ShapeDtypeStruct((t, n), BF16 if gate else F32),
        compiler_params=_cparams("parallel"),
        name="inproj_gate" if gate else "inproj",
    )(x2d, norm_w.reshape(1, d), w_bf16, after)


def _ffn_in_kernel(x_ref, nw_ref, w_ref, o_ref, *, tn):
    ff = o_ref.shape[1]
    u = _rms(x_ref[...], nw_ref[...]).astype(BF16)
    for n0 in range(0, ff, tn):
        gate = jnp.dot(u, w_ref[:, n0:n0 + tn], preferred_element_type=F32)
        up = jnp.dot(u, w_ref[:, ff + n0:ff + n0 + tn], preferred_element_type=F32)
        o_ref[:, n0:n0 + tn] = (_silu(gate) * up).astype(BF16)


def _ffn_in(x2d, norm_w, w_bf16, tm, tn):
    t, d = x2d.shape
    ff = w_bf16.shape[1] // 2
    return pl.pallas_call(
        functools.partial(_ffn_in_kernel, tn=tn),
        grid=(t // tm,),
        in_specs=[pl.BlockSpec((tm, d), lambda i: (i, 0)),
                  pl.BlockSpec((1, d), lambda i: (0, 0)),
                  _resident((d, 2 * ff))],
        out_specs=pl.BlockSpec((tm, ff), lambda i: (i, 0)),
        out_shape=jax.ShapeDtypeStruct((t, ff), BF16),
        compiler_params=_cparams("parallel"),
        name="ffn_in",
    )(x2d, norm_w.reshape(1, d), w_bf16)


def _ffn_out_kernel(a_ref, w_ref, x_ref, fw_ref, o_ref, *, final_norm):
    y = x_ref[...] + jnp.dot(a_ref[...], w_ref[...], preferred_element_type=F32)
    if final_norm:
        y = _rms(y, fw_ref[...])
    o_ref[...] = y


def _ffn_out(act, w_bf16, x2d, final_w, final_norm, tm):
    t, d = x2d.shape
    ff = act.shape[1]
    return pl.pallas_call(
        functools.partial(_ffn_out_kernel, final_norm=final_norm),
        grid=(t // tm,),
        in_specs=[pl.BlockSpec((tm, ff), lambda i: (i, 0)),
                  _resident((ff, d)),
                  pl.BlockSpec((tm, d), lambda i: (i, 0)),
                  pl.BlockSpec((1, d), lambda i: (0, 0))],
        out_specs=pl.BlockSpec((tm, d), lambda i: (i, 0)),
        out_shape=jax.ShapeDtypeStruct((t, d), F32),
        compiler_params=_cparams("parallel"),
        name="ffn_out",
    )(act, w_bf16, x2d, final_w.reshape(1, d))


def _merge_kernel(yh_ref, ys_ref, ym_ref, yr_ref, rg_ref, rb_ref, lnw_ref, lnb_ref, ones_ref,
                  zg_ref, wb_ref, wo_ref, x_ref, o_ref):
    d = x_ref.shape[1]
    y_rw = _rw_finish(yr_ref[...], rg_ref[...], rb_ref[...], lnw_ref[...], lnb_ref[...],
                      ones_ref[...])
    acc = jnp.zeros(x_ref.shape, F32)
    for k, y in enumerate((yh_ref[...], y_rw, ys_ref[...], ym_ref[...])):
        proj = jnp.dot(y.astype(BF16), wb_ref[k], preferred_element_type=F32)
        acc = acc + zg_ref[:, k * d:(k + 1) * d].astype(F32) * proj
    o_ref[...] = x_ref[...] + jnp.dot(acc.astype(BF16), wo_ref[...], preferred_element_type=F32)


def _merge(y_hg, y_s5, y_mb, rw, gates, wb_bf16, wo_bf16, x2d, tm):
    t, d = x2d.shape
    w = y_hg.shape[1]
    yspec = pl.BlockSpec((tm, w), lambda i: (i, 0))
    row = pl.BlockSpec((1, w), lambda i: (0, 0))
    y_raw, g, bonus, ln_w, ln_b, ones = rw
    return pl.pallas_call(
        _merge_kernel,
        grid=(t // tm,),
        in_specs=[yspec, yspec, yspec, yspec, yspec, yspec, row, row, _resident((w, w)),
                  pl.BlockSpec((tm, N_BRANCH * d), lambda i: (i, 0)),
                  _resident((N_BRANCH, w, d)),
                  _resident((d, d)),
                  pl.BlockSpec((tm, d), lambda i: (i, 0))],
        out_specs=pl.BlockSpec((tm, d), lambda i: (i, 0)),
        out_shape=jax.ShapeDtypeStruct((t, d), F32),
        compiler_params=_cparams("parallel"),
        name="merge",
    )(y_hg, y_s5, y_mb, y_raw, g, bonus, ln_w.reshape(1, w), ln_b.reshape(1, w), ones,
      gates, wb_bf16, wo_bf16, x2d)


def _hg_bounds_kernel(h_ref, o_ref):
    h = h_ref[...]
    depth = h.shape[0]
    m = jnp.max(h, axis=0, keepdims=True)
    e = jnp.exp(h - m)
    p = e / jnp.sum(e, axis=0, keepdims=True)
    run = jnp.zeros_like(p[0:1])
    rows = []
    for l in range(depth):
        run = run + p[l:l + 1]
        rows.append(run - p[0:1])
    o_ref[...] = jnp.concatenate(rows, axis=0)


def _hg_bounds(hgrn_lower_bounds):
    return pl.pallas_call(
        _hg_bounds_kernel,
        out_shape=jax.ShapeDtypeStruct(hgrn_lower_bounds.shape, F32),
        name="hg_bounds",
    )(hgrn_lower_bounds)


def _hg_chunk_kernel(q_ref, f_ref, i_ref, g_ref, lb_ref, nw_ref, tril_ref, after_ref, y_ref,
                     st_ref, *, tb, c, cs, heads, dk):
    sub = V7X_SUBLANES
    nt = (((1,), (1,)), ((), ()))

    @pl.when(pl.program_id(1) == 0)
    def _():
        st_ref[...] = jnp.zeros(st_ref.shape, F32)

    def shift_rows(x, dlt):
        return pltpu.roll(x.reshape(c // sub, sub, dk), dlt, axis=1).reshape(c, dk)

    lb = lb_ref[...]
    tril = tril_ref[...]
    rid = lax.broadcasted_iota(jnp.int32, (c, dk), 0)
    rid1 = lax.broadcasted_iota(jnp.int32, (c, 1), 0)
    pr = lax.broadcasted_iota(jnp.int32, (c, c), 0)
    pc = lax.broadcasted_iota(jnp.int32, (c, c), 1)

    def chunk(ci):
        rows = pl.ds(ci * c, c)
        ff = f_ref[0, rows, :]
        q = _silu(q_ref[0, rows, :])
        dec = jnp.maximum(lb + (1.0 - lb) * jax.nn.sigmoid(ff), HG_TINY)
        k = (1.0 - lb) * jax.nn.sigmoid(-ff)
        v = i_ref[0, rows, :]
        b = _dot_exact_lhs(tril, jnp.log(dec))
        outs = []
        for h in range(heads):
            hs = slice(h * dk, (h + 1) * dk)
            qh, kh, bh, vh = q[:, hs], k[:, hs], b[:, hs], v[:, hs]
            blast = bh[c - 1:c, :]
            st = st_ref[h]
            o = lax.dot_general((qh * jnp.exp(bh)).astype(BF16), st.astype(BF16), nt,
                                preferred_element_type=F32)
            att = jnp.zeros((c, c), F32)
            grp = 2 * cs
            while grp <= c:
                half = grp // 2
                qparts, kparts = [], []
                for r0 in range(0, c, grp):
                    d = bh[r0:r0 + grp, :] - bh[r0 + half - 1:r0 + half, :]
                    if half % sub == 0:
                        zero = jnp.zeros((half, dk), F32)
                        kparts += [kh[r0:r0 + half, :] * jnp.exp(-d[0:half, :]), zero]
                        qparts += [zero, qh[r0 + half:r0 + grp, :] * jnp.exp(d[half:grp, :])]
                    else:
                        upper = (rid[0:grp, :] % grp) >= half
                        e = jnp.exp(jnp.where(upper, d, -d))
                        kparts.append(jnp.where(upper, 0.0, kh[r0:r0 + grp, :] * e))
                        qparts.append(jnp.where(upper, qh[r0:r0 + grp, :] * e, 0.0))
                a = lax.dot_general(jnp.concatenate(qparts, axis=0).astype(BF16),
                                    jnp.concatenate(kparts, axis=0).astype(BF16), nt,
                                    preferred_element_type=F32)
                att = att + (a if grp == c else jnp.where(pr // grp == pc // grp, a, 0.0))
                grp *= 2
            o = o + _bdot(att, vh)
            for dlt in range(cs):
                if dlt == 0:
                    a = jnp.sum(qh * kh, axis=-1, keepdims=True)
                    o = o + a * vh
                else:
                    e = jnp.exp(jnp.minimum(bh - shift_rows(bh, dlt), 0.0))
                    a = jnp.sum(qh * shift_rows(kh, dlt) * e, axis=-1, keepdims=True)
                    a = jnp.where((rid1 % cs) >= dlt, a, 0.0)
                    o = o + a * shift_rows(vh, dlt)
            kd = kh * jnp.exp(blast - bh)
            st_ref[h] = st * jnp.exp(blast) + _bdot(vh.T, kd)
            outs.append(o * lax.rsqrt(jnp.mean(o * o, axis=-1, keepdims=True) + RMS_EPS))
        y = jnp.concatenate(outs, axis=-1) * nw_ref[...]
        y_ref[0, rows, :] = (y * _silu(g_ref[0, rows, :])).astype(y_ref.dtype)

    for ci in range(tb // c):
        chunk(ci)


def _hgrn2(z3d, lb, norm_w, w, col0, tb, c, cs, after):
    bsz, seq, _ = z3d.shape
    heads = w // HG_DK
    cb = col0 // w
    assert V7X_SUBLANES % cs == 0 and c % V7X_SUBLANES == 0
    tril = (jnp.arange(c)[:, None] >= jnp.arange(c)[None, :]).astype(BF16)
    zspec = lambda j: pl.BlockSpec((1, tb, w), lambda b, i: (b, i, cb + j))
    row = pl.BlockSpec((1, w), lambda b, i: (0, 0))
    return pl.pallas_call(
        functools.partial(_hg_chunk_kernel, tb=tb, c=c, cs=cs, heads=heads, dk=HG_DK),
        grid=(bsz, seq // tb),
        in_specs=[zspec(0), zspec(1), zspec(2), zspec(3), row, row,
                  pl.BlockSpec((c, c), lambda b, i: (0, 0)),
                  pl.BlockSpec((1, V7X_SUBLANES, V7X_LANES), lambda b, i: (0, 0, 0))],
        out_specs=pl.BlockSpec((1, tb, w), lambda b, i: (b, i, 0)),
        out_shape=jax.ShapeDtypeStruct((bsz, seq, w), BF16),
        scratch_shapes=[pltpu.VMEM((heads, HG_DK, HG_DK), F32)],
        compiler_params=_cparams("parallel", "arbitrary"),
        name="hgrn2",
    )(z3d, z3d, z3d, z3d, lb.reshape(1, w), norm_w.reshape(1, w), tril, after)


def _lanes_to_chains_v(a, bsz, seq, heads, dv, nq):
    a = a.reshape(seq, dv // nq, nq, bsz, heads).transpose(3, 0, 4, 2, 1)
    return a.reshape(bsz * seq, heads * dv)


def _shift_rows(cur, prev_last, first):
    rolled = pltpu.roll(cur, 1, axis=0)
    row0 = jnp.where(first, 0.0, prev_last)
    rid = lax.broadcasted_iota(jnp.int32, cur.shape, 0)
    return jnp.where(rid == 0, jnp.broadcast_to(row0, cur.shape), rolled)


def _rw_prep_kernel(zm_ref, zmp_ref, zl_ref, zlp_ref, vf_ref,
                    mum_ref, mul_ref, w0_ref, w2_ref, a0_ref, a2_ref, g2_ref,
                    kk_ref, ka_ref, rk_ref, v0_ref, v1_ref, v2_ref, ones_ref,
                    kvec_out, v_out, g_out, bonus_out,
                    *, w, tiles_per_seq, has_vres):
    first = (pl.program_id(0) % tiles_per_seq) == 0
    sub = V7X_SUBLANES
    zm = zm_ref[...]
    zl = zl_ref[...]
    zms = zm + (_shift_rows(zm, zmp_ref[sub - 1:sub, :], first) - zm) * mum_ref[...]
    zls = zl + (_shift_rows(zl, zlp_ref[sub - 1:sub, :], first) - zl) * mul_ref[...]
    r = zms[:, 0:w]
    k = zms[:, w:2 * w]
    v = zms[:, 2 * w:3 * w]
    w_log = -_softplus(-(w0_ref[...] + _bdot(jnp.tanh(zls), w2_ref[...]))) - 0.5
    decay = jnp.exp(-jnp.exp(w_log))
    if has_vres:
        mix = jax.nn.sigmoid(v0_ref[...] + _bdot(_bdot(v, v1_ref[...]), v2_ref[...]))
        v = v + (vf_ref[...] - v) * mix
    a = jax.nn.sigmoid(a0_ref[...] + _bdot(zls, a2_ref[...]))
    g = _bdot(jax.nn.sigmoid(zls), g2_ref[...])
    ones = ones_ref[...]
    kk = k * kk_ref[...]
    ss = _dot_exact_rhs(kk * kk, ones)
    kk = kk / jnp.maximum(jnp.sqrt(ss), 1e-12)
    k2 = k * (1.0 + (a - 1.0) * ka_ref[...])
    vecs = (r, decay, k2, -kk, kk * a, v)
    for h in range(w // RW_HEAD):
        hs = slice(h * RW_HEAD, (h + 1) * RW_HEAD)
        row = jnp.concatenate([x[:, hs] for x in vecs], axis=1)
        for q in range(kvec_out.shape[0]):
            kvec_out[q, 0, h] = row
    v_out[...] = v
    g_out[...] = g.astype(g_out.dtype)
    bonus_out[...] = (_dot_exact_rhs(r * k2 * rk_ref[...], ones) * v).astype(bonus_out.dtype)


def _rw_prep(z2d, v_first, p, col_main, col_lora, w, lw, seq, tm, has_vres):
    t = z2d.shape[0]
    sub = V7X_SUBLANES
    mb = col_main // (3 * w)
    lbk = col_lora // lw
    rows8 = tm // sub
    heads = w // RW_HEAD
    tps = seq // tm
    nq = V7X_LANES // (t // seq * heads)

    def prev_idx(i):
        return jnp.maximum(i * rows8 - 1, 0)

    row = lambda n: pl.BlockSpec((1, n), lambda i: (0, 0))
    full = lambda a: pl.BlockSpec(a.shape, lambda i: (0,) * a.ndim)
    ospec = pl.BlockSpec((tm, w), lambda i: (i, 0))
    oshape = jax.ShapeDtypeStruct((t, w), F32)
    args = [z2d, z2d, z2d, z2d, v_first,
            p["mu_main"], p["mu_lora"], p["w0"], p["w2p"], p["a0"], p["a2p"], p["g2p"],
            p["k_k"], p["k_a"], p["r_k"], p["v0"], p["v1p"], p["v2p"], p["ones"]]
    in_specs = [pl.BlockSpec((tm, 3 * w), lambda i: (i, mb)),
                pl.BlockSpec((sub, 3 * w), lambda i: (prev_idx(i), mb)),
                pl.BlockSpec((tm, lw), lambda i: (i, lbk)),
                pl.BlockSpec((sub, lw), lambda i: (prev_idx(i), lbk)),
                ospec,
                row(3 * w), row(lw), row(w), full(p["w2p"]), row(w), full(p["a2p"]), full(p["g2p"]),
                row(w), row(w), row(w), row(w), full(p["v1p"]), full(p["v2p"]), full(p["ones"])]
    return pl.pallas_call(
        functools.partial(_rw_prep_kernel, w=w, tiles_per_seq=seq // tm, has_vres=has_vres),
        grid=(t // tm,),
        in_specs=in_specs,
        out_specs=[pl.BlockSpec((nq, 1, heads, tm, RW_NVEC * RW_HEAD),
                                lambda i: (0, i // tps, 0, i % tps, 0)), ospec, ospec, ospec],
        out_shape=[jax.ShapeDtypeStruct((nq, t // seq, heads, seq, RW_NVEC * RW_HEAD), F32),
                   oshape, jax.ShapeDtypeStruct((t, w), BF16), jax.ShapeDtypeStruct((t, w), BF16)],
        compiler_params=_cparams("parallel"),
        name="rw_prep",
    )(*args)


def _rw_scan_kernel(x_ref, *rest, tc, dk, nvb, nq):
    y_ref, s_ref = rest[-2:]
    sub = V7X_SUBLANES
    lanes = V7X_LANES
    jr, jw, jk, ja, jb, jv = range(RW_NVEC)
    vl = nvb * sub
    qid = lax.broadcasted_iota(jnp.int32, (sub, lanes), 1) // (lanes // nq)

    @pl.when(pl.program_id(0) == 0)
    def _():
        s_ref[...] = jnp.zeros(s_ref.shape, F32)

    def bc(j, t, kk):
        return jnp.broadcast_to(x_ref[t, j, pl.ds(kk, 1), :], (sub, lanes))

    sa0 = [jnp.zeros((sub, lanes), F32) for _ in range(nvb)]
    for kk in range(dk):
        arow = bc(ja, 0, kk)
        for j in range(nvb):
            sa0[j] = sa0[j] + s_ref[kk, pl.ds(sub * j, sub), :] * arow

    def step(t, sa):
        tn = jnp.minimum(t + 1, tc - 1)
        vb = []
        for j in range(nvb):
            vj = x_ref[t, jv, pl.ds(sub * j, sub), :]
            for q in range(1, nq):
                vj = jnp.where(qid == q, x_ref[t, jv, pl.ds(q * vl + sub * j, sub), :], vj)
            vb.append(vj)
        yacc = [jnp.zeros((sub, lanes), F32) for _ in range(nvb)]
        sacc = [jnp.zeros((sub, lanes), F32) for _ in range(nvb)]
        for kk in range(dk):
            wrow = bc(jw, t, kk)
            brow = bc(jb, t, kk)
            krow = bc(jk, t, kk)
            rrow = bc(jr, t, kk)
            anext = bc(ja, tn, kk)
            for j in range(nvb):
                s = s_ref[kk, pl.ds(sub * j, sub), :] * wrow + sa[j] * brow + vb[j] * krow
                s_ref[kk, pl.ds(sub * j, sub), :] = s
                yacc[j] = yacc[j] + s * rrow
                sacc[j] = sacc[j] + s * anext
        for j in range(nvb):
            y_ref[t, pl.ds(sub * j, sub), :] = yacc[j]
        return tuple(sacc)

    lax.fori_loop(0, tc, step, tuple(sa0))


def _rw_scan(x_l, nq, tc, after):
    s, nvec, dk, lanes = x_l.shape
    vl = dk // nq
    kspec = pl.BlockSpec((tc, nvec, dk, lanes), lambda i: (i, 0, 0, 0))
    vspec = pl.BlockSpec((tc, vl, lanes), lambda i: (i, 0, 0))
    order = [pl.BlockSpec((16, lanes), lambda i: (0, 0))] * len(after)
    return pl.pallas_call(
        functools.partial(_rw_scan_kernel, tc=tc, dk=dk, nvb=vl // V7X_SUBLANES, nq=nq),
        grid=(s // tc,),
        in_specs=[kspec] + order,
        out_specs=vspec,
        out_shape=jax.ShapeDtypeStruct((s, vl, lanes), F32),
        scratch_shapes=[pltpu.VMEM((dk, vl, lanes), F32)],
        compiler_params=_cparams("arbitrary"),
        name="rw_scan",
    )(x_l, *after)


def _rw_finish(y, g, bonus, ln_w, ln_b, ones):
    inv_n = 1.0 / RW_HEAD
    mean = _dot_exact_rhs(y, ones, parts=2) * inv_n
    yc = y - mean
    var = _dot_exact_rhs(yc * yc, ones, parts=2) * inv_n
    y = yc * lax.rsqrt(var + RW_LN_EPS) * ln_w + ln_b
    return (y + bonus) * g


def _rwkv7_front(z2d, v_first, p, bsz, seq, w, col_main, col_lora, lw, tm, has_vres):
    kvec, v, g, bonus = _rw_prep(z2d, v_first, p, col_main, col_lora, w, lw, seq, tm, has_vres)
    lanes = kvec.shape[0] * kvec.shape[1] * kvec.shape[2]
    x_l = kvec.reshape(lanes, seq, RW_NVEC * RW_HEAD).transpose(1, 2, 0)
    return x_l.reshape(seq, RW_NVEC, RW_HEAD, lanes), v, g, bonus


def _s5_disc_kernel(are_ref, aim_ref, dt_ref, bre_ref, bim_ref, lre_ref, lim_ref, bbre_ref, bbim_ref):
    a_re = are_ref[...]
    a_im = aim_ref[...]
    dt = jnp.exp(dt_ref[...])
    mag = jnp.exp(dt * a_re)
    lam_re = mag * jnp.cos(dt * a_im)
    lam_im = mag * jnp.sin(dt * a_im)
    den = a_re * a_re + a_im * a_im
    coef_re = ((lam_re - 1.0) * a_re + lam_im * a_im) / den
    coef_im = (lam_im * a_re - (lam_re - 1.0) * a_im) / den
    b_re = bre_ref[...]
    b_im = bim_ref[...]
    lre_ref[...] = lam_re
    lim_ref[...] = lam_im
    bbre_ref[...] = coef_re * b_re - coef_im * b_im
    bbim_ref[...] = coef_re * b_im + coef_im * b_re


def _s5_discretize(a_re, a_im, log_dt, b_re, b_im):
    g, n, c = b_re.shape
    shp = (g, n * c)
    bc = lambda a: jnp.broadcast_to(a[..., None], (g, n, c)).reshape(shp)
    dtb = jnp.broadcast_to(log_dt[:, None], shp)
    o = jax.ShapeDtypeStruct(shp, F32)
    lre, lim, bbre, bbim = pl.pallas_call(
        _s5_disc_kernel, out_shape=[o, o, o, o], name="s5_disc",
    )(bc(a_re), bc(a_im), dtb, b_re.reshape(shp), b_im.reshape(shp))
    un = lambda a: a.reshape(g, n, c)
    return un(lre)[..., 0], un(lim)[..., 0], un(bbre), un(bbim)


def _s5_kernel(u_ref, bblk_ref, cblk_ref, lre_ref, lim_ref, d_ref, wg_ref, bg_ref, y_ref,
               h_ref, hr_ref, hi_ref, *, tc, bsz, ns, lane_chunk):
    @pl.when(pl.program_id(0) == 0)
    def _():
        hr_ref[...] = jnp.zeros(hr_ref.shape, F32)
        hi_ref[...] = jnp.zeros(hi_ref.shape, F32)

    w = u_ref.shape[2]
    u = jnp.swapaxes(u_ref[...], 0, 1).reshape(tc * bsz, w)
    nsg = bblk_ref.shape[0]
    lanes = V7X_LANES
    for sg in range(nsg):
        c0 = sg * lane_chunk
        drive = _bdot(u[:, sg * lanes:(sg + 1) * lanes], bblk_ref[sg])
        h_ref[:, c0:c0 + lane_chunk] = drive[:, 0:lane_chunk]
        h_ref[:, ns + c0:ns + c0 + lane_chunk] = drive[:, lane_chunk:2 * lane_chunk]
    for c0 in range(0, ns, lane_chunk):
        lr = jnp.broadcast_to(lre_ref[:, c0:c0 + lane_chunk], (bsz, lane_chunk))
        li = jnp.broadcast_to(lim_ref[:, c0:c0 + lane_chunk], (bsz, lane_chunk))

        def step(t, carry, c0=c0, lr=lr, li=li):
            hr, hi = carry
            rows = pl.ds(pl.multiple_of(t * bsz, bsz), bsz)
            nr = lr * hr - li * hi + h_ref[rows, c0:c0 + lane_chunk]
            ni = lr * hi + li * hr + h_ref[rows, ns + c0:ns + c0 + lane_chunk]
            h_ref[rows, c0:c0 + lane_chunk] = nr
            h_ref[rows, ns + c0:ns + c0 + lane_chunk] = ni
            return nr, ni

        hr, hi = lax.fori_loop(0, tc, step,
                               (hr_ref[:, c0:c0 + lane_chunk], hi_ref[:, c0:c0 + lane_chunk]))
        hr_ref[:, c0:c0 + lane_chunk] = hr
        hi_ref[:, c0:c0 + lane_chunk] = hi
    outs = []
    for sg in range(nsg):
        c0 = sg * lane_chunk
        outs.append(_bdot(h_ref[:, c0:c0 + lane_chunk], cblk_ref[sg, 0:lane_chunk, :])
                    + _bdot(h_ref[:, ns + c0:ns + c0 + lane_chunk], cblk_ref[sg, lane_chunk:, :]))
    y = jnp.concatenate(outs, axis=-1) + d_ref[...] * u
    y = jax.nn.gelu(y)
    y = y * jax.nn.sigmoid(_bdot(y, wg_ref[...]) + bg_ref[...])
    y_ref[...] = jnp.swapaxes(y.reshape(tc, bsz, w), 0, 1).astype(y_ref.dtype)


def _s5(z3d, col0, w, p, tc):
    bsz, seq, _ = z3d.shape
    ns = p["lam_re"].shape[1]
    blk = tc * bsz
    full = lambda a: pl.BlockSpec(a.shape, lambda i: (0,) * a.ndim)
    return pl.pallas_call(
        functools.partial(_s5_kernel, tc=tc, bsz=bsz, ns=ns, lane_chunk=ns // p["bblk"].shape[0]),
        grid=(seq // tc,),
        in_specs=[pl.BlockSpec((bsz, tc, w), lambda i: (0, i, col0 // w)),
                  full(p["bblk"]), full(p["cblk"]), full(p["lam_re"]), full(p["lam_im"]),
                  full(p["d"]), full(p["w_glu"]), full(p["b_glu"])],
        out_specs=pl.BlockSpec((bsz, tc, w), lambda i: (0, i, 0)),
        out_shape=jax.ShapeDtypeStruct((bsz, seq, w), BF16),
        scratch_shapes=[pltpu.VMEM((blk, 2 * ns), F32),
                        pltpu.VMEM((bsz, ns), F32), pltpu.VMEM((bsz, ns), F32)],
        compiler_params=_cparams("arbitrary"),
        name="s5",
    )(z3d, p["bblk"], p["cblk"], p["lam_re"], p["lam_im"], p["d"], p["w_glu"], p["b_glu"])


def _mb_kernel(gate_ref, x_ref, bc_ref, dt_ref, cw_ref, cb_ref, dtb_ref, alog_ref, dsk_ref, nw_ref,
               expand_ref, tril_ref, after_ref, y_ref, prev_ref, st_ref,
               *, q, w, heads, groups, nstate):
    hd = w // heads
    gw = w // groups
    hpg = heads // groups

    @pl.when(pl.program_id(1) == 0)
    def _():
        prev_ref[...] = jnp.zeros(prev_ref.shape, F32)
        st_ref[...] = jnp.zeros(st_ref.shape, F32)

    rid = lax.broadcasted_iota(jnp.int32, (q, q), 0)
    cid = lax.broadcasted_iota(jnp.int32, (q, q), 1)
    causal = rid >= cid
    lane_w = lax.broadcasted_iota(jnp.int32, (q, gw), 1)
    expand = expand_ref[...]
    prev = prev_ref[...]
    for ci in range(x_ref.shape[1] // q):
        rows = pl.ds(ci * q, q)
        prev = _mb_chunk(rows, prev, causal, lane_w, expand, gate_ref, x_ref, bc_ref, dt_ref, cw_ref,
                         cb_ref, dtb_ref, alog_ref, dsk_ref, nw_ref, tril_ref, y_ref, st_ref,
                         q=q, w=w, groups=groups, nstate=nstate, gw=gw, hpg=hpg, hd=hd)
    prev_ref[...] = prev


def _mb_chunk(rows, prev, causal, lane_w, expand, gate_ref, x_ref, bc_ref, dt_ref, cw_ref, cb_ref,
              dtb_ref, alog_ref, dsk_ref, nw_ref, tril_ref, y_ref, st_ref,
              *, q, w, groups, nstate, gw, hpg, hd):
    sub = V7X_SUBLANES
    xbc = jnp.concatenate([x_ref[0, rows, :], bc_ref[0, rows, :]], axis=-1)
    full = jnp.concatenate([prev, xbc], axis=0)
    conv = jnp.broadcast_to(cb_ref[...], xbc.shape)
    for j in range(MB_CONV):
        shift = MB_CONV - 1 - j
        src = full if shift == 0 else pltpu.roll(full, shift, axis=0)
        conv = conv + src[sub:sub + q, :] * cw_ref[j:j + 1, :]
    act = _silu(conv)
    xs = act[:, 0:w]
    bmat = act[:, w:w + groups * nstate]
    cmat = act[:, w + groups * nstate:w + 2 * groups * nstate]

    dt = _softplus(dt_ref[0, rows, :] + dtb_ref[...])
    a = -jnp.exp(alog_ref[...]) * dt
    a_cum = _dot_exact_lhs(tril_ref[...], a)
    dt_e = _dot_exact_rhs(dt, expand)
    acum_e = _dot_exact_rhs(a_cum, expand)
    alast_e = acum_e[q - 1:q, :]
    xdt = xs * dt_e
    xdec = xdt * jnp.exp(alast_e - acum_e)
    a_cum_t = a_cum.T

    y_parts = []
    for g in range(groups):
        bg = bmat[:, g * nstate:(g + 1) * nstate]
        cg = cmat[:, g * nstate:(g + 1) * nstate]
        scores = lax.dot_general(cg.astype(BF16), bg.astype(BF16), (((1,), (1,)), ((), ())),
                                 preferred_element_type=F32)
        xg = xdt[:, g * gw:(g + 1) * gw]
        yg = _bdot(cg, st_ref[g]) * jnp.exp(acum_e[:, g * gw:(g + 1) * gw])
        for hh in range(hpg):
            h = g * hpg + hh
            col = jnp.broadcast_to(a_cum[:, h:h + 1], (q, q))
            rowv = jnp.broadcast_to(a_cum_t[h:h + 1, :], (q, q))
            decay = jnp.where(causal, jnp.exp(col - rowv), 0.0)
            xh = jnp.where((lane_w >= hh * hd) & (lane_w < (hh + 1) * hd), xg, 0.0)
            yg = yg + _bdot(scores * decay, xh)
        y_parts.append(yg)
        upd = _bdot(bg.T, xdec[:, g * gw:(g + 1) * gw])
        st_ref[g] = st_ref[g] * jnp.exp(alast_e[:, g * gw:(g + 1) * gw]) + upd
    y = jnp.concatenate(y_parts, axis=-1) + dsk_ref[...] * xs
    y_ref[0, rows, :] = _rms(y * _silu(gate_ref[0, rows, :]), nw_ref[...]).astype(y_ref.dtype)
    return xbc[q - sub:q, :]


def _mamba2(z3d, p, w, cols, q, tb, after):
    bsz, seq, _ = z3d.shape
    heads = w // MB_HEADDIM
    lanes = V7X_LANES
    cg, cx, cbc, cdt = cols
    blk = lambda width, col: pl.BlockSpec((1, tb, width), lambda b, c: (b, c, col // width))
    full = lambda a: pl.BlockSpec(a.shape, lambda b, c: (0,) * a.ndim)
    consts = [p["conv_w"], p["conv_b"], p["dt_bias"], p["a_log"], p["d"], p["norm_w"],
              p["expand"], p["tril"]]
    return pl.pallas_call(
        functools.partial(_mb_kernel, q=q, w=w, heads=heads, groups=MB_GROUPS, nstate=MB_STATE),
        grid=(bsz, seq // tb),
        in_specs=[blk(w, cg), blk(w, cx), blk(w, cbc), blk(lanes, cdt)] + [full(a) for a in consts]
        + [pl.BlockSpec((1, V7X_SUBLANES, lanes), lambda b, c: (0, 0, 0))],
        out_specs=pl.BlockSpec((1, tb, w), lambda b, c: (b, c, 0)),
        out_shape=jax.ShapeDtypeStruct((bsz, seq, w), BF16),
        scratch_shapes=[pltpu.VMEM((V7X_SUBLANES, 2 * w), F32),
                        pltpu.VMEM((MB_GROUPS, MB_STATE, w // MB_GROUPS), F32)],
        compiler_params=_cparams("parallel", "arbitrary"),
        name="mamba2",
    )(z3d, z3d, z3d, z3d, *consts, after)


def _pad_rows(a, rows, at):
    out = jnp.zeros((rows, a.shape[1]), a.dtype)
    return lax.dynamic_update_slice(out, a, (at, 0))


def _block_ones(w, head):
    idx = jnp.arange(w) // head
    return (idx[:, None] == idx[None, :]).astype(BF16)


def _pick_tile(n, target):
    t = min(n, target)
    while n % t:
        t //= 2
    return t


def kernel(x, norm_mix_w, w_in, w_branch, w_out, norm_ffn_w, w_ffn_in, w_ffn_out, norm_final_w, hgrn_lower_bounds, hgrn_norm_w, rwkv_mu, rwkv_w0, rwkv_w2, rwkv_a0, rwkv_a2, rwkv_g2, rwkv_k_k, rwkv_k_a, rwkv_r_k, rwkv_ln_w, rwkv_ln_b, rwkv_v0, rwkv_v1, rwkv_v2, s5_a_re, s5_a_im, s5_b_re, s5_b_im, s5_c_re, s5_c_im, s5_d, s5_log_dt, s5_w_glu, s5_b_glu, mamba_conv_w, mamba_conv_b, mamba_dt_bias, mamba_a_log, mamba_d, mamba_norm_w):
    bsz, seq, d = x.shape
    depth = w_in.shape[0]
    w = d // 2
    lanes = V7X_LANES
    t = bsz * seq
    mb_heads = w // MB_HEADDIM
    mb_bc = 2 * MB_GROUPS * MB_STATE
    lw = RW_DECAY_LORA + RW_A_LORA + RW_G_LORA
    assert mb_bc == w and 3 * w % lw == 0

    o_gate = 0
    o_hg = o_gate + N_BRANCH * d
    o_rw = o_hg + 4 * w
    o_rwl = o_rw + 3 * w
    o_s5 = o_rwl + lw
    o_mbg = o_s5 + w
    o_mbx = o_mbg + w
    o_mbbc = o_mbx + w
    o_mbdt = o_mbbc + mb_bc
    c_rw = 0
    c_hg = c_rw + 3 * w
    c_s5 = c_hg + 4 * w
    c_mbg = c_s5 + w
    c_mbx = c_mbg + w
    c_mbbc = c_mbx + w
    c_rwl = c_mbbc + mb_bc
    c_mbdt = c_rwl + lw
    n_cols = c_mbdt + lanes
    tn = 512
    n_pad = -(-n_cols // tn) * tn

    def mixer_cols(wl):
        sl = lambda a, b: wl[:, a:b]
        pieces = [sl(o_rw, o_rwl), sl(o_hg, o_rw), sl(o_s5, o_mbdt), sl(o_rwl, o_s5),
                  sl(o_mbdt, o_mbdt + mb_heads),
                  jnp.zeros((d, n_pad - c_mbdt - mb_heads), wl.dtype)]
        return jnp.concatenate(pieces, axis=1).astype(BF16)

    tm = _pick_tile(seq, 256)
    tm_in = _pick_tile(seq, 512)
    tb_hg = _pick_tile(seq, 512)
    c_hg_chunk = min(HG_CHUNK, tb_hg)
    tc_rw = _pick_tile(seq, 64)
    tc_s5 = _pick_tile(seq, 64)
    q_mb = min(MB_CHUNK, seq)
    tb_mb = _pick_tile(seq, 4 * q_mb)
    ff = w_ffn_out.shape[1]
    tn_ff = 256 if ff % 256 == 0 else lanes

    lower_bounds = _hg_bounds(hgrn_lower_bounds)
    ones_rw = _block_ones(w, RW_HEAD)
    eye_g = jnp.eye(lanes // S5_GROUP, dtype=F32)
    expand = (jnp.arange(lanes)[:, None] == (jnp.arange(w) // MB_HEADDIM)[None, :]).astype(BF16)
    tril = (jnp.arange(q_mb)[:, None] >= jnp.arange(q_mb)[None, :]).astype(BF16)

    x2d = x.reshape(t, d)
    v_first = jnp.zeros((t, w), F32)
    for l in range(depth):
        x3d = x2d.reshape(bsz, seq, d)
        z2d = _inproj(x2d, norm_mix_w[l], mixer_cols(w_in[l]), tm_in, tn, False, x3d)
        z3d = z2d.reshape(bsz, seq, n_pad)

        mu = rwkv_mu[l]
        has_vres = l > 0
        lv = max(l - 1, 0)
        rp = {
            "mu_main": mu[None, :3 * w], "mu_lora": mu[None, 3 * w:],
            "w0": rwkv_w0[l][None], "a0": rwkv_a0[l][None],
            "w2p": _pad_rows(rwkv_w2[l], lw, 0).astype(BF16),
            "a2p": _pad_rows(rwkv_a2[l], lw, RW_DECAY_LORA).astype(BF16),
            "g2p": _pad_rows(rwkv_g2[l], lw, RW_DECAY_LORA + RW_A_LORA).astype(BF16),
            "k_k": rwkv_k_k[l][None], "k_a": rwkv_k_a[l][None], "r_k": rwkv_r_k[l].reshape(1, w),
            "v0": rwkv_v0[lv][None],
            "v1p": jnp.pad(rwkv_v1[lv], ((0, 0), (0, lanes - rwkv_v1.shape[2]))).astype(BF16),
            "v2p": _pad_rows(rwkv_v2[lv], lanes, 0).astype(BF16),
            "ones": ones_rw, "ln_w": rwkv_ln_w[l], "ln_b": rwkv_ln_b[l],
        }
        rw_x, v_l, rw_g, rw_bonus = _rwkv7_front(z2d, v_first, rp, bsz, seq, w, c_rw, c_rwl, lw, tm,
                                                 has_vres)
        if l == 0:
            v_first = v_l

        lam_re, lam_im, bb_re, bb_im = _s5_discretize(s5_a_re[l], s5_a_im[l], s5_log_dt[l],
                                                      s5_b_re[l], s5_b_im[l])
        ns = lam_re.size
        nsg = w // lanes
        gps = lanes // S5_GROUP
        blk_in = lambda bb: jnp.einsum(
            "sgnc,gh->sgchn", bb.reshape(nsg, gps, S5_STATE, S5_GROUP), eye_g
        ).reshape(nsg, lanes, gps * S5_STATE)
        blk_out = lambda cc: jnp.einsum(
            "sgcn,gh->sgnhc", cc.reshape(nsg, gps, S5_GROUP, S5_STATE), eye_g
        ).reshape(nsg, gps * S5_STATE, lanes)
        sp = {
            "bblk": jnp.concatenate([blk_in(bb_re), blk_in(bb_im)], axis=2).astype(BF16),
            "cblk": jnp.concatenate([blk_out(s5_c_re[l]), -blk_out(s5_c_im[l])], axis=1).astype(BF16),
            "lam_re": lam_re.reshape(1, ns), "lam_im": lam_im.reshape(1, ns),
            "d": s5_d[l][None], "w_glu": s5_w_glu[l].astype(BF16), "b_glu": s5_b_glu[l][None],
        }
        y_s5 = _s5(z3d, c_s5, w, sp, tc_s5).reshape(t, w)

        mp = {
            "conv_w": mamba_conv_w[l], "conv_b": mamba_conv_b[l][None],
            "dt_bias": jnp.pad(mamba_dt_bias[l], (0, lanes - mb_heads))[None],
            "a_log": jnp.pad(mamba_a_log[l], (0, lanes - mb_heads))[None],
            "d": jnp.repeat(mamba_d[l], MB_HEADDIM)[None], "norm_w": mamba_norm_w[l][None],
            "expand": expand, "tril": tril,
        }
        y_hg = _hgrn2(z3d, lower_bounds[l], hgrn_norm_w[l], w, c_hg, tb_hg, c_hg_chunk,
                      HG_SUBBLOCK, z3d).reshape(t, w)
        rw_y = _rw_scan(rw_x, lanes // (bsz * (w // RW_HEAD)), tc_rw, (y_hg, y_s5))
        gates = _inproj(x2d, norm_mix_w[l], w_in[l][:, o_gate:o_hg].astype(BF16), tm_in, tn, True,
                        rw_y)
        y_mb = _mamba2(z3d, mp, w, (c_mbg, c_mbx, c_mbbc, c_mbdt), q_mb, tb_mb, rw_y).reshape(t, w)
        rw_heads = w // RW_HEAD
        rw_raw = _lanes_to_chains_v(rw_y, bsz, seq, rw_heads, RW_HEAD, lanes // (bsz * rw_heads))
        x2d = _merge(y_hg, y_s5, y_mb, (rw_raw, rw_g, rw_bonus, rp["ln_w"], rp["ln_b"], rp["ones"]),
                     gates, w_branch[l].astype(BF16), w_out[l].astype(BF16), x2d, tm)
        act = _ffn_in(x2d, norm_ffn_w[l], w_ffn_in[l].astype(BF16), tm_in, tn_ff)
        x2d = _ffn_out(act, w_ffn_out[l].astype(BF16), x2d, norm_final_w, l == depth - 1, tm)
    return x2d.reshape(bsz, seq, d)
```

```python
import functools
import math

import jax
import jax.numpy as jnp
from jax import lax
from jax.experimental import pallas as pl
from jax.experimental.pallas import tpu as pltpu

F32 = jnp.float32
BF16 = jnp.bfloat16

V7X_LANES = 128
V7X_SUBLANES = 8
V7X_VMEM_LIMIT_BYTES = 56 * 1024 * 1024

RMS_EPS = 1e-6
N_BRANCH = 4
HG_DK = 128
HG_TINY = 1e-30
HG_CHUNK = 128
HG_SUBBLOCK = 4
RW_HEAD = 64
RW_DECAY_LORA = 64
RW_A_LORA = 64
RW_G_LORA = 128
RW_LN_EPS = 64e-5
RW_NVEC = 6
S5_GROUP = 16
S5_STATE = 64
MB_HEADDIM = 64
MB_GROUPS = 2
MB_STATE = 128
MB_CONV = 4
MB_CHUNK = 128


def _cparams(*sem):
    return pltpu.CompilerParams(dimension_semantics=sem, vmem_limit_bytes=V7X_VMEM_LIMIT_BYTES)


def _bdot(a, b):
    return jnp.dot(a.astype(BF16), b.astype(BF16), preferred_element_type=F32)


def _split(x, parts):
    pieces, rest = [], x
    for i in range(parts):
        piece = rest.astype(BF16)
        pieces.append(piece)
        if i + 1 < parts:
            rest = rest - piece.astype(F32)
    return pieces


def _dot_exact_rhs(x, m, parts=3):
    return sum(jnp.dot(p, m, preferred_element_type=F32) for p in _split(x, parts))


def _dot_exact_lhs(m, x, parts=3):
    return sum(jnp.dot(m, p, preferred_element_type=F32) for p in _split(x, parts))


def _softplus(x):
    return jnp.maximum(x, 0.0) + jnp.log(1.0 + jnp.exp(-jnp.abs(x)))


def _silu(x):
    return x * jax.nn.sigmoid(x)


def _rms(x, w):
    return x * lax.rsqrt(jnp.mean(x * x, axis=-1, keepdims=True) + RMS_EPS) * w


def _resident(shape):
    return pl.BlockSpec(shape, lambda i: (0,) * len(shape), pipeline_mode=pl.Buffered(1))


def _inproj_kernel(x_ref, nw_ref, w_ref, after_ref, o_ref, *, gate, tn):
    u = _rms(x_ref[...], nw_ref[...]).astype(BF16)
    for n0 in range(0, o_ref.shape[1], tn):
        z = jnp.dot(u, w_ref[:, n0:n0 + tn], preferred_element_type=F32)
        o_ref[:, n0:n0 + tn] = jax.nn.sigmoid(z).astype(o_ref.dtype) if gate else z


def _inproj(x2d, norm_w, w_bf16, tm, tn, gate, after):
    t, d = x2d.shape
    n = w_bf16.shape[1]
    return pl.pallas_call(
        functools.partial(_inproj_kernel, gate=gate, tn=tn),
        grid=(t // tm,),
        in_specs=[pl.BlockSpec((tm, d), lambda i: (i, 0)),
                  pl.BlockSpec((1, d), lambda i: (0, 0)),
                  _resident((d, n)),
                  pl.BlockSpec((1, V7X_SUBLANES, V7X_LANES), lambda i: (0, 0, 0))],
        out_specs=pl.BlockSpec((tm, n), lambda i: (i, 0)),
        out_shape=jax.ShapeDtypeStruct((t, n), BF16 if gate else F32),
        compiler_params=_cparams("parallel"),
        name="inproj_gate" if gate else "inproj",
    )(x2d, norm_w.reshape(1, d), w_bf16, after)


def _ffn_in_kernel(x_ref, nw_ref, w_ref, o_ref, *, tn):
    ff = o_ref.shape[1]
    u = _rms(x_ref[...], nw_ref[...]).astype(BF16)
    for n0 in range(0, ff, tn):
        gate = jnp.dot(u, w_ref[:, n0:n0 + tn], preferred_element_type=F32)
        up = jnp.dot(u, w_ref[:, ff + n0:ff + n0 + tn], preferred_element_type=F32)
        o_ref[:, n0:n0 + tn] = (_silu(gate) * up).astype(BF16)


def _ffn_in(x2d, norm_w, w_bf16, tm, tn):
    t, d = x2d.shape
    ff = w_bf16.shape[1] // 2
    return pl.pallas_call(
        functools.partial(_ffn_in_kernel, tn=tn),
        grid=(t // tm,),
        in_specs=[pl.BlockSpec((tm, d), lambda i: (i, 0)),
                  pl.BlockSpec((1, d), lambda i: (0, 0)),
                  _resident((d, 2 * ff))],
        out_specs=pl.BlockSpec((tm, ff), lambda i: (i, 0)),
        out_shape=jax.ShapeDtypeStruct((t, ff), BF16),
        compiler_params=_cparams("parallel"),
        name="ffn_in",
    )(x2d, norm_w.reshape(1, d), w_bf16)


def _ffn_out_kernel(a_ref, w_ref, x_ref, fw_ref, o_ref, *, final_norm):
    y = x_ref[...] + jnp.dot(a_ref[...], w_ref[...], preferred_element_type=F32)
    if final_norm:
        y = _rms(y, fw_ref[...])
    o_ref[...] = y


def _ffn_out(act, w_bf16, x2d, final_w, final_norm, tm):
    t, d = x2d.shape
    ff = act.shape[1]
    return pl.pallas_call(
        functools.partial(_ffn_out_kernel, final_norm=final_norm),
        grid=(t // tm,),
        in_specs=[pl.BlockSpec((tm, ff), lambda i: (i, 0)),
                  _resident((ff, d)),
                  pl.BlockSpec((tm, d), lambda i: (i, 0)),
                  pl.BlockSpec((1, d), lambda i: (0, 0))],
        out_specs=pl.BlockSpec((tm, d), lambda i: (i, 0)),
        out_shape=jax.ShapeDtypeStruct((t, d), F32),
        compiler_params=_cparams("parallel"),
        name="ffn_out",
    )(act, w_bf16, x2d, final_w.reshape(1, d))


def _merge_kernel(yh_ref, ys_ref, ym_ref, yr_ref, rg_ref, rb_ref, lnw_ref, lnb_ref, ones_ref,
                  zg_ref, wb_ref, wo_ref, x_ref, o_ref):
    d = x_ref.shape[1]
    y_rw = _rw_finish(yr_ref[...], rg_ref[...], rb_ref[...], lnw_ref[...], lnb_ref[...],
                      ones_ref[...])
    acc = jnp.zeros(x_ref.shape, F32)
    for k, y in enumerate((yh_ref[...], y_rw, ys_ref[...], ym_ref[...])):
        proj = jnp.dot(y.astype(BF16), wb_ref[k], preferred_element_type=F32)
        acc = acc + zg_ref[:, k * d:(k + 1) * d].astype(F32) * proj
    o_ref[...] = x_ref[...] + jnp.dot(acc.astype(BF16), wo_ref[...], preferred_element_type=F32)


def _merge(y_hg, y_s5, y_mb, rw, gates, wb_bf16, wo_bf16, x2d, tm):
    t, d = x2d.shape
    w = y_hg.shape[1]
    yspec = pl.BlockSpec((tm, w), lambda i: (i, 0))
    row = pl.BlockSpec((1, w), lambda i: (0, 0))
    y_raw, g, bonus, ln_w, ln_b, ones = rw
    return pl.pallas_call(
        _merge_kernel,
        grid=(t // tm,),
        in_specs=[yspec, yspec, yspec, yspec, yspec, yspec, row, row, _resident((w, w)),
                  pl.BlockSpec((tm, N_BRANCH * d), lambda i: (i, 0)),
                  _resident((N_BRANCH, w, d)),
                  _resident((d, d)),
                  pl.BlockSpec((tm, d), lambda i: (i, 0))],
        out_specs=pl.BlockSpec((tm, d), lambda i: (i, 0)),
        out_shape=jax.ShapeDtypeStruct((t, d), F32),
        compiler_params=_cparams("parallel"),
        name="merge",
    )(y_hg, y_s5, y_mb, y_raw, g, bonus, ln_w.reshape(1, w), ln_b.reshape(1, w), ones,
      gates, wb_bf16, wo_bf16, x2d)


def _hg_bounds_kernel(h_ref, o_ref):
    h = h_ref[...]
    depth = h.shape[0]
    m = jnp.max(h, axis=0, keepdims=True)
    e = jnp.exp(h - m)
    p = e / jnp.sum(e, axis=0, keepdims=True)
    run = jnp.zeros_like(p[0:1])
    rows = []
    for l in range(depth):
        run = run + p[l:l + 1]
        rows.append(run - p[0:1])
    o_ref[...] = jnp.concatenate(rows, axis=0)


def _hg_bounds(hgrn_lower_bounds):
    return pl.pallas_call(
        _hg_bounds_kernel,
        out_shape=jax.ShapeDtypeStruct(hgrn_lower_bounds.shape, F32),
        name="hg_bounds",
    )(hgrn_lower_bounds)


def _hg_chunk_kernel(q_ref, f_ref, i_ref, g_ref, lb_ref, nw_ref, tril_ref, after_ref, y_ref,
                     st_ref, *, tb, c, cs, heads, dk):
    sub = V7X_SUBLANES
    nt = (((1,), (1,)), ((), ()))

    @pl.when(pl.program_id(1) == 0)
    def _():
        st_ref[...] = jnp.zeros(st_ref.shape, F32)

    def shift_rows(x, dlt):
        return pltpu.roll(x.reshape(c // sub, sub, dk), dlt, axis=1).reshape(c, dk)

    lb = lb_ref[...]
    tril = tril_ref[...]
    rid = lax.broadcasted_iota(jnp.int32, (c, dk), 0)
    rid1 = lax.broadcasted_iota(jnp.int32, (c, 1), 0)
    pr = lax.broadcasted_iota(jnp.int32, (c, c), 0)
    pc = lax.broadcasted_iota(jnp.int32, (c, c), 1)

    def chunk(ci):
        rows = pl.ds(ci * c, c)
        ff = f_ref[0, rows, :]
        q = _silu(q_ref[0, rows, :])
        dec = jnp.maximum(lb + (1.0 - lb) * jax.nn.sigmoid(ff), HG_TINY)
        k = (1.0 - lb) * jax.nn.sigmoid(-ff)
        v = i_ref[0, rows, :]
        b = _dot_exact_lhs(tril, jnp.log(dec))
        outs = []
        for h in range(heads):
            hs = slice(h * dk, (h + 1) * dk)
            qh, kh, bh, vh = q[:, hs], k[:, hs], b[:, hs], v[:, hs]
            blast = bh[c - 1:c, :]
            st = st_ref[h]
            o = lax.dot_general((qh * jnp.exp(bh)).astype(BF16), st.astype(BF16), nt,
                                preferred_element_type=F32)
            att = jnp.zeros((c, c), F32)
            grp = 2 * cs
            while grp <= c:
                half = grp // 2
                qparts, kparts = [], []
                for r0 in range(0, c, grp):
                    d = bh[r0:r0 + grp, :] - bh[r0 + half - 1:r0 + half, :]
                    if half % sub == 0:
                        zero = jnp.zeros((half, dk), F32)
                        kparts += [kh[r0:r0 + half, :] * jnp.exp(-d[0:half, :]), zero]
                        qparts += [zero, qh[r0 + half:r0 + grp, :] * jnp.exp(d[half:grp, :])]
                    else:
                        upper = (rid[0:grp, :] % grp) >= half
                        e = jnp.exp(jnp.where(upper, d, -d))
                        kparts.append(jnp.where(upper, 0.0, kh[r0:r0 + grp, :] * e))
                        qparts.append(jnp.where(upper, qh[r0:r0 + grp, :] * e, 0.0))
                a = lax.dot_general(jnp.concatenate(qparts, axis=0).astype(BF16),
                                    jnp.concatenate(kparts, axis=0).astype(BF16), nt,
                                    preferred_element_type=F32)
                att = att + (a if grp == c else jnp.where(pr // grp == pc // grp, a, 0.0))
                grp *= 2
            o = o + _bdot(att, vh)
            for dlt in range(cs):
                if dlt == 0:
                    a = jnp.sum(qh * kh, axis=-1, keepdims=True)
                    o = o + a * vh
                else:
                    e = jnp.exp(jnp.minimum(bh - shift_rows(bh, dlt), 0.0))
                    a = jnp.sum(qh * shift_rows(kh, dlt) * e, axis=-1, keepdims=True)
                    a = jnp.where((rid1 % cs) >= dlt, a, 0.0)
                    o = o + a * shift_rows(vh, dlt)
            kd = kh * jnp.exp(blast - bh)
            st_ref[h] = st * jnp.exp(blast) + _bdot(vh.T, kd)
            outs.append(o * lax.rsqrt(jnp.mean(o * o, axis=-1, keepdims=True) + RMS_EPS))
        y = jnp.concatenate(outs, axis=-1) * nw_ref[...]
        y_ref[0, rows, :] = (y * _silu(g_ref[0, rows, :])).astype(y_ref.dtype)

    for ci in range(tb // c):
        chunk(ci)


def _hgrn2(z3d, lb, norm_w, w, col0, tb, c, cs, after):
    bsz, seq, _ = z3d.shape
    heads = w // HG_DK
    cb = col0 // w
    assert V7X_SUBLANES % cs == 0 and c % V7X_SUBLANES == 0
    tril = (jnp.arange(c)[:, None] >= jnp.arange(c)[None, :]).astype(BF16)
    zspec = lambda j: pl.BlockSpec((1, tb, w), lambda b, i: (b, i, cb + j))
    row = pl.BlockSpec((1, w), lambda b, i: (0, 0))
    return pl.pallas_call(
        functools.partial(_hg_chunk_kernel, tb=tb, c=c, cs=cs, heads=heads, dk=HG_DK),
        grid=(bsz, seq // tb),
        in_specs=[zspec(0), zspec(1), zspec(2), zspec(3), row, row,
                  pl.BlockSpec((c, c), lambda b, i: (0, 0)),
                  pl.BlockSpec((1, V7X_SUBLANES, V7X_LANES), lambda b, i: (0, 0, 0))],
        out_specs=pl.BlockSpec((1, tb, w), lambda b, i: (b, i, 0)),
        out_shape=jax.ShapeDtypeStruct((bsz, seq, w), BF16),
        scratch_shapes=[pltpu.VMEM((heads, HG_DK, HG_DK), F32)],
        compiler_params=_cparams("parallel", "arbitrary"),
        name="hgrn2",
    )(z3d, z3d, z3d, z3d, lb.reshape(1, w), norm_w.reshape(1, w), tril, after)


def _lanes_to_chains_v(a, bsz, seq, heads, dv, nq):
    a = a.reshape(seq, dv // nq, nq, bsz, heads).transpose(3, 0, 4, 2, 1)
    return a.reshape(bsz * seq, heads * dv)


def _shift_rows(cur, prev_last, first):
    rolled = pltpu.roll(cur, 1, axis=0)
    row0 = jnp.where(first, 0.0, prev_last)
    rid = lax.broadcasted_iota(jnp.int32, cur.shape, 0)
    return jnp.where(rid == 0, jnp.broadcast_to(row0, cur.shape), rolled)


def _rw_prep_kernel(zm_ref, zmp_ref, zl_ref, zlp_ref, vf_ref,
                    mum_ref, mul_ref, w0_ref, w2_ref, a0_ref, a2_ref, g2_ref,
                    kk_ref, ka_ref, rk_ref, v0_ref, v1_ref, v2_ref, ones_ref,
                    kvec_out, v_out, g_out, bonus_out,
                    *, w, tiles_per_seq, has_vres):
    first = (pl.program_id(0) % tiles_per_seq) == 0
    sub = V7X_SUBLANES
    zm = zm_ref[...]
    zl = zl_ref[...]
    zms = zm + (_shift_rows(zm, zmp_ref[sub - 1:sub, :], first) - zm) * mum_ref[...]
    zls = zl + (_shift_rows(zl, zlp_ref[sub - 1:sub, :], first) - zl) * mul_ref[...]
    r = zms[:, 0:w]
    k = zms[:, w:2 * w]
    v = zms[:, 2 * w:3 * w]
    w_log = -_softplus(-(w0_ref[...] + _bdot(jnp.tanh(zls), w2_ref[...]))) - 0.5
    decay = jnp.exp(-jnp.exp(w_log))
    if has_vres:
        mix = jax.nn.sigmoid(v0_ref[...] + _bdot(_bdot(v, v1_ref[...]), v2_ref[...]))
        v = v + (vf_ref[...] - v) * mix
    a = jax.nn.sigmoid(a0_ref[...] + _bdot(zls, a2_ref[...]))
    g = _bdot(jax.nn.sigmoid(zls), g2_ref[...])
    ones = ones_ref[...]
    kk = k * kk_ref[...]
    ss = _dot_exact_rhs(kk * kk, ones)
    kk = kk / jnp.maximum(jnp.sqrt(ss), 1e-12)
    k2 = k * (1.0 + (a - 1.0) * ka_ref[...])
    vecs = (r, decay, k2, -kk, kk * a, v)
    for h in range(w // RW_HEAD):
        hs = slice(h * RW_HEAD, (h + 1) * RW_HEAD)
        row = jnp.concatenate([x[:, hs] for x in vecs], axis=1)
        for q in range(kvec_out.shape[0]):
            kvec_out[q, 0, h] = row
    v_out[...] = v
    g_out[...] = g.astype(g_out.dtype)
    bonus_out[...] = (_dot_exact_rhs(r * k2 * rk_ref[...], ones) * v).astype(bonus_out.dtype)


def _rw_prep(z2d, v_first, p, col_main, col_lora, w, lw, seq, tm, has_vres):
    t = z2d.shape[0]
    sub = V7X_SUBLANES
    mb = col_main // (3 * w)
    lbk = col_lora // lw
    rows8 = tm // sub
    heads = w // RW_HEAD
    tps = seq // tm
    nq = V7X_LANES // (t // seq * heads)

    def prev_idx(i):
        return jnp.maximum(i * rows8 - 1, 0)

    row = lambda n: pl.BlockSpec((1, n), lambda i: (0, 0))
    full = lambda a: pl.BlockSpec(a.shape, lambda i: (0,) * a.ndim)
    ospec = pl.BlockSpec((tm, w), lambda i: (i, 0))
    oshape = jax.ShapeDtypeStruct((t, w), F32)
    args = [z2d, z2d, z2d, z2d, v_first,
            p["mu_main"], p["mu_lora"], p["w0"], p["w2p"], p["a0"], p["a2p"], p["g2p"],
            p["k_k"], p["k_a"], p["r_k"], p["v0"], p["v1p"], p["v2p"], p["ones"]]
    in_specs = [pl.BlockSpec((tm, 3 * w), lambda i: (i, mb)),
                pl.BlockSpec((sub, 3 * w), lambda i: (prev_idx(i), mb)),
                pl.BlockSpec((tm, lw), lambda i: (i, lbk)),
                pl.BlockSpec((sub, lw), lambda i: (prev_idx(i), lbk)),
                ospec,
                row(3 * w), row(lw), row(w), full(p["w2p"]), row(w), full(p["a2p"]), full(p["g2p"]),
                row(w), row(w), row(w), row(w), full(p["v1p"]), full(p["v2p"]), full(p["ones"])]
    return pl.pallas_call(
        functools.partial(_rw_prep_kernel, w=w, tiles_per_seq=seq // tm, has_vres=has_vres),
        grid=(t // tm,),
        in_specs=in_specs,
        out_specs=[pl.BlockSpec((nq, 1, heads, tm, RW_NVEC * RW_HEAD),
                                lambda i: (0, i // tps, 0, i % tps, 0)), ospec, ospec, ospec],
        out_shape=[jax.ShapeDtypeStruct((nq, t // seq, heads, seq, RW_NVEC * RW_HEAD), F32),
                   oshape, jax.ShapeDtypeStruct((t, w), BF16), jax.ShapeDtypeStruct((t, w), BF16)],
        compiler_params=_cparams("parallel"),
        name="rw_prep",
    )(*args)


def _rw_scan_kernel(x_ref, *rest, tc, dk, nvb, nq):
    y_ref, s_ref = rest[-2:]
    sub = V7X_SUBLANES
    lanes = V7X_LANES
    jr, jw, jk, ja, jb, jv = range(RW_NVEC)
    vl = nvb * sub
    qid = lax.broadcasted_iota(jnp.int32, (sub, lanes), 1) // (lanes // nq)

    @pl.when(pl.program_id(0) == 0)
    def _():
        s_ref[...] = jnp.zeros(s_ref.shape, F32)

    def bc(j, t, kk):
        return jnp.broadcast_to(x_ref[t, j, pl.ds(kk, 1), :], (sub, lanes))

    sa0 = [jnp.zeros((sub, lanes), F32) for _ in range(nvb)]
    for kk in range(dk):
        arow = bc(ja, 0, kk)
        for j in range(nvb):
            sa0[j] = sa0[j] + s_ref[kk, pl.ds(sub * j, sub), :] * arow

    def step(t, sa):
        tn = jnp.minimum(t + 1, tc - 1)
        vb = []
        for j in range(nvb):
            vj = x_ref[t, jv, pl.ds(sub * j, sub), :]
            for q in range(1, nq):
                vj = jnp.where(qid == q, x_ref[t, jv, pl.ds(q * vl + sub * j, sub), :], vj)
            vb.append(vj)
        yacc = [jnp.zeros((sub, lanes), F32) for _ in range(nvb)]
        sacc = [jnp.zeros((sub, lanes), F32) for _ in range(nvb)]
        for kk in range(dk):
            wrow = bc(jw, t, kk)
            brow = bc(jb, t, kk)
            krow = bc(jk, t, kk)
            rrow = bc(jr, t, kk)
            anext = bc(ja, tn, kk)
            for j in range(nvb):
                s = s_ref[kk, pl.ds(sub * j, sub), :] * wrow + sa[j] * brow + vb[j] * krow
                s_ref[kk, pl.ds(sub * j, sub), :] = s
                yacc[j] = yacc[j] + s * rrow
                sacc[j] = sacc[j] + s * anext
        for j in range(nvb):
            y_ref[t, pl.ds(sub * j, sub), :] = yacc[j]
        return tuple(sacc)

    lax.fori_loop(0, tc, step, tuple(sa0))


def _rw_scan(x_l, nq, tc, after):
    s, nvec, dk, lanes = x_l.shape
    vl = dk // nq
    kspec = pl.BlockSpec((tc, nvec, dk, lanes), lambda i: (i, 0, 0, 0))
    vspec = pl.BlockSpec((tc, vl, lanes), lambda i: (i, 0, 0))
    order = [pl.BlockSpec((16, lanes), lambda i: (0, 0))] * len(after)
    return pl.pallas_call(
        functools.partial(_rw_scan_kernel, tc=tc, dk=dk, nvb=vl // V7X_SUBLANES, nq=nq),
        grid=(s // tc,),
        in_specs=[kspec] + order,
        out_specs=vspec,
        out_shape=jax.ShapeDtypeStruct((s, vl, lanes), F32),
        scratch_shapes=[pltpu.VMEM((dk, vl, lanes), F32)],
        compiler_params=_cparams("arbitrary"),
        name="rw_scan",
    )(x_l, *after)


def _rw_finish(y, g, bonus, ln_w, ln_b, ones):
    inv_n = 1.0 / RW_HEAD
    mean = _dot_exact_rhs(y, ones, parts=2) * inv_n
    yc = y - mean
    var = _dot_exact_rhs(yc * yc, ones, parts=2) * inv_n
    y = yc * lax.rsqrt(var + RW_LN_EPS) * ln_w + ln_b
    return (y + bonus) * g


def _rwkv7_front(z2d, v_first, p, bsz, seq, w, col_main, col_lora, lw, tm, has_vres):
    kvec, v, g, bonus = _rw_prep(z2d, v_first, p, col_main, col_lora, w, lw, seq, tm, has_vres)
    lanes = kvec.shape[0] * kvec.shape[1] * kvec.shape[2]
    x_l = kvec.reshape(lanes, seq, RW_NVEC * RW_HEAD).transpose(1, 2, 0)
    return x_l.reshape(seq, RW_NVEC, RW_HEAD, lanes), v, g, bonus


def _s5_disc_kernel(are_ref, aim_ref, dt_ref, bre_ref, bim_ref, lre_ref, lim_ref, bbre_ref, bbim_ref):
    a_re = are_ref[...]
    a_im = aim_ref[...]
    dt = jnp.exp(dt_ref[...])
    mag = jnp.exp(dt * a_re)
    lam_re = mag * jnp.cos(dt * a_im)
    lam_im = mag * jnp.sin(dt * a_im)
    den = a_re * a_re + a_im * a_im
    coef_re = ((lam_re - 1.0) * a_re + lam_im * a_im) / den
    coef_im = (lam_im * a_re - (lam_re - 1.0) * a_im) / den
    b_re = bre_ref[...]
    b_im = bim_ref[...]
    lre_ref[...] = lam_re
    lim_ref[...] = lam_im
    bbre_ref[...] = coef_re * b_re - coef_im * b_im
    bbim_ref[...] = coef_re * b_im + coef_im * b_re


def _s5_discretize(a_re, a_im, log_dt, b_re, b_im):
    g, n, c = b_re.shape
    shp = (g, n * c)
    bc = lambda a: jnp.broadcast_to(a[..., None], (g, n, c)).reshape(shp)
    dtb = jnp.broadcast_to(log_dt[:, None], shp)
    o = jax.ShapeDtypeStruct(shp, F32)
    lre, lim, bbre, bbim = pl.pallas_call(
        _s5_disc_kernel, out_shape=[o, o, o, o], name="s5_disc",
    )(bc(a_re), bc(a_im), dtb, b_re.reshape(shp), b_im.reshape(shp))
    un = lambda a: a.reshape(g, n, c)
    return un(lre)[..., 0], un(lim)[..., 0], un(bbre), un(bbim)


def _s5_kernel(u_ref, bblk_ref, cblk_ref, lre_ref, lim_ref, d_ref, wg_ref, bg_ref, y_ref,
               h_ref, hr_ref, hi_ref, *, tc, bsz, ns, lane_chunk):
    @pl.when(pl.program_id(0) == 0)
    def _():
        hr_ref[...] = jnp.zeros(hr_ref.shape, F32)
        hi_ref[...] = jnp.zeros(hi_ref.shape, F32)

    w = u_ref.shape[2]
    u = jnp.swapaxes(u_ref[...], 0, 1).reshape(tc * bsz, w)
    nsg = bblk_ref.shape[0]
    lanes = V7X_LANES
    for sg in range(nsg):
        c0 = sg * lane_chunk
        drive = _bdot(u[:, sg * lanes:(sg + 1) * lanes], bblk_ref[sg])
        h_ref[:, c0:c0 + lane_chunk] = drive[:, 0:lane_chunk]
        h_ref[:, ns + c0:ns + c0 + lane_chunk] = drive[:, lane_chunk:2 * lane_chunk]
    for c0 in range(0, ns, lane_chunk):
        lr = jnp.broadcast_to(lre_ref[:, c0:c0 + lane_chunk], (bsz, lane_chunk))
        li = jnp.broadcast_to(lim_ref[:, c0:c0 + lane_chunk], (bsz, lane_chunk))

        def step(t, carry, c0=c0, lr=lr, li=li):
            hr, hi = carry
            rows = pl.ds(pl.multiple_of(t * bsz, bsz), bsz)
            nr = lr * hr - li * hi + h_ref[rows, c0:c0 + lane_chunk]
            ni = lr * hi + li * hr + h_ref[rows, ns + c0:ns + c0 + lane_chunk]
            h_ref[rows, c0:c0 + lane_chunk] = nr
            h_ref[rows, ns + c0:ns + c0 + lane_chunk] = ni
            return nr, ni

        hr, hi = lax.fori_loop(0, tc, step,
                               (hr_ref[:, c0:c0 + lane_chunk], hi_ref[:, c0:c0 + lane_chunk]))
        hr_ref[:, c0:c0 + lane_chunk] = hr
        hi_ref[:, c0:c0 + lane_chunk] = hi
    outs = []
    for sg in range(nsg):
        c0 = sg * lane_chunk
        outs.append(_bdot(h_ref[:, c0:c0 + lane_chunk], cblk_ref[sg, 0:lane_chunk, :])
                    + _bdot(h_ref[:, ns + c0:ns + c0 + lane_chunk], cblk_ref[sg, lane_chunk:, :]))
    y = jnp.concatenate(outs, axis=-1) + d_ref[...] * u
    y = jax.nn.gelu(y)
    y = y * jax.nn.sigmoid(_bdot(y, wg_ref[...]) + bg_ref[...])
    y_ref[...] = jnp.swapaxes(y.reshape(tc, bsz, w), 0, 1).astype(y_ref.dtype)


def _s5(z3d, col0, w, p, tc):
    bsz, seq, _ = z3d.shape
    ns = p["lam_re"].shape[1]
    blk = tc * bsz
    full = lambda a: pl.BlockSpec(a.shape, lambda i: (0,) * a.ndim)
    return pl.pallas_call(
        functools.partial(_s5_kernel, tc=tc, bsz=bsz, ns=ns, lane_chunk=ns // p["bblk"].shape[0]),
        grid=(seq // tc,),
        in_specs=[pl.BlockSpec((bsz, tc, w), lambda i: (0, i, col0 // w)),
                  full(p["bblk"]), full(p["cblk"]), full(p["lam_re"]), full(p["lam_im"]),
                  full(p["d"]), full(p["w_glu"]), full(p["b_glu"])],
        out_specs=pl.BlockSpec((bsz, tc, w), lambda i: (0, i, 0)),
        out_shape=jax.ShapeDtypeStruct((bsz, seq, w), BF16),
        scratch_shapes=[pltpu.VMEM((blk, 2 * ns), F32),
                        pltpu.VMEM((bsz, ns), F32), pltpu.VMEM((bsz, ns), F32)],
        compiler_params=_cparams("arbitrary"),
        name="s5",
    )(z3d, p["bblk"], p["cblk"], p["lam_re"], p["lam_im"], p["d"], p["w_glu"], p["b_glu"])


def _mb_kernel(gate_ref, x_ref, bc_ref, dt_ref, cw_ref, cb_ref, dtb_ref, alog_ref, dsk_ref, nw_ref,
               expand_ref, tril_ref, after_ref, y_ref, prev_ref, st_ref,
               *, q, w, heads, groups, nstate):
    hd = w // heads
    gw = w // groups
    hpg = heads // groups

    @pl.when(pl.program_id(1) == 0)
    def _():
        prev_ref[...] = jnp.zeros(prev_ref.shape, F32)
        st_ref[...] = jnp.zeros(st_ref.shape, F32)

    rid = lax.broadcasted_iota(jnp.int32, (q, q), 0)
    cid = lax.broadcasted_iota(jnp.int32, (q, q), 1)
    causal = rid >= cid
    lane_w = lax.broadcasted_iota(jnp.int32, (q, gw), 1)
    expand = expand_ref[...]
    prev = prev_ref[...]
    for ci in range(x_ref.shape[1] // q):
        rows = pl.ds(ci * q, q)
        prev = _mb_chunk(rows, prev, causal, lane_w, expand, gate_ref, x_ref, bc_ref, dt_ref, cw_ref,
                         cb_ref, dtb_ref, alog_ref, dsk_ref, nw_ref, tril_ref, y_ref, st_ref,
                         q=q, w=w, groups=groups, nstate=nstate, gw=gw, hpg=hpg, hd=hd)
    prev_ref[...] = prev


def _mb_chunk(rows, prev, causal, lane_w, expand, gate_ref, x_ref, bc_ref, dt_ref, cw_ref, cb_ref,
              dtb_ref, alog_ref, dsk_ref, nw_ref, tril_ref, y_ref, st_ref,
              *, q, w, groups, nstate, gw, hpg, hd):
    sub = V7X_SUBLANES
    xbc = jnp.concatenate([x_ref[0, rows, :], bc_ref[0, rows, :]], axis=-1)
    full = jnp.concatenate([prev, xbc], axis=0)
    conv = jnp.broadcast_to(cb_ref[...], xbc.shape)
    for j in range(MB_CONV):
        shift = MB_CONV - 1 - j
        src = full if shift == 0 else pltpu.roll(full, shift, axis=0)
        conv = conv + src[sub:sub + q, :] * cw_ref[j:j + 1, :]
    act = _silu(conv)
    xs = act[:, 0:w]
    bmat = act[:, w:w + groups * nstate]
    cmat = act[:, w + groups * nstate:w + 2 * groups * nstate]

    dt = _softplus(dt_ref[0, rows, :] + dtb_ref[...])
    a = -jnp.exp(alog_ref[...]) * dt
    a_cum = _dot_exact_lhs(tril_ref[...], a)
    dt_e = _dot_exact_rhs(dt, expand)
    acum_e = _dot_exact_rhs(a_cum, expand)
    alast_e = acum_e[q - 1:q, :]
    xdt = xs * dt_e
    xdec = xdt * jnp.exp(alast_e - acum_e)
    a_cum_t = a_cum.T

    y_parts = []
    for g in range(groups):
        bg = bmat[:, g * nstate:(g + 1) * nstate]
        cg = cmat[:, g * nstate:(g + 1) * nstate]
        scores = lax.dot_general(cg.astype(BF16), bg.astype(BF16), (((1,), (1,)), ((), ())),
                                 preferred_element_type=F32)
        xg = xdt[:, g * gw:(g + 1) * gw]
        yg = _bdot(cg, st_ref[g]) * jnp.exp(acum_e[:, g * gw:(g + 1) * gw])
        for hh in range(hpg):
            h = g * hpg + hh
            col = jnp.broadcast_to(a_cum[:, h:h + 1], (q, q))
            rowv = jnp.broadcast_to(a_cum_t[h:h + 1, :], (q, q))
            decay = jnp.where(causal, jnp.exp(col - rowv), 0.0)
            xh = jnp.where((lane_w >= hh * hd) & (lane_w < (hh + 1) * hd), xg, 0.0)
            yg = yg + _bdot(scores * decay, xh)
        y_parts.append(yg)
        upd = _bdot(bg.T, xdec[:, g * gw:(g + 1) * gw])
        st_ref[g] = st_ref[g] * jnp.exp(alast_e[:, g * gw:(g + 1) * gw]) + upd
    y = jnp.concatenate(y_parts, axis=-1) + dsk_ref[...] * xs
    y_ref[0, rows, :] = _rms(y * _silu(gate_ref[0, rows, :]), nw_ref[...]).astype(y_ref.dtype)
    return xbc[q - sub:q, :]


def _mamba2(z3d, p, w, cols, q, tb, after):
    bsz, seq, _ = z3d.shape
    heads = w // MB_HEADDIM
    lanes = V7X_LANES
    cg, cx, cbc, cdt = cols
    blk = lambda width, col: pl.BlockSpec((1, tb, width), lambda b, c: (b, c, col // width))
    full = lambda a: pl.BlockSpec(a.shape, lambda b, c: (0,) * a.ndim)
    consts = [p["conv_w"], p["conv_b"], p["dt_bias"], p["a_log"], p["d"], p["norm_w"],
              p["expand"], p["tril"]]
    return pl.pallas_call(
        functools.partial(_mb_kernel, q=q, w=w, heads=heads, groups=MB_GROUPS, nstate=MB_STATE),
        grid=(bsz, seq // tb),
        in_specs=[blk(w, cg), blk(w, cx), blk(w, cbc), blk(lanes, cdt)] + [full(a) for a in consts]
        + [pl.BlockSpec((1, V7X_SUBLANES, lanes), lambda b, c: (0, 0, 0))],
        out_specs=pl.BlockSpec((1, tb, w), lambda b, c: (b, c, 0)),
        out_shape=jax.ShapeDtypeStruct((bsz, seq, w), BF16),
        scratch_shapes=[pltpu.VMEM((V7X_SUBLANES, 2 * w), F32),
                        pltpu.VMEM((MB_GROUPS, MB_STATE, w // MB_GROUPS), F32)],
        compiler_params=_cparams("parallel", "arbitrary"),
        name="mamba2",
    )(z3d, z3d, z3d, z3d, *consts, after)


def _pad_rows(a, rows, at):
    out = jnp.zeros((rows, a.shape[1]), a.dtype)
    return lax.dynamic_update_slice(out, a, (at, 0))


def _block_ones(w, head):
    idx = jnp.arange(w) // head
    return (idx[:, None] == idx[None, :]).astype(BF16)


def _pick_tile(n, target):
    t = min(n, target)
    while n % t:
        t //= 2
    return t


def kernel(x, norm_mix_w, w_in, w_branch, w_out, norm_ffn_w, w_ffn_in, w_ffn_out, norm_final_w, hgrn_lower_bounds, hgrn_norm_w, rwkv_mu, rwkv_w0, rwkv_w2, rwkv_a0, rwkv_a2, rwkv_g2, rwkv_k_k, rwkv_k_a, rwkv_r_k, rwkv_ln_w, rwkv_ln_b, rwkv_v0, rwkv_v1, rwkv_v2, s5_a_re, s5_a_im, s5_b_re, s5_b_im, s5_c_re, s5_c_im, s5_d, s5_log_dt, s5_w_glu, s5_b_glu, mamba_conv_w, mamba_conv_b, mamba_dt_bias, mamba_a_log, mamba_d, mamba_norm_w):
    bsz, seq, d = x.shape
    depth = w_in.shape[0]
    w = d // 2
    lanes = V7X_LANES
    t = bsz * seq
    mb_heads = w // MB_HEADDIM
    mb_bc = 2 * MB_GROUPS * MB_STATE
    lw = RW_DECAY_LORA + RW_A_LORA + RW_G_LORA
    assert mb_bc == w and 3 * w % lw == 0

    o_gate = 0
    o_hg = o_gate + N_BRANCH * d
    o_rw = o_hg + 4 * w
    o_rwl = o_rw + 3 * w
    o_s5 = o_rwl + lw
    o_mbg = o_s5 + w
    o_mbx = o_mbg + w
    o_mbbc = o_mbx + w
    o_mbdt = o_mbbc + mb_bc
    c_rw = 0
    c_hg = c_rw + 3 * w
    c_s5 = c_hg + 4 * w
    c_mbg = c_s5 + w
    c_mbx = c_mbg + w
    c_mbbc = c_mbx + w
    c_rwl = c_mbbc + mb_bc
    c_mbdt = c_rwl + lw
    n_cols = c_mbdt + lanes
    tn = 512
    n_pad = -(-n_cols // tn) * tn

    def mixer_cols(wl):
        sl = lambda a, b: wl[:, a:b]
        pieces = [sl(o_rw, o_rwl), sl(o_hg, o_rw), sl(o_s5, o_mbdt), sl(o_rwl, o_s5),
                  sl(o_mbdt, o_mbdt + mb_heads),
                  jnp.zeros((d, n_pad - c_mbdt - mb_heads), wl.dtype)]
        return jnp.concatenate(pieces, axis=1).astype(BF16)

    tm = _pick_tile(seq, 256)
    tm_in = _pick_tile(seq, 512)
    tb_hg = _pick_tile(seq, 512)
    c_hg_chunk = min(HG_CHUNK, tb_hg)
    tc_rw = _pick_tile(seq, 64)
    tc_s5 = _pick_tile(seq, 128)
    q_mb = min(MB_CHUNK, seq)
    tb_mb = _pick_tile(seq, 4 * q_mb)
    ff = w_ffn_out.shape[1]
    tn_ff = 256 if ff % 256 == 0 else lanes

    lower_bounds = _hg_bounds(hgrn_lower_bounds)
    ones_rw = _block_ones(w, RW_HEAD)
    eye_g = jnp.eye(lanes // S5_GROUP, dtype=F32)
    expand = (jnp.arange(lanes)[:, None] == (jnp.arange(w) // MB_HEADDIM)[None, :]).astype(BF16)
    tril = (jnp.arange(q_mb)[:, None] >= jnp.arange(q_mb)[None, :]).astype(BF16)

    x2d = x.reshape(t, d)
    v_first = jnp.zeros((t, w), F32)
    for l in range(depth):
        x3d = x2d.reshape(bsz, seq, d)
        z2d = _inproj(x2d, norm_mix_w[l], mixer_cols(w_in[l]), tm_in, tn, False, x3d)
        z3d = z2d.reshape(bsz, seq, n_pad)

        mu = rwkv_mu[l]
        has_vres = l > 0
        lv = max(l - 1, 0)
        rp = {
            "mu_main": mu[None, :3 * w], "mu_lora": mu[None, 3 * w:],
            "w0": rwkv_w0[l][None], "a0": rwkv_a0[l][None],
            "w2p": _pad_rows(rwkv_w2[l], lw, 0).astype(BF16),
            "a2p": _pad_rows(rwkv_a2[l], lw, RW_DECAY_LORA).astype(BF16),
            "g2p": _pad_rows(rwkv_g2[l], lw, RW_DECAY_LORA + RW_A_LORA).astype(BF16),
            "k_k": rwkv_k_k[l][None], "k_a": rwkv_k_a[l][None], "r_k": rwkv_r_k[l].reshape(1, w),
            "v0": rwkv_v0[lv][None],
            "v1p": jnp.pad(rwkv_v1[lv], ((0, 0), (0, lanes - rwkv_v1.shape[2]))).astype(BF16),
            "v2p": _pad_rows(rwkv_v2[lv], lanes, 0).astype(BF16),
            "ones": ones_rw, "ln_w": rwkv_ln_w[l], "ln_b": rwkv_ln_b[l],
        }
        rw_x, v_l, rw_g, rw_bonus = _rwkv7_front(z2d, v_first, rp, bsz, seq, w, c_rw, c_rwl, lw, tm,
                                                 has_vres)
        if l == 0:
            v_first = v_l

        lam_re, lam_im, bb_re, bb_im = _s5_discretize(s5_a_re[l], s5_a_im[l], s5_log_dt[l],
                                                      s5_b_re[l], s5_b_im[l])
        ns = lam_re.size
        nsg = w // lanes
        gps = lanes // S5_GROUP
        blk_in = lambda bb: jnp.einsum(
            "sgnc,gh->sgchn", bb.reshape(nsg, gps, S5_STATE, S5_GROUP), eye_g
        ).reshape(nsg, lanes, gps * S5_STATE)
        blk_out = lambda cc: jnp.einsum(
            "sgcn,gh->sgnhc", cc.reshape(nsg, gps, S5_GROUP, S5_STATE), eye_g
        ).reshape(nsg, gps * S5_STATE, lanes)
        sp = {
            "bblk": jnp.concatenate([blk_in(bb_re), blk_in(bb_im)], axis=2).astype(BF16),
            "cblk": jnp.concatenate([blk_out(s5_c_re[l]), -blk_out(s5_c_im[l])], axis=1).astype(BF16),
            "lam_re": lam_re.reshape(1, ns), "lam_im": lam_im.reshape(1, ns),
            "d": s5_d[l][None], "w_glu": s5_w_glu[l].astype(BF16), "b_glu": s5_b_glu[l][None],
        }
        y_s5 = _s5(z3d, c_s5, w, sp, tc_s5).reshape(t, w)

        mp = {
            "conv_w": mamba_conv_w[l], "conv_b": mamba_conv_b[l][None],
            "dt_bias": jnp.pad(mamba_dt_bias[l], (0, lanes - mb_heads))[None],
            "a_log": jnp.pad(mamba_a_log[l], (0, lanes - mb_heads))[None],
            "d": jnp.repeat(mamba_d[l], MB_HEADDIM)[None], "norm_w": mamba_norm_w[l][None],
            "expand": expand, "tril": tril,
        }
        y_hg = _hgrn2(z3d, lower_bounds[l], hgrn_norm_w[l], w, c_hg, tb_hg, c_hg_chunk,
                      HG_SUBBLOCK, z3d).reshape(t, w)
        rw_y = _rw_scan(rw_x, lanes // (bsz * (w // RW_HEAD)), tc_rw, (y_hg, y_s5))
        gates = _inproj(x2d, norm_mix_w[l], w_in[l][:, o_gate:o_hg].astype(BF16), tm_in, tn, True,
                        rw_y)
        y_mb = _mamba2(z3d, mp, w, (c_mbg, c_mbx, c_mbbc, c_mbdt), q_mb, tb_mb, rw_y).reshape(t, w)
        rw_heads = w // RW_HEAD
        rw_raw = _lanes_to_chains_v(rw_y, bsz, seq, rw_heads, RW_HEAD, lanes // (bsz * rw_heads))
        x2d = _merge(y_hg, y_s5, y_mb, (rw_raw, rw_g, rw_bonus, rp["ln_w"], rp["ln_b"], rp["ones"]),
                     gates, w_branch[l].astype(BF16), w_out[l].astype(BF16), x2d, tm)
        act = _ffn_in(x2d, norm_ffn_w[l], w_ffn_in[l].astype(BF16), tm_in, tn_ff)
        x2d = _ffn_out(act, w_ffn_out[l].astype(BF16), x2d, norm_final_w, l == depth - 1, tm)
    return x2d.reshape(bsz, seq, d)
```

```python
import functools
import math

import jax
import jax.numpy as jnp
from jax import lax
from jax.experimental import pallas as pl
from jax.experimental.pallas import tpu as pltpu

F32 = jnp.float32
BF16 = jnp.bfloat16

V7X_LANES = 128
V7X_SUBLANES = 8
V7X_VMEM_LIMIT_BYTES = 56 * 1024 * 1024

RMS_EPS = 1e-6
N_BRANCH = 4
HG_DK = 128
HG_TINY = 1e-30
HG_CHUNK = 128
HG_SUBBLOCK = 4
RW_HEAD = 64
RW_DECAY_LORA = 64
RW_A_LORA = 64
RW_G_LORA = 128
RW_LN_EPS = 64e-5
RW_NVEC = 6
S5_GROUP = 16
S5_STATE = 64
MB_HEADDIM = 64
MB_GROUPS = 2
MB_STATE = 128
MB_CONV = 4
MB_CHUNK = 128


def _cparams(*sem):
    return pltpu.CompilerParams(dimension_semantics=sem, vmem_limit_bytes=V7X_VMEM_LIMIT_BYTES)


def _bdot(a, b):
    return jnp.dot(a.astype(BF16), b.astype(BF16), preferred_element_type=F32)


def _split(x, parts):
    pieces, rest = [], x
    for i in range(parts):
        piece = rest.astype(BF16)
        pieces.append(piece)
        if i + 1 < parts:
            rest = rest - piece.astype(F32)
    return pieces


def _dot_exact_rhs(x, m, parts=3):
    return sum(jnp.dot(p, m, preferred_element_type=F32) for p in _split(x, parts))


def _dot_exact_lhs(m, x, parts=3):
    return sum(jnp.dot(m, p, preferred_element_type=F32) for p in _split(x, parts))


def _softplus(x):
    return jnp.maximum(x, 0.0) + jnp.log(1.0 + jnp.exp(-jnp.abs(x)))


def _silu(x):
    return x * jax.nn.sigmoid(x)


def _rms(x, w):
    return x * lax.rsqrt(jnp.mean(x * x, axis=-1, keepdims=True) + RMS_EPS) * w


def _resident(shape):
    return pl.BlockSpec(shape, lambda i: (0,) * len(shape), pipeline_mode=pl.Buffered(1))


def _inproj_kernel(x_ref, nw_ref, w_ref, after_ref, o_ref, *, gate, tn):
    u = _rms(x_ref[...], nw_ref[...]).astype(BF16)
    for n0 in range(0, o_ref.shape[1], tn):
        z = jnp.dot(u, w_ref[:, n0:n0 + tn], preferred_element_type=F32)
        o_ref[:, n0:n0 + tn] = jax.nn.sigmoid(z).astype(o_ref.dtype) if gate else z


def _inproj(x2d, norm_w, w_bf16, tm, tn, gate, after):
    t, d = x2d.shape
    n = w_bf16.shape[1]
    return pl.pallas_call(
        functools.partial(_inproj_kernel, gate=gate, tn=tn),
        grid=(t // tm,),
        in_specs=[pl.BlockSpec((tm, d), lambda i: (i, 0)),
                  pl.BlockSpec((1, d), lambda i: (0, 0)),
                  _resident((d, n)),
                  pl.BlockSpec((1, V7X_SUBLANES, V7X_LANES), lambda i: (0, 0, 0))],
        out_specs=pl.BlockSpec((tm, n), lambda i: (i, 0)),
        out_shape=jax.ShapeDtypeStruct((t, n), BF16 if gate else F32),
        compiler_params=_cparams("parallel"),
        name="inproj_gate" if gate else "inproj",
    )(x2d, norm_w.reshape(1, d), w_bf16, after)


def _cast_weight_once(src_ref, dst_ref):
    @pl.when(pl.program_id(0) == 0)
    def _():
        dst_ref[...] = src_ref[...].astype(BF16)


def _ffn_in_kernel(x_ref, nw_ref, wf_ref, o_ref, w_ref, *, tn):
    _cast_weight_once(wf_ref, w_ref)
    ff = o_ref.shape[1]
    u = _rms(x_ref[...], nw_ref[...]).astype(BF16)
    for n0 in range(0, ff, tn):
        gate = jnp.dot(u, w_ref[:, n0:n0 + tn], preferred_element_type=F32)
        up = jnp.dot(u, w_ref[:, ff + n0:ff + n0 + tn], preferred_element_type=F32)
        o_ref[:, n0:n0 + tn] = (_silu(gate) * up).astype(BF16)


def _ffn_in(x2d, norm_w, w_f32, tm, tn):
    t, d = x2d.shape
    ff = w_f32.shape[1] // 2
    return pl.pallas_call(
        functools.partial(_ffn_in_kernel, tn=tn),
        grid=(t // tm,),
        in_specs=[pl.BlockSpec((tm, d), lambda i: (i, 0)),
                  pl.BlockSpec((1, d), lambda i: (0, 0)),
                  _resident((d, 2 * ff))],
        out_specs=pl.BlockSpec((tm, ff), lambda i: (i, 0)),
        out_shape=jax.ShapeDtypeStruct((t, ff), BF16),
        scratch_shapes=[pltpu.VMEM((d, 2 * ff), BF16)],
        compiler_params=_cparams("arbitrary"),
        name="ffn_in",
    )(x2d, norm_w.reshape(1, d), w_f32)


def _ffn_out_kernel(a_ref, wf_ref, x_ref, fw_ref, o_ref, w_ref, *, final_norm):
    _cast_weight_once(wf_ref, w_ref)
    y = x_ref[...] + jnp.dot(a_ref[...], w_ref[...], preferred_element_type=F32)
    if final_norm:
        y = _rms(y, fw_ref[...])
    o_ref[...] = y


def _ffn_out(act, w_f32, x2d, final_w, final_norm, tm):
    t, d = x2d.shape
    ff = act.shape[1]
    return pl.pallas_call(
        functools.partial(_ffn_out_kernel, final_norm=final_norm),
        grid=(t // tm,),
        in_specs=[pl.BlockSpec((tm, ff), lambda i: (i, 0)),
                  _resident((ff, d)),
                  pl.BlockSpec((tm, d), lambda i: (i, 0)),
                  pl.BlockSpec((1, d), lambda i: (0, 0))],
        out_specs=pl.BlockSpec((tm, d), lambda i: (i, 0)),
        out_shape=jax.ShapeDtypeStruct((t, d), F32),
        scratch_shapes=[pltpu.VMEM((ff, d), BF16)],
        compiler_params=_cparams("arbitrary"),
        name="ffn_out",
    )(act, w_f32, x2d, final_w.reshape(1, d))


def _merge_kernel(yh_ref, ys_ref, ym_ref, yr_ref, rg_ref, rb_ref, lnw_ref, lnb_ref, ones_ref,
                  zg_ref, wbf_ref, wof_ref, x_ref, o_ref, wb_ref, wo_ref):
    _cast_weight_once(wbf_ref, wb_ref)
    _cast_weight_once(wof_ref, wo_ref)
    d = x_ref.shape[1]
    y_rw = _rw_finish(yr_ref[...], rg_ref[...], rb_ref[...], lnw_ref[...], lnb_ref[...],
                      ones_ref[...])
    acc = jnp.zeros(x_ref.shape, F32)
    for k, y in enumerate((yh_ref[...], y_rw, ys_ref[...], ym_ref[...])):
        proj = jnp.dot(y.astype(BF16), wb_ref[k], preferred_element_type=F32)
        acc = acc + zg_ref[:, k * d:(k + 1) * d].astype(F32) * proj
    o_ref[...] = x_ref[...] + jnp.dot(acc.astype(BF16), wo_ref[...], preferred_element_type=F32)


def _merge(y_hg, y_s5, y_mb, rw, gates, wb_f32, wo_f32, x2d, tm):
    t, d = x2d.shape
    w = y_hg.shape[1]
    yspec = pl.BlockSpec((tm, w), lambda i: (i, 0))
    row = pl.BlockSpec((1, w), lambda i: (0, 0))
    y_raw, g, bonus, ln_w, ln_b, ones = rw
    return pl.pallas_call(
        _merge_kernel,
        grid=(t // tm,),
        in_specs=[yspec, yspec, yspec, yspec, yspec, yspec, row, row, _resident((w, w)),
                  pl.BlockSpec((tm, N_BRANCH * d), lambda i: (i, 0)),
                  _resident((N_BRANCH, w, d)),
                  _resident((d, d)),
                  pl.BlockSpec((tm, d), lambda i: (i, 0))],
        out_specs=pl.BlockSpec((tm, d), lambda i: (i, 0)),
        out_shape=jax.ShapeDtypeStruct((t, d), F32),
        scratch_shapes=[pltpu.VMEM((N_BRANCH, w, d), BF16), pltpu.VMEM((d, d), BF16)],
        compiler_params=_cparams("arbitrary"),
        name="merge",
    )(y_hg, y_s5, y_mb, y_raw, g, bonus, ln_w.reshape(1, w), ln_b.reshape(1, w), ones,
      gates, wb_f32, wo_f32, x2d)


def _hg_bounds_kernel(h_ref, o_ref):
    h = h_ref[...]
    depth = h.shape[0]
    m = jnp.max(h, axis=0, keepdims=True)
    e = jnp.exp(h - m)
    p = e / jnp.sum(e, axis=0, keepdims=True)
    run = jnp.zeros_like(p[0:1])
    rows = []
    for l in range(depth):
        run = run + p[l:l + 1]
        rows.append(run - p[0:1])
    o_ref[...] = jnp.concatenate(rows, axis=0)


def _hg_bounds(hgrn_lower_bounds):
    return pl.pallas_call(
        _hg_bounds_kernel,
        out_shape=jax.ShapeDtypeStruct(hgrn_lower_bounds.shape, F32),
        name="hg_bounds",
    )(hgrn_lower_bounds)


def _hg_chunk_kernel(q_ref, f_ref, i_ref, g_ref, lb_ref, nw_ref, tril_ref, after_ref, y_ref,
                     st_ref, *, tb, c, cs, heads, dk):
    sub = V7X_SUBLANES
    nt = (((1,), (1,)), ((), ()))

    @pl.when(pl.program_id(1) == 0)
    def _():
        st_ref[...] = jnp.zeros(st_ref.shape, F32)

    def shift_rows(x, dlt):
        return pltpu.roll(x.reshape(c // sub, sub, dk), dlt, axis=1).reshape(c, dk)

    lb = lb_ref[...]
    tril = tril_ref[...]
    rid = lax.broadcasted_iota(jnp.int32, (c, dk), 0)
    rid1 = lax.broadcasted_iota(jnp.int32, (c, 1), 0)
    pr = lax.broadcasted_iota(jnp.int32, (c, c), 0)
    pc = lax.broadcasted_iota(jnp.int32, (c, c), 1)

    def chunk(ci):
        rows = pl.ds(ci * c, c)
        ff = f_ref[0, rows, :]
        q = _silu(q_ref[0, rows, :])
        dec = jnp.maximum(lb + (1.0 - lb) * jax.nn.sigmoid(ff), HG_TINY)
        k = (1.0 - lb) * jax.nn.sigmoid(-ff)
        v = i_ref[0, rows, :]
        b = _dot_exact_lhs(tril, jnp.log(dec))
        outs = []
        for h in range(heads):
            hs = slice(h * dk, (h + 1) * dk)
            qh, kh, bh, vh = q[:, hs], k[:, hs], b[:, hs], v[:, hs]
            blast = bh[c - 1:c, :]
            st = st_ref[h]
            o = lax.dot_general((qh * jnp.exp(bh)).astype(BF16), st.astype(BF16), nt,
                                preferred_element_type=F32)
            att = jnp.zeros((c, c), F32)
            grp = 2 * cs
            while grp <= c:
                half = grp // 2
                qparts, kparts = [], []
                for r0 in range(0, c, grp):
                    d = bh[r0:r0 + grp, :] - bh[r0 + half - 1:r0 + half, :]
                    if half % sub == 0:
                        zero = jnp.zeros((half, dk), F32)
                        kparts += [kh[r0:r0 + half, :] * jnp.exp(-d[0:half, :]), zero]
                        qparts += [zero, qh[r0 + half:r0 + grp, :] * jnp.exp(d[half:grp, :])]
                    else:
                        upper = (rid[0:grp, :] % grp) >= half
                        e = jnp.exp(jnp.where(upper, d, -d))
                        kparts.append(jnp.where(upper, 0.0, kh[r0:r0 + grp, :] * e))
                        qparts.append(jnp.where(upper, qh[r0:r0 + grp, :] * e, 0.0))
                a = lax.dot_general(jnp.concatenate(qparts, axis=0).astype(BF16),
                                    jnp.concatenate(kparts, axis=0).astype(BF16), nt,
                                    preferred_element_type=F32)
                att = att + (a if grp == c else jnp.where(pr // grp == pc // grp, a, 0.0))
                grp *= 2
            o = o + _bdot(att, vh)
            for dlt in range(cs):
                if dlt == 0:
                    a = jnp.sum(qh * kh, axis=-1, keepdims=True)
                    o = o + a * vh
                else:
                    e = jnp.exp(jnp.minimum(bh - shift_rows(bh, dlt), 0.0))
                    a = jnp.sum(qh * shift_rows(kh, dlt) * e, axis=-1, keepdims=True)
                    a = jnp.where((rid1 % cs) >= dlt, a, 0.0)
                    o = o + a * shift_rows(vh, dlt)
            kd = kh * jnp.exp(blast - bh)
            st_ref[h] = st * jnp.exp(blast) + _bdot(vh.T, kd)
            outs.append(o * lax.rsqrt(jnp.mean(o * o, axis=-1, keepdims=True) + RMS_EPS))
        y = jnp.concatenate(outs, axis=-1) * nw_ref[...]
        y_ref[0, rows, :] = (y * _silu(g_ref[0, rows, :])).astype(y_ref.dtype)

    for ci in range(tb // c):
        chunk(ci)


def _hgrn2(z3d, lb, norm_w, w, col0, tb, c, cs, after):
    bsz, seq, _ = z3d.shape
    heads = w // HG_DK
    cb = col0 // w
    assert V7X_SUBLANES % cs == 0 and c % V7X_SUBLANES == 0
    tril = (jnp.arange(c)[:, None] >= jnp.arange(c)[None, :]).astype(BF16)
    zspec = lambda j: pl.BlockSpec((1, tb, w), lambda b, i: (b, i, cb + j))
    row = pl.BlockSpec((1, w), lambda b, i: (0, 0))
    return pl.pallas_call(
        functools.partial(_hg_chunk_kernel, tb=tb, c=c, cs=cs, heads=heads, dk=HG_DK),
        grid=(bsz, seq // tb),
        in_specs=[zspec(0), zspec(1), zspec(2), zspec(3), row, row,
                  pl.BlockSpec((c, c), lambda b, i: (0, 0)),
                  pl.BlockSpec((1, V7X_SUBLANES, V7X_LANES), lambda b, i: (0, 0, 0))],
        out_specs=pl.BlockSpec((1, tb, w), lambda b, i: (b, i, 0)),
        out_shape=jax.ShapeDtypeStruct((bsz, seq, w), BF16),
        scratch_shapes=[pltpu.VMEM((heads, HG_DK, HG_DK), F32)],
        compiler_params=_cparams("parallel", "arbitrary"),
        name="hgrn2",
    )(z3d, z3d, z3d, z3d, lb.reshape(1, w), norm_w.reshape(1, w), tril, after)


def _lanes_to_chains_v(a, bsz, seq, heads, dv, nq):
    a = a.reshape(seq, dv // nq, nq, bsz, heads).transpose(3, 0, 4, 2, 1)
    return a.reshape(bsz * seq, heads * dv)


def _shift_rows(cur, prev_last, first):
    rolled = pltpu.roll(cur, 1, axis=0)
    row0 = jnp.where(first, 0.0, prev_last)
    rid = lax.broadcasted_iota(jnp.int32, cur.shape, 0)
    return jnp.where(rid == 0, jnp.broadcast_to(row0, cur.shape), rolled)


def _rw_prep_kernel(zm_ref, zmp_ref, zl_ref, zlp_ref, vf_ref,
                    mum_ref, mul_ref, w0_ref, w2_ref, a0_ref, a2_ref, g2_ref,
                    kk_ref, ka_ref, rk_ref, v0_ref, v1_ref, v2_ref, ones_ref,
                    kvec_out, v_out, g_out, bonus_out,
                    *, w, tiles_per_seq, has_vres):
    first = (pl.program_id(0) % tiles_per_seq) == 0
    sub = V7X_SUBLANES
    zm = zm_ref[...]
    zl = zl_ref[...]
    zms = zm + (_shift_rows(zm, zmp_ref[sub - 1:sub, :], first) - zm) * mum_ref[...]
    zls = zl + (_shift_rows(zl, zlp_ref[sub - 1:sub, :], first) - zl) * mul_ref[...]
    r = zms[:, 0:w]
    k = zms[:, w:2 * w]
    v = zms[:, 2 * w:3 * w]
    w_log = -_softplus(-(w0_ref[...] + _bdot(jnp.tanh(zls), w2_ref[...]))) - 0.5
    decay = jnp.exp(-jnp.exp(w_log))
    if has_vres:
        mix = jax.nn.sigmoid(v0_ref[...] + _bdot(_bdot(v, v1_ref[...]), v2_ref[...]))
        v = v + (vf_ref[...] - v) * mix
    a = jax.nn.sigmoid(a0_ref[...] + _bdot(zls, a2_ref[...]))
    g = _bdot(jax.nn.sigmoid(zls), g2_ref[...])
    ones = ones_ref[...]
    kk = k * kk_ref[...]
    ss = _dot_exact_rhs(kk * kk, ones)
    kk = kk / jnp.maximum(jnp.sqrt(ss), 1e-12)
    k2 = k * (1.0 + (a - 1.0) * ka_ref[...])
    vecs = (r, decay, k2, -kk, kk * a, v)
    for h in range(w // RW_HEAD):
        hs = slice(h * RW_HEAD, (h + 1) * RW_HEAD)
        row = jnp.concatenate([x[:, hs] for x in vecs], axis=1)
        for q in range(kvec_out.shape[0]):
            kvec_out[q, 0, h] = row
    v_out[...] = v
    g_out[...] = g.astype(g_out.dtype)
    bonus_out[...] = (_dot_exact_rhs(r * k2 * rk_ref[...], ones) * v).astype(bonus_out.dtype)


def _rw_prep(z2d, v_first, p, col_main, col_lora, w, lw, seq, tm, has_vres):
    t = z2d.shape[0]
    sub = V7X_SUBLANES
    mb = col_main // (3 * w)
    lbk = col_lora // lw
    rows8 = tm // sub
    heads = w // RW_HEAD
    tps = seq // tm
    nq = V7X_LANES // (t // seq * heads)

    def prev_idx(i):
        return jnp.maximum(i * rows8 - 1, 0)

    row = lambda n: pl.BlockSpec((1, n), lambda i: (0, 0))
    full = lambda a: pl.BlockSpec(a.shape, lambda i: (0,) * a.ndim)
    ospec = pl.BlockSpec((tm, w), lambda i: (i, 0))
    oshape = jax.ShapeDtypeStruct((t, w), F32)
    args = [z2d, z2d, z2d, z2d, v_first,
            p["mu_main"], p["mu_lora"], p["w0"], p["w2p"], p["a0"], p["a2p"], p["g2p"],
            p["k_k"], p["k_a"], p["r_k"], p["v0"], p["v1p"], p["v2p"], p["ones"]]
    in_specs = [pl.BlockSpec((tm, 3 * w), lambda i: (i, mb)),
                pl.BlockSpec((sub, 3 * w), lambda i: (prev_idx(i), mb)),
                pl.BlockSpec((tm, lw), lambda i: (i, lbk)),
                pl.BlockSpec((sub, lw), lambda i: (prev_idx(i), lbk)),
                ospec,
                row(3 * w), row(lw), row(w), full(p["w2p"]), row(w), full(p["a2p"]), full(p["g2p"]),
                row(w), row(w), row(w), row(w), full(p["v1p"]), full(p["v2p"]), full(p["ones"])]
    return pl.pallas_call(
        functools.partial(_rw_prep_kernel, w=w, tiles_per_seq=seq // tm, has_vres=has_vres),
        grid=(t // tm,),
        in_specs=in_specs,
        out_specs=[pl.BlockSpec((nq, 1, heads, tm, RW_NVEC * RW_HEAD),
                                lambda i: (0, i // tps, 0, i % tps, 0)), ospec, ospec, ospec],
        out_shape=[jax.ShapeDtypeStruct((nq, t // seq, heads, seq, RW_NVEC * RW_HEAD), F32),
                   oshape, jax.ShapeDtypeStruct((t, w), BF16), jax.ShapeDtypeStruct((t, w), BF16)],
        compiler_params=_cparams("parallel"),
        name="rw_prep",
    )(*args)


def _rw_scan_kernel(x_ref, *rest, tc, dk, nvb, nq):
    y_ref, s_ref = rest[-2:]
    sub = V7X_SUBLANES
    lanes = V7X_LANES
    jr, jw, jk, ja, jb, jv = range(RW_NVEC)
    vl = nvb * sub
    qid = lax.broadcasted_iota(jnp.int32, (sub, lanes), 1) // (lanes // nq)

    @pl.when(pl.program_id(0) == 0)
    def _():
        s_ref[...] = jnp.zeros(s_ref.shape, F32)

    def bc(j, t, kk):
        return jnp.broadcast_to(x_ref[t, j, pl.ds(kk, 1), :], (sub, lanes))

    sa0 = [jnp.zeros((sub, lanes), F32) for _ in range(nvb)]
    for kk in range(dk):
        arow = bc(ja, 0, kk)
        for j in range(nvb):
            sa0[j] = sa0[j] + s_ref[kk, pl.ds(sub * j, sub), :] * arow

    def step(t, sa):
        tn = jnp.minimum(t + 1, tc - 1)
        vb = []
        for j in range(nvb):
            vj = x_ref[t, jv, pl.ds(sub * j, sub), :]
            for q in range(1, nq):
                vj = jnp.where(qid == q, x_ref[t, jv, pl.ds(q * vl + sub * j, sub), :], vj)
            vb.append(vj)
        yacc = [jnp.zeros((sub, lanes), F32) for _ in range(nvb)]
        sacc = [jnp.zeros((sub, lanes), F32) for _ in range(nvb)]
        for kk in range(dk):
            wrow = bc(jw, t, kk)
            brow = bc(jb, t, kk)
            krow = bc(jk, t, kk)
            rrow = bc(jr, t, kk)
            anext = bc(ja, tn, kk)
            for j in range(nvb):
                s = s_ref[kk, pl.ds(sub * j, sub), :] * wrow + sa[j] * brow + vb[j] * krow
                s_ref[kk, pl.ds(sub * j, sub), :] = s
                yacc[j] = yacc[j] + s * rrow
                sacc[j] = sacc[j] + s * anext
        for j in range(nvb):
            y_ref[t, pl.ds(sub * j, sub), :] = yacc[j]
        return tuple(sacc)

    lax.fori_loop(0, tc, step, tuple(sa0))


def _rw_scan(x_l, nq, tc, after):
    s, nvec, dk, lanes = x_l.shape
    vl = dk // nq
    kspec = pl.BlockSpec((tc, nvec, dk, lanes), lambda i: (i, 0, 0, 0))
    vspec = pl.BlockSpec((tc, vl, lanes), lambda i: (i, 0, 0))
    order = [pl.BlockSpec((16, lanes), lambda i: (0, 0))] * len(after)
    return pl.pallas_call(
        functools.partial(_rw_scan_kernel, tc=tc, dk=dk, nvb=vl // V7X_SUBLANES, nq=nq),
        grid=(s // tc,),
        in_specs=[kspec] + order,
        out_specs=vspec,
        out_shape=jax.ShapeDtypeStruct((s, vl, lanes), F32),
        scratch_shapes=[pltpu.VMEM((dk, vl, lanes), F32)],
        compiler_params=_cparams("arbitrary"),
        name="rw_scan",
    )(x_l, *after)


def _rw_finish(y, g, bonus, ln_w, ln_b, ones):
    inv_n = 1.0 / RW_HEAD
    mean = _dot_exact_rhs(y, ones, parts=2) * inv_n
    yc = y - mean
    var = _dot_exact_rhs(yc * yc, ones, parts=2) * inv_n
    y = yc * lax.rsqrt(var + RW_LN_EPS) * ln_w + ln_b
    return (y + bonus) * g


def _rwkv7_front(z2d, v_first, p, bsz, seq, w, col_main, col_lora, lw, tm, has_vres):
    kvec, v, g, bonus = _rw_prep(z2d, v_first, p, col_main, col_lora, w, lw, seq, tm, has_vres)
    lanes = kvec.shape[0] * kvec.shape[1] * kvec.shape[2]
    x_l = kvec.reshape(lanes, seq, RW_NVEC * RW_HEAD).transpose(1, 2, 0)
    return x_l.reshape(seq, RW_NVEC, RW_HEAD, lanes), v, g, bonus


def _s5_disc_kernel(are_ref, aim_ref, dt_ref, bre_ref, bim_ref, lre_ref, lim_ref, bbre_ref, bbim_ref):
    a_re = are_ref[...]
    a_im = aim_ref[...]
    dt = jnp.exp(dt_ref[...])
    mag = jnp.exp(dt * a_re)
    lam_re = mag * jnp.cos(dt * a_im)
    lam_im = mag * jnp.sin(dt * a_im)
    den = a_re * a_re + a_im * a_im
    coef_re = ((lam_re - 1.0) * a_re + lam_im * a_im) / den
    coef_im = (lam_im * a_re - (lam_re - 1.0) * a_im) / den
    b_re = bre_ref[...]
    b_im = bim_ref[...]
    lre_ref[...] = lam_re
    lim_ref[...] = lam_im
    bbre_ref[...] = coef_re * b_re - coef_im * b_im
    bbim_ref[...] = coef_re * b_im + coef_im * b_re


def _s5_discretize(a_re, a_im, log_dt, b_re, b_im):
    g, n, c = b_re.shape
    shp = (g, n * c)
    bc = lambda a: jnp.broadcast_to(a[..., None], (g, n, c)).reshape(shp)
    dtb = jnp.broadcast_to(log_dt[:, None], shp)
    o = jax.ShapeDtypeStruct(shp, F32)
    lre, lim, bbre, bbim = pl.pallas_call(
        _s5_disc_kernel, out_shape=[o, o, o, o], name="s5_disc",
    )(bc(a_re), bc(a_im), dtb, b_re.reshape(shp), b_im.reshape(shp))
    un = lambda a: a.reshape(g, n, c)
    return un(lre)[..., 0], un(lim)[..., 0], un(bbre), un(bbim)


def _s5_kernel(u_ref, bblk_ref, cblk_ref, lre_ref, lim_ref, d_ref, wg_ref, bg_ref, y_ref,
               h_ref, hr_ref, hi_ref, *, tc, bsz, ns, lane_chunk):
    @pl.when(pl.program_id(0) == 0)
    def _():
        hr_ref[...] = jnp.zeros(hr_ref.shape, F32)
        hi_ref[...] = jnp.zeros(hi_ref.shape, F32)

    w = u_ref.shape[2]
    u = jnp.swapaxes(u_ref[...], 0, 1).reshape(tc * bsz, w)
    nsg = bblk_ref.shape[0]
    lanes = V7X_LANES
    for sg in range(nsg):
        c0 = sg * lane_chunk
        drive = _bdot(u[:, sg * lanes:(sg + 1) * lanes], bblk_ref[sg])
        h_ref[:, c0:c0 + lane_chunk] = drive[:, 0:lane_chunk]
        h_ref[:, ns + c0:ns + c0 + lane_chunk] = drive[:, lane_chunk:2 * lane_chunk]
    for c0 in range(0, ns, lane_chunk):
        lr = jnp.broadcast_to(lre_ref[:, c0:c0 + lane_chunk], (bsz, lane_chunk))
        li = jnp.broadcast_to(lim_ref[:, c0:c0 + lane_chunk], (bsz, lane_chunk))

        def step(t, carry, c0=c0, lr=lr, li=li):
            hr, hi = carry
            rows = pl.ds(pl.multiple_of(t * bsz, bsz), bsz)
            nr = lr * hr - li * hi + h_ref[rows, c0:c0 + lane_chunk]
            ni = lr * hi + li * hr + h_ref[rows, ns + c0:ns + c0 + lane_chunk]
            h_ref[rows, c0:c0 + lane_chunk] = nr
            h_ref[rows, ns + c0:ns + c0 + lane_chunk] = ni
            return nr, ni

        hr, hi = lax.fori_loop(0, tc, step,
                               (hr_ref[:, c0:c0 + lane_chunk], hi_ref[:, c0:c0 + lane_chunk]))
        hr_ref[:, c0:c0 + lane_chunk] = hr
        hi_ref[:, c0:c0 + lane_chunk] = hi
    outs = []
    for sg in range(nsg):
        c0 = sg * lane_chunk
        outs.append(_bdot(h_ref[:, c0:c0 + lane_chunk], cblk_ref[sg, 0:lane_chunk, :])
                    + _bdot(h_ref[:, ns + c0:ns + c0 + lane_chunk], cblk_ref[sg, lane_chunk:, :]))
    y = jnp.concatenate(outs, axis=-1) + d_ref[...] * u
    y = jax.nn.gelu(y)
    y = y * jax.nn.sigmoid(_bdot(y, wg_ref[...]) + bg_ref[...])
    y_ref[...] = jnp.swapaxes(y.reshape(tc, bsz, w), 0, 1).astype(y_ref.dtype)


def _s5(z3d, col0, w, p, tc):
    bsz, seq, _ = z3d.shape
    ns = p["lam_re"].shape[1]
    blk = tc * bsz
    full = lambda a: pl.BlockSpec(a.shape, lambda i: (0,) * a.ndim)
    return pl.pallas_call(
        functools.partial(_s5_kernel, tc=tc, bsz=bsz, ns=ns, lane_chunk=ns // p["bblk"].shape[0]),
        grid=(seq // tc,),
        in_specs=[pl.BlockSpec((bsz, tc, w), lambda i: (0, i, col0 // w)),
                  full(p["bblk"]), full(p["cblk"]), full(p["lam_re"]), full(p["lam_im"]),
                  full(p["d"]), full(p["w_glu"]), full(p["b_glu"])],
        out_specs=pl.BlockSpec((bsz, tc, w), lambda i: (0, i, 0)),
        out_shape=jax.ShapeDtypeStruct((bsz, seq, w), BF16),
        scratch_shapes=[pltpu.VMEM((blk, 2 * ns), F32),
                        pltpu.VMEM((bsz, ns), F32), pltpu.VMEM((bsz, ns), F32)],
        compiler_params=_cparams("arbitrary"),
        name="s5",
    )(z3d, p["bblk"], p["cblk"], p["lam_re"], p["lam_im"], p["d"], p["w_glu"], p["b_glu"])


def _mb_kernel(gate_ref, x_ref, bc_ref, dt_ref, cw_ref, cb_ref, dtb_ref, alog_ref, dsk_ref, nw_ref,
               expand_ref, tril_ref, after_ref, y_ref, prev_ref, st_ref,
               *, q, w, heads, groups, nstate):
    hd = w // heads
    gw = w // groups
    hpg = heads // groups

    @pl.when(pl.program_id(1) == 0)
    def _():
        prev_ref[...] = jnp.zeros(prev_ref.shape, F32)
        st_ref[...] = jnp.zeros(st_ref.shape, F32)

    rid = lax.broadcasted_iota(jnp.int32, (q, q), 0)
    cid = lax.broadcasted_iota(jnp.int32, (q, q), 1)
    causal = rid >= cid
    lane_w = lax.broadcasted_iota(jnp.int32, (q, gw), 1)
    expand = expand_ref[...]
    prev = prev_ref[...]
    for ci in range(x_ref.shape[1] // q):
        rows = pl.ds(ci * q, q)
        prev = _mb_chunk(rows, prev, causal, lane_w, expand, gate_ref, x_ref, bc_ref, dt_ref, cw_ref,
                         cb_ref, dtb_ref, alog_ref, dsk_ref, nw_ref, tril_ref, y_ref, st_ref,
                         q=q, w=w, groups=groups, nstate=nstate, gw=gw, hpg=hpg, hd=hd)
    prev_ref[...] = prev


def _mb_chunk(rows, prev, causal, lane_w, expand, gate_ref, x_ref, bc_ref, dt_ref, cw_ref, cb_ref,
              dtb_ref, alog_ref, dsk_ref, nw_ref, tril_ref, y_ref, st_ref,
              *, q, w, groups, nstate, gw, hpg, hd):
    sub = V7X_SUBLANES
    xbc = jnp.concatenate([x_ref[0, rows, :], bc_ref[0, rows, :]], axis=-1)
    full = jnp.concatenate([prev, xbc], axis=0)
    conv = jnp.broadcast_to(cb_ref[...], xbc.shape)
    for j in range(MB_CONV):
        shift = MB_CONV - 1 - j
        src = full if shift == 0 else pltpu.roll(full, shift, axis=0)
        conv = conv + src[sub:sub + q, :] * cw_ref[j:j + 1, :]
    act = _silu(conv)
    xs = act[:, 0:w]
    bmat = act[:, w:w + groups * nstate]
    cmat = act[:, w + groups * nstate:w + 2 * groups * nstate]

    dt = _softplus(dt_ref[0, rows, :] + dtb_ref[...])
    a = -jnp.exp(alog_ref[...]) * dt
    a_cum = _dot_exact_lhs(tril_ref[...], a)
    dt_e = _dot_exact_rhs(dt, expand)
    acum_e = _dot_exact_rhs(a_cum, expand)
    alast_e = acum_e[q - 1:q, :]
    xdt = xs * dt_e
    xdec = xdt * jnp.exp(alast_e - acum_e)
    a_cum_t = a_cum.T

    y_parts = []
    for g in range(groups):
        bg = bmat[:, g * nstate:(g + 1) * nstate]
        cg = cmat[:, g * nstate:(g + 1) * nstate]
        scores = lax.dot_general(cg.astype(BF16), bg.astype(BF16), (((1,), (1,)), ((), ())),
                                 preferred_element_type=F32)
        xg = xdt[:, g * gw:(g + 1) * gw]
        yg = _bdot(cg, st_ref[g]) * jnp.exp(acum_e[:, g * gw:(g + 1) * gw])
        for hh in range(hpg):
            h = g * hpg + hh
            col = jnp.broadcast_to(a_cum[:, h:h + 1], (q, q))
            rowv = jnp.broadcast_to(a_cum_t[h:h + 1, :], (q, q))
            decay = jnp.where(causal, jnp.exp(col - rowv), 0.0)
            xh = jnp.where((lane_w >= hh * hd) & (lane_w < (hh + 1) * hd), xg, 0.0)
            yg = yg + _bdot(scores * decay, xh)
        y_parts.append(yg)
        upd = _bdot(bg.T, xdec[:, g * gw:(g + 1) * gw])
        st_ref[g] = st_ref[g] * jnp.exp(alast_e[:, g * gw:(g + 1) * gw]) + upd
    y = jnp.concatenate(y_parts, axis=-1) + dsk_ref[...] * xs
    y_ref[0, rows, :] = _rms(y * _silu(gate_ref[0, rows, :]), nw_ref[...]).astype(y_ref.dtype)
    return xbc[q - sub:q, :]


def _mamba2(z3d, p, w, cols, q, tb, after):
    bsz, seq, _ = z3d.shape
    heads = w // MB_HEADDIM
    lanes = V7X_LANES
    cg, cx, cbc, cdt = cols
    blk = lambda width, col: pl.BlockSpec((1, tb, width), lambda b, c: (b, c, col // width))
    full = lambda a: pl.BlockSpec(a.shape, lambda b, c: (0,) * a.ndim)
    consts = [p["conv_w"], p["conv_b"], p["dt_bias"], p["a_log"], p["d"], p["norm_w"],
              p["expand"], p["tril"]]
    return pl.pallas_call(
        functools.partial(_mb_kernel, q=q, w=w, heads=heads, groups=MB_GROUPS, nstate=MB_STATE),
        grid=(bsz, seq // tb),
        in_specs=[blk(w, cg), blk(w, cx), blk(w, cbc), blk(lanes, cdt)] + [full(a) for a in consts]
        + [pl.BlockSpec((1, V7X_SUBLANES, lanes), lambda b, c: (0, 0, 0))],
        out_specs=pl.BlockSpec((1, tb, w), lambda b, c: (b, c, 0)),
        out_shape=jax.ShapeDtypeStruct((bsz, seq, w), BF16),
        scratch_shapes=[pltpu.VMEM((V7X_SUBLANES, 2 * w), F32),
                        pltpu.VMEM((MB_GROUPS, MB_STATE, w // MB_GROUPS), F32)],
        compiler_params=_cparams("parallel", "arbitrary"),
        name="mamba2",
    )(z3d, z3d, z3d, z3d, *consts, after)


def _pad_rows(a, rows, at):
    out = jnp.zeros((rows, a.shape[1]), a.dtype)
    return lax.dynamic_update_slice(out, a, (at, 0))


def _block_ones(w, head):
    idx = jnp.arange(w) // head
    return (idx[:, None] == idx[None, :]).astype(BF16)


def _pick_tile(n, target):
    t = min(n, target)
    while n % t:
        t //= 2
    return t


def kernel(x, norm_mix_w, w_in, w_branch, w_out, norm_ffn_w, w_ffn_in, w_ffn_out, norm_final_w, hgrn_lower_bounds, hgrn_norm_w, rwkv_mu, rwkv_w0, rwkv_w2, rwkv_a0, rwkv_a2, rwkv_g2, rwkv_k_k, rwkv_k_a, rwkv_r_k, rwkv_ln_w, rwkv_ln_b, rwkv_v0, rwkv_v1, rwkv_v2, s5_a_re, s5_a_im, s5_b_re, s5_b_im, s5_c_re, s5_c_im, s5_d, s5_log_dt, s5_w_glu, s5_b_glu, mamba_conv_w, mamba_conv_b, mamba_dt_bias, mamba_a_log, mamba_d, mamba_norm_w):
    bsz, seq, d = x.shape
    depth = w_in.shape[0]
    w = d // 2
    lanes = V7X_LANES
    t = bsz * seq
    mb_heads = w // MB_HEADDIM
    mb_bc = 2 * MB_GROUPS * MB_STATE
    lw = RW_DECAY_LORA + RW_A_LORA + RW_G_LORA
    assert mb_bc == w and 3 * w % lw == 0

    o_gate = 0
    o_hg = o_gate + N_BRANCH * d
    o_rw = o_hg + 4 * w
    o_rwl = o_rw + 3 * w
    o_s5 = o_rwl + lw
    o_mbg = o_s5 + w
    o_mbx = o_mbg + w
    o_mbbc = o_mbx + w
    o_mbdt = o_mbbc + mb_bc
    c_rw = 0
    c_hg = c_rw + 3 * w
    c_s5 = c_hg + 4 * w
    c_mbg = c_s5 + w
    c_mbx = c_mbg + w
    c_mbbc = c_mbx + w
    c_rwl = c_mbbc + mb_bc
    c_mbdt = c_rwl + lw
    n_cols = c_mbdt + lanes
    tn = 512
    n_pad = -(-n_cols // tn) * tn

    def mixer_cols(wl):
        sl = lambda a, b: wl[:, a:b]
        pieces = [sl(o_rw, o_rwl), sl(o_hg, o_rw), sl(o_s5, o_mbdt), sl(o_rwl, o_s5),
                  sl(o_mbdt, o_mbdt + mb_heads),
                  jnp.zeros((d, n_pad - c_mbdt - mb_heads), wl.dtype)]
        return jnp.concatenate(pieces, axis=1).astype(BF16)

    tm = _pick_tile(seq, 256)
    tm_in = _pick_tile(seq, 512)
    tb_hg = _pick_tile(seq, 512)
    c_hg_chunk = min(HG_CHUNK, tb_hg)
    tc_rw = _pick_tile(seq, 64)
    tc_s5 = _pick_tile(seq, 128)
    q_mb = min(MB_CHUNK, seq)
    tb_mb = _pick_tile(seq, 4 * q_mb)
    ff = w_ffn_out.shape[1]
    tn_ff = 256 if ff % 256 == 0 else lanes

    lower_bounds = _hg_bounds(hgrn_lower_bounds)
    ones_rw = _block_ones(w, RW_HEAD)
    eye_g = jnp.eye(lanes // S5_GROUP, dtype=F32)
    expand = (jnp.arange(lanes)[:, None] == (jnp.arange(w) // MB_HEADDIM)[None, :]).astype(BF16)
    tril = (jnp.arange(q_mb)[:, None] >= jnp.arange(q_mb)[None, :]).astype(BF16)

    x2d = x.reshape(t, d)
    v_first = jnp.zeros((t, w), F32)
    for l in range(depth):
        x3d = x2d.reshape(bsz, seq, d)
        z2d = _inproj(x2d, norm_mix_w[l], mixer_cols(w_in[l]), tm_in, tn, False, x3d)
        z3d = z2d.reshape(bsz, seq, n_pad)

        mu = rwkv_mu[l]
        has_vres = l > 0
        lv = max(l - 1, 0)
        rp = {
            "mu_main": mu[None, :3 * w], "mu_lora": mu[None, 3 * w:],
            "w0": rwkv_w0[l][None], "a0": rwkv_a0[l][None],
            "w2p": _pad_rows(rwkv_w2[l], lw, 0).astype(BF16),
            "a2p": _pad_rows(rwkv_a2[l], lw, RW_DECAY_LORA).astype(BF16),
            "g2p": _pad_rows(rwkv_g2[l], lw, RW_DECAY_LORA + RW_A_LORA).astype(BF16),
            "k_k": rwkv_k_k[l][None], "k_a": rwkv_k_a[l][None], "r_k": rwkv_r_k[l].reshape(1, w),
            "v0": rwkv_v0[lv][None],
            "v1p": jnp.pad(rwkv_v1[lv], ((0, 0), (0, lanes - rwkv_v1.shape[2]))).astype(BF16),
            "v2p": _pad_rows(rwkv_v2[lv], lanes, 0).astype(BF16),
            "ones": ones_rw, "ln_w": rwkv_ln_w[l], "ln_b": rwkv_ln_b[l],
        }
        rw_x, v_l, rw_g, rw_bonus = _rwkv7_front(z2d, v_first, rp, bsz, seq, w, c_rw, c_rwl, lw, tm,
                                                 has_vres)
        if l == 0:
            v_first = v_l

        lam_re, lam_im, bb_re, bb_im = _s5_discretize(s5_a_re[l], s5_a_im[l], s5_log_dt[l],
                                                      s5_b_re[l], s5_b_im[l])
        ns = lam_re.size
        nsg = w // lanes
        gps = lanes // S5_GROUP
        blk_in = lambda bb: jnp.einsum(
            "sgnc,gh->sgchn", bb.reshape(nsg, gps, S5_STATE, S5_GROUP), eye_g
        ).reshape(nsg, lanes, gps * S5_STATE)
        blk_out = lambda cc: jnp.einsum(
            "sgcn,gh->sgnhc", cc.reshape(nsg, gps, S5_GROUP, S5_STATE), eye_g
        ).reshape(nsg, gps * S5_STATE, lanes)
        sp = {
            "bblk": jnp.concatenate([blk_in(bb_re), blk_in(bb_im)], axis=2).astype(BF16),
            "cblk": jnp.concatenate([blk_out(s5_c_re[l]), -blk_out(s5_c_im[l])], axis=1).astype(BF16),
            "lam_re": lam_re.reshape(1, ns), "lam_im": lam_im.reshape(1, ns),
            "d": s5_d[l][None], "w_glu": s5_w_glu[l].astype(BF16), "b_glu": s5_b_glu[l][None],
        }
        y_s5 = _s5(z3d, c_s5, w, sp, tc_s5).reshape(t, w)

        mp = {
            "conv_w": mamba_conv_w[l], "conv_b": mamba_conv_b[l][None],
            "dt_bias": jnp.pad(mamba_dt_bias[l], (0, lanes - mb_heads))[None],
            "a_log": jnp.pad(mamba_a_log[l], (0, lanes - mb_heads))[None],
            "d": jnp.repeat(mamba_d[l], MB_HEADDIM)[None], "norm_w": mamba_norm_w[l][None],
            "expand": expand, "tril": tril,
        }
        y_hg = _hgrn2(z3d, lower_bounds[l], hgrn_norm_w[l], w, c_hg, tb_hg, c_hg_chunk,
                      HG_SUBBLOCK, z3d).reshape(t, w)
        rw_y = _rw_scan(rw_x, lanes // (bsz * (w // RW_HEAD)), tc_rw, (y_hg, y_s5))
        gates = _inproj(x2d, norm_mix_w[l], w_in[l][:, o_gate:o_hg].astype(BF16), tm_in, tn, True,
                        rw_y)
        y_mb = _mamba2(z3d, mp, w, (c_mbg, c_mbx, c_mbbc, c_mbdt), q_mb, tb_mb, rw_y).reshape(t, w)
        rw_heads = w // RW_HEAD
        rw_raw = _lanes_to_chains_v(rw_y, bsz, seq, rw_heads, RW_HEAD, lanes // (bsz * rw_heads))
        x2d = _merge(y_hg, y_s5, y_mb, (rw_raw, rw_g, rw_bonus, rp["ln_w"], rp["ln_b"], rp["ones"]),
                     gates, w_branch[l], w_out[l], x2d, tm)
        act = _ffn_in(x2d, norm_ffn_w[l], w_ffn_in[l], tm_in, tn_ff)
        x2d = _ffn_out(act, w_ffn_out[l], x2d, norm_final_w, l == depth - 1, tm)
    return x2d.reshape(bsz, seq, d)
```

```python
import functools
import math

import jax
import jax.numpy as jnp
from jax import lax
from jax.experimental import pallas as pl
from jax.experimental.pallas import tpu as pltpu

F32 = jnp.float32
BF16 = jnp.bfloat16

V7X_LANES = 128
V7X_SUBLANES = 8
V7X_VMEM_LIMIT_BYTES = 56 * 1024 * 1024

RMS_EPS = 1e-6
N_BRANCH = 4
HG_DK = 128
HG_TINY = 1e-30
HG_CHUNK = 128
HG_SUBBLOCK = 4
RW_HEAD = 64
RW_DECAY_LORA = 64
RW_A_LORA = 64
RW_G_LORA = 128
RW_LN_EPS = 64e-5
RW_NVEC = 6
S5_GROUP = 16
S5_STATE = 64
MB_HEADDIM = 64
MB_GROUPS = 2
MB_STATE = 128
MB_CONV = 4
MB_CHUNK = 128


def _cparams(*sem):
    return pltpu.CompilerParams(dimension_semantics=sem, vmem_limit_bytes=V7X_VMEM_LIMIT_BYTES)


def _bdot(a, b):
    return jnp.dot(a.astype(BF16), b.astype(BF16), preferred_element_type=F32)


def _split(x, parts):
    pieces, rest = [], x
    for i in range(parts):
        piece = rest.astype(BF16)
        pieces.append(piece)
        if i + 1 < parts:
            rest = rest - piece.astype(F32)
    return pieces


def _dot_exact_rhs(x, m, parts=3):
    return sum(jnp.dot(p, m, preferred_element_type=F32) for p in _split(x, parts))


def _dot_exact_lhs(m, x, parts=3):
    return sum(jnp.dot(m, p, preferred_element_type=F32) for p in _split(x, parts))


def _softplus(x):
    return jnp.maximum(x, 0.0) + jnp.log(1.0 + jnp.exp(-jnp.abs(x)))


def _silu(x):
    return x * jax.nn.sigmoid(x)


def _rms(x, w):
    return x * lax.rsqrt(jnp.mean(x * x, axis=-1, keepdims=True) + RMS_EPS) * w


def _resident(shape):
    return pl.BlockSpec(shape, lambda i: (0,) * len(shape), pipeline_mode=pl.Buffered(1))


def _inproj_kernel(x_ref, nw_ref, w_ref, after_ref, o_ref, *, gate, tn):
    u = _rms(x_ref[...], nw_ref[...]).astype(BF16)
    for n0 in range(0, o_ref.shape[1], tn):
        z = jnp.dot(u, w_ref[:, n0:n0 + tn], preferred_element_type=F32)
        o_ref[:, n0:n0 + tn] = jax.nn.sigmoid(z).astype(o_ref.dtype) if gate else z


def _inproj(x2d, norm_w, w_bf16, tm, tn, gate, after):
    t, d = x2d.shape
    n = w_bf16.shape[1]
    return pl.pallas_call(
        functools.partial(_inproj_kernel, gate=gate, tn=tn),
        grid=(t // tm,),
        in_specs=[pl.BlockSpec((tm, d), lambda i: (i, 0)),
                  pl.BlockSpec((1, d), lambda i: (0, 0)),
                  _resident((d, n)),
                  pl.BlockSpec((1, V7X_SUBLANES, V7X_LANES), lambda i: (0, 0, 0))],
        out_specs=pl.BlockSpec((tm, n), lambda i: (i, 0)),
        out_shape=jax.ShapeDtypeStruct((t, n), BF16 if gate else F32),
        compiler_params=_cparams("parallel"),
        name="inproj_gate" if gate else "inproj",
    )(x2d, norm_w.reshape(1, d), w_bf16, after)


def _ffn_in_kernel(x_ref, nw_ref, w_ref, o_ref, *, tn):
    ff = o_ref.shape[1]
    u = _rms(x_ref[...], nw_ref[...]).astype(BF16)
    for n0 in range(0, ff, tn):
        gate = jnp.dot(u, w_ref[:, n0:n0 + tn], preferred_element_type=F32)
        up = jnp.dot(u, w_ref[:, ff + n0:ff + n0 + tn], preferred_element_type=F32)
        o_ref[:, n0:n0 + tn] = (_silu(gate) * up).astype(BF16)


def _ffn_in(x2d, norm_w, w_bf16, tm, tn):
    t, d = x2d.shape
    ff = w_bf16.shape[1] // 2
    return pl.pallas_call(
        functools.partial(_ffn_in_kernel, tn=tn),
        grid=(t // tm,),
        in_specs=[pl.BlockSpec((tm, d), lambda i: (i, 0)),
                  pl.BlockSpec((1, d), lambda i: (0, 0)),
                  _resident((d, 2 * ff))],
        out_specs=pl.BlockSpec((tm, ff), lambda i: (i, 0)),
        out_shape=jax.ShapeDtypeStruct((t, ff), BF16),
        compiler_params=_cparams("parallel"),
        name="ffn_in",
    )(x2d, norm_w.reshape(1, d), w_bf16)


def _ffn_out_kernel(a_ref, w_ref, x_ref, fw_ref, o_ref, *, final_norm):
    y = x_ref[...] + jnp.dot(a_ref[...], w_ref[...], preferred_element_type=F32)
    if final_norm:
        y = _rms(y, fw_ref[...])
    o_ref[...] = y


def _ffn_out(act, w_bf16, x2d, final_w, final_norm, tm):
    t, d = x2d.shape
    ff = act.shape[1]
    return pl.pallas_call(
        functools.partial(_ffn_out_kernel, final_norm=final_norm),
        grid=(t // tm,),
        in_specs=[pl.BlockSpec((tm, ff), lambda i: (i, 0)),
                  _resident((ff, d)),
                  pl.BlockSpec((tm, d), lambda i: (i, 0)),
                  pl.BlockSpec((1, d), lambda i: (0, 0))],
        out_specs=pl.BlockSpec((tm, d), lambda i: (i, 0)),
        out_shape=jax.ShapeDtypeStruct((t, d), F32),
        compiler_params=_cparams("parallel"),
        name="ffn_out",
    )(act, w_bf16, x2d, final_w.reshape(1, d))


def _merge_kernel(yh_ref, ys_ref, ym_ref, yr_ref, rg_ref, rb_ref, lnw_ref, lnb_ref, ones_ref,
                  zg_ref, wb_ref, wo_ref, x_ref, o_ref):
    d = x_ref.shape[1]
    y_rw = _rw_finish(yr_ref[...], rg_ref[...], rb_ref[...], lnw_ref[...], lnb_ref[...],
                      ones_ref[...])
    acc = jnp.zeros(x_ref.shape, F32)
    for k, y in enumerate((yh_ref[...], y_rw, ys_ref[...], ym_ref[...])):
        proj = jnp.dot(y.astype(BF16), wb_ref[k], preferred_element_type=F32)
        acc = acc + zg_ref[:, k * d:(k + 1) * d].astype(F32) * proj
    o_ref[...] = x_ref[...] + jnp.dot(acc.astype(BF16), wo_ref[...], preferred_element_type=F32)


def _merge(y_hg, y_s5, y_mb, rw, gates, wb_bf16, wo_bf16, x2d, tm):
    t, d = x2d.shape
    w = y_hg.shape[1]
    yspec = pl.BlockSpec((tm, w), lambda i: (i, 0))
    row = pl.BlockSpec((1, w), lambda i: (0, 0))
    y_raw, g, bonus, ln_w, ln_b, ones = rw
    return pl.pallas_call(
        _merge_kernel,
        grid=(t // tm,),
        in_specs=[yspec, yspec, yspec, yspec, yspec, yspec, row, row, _resident((w, w)),
                  pl.BlockSpec((tm, N_BRANCH * d), lambda i: (i, 0)),
                  _resident((N_BRANCH, w, d)),
                  _resident((d, d)),
                  pl.BlockSpec((tm, d), lambda i: (i, 0))],
        out_specs=pl.BlockSpec((tm, d), lambda i: (i, 0)),
        out_shape=jax.ShapeDtypeStruct((t, d), F32),
        compiler_params=_cparams("parallel"),
        name="merge",
    )(y_hg, y_s5, y_mb, y_raw, g, bonus, ln_w.reshape(1, w), ln_b.reshape(1, w), ones,
      gates, wb_bf16, wo_bf16, x2d)


def _hg_bounds_kernel(h_ref, o_ref):
    h = h_ref[...]
    depth = h.shape[0]
    m = jnp.max(h, axis=0, keepdims=True)
    e = jnp.exp(h - m)
    p = e / jnp.sum(e, axis=0, keepdims=True)
    run = jnp.zeros_like(p[0:1])
    rows = []
    for l in range(depth):
        run = run + p[l:l + 1]
        rows.append(run - p[0:1])
    o_ref[...] = jnp.concatenate(rows, axis=0)


def _hg_bounds(hgrn_lower_bounds):
    return pl.pallas_call(
        _hg_bounds_kernel,
        out_shape=jax.ShapeDtypeStruct(hgrn_lower_bounds.shape, F32),
        name="hg_bounds",
    )(hgrn_lower_bounds)


def _hg_chunk_kernel(q_ref, f_ref, i_ref, g_ref, lb_ref, nw_ref, tril_ref, after_ref, y_ref,
                     st_ref, *, tb, c, cs, heads, dk):
    sub = V7X_SUBLANES
    nt = (((1,), (1,)), ((), ()))

    @pl.when(pl.program_id(1) == 0)
    def _():
        st_ref[...] = jnp.zeros(st_ref.shape, F32)

    def shift_rows(x, dlt):
        return pltpu.roll(x.reshape(c // sub, sub, dk), dlt, axis=1).reshape(c, dk)

    lb = lb_ref[...]
    tril = tril_ref[...]
    rid = lax.broadcasted_iota(jnp.int32, (c, dk), 0)
    rid1 = lax.broadcasted_iota(jnp.int32, (c, 1), 0)
    pr = lax.broadcasted_iota(jnp.int32, (c, c), 0)
    pc = lax.broadcasted_iota(jnp.int32, (c, c), 1)

    def chunk(ci):
        rows = pl.ds(ci * c, c)
        ff = f_ref[0, rows, :]
        q = _silu(q_ref[0, rows, :])
        dec = jnp.maximum(lb + (1.0 - lb) * jax.nn.sigmoid(ff), HG_TINY)
        k = (1.0 - lb) * jax.nn.sigmoid(-ff)
        v = i_ref[0, rows, :]
        b = _dot_exact_lhs(tril, jnp.log(dec))
        outs = []
        for h in range(heads):
            hs = slice(h * dk, (h + 1) * dk)
            qh, kh, bh, vh = q[:, hs], k[:, hs], b[:, hs], v[:, hs]
            blast = bh[c - 1:c, :]
            st = st_ref[h]
            o = lax.dot_general((qh * jnp.exp(bh)).astype(BF16), st.astype(BF16), nt,
                                preferred_element_type=F32)
            att = jnp.zeros((c, c), F32)
            grp = 2 * cs
            while grp <= c:
                half = grp // 2
                qparts, kparts = [], []
                for r0 in range(0, c, grp):
                    d = bh[r0:r0 + grp, :] - bh[r0 + half - 1:r0 + half, :]
                    if half % sub == 0:
                        zero = jnp.zeros((half, dk), F32)
                        kparts += [kh[r0:r0 + half, :] * jnp.exp(-d[0:half, :]), zero]
                        qparts += [zero, qh[r0 + half:r0 + grp, :] * jnp.exp(d[half:grp, :])]
                    else:
                        upper = (rid[0:grp, :] % grp) >= half
                        e = jnp.exp(jnp.where(upper, d, -d))
                        kparts.append(jnp.where(upper, 0.0, kh[r0:r0 + grp, :] * e))
                        qparts.append(jnp.where(upper, qh[r0:r0 + grp, :] * e, 0.0))
                a = lax.dot_general(jnp.concatenate(qparts, axis=0).astype(BF16),
                                    jnp.concatenate(kparts, axis=0).astype(BF16), nt,
                                    preferred_element_type=F32)
                att = att + (a if grp == c else jnp.where(pr // grp == pc // grp, a, 0.0))
                grp *= 2
            o = o + _bdot(att, vh)
            for dlt in range(cs):
                if dlt == 0:
                    a = jnp.sum(qh * kh, axis=-1, keepdims=True)
                    o = o + a * vh
                else:
                    e = jnp.exp(jnp.minimum(bh - shift_rows(bh, dlt), 0.0))
                    a = jnp.sum(qh * shift_rows(kh, dlt) * e, axis=-1, keepdims=True)
                    a = jnp.where((rid1 % cs) >= dlt, a, 0.0)
                    o = o + a * shift_rows(vh, dlt)
            kd = kh * jnp.exp(blast - bh)
            st_ref[h] = st * jnp.exp(blast) + _bdot(vh.T, kd)
            outs.append(o * lax.rsqrt(jnp.mean(o * o, axis=-1, keepdims=True) + RMS_EPS))
        y = jnp.concatenate(outs, axis=-1) * nw_ref[...]
        y_ref[0, rows, :] = (y * _silu(g_ref[0, rows, :])).astype(y_ref.dtype)

    for ci in range(tb // c):
        chunk(ci)


def _hgrn2(z3d, lb, norm_w, w, col0, tb, c, cs, after):
    bsz, seq, _ = z3d.shape
    heads = w // HG_DK
    cb = col0 // w
    assert V7X_SUBLANES % cs == 0 and c % V7X_SUBLANES == 0
    tril = (jnp.arange(c)[:, None] >= jnp.arange(c)[None, :]).astype(BF16)
    zspec = lambda j: pl.BlockSpec((1, tb, w), lambda b, i: (b, i, cb + j))
    row = pl.BlockSpec((1, w), lambda b, i: (0, 0))
    return pl.pallas_call(
        functools.partial(_hg_chunk_kernel, tb=tb, c=c, cs=cs, heads=heads, dk=HG_DK),
        grid=(bsz, seq // tb),
        in_specs=[zspec(0), zspec(1), zspec(2), zspec(3), row, row,
                  pl.BlockSpec((c, c), lambda b, i: (0, 0)),
                  pl.BlockSpec((1, V7X_SUBLANES, V7X_LANES), lambda b, i: (0, 0, 0))],
        out_specs=pl.BlockSpec((1, tb, w), lambda b, i: (b, i, 0)),
        out_shape=jax.ShapeDtypeStruct((bsz, seq, w), BF16),
        scratch_shapes=[pltpu.VMEM((heads, HG_DK, HG_DK), F32)],
        compiler_params=_cparams("parallel", "arbitrary"),
        name="hgrn2",
    )(z3d, z3d, z3d, z3d, lb.reshape(1, w), norm_w.reshape(1, w), tril, after)


def _lanes_to_chains_v(a, bsz, seq, heads, dv, nq):
    a = a.reshape(seq, dv // nq, nq, bsz, heads).transpose(3, 0, 4, 2, 1)
    return a.reshape(bsz * seq, heads * dv)


def _shift_rows(cur, prev_last, first):
    rolled = pltpu.roll(cur, 1, axis=0)
    row0 = jnp.where(first, 0.0, prev_last)
    rid = lax.broadcasted_iota(jnp.int32, cur.shape, 0)
    return jnp.where(rid == 0, jnp.broadcast_to(row0, cur.shape), rolled)


def _rw_prep_kernel(zm_ref, zmp_ref, zl_ref, zlp_ref, vf_ref,
                    mum_ref, mul_ref, w0_ref, w2_ref, a0_ref, a2_ref, g2_ref,
                    kk_ref, ka_ref, rk_ref, v0_ref, v1_ref, v2_ref, ones_ref, cum_ref,
                    kvec_out, v_out, g_out, bonus_out,
                    *, w, tiles_per_seq, has_vres):
    first = (pl.program_id(0) % tiles_per_seq) == 0
    sub = V7X_SUBLANES
    zm = zm_ref[...]
    zl = zl_ref[...]
    zms = zm + (_shift_rows(zm, zmp_ref[sub - 1:sub, :], first) - zm) * mum_ref[...]
    zls = zl + (_shift_rows(zl, zlp_ref[sub - 1:sub, :], first) - zl) * mul_ref[...]
    r = zms[:, 0:w]
    k = zms[:, w:2 * w]
    v = zms[:, 2 * w:3 * w]
    w_log = -_softplus(-(w0_ref[...] + _bdot(jnp.tanh(zls), w2_ref[...]))) - 0.5
    log_decay = -jnp.exp(w_log)
    log_gam = _dot_exact_lhs(cum_ref[...], log_decay)
    gam = jnp.exp(log_gam)
    inv_gam = jnp.exp(-log_gam)
    gam_prev = jnp.exp(log_gam - log_decay)
    if has_vres:
        mix = jax.nn.sigmoid(v0_ref[...] + _bdot(_bdot(v, v1_ref[...]), v2_ref[...]))
        v = v + (vf_ref[...] - v) * mix
    a = jax.nn.sigmoid(a0_ref[...] + _bdot(zls, a2_ref[...]))
    g = _bdot(jax.nn.sigmoid(zls), g2_ref[...])
    ones = ones_ref[...]
    kk = k * kk_ref[...]
    ss = _dot_exact_rhs(kk * kk, ones)
    kk = kk / jnp.maximum(jnp.sqrt(ss), 1e-12)
    k2 = k * (1.0 + (a - 1.0) * ka_ref[...])
    vecs = (r * gam, gam, k2 * inv_gam, -kk * gam_prev, kk * a * inv_gam, v)
    for h in range(w // RW_HEAD):
        hs = slice(h * RW_HEAD, (h + 1) * RW_HEAD)
        row = jnp.concatenate([x[:, hs] for x in vecs], axis=1)
        for q in range(kvec_out.shape[0]):
            kvec_out[q, 0, h] = row
    v_out[...] = v
    g_out[...] = g.astype(g_out.dtype)
    bonus_out[...] = (_dot_exact_rhs(r * k2 * rk_ref[...], ones) * v).astype(bonus_out.dtype)


def _rw_prep(z2d, v_first, p, col_main, col_lora, w, lw, seq, tm, tc, has_vres):
    assert tm % tc == 0 and tc <= 128
    blk = jnp.arange(tm) // tc
    cum = ((jnp.arange(tm)[:, None] >= jnp.arange(tm)[None, :])
           & (blk[:, None] == blk[None, :])).astype(BF16)
    t = z2d.shape[0]
    sub = V7X_SUBLANES
    mb = col_main // (3 * w)
    lbk = col_lora // lw
    rows8 = tm // sub
    heads = w // RW_HEAD
    tps = seq // tm
    nq = V7X_LANES // (t // seq * heads)

    def prev_idx(i):
        return jnp.maximum(i * rows8 - 1, 0)

    row = lambda n: pl.BlockSpec((1, n), lambda i: (0, 0))
    full = lambda a: pl.BlockSpec(a.shape, lambda i: (0,) * a.ndim)
    ospec = pl.BlockSpec((tm, w), lambda i: (i, 0))
    oshape = jax.ShapeDtypeStruct((t, w), F32)
    args = [z2d, z2d, z2d, z2d, v_first,
            p["mu_main"], p["mu_lora"], p["w0"], p["w2p"], p["a0"], p["a2p"], p["g2p"],
            p["k_k"], p["k_a"], p["r_k"], p["v0"], p["v1p"], p["v2p"], p["ones"], cum]
    in_specs = [pl.BlockSpec((tm, 3 * w), lambda i: (i, mb)),
                pl.BlockSpec((sub, 3 * w), lambda i: (prev_idx(i), mb)),
                pl.BlockSpec((tm, lw), lambda i: (i, lbk)),
                pl.BlockSpec((sub, lw), lambda i: (prev_idx(i), lbk)),
                ospec,
                row(3 * w), row(lw), row(w), full(p["w2p"]), row(w), full(p["a2p"]), full(p["g2p"]),
                row(w), row(w), row(w), row(w), full(p["v1p"]), full(p["v2p"]), full(p["ones"]),
                full(cum)]
    return pl.pallas_call(
        functools.partial(_rw_prep_kernel, w=w, tiles_per_seq=seq // tm, has_vres=has_vres),
        grid=(t // tm,),
        in_specs=in_specs,
        out_specs=[pl.BlockSpec((nq, 1, heads, tm, RW_NVEC * RW_HEAD),
                                lambda i: (0, i // tps, 0, i % tps, 0)), ospec, ospec, ospec],
        out_shape=[jax.ShapeDtypeStruct((nq, t // seq, heads, seq, RW_NVEC * RW_HEAD), F32),
                   oshape, jax.ShapeDtypeStruct((t, w), BF16), jax.ShapeDtypeStruct((t, w), BF16)],
        compiler_params=_cparams("parallel"),
        name="rw_prep",
    )(*args)


def _rw_scan_kernel(x_ref, *rest, tc, dk, nvb, nq):
    y_ref, s_ref = rest[-2:]
    sub = V7X_SUBLANES
    lanes = V7X_LANES
    jr, jw, jk, ja, jb, jv = range(RW_NVEC)
    vl = nvb * sub
    qid = lax.broadcasted_iota(jnp.int32, (sub, lanes), 1) // (lanes // nq)

    @pl.when(pl.program_id(0) == 0)
    def _():
        s_ref[...] = jnp.zeros(s_ref.shape, F32)

    def bc(j, t, kk):
        return jnp.broadcast_to(x_ref[t, j, pl.ds(kk, 1), :], (sub, lanes))

    sa0 = [jnp.zeros((sub, lanes), F32) for _ in range(nvb)]
    for kk in range(dk):
        arow = bc(ja, 0, kk)
        for j in range(nvb):
            sa0[j] = sa0[j] + s_ref[kk, pl.ds(sub * j, sub), :] * arow

    def step(t, sa):
        tn = jnp.minimum(t + 1, tc - 1)
        vb = []
        for j in range(nvb):
            vj = x_ref[t, jv, pl.ds(sub * j, sub), :]
            for q in range(1, nq):
                vj = jnp.where(qid == q, x_ref[t, jv, pl.ds(q * vl + sub * j, sub), :], vj)
            vb.append(vj)
        yacc = [jnp.zeros((sub, lanes), F32) for _ in range(nvb)]
        sacc = [jnp.zeros((sub, lanes), F32) for _ in range(nvb)]
        for kk in range(dk):
            brow = bc(jb, t, kk)
            krow = bc(jk, t, kk)
            rrow = bc(jr, t, kk)
            anext = bc(ja, tn, kk)
            for j in range(nvb):
                s = s_ref[kk, pl.ds(sub * j, sub), :] + sa[j] * brow + vb[j] * krow
                s_ref[kk, pl.ds(sub * j, sub), :] = s
                yacc[j] = yacc[j] + s * rrow
                sacc[j] = sacc[j] + s * anext
        for j in range(nvb):
            y_ref[t, pl.ds(sub * j, sub), :] = yacc[j]
        return tuple(sacc)

    lax.fori_loop(0, tc, step, tuple(sa0))
    for kk in range(dk):
        gam = bc(jw, tc - 1, kk)
        for j in range(nvb):
            s_ref[kk, pl.ds(sub * j, sub), :] = s_ref[kk, pl.ds(sub * j, sub), :] * gam


def _rw_scan(x_l, nq, tc, after):
    s, nvec, dk, lanes = x_l.shape
    vl = dk // nq
    kspec = pl.BlockSpec((tc, nvec, dk, lanes), lambda i: (i, 0, 0, 0))
    vspec = pl.BlockSpec((tc, vl, lanes), lambda i: (i, 0, 0))
    order = [pl.BlockSpec((16, lanes), lambda i: (0, 0))] * len(after)
    return pl.pallas_call(
        functools.partial(_rw_scan_kernel, tc=tc, dk=dk, nvb=vl // V7X_SUBLANES, nq=nq),
        grid=(s // tc,),
        in_specs=[kspec] + order,
        out_specs=vspec,
        out_shape=jax.ShapeDtypeStruct((s, vl, lanes), F32),
        scratch_shapes=[pltpu.VMEM((dk, vl, lanes), F32)],
        compiler_params=_cparams("arbitrary"),
        name="rw_scan",
    )(x_l, *after)


def _rw_finish(y, g, bonus, ln_w, ln_b, ones):
    inv_n = 1.0 / RW_HEAD
    mean = _dot_exact_rhs(y, ones, parts=2) * inv_n
    yc = y - mean
    var = _dot_exact_rhs(yc * yc, ones, parts=2) * inv_n
    y = yc * lax.rsqrt(var + RW_LN_EPS) * ln_w + ln_b
    return (y + bonus) * g


def _rwkv7_front(z2d, v_first, p, bsz, seq, w, col_main, col_lora, lw, tm, tc, has_vres):
    kvec, v, g, bonus = _rw_prep(z2d, v_first, p, col_main, col_lora, w, lw, seq, tm, tc, has_vres)
    lanes = kvec.shape[0] * kvec.shape[1] * kvec.shape[2]
    x_l = kvec.reshape(lanes, seq, RW_NVEC * RW_HEAD).transpose(1, 2, 0)
    return x_l.reshape(seq, RW_NVEC, RW_HEAD, lanes), v, g, bonus


def _s5_disc_kernel(are_ref, aim_ref, dt_ref, bre_ref, bim_ref, lre_ref, lim_ref, bbre_ref, bbim_ref):
    a_re = are_ref[...]
    a_im = aim_ref[...]
    dt = jnp.exp(dt_ref[...])
    mag = jnp.exp(dt * a_re)
    lam_re = mag * jnp.cos(dt * a_im)
    lam_im = mag * jnp.sin(dt * a_im)
    den = a_re * a_re + a_im * a_im
    coef_re = ((lam_re - 1.0) * a_re + lam_im * a_im) / den
    coef_im = (lam_im * a_re - (lam_re - 1.0) * a_im) / den
    b_re = bre_ref[...]
    b_im = bim_ref[...]
    lre_ref[...] = lam_re
    lim_ref[...] = lam_im
    bbre_ref[...] = coef_re * b_re - coef_im * b_im
    bbim_ref[...] = coef_re * b_im + coef_im * b_re


def _s5_discretize(a_re, a_im, log_dt, b_re, b_im):
    g, n, c = b_re.shape
    shp = (g, n * c)
    bc = lambda a: jnp.broadcast_to(a[..., None], (g, n, c)).reshape(shp)
    dtb = jnp.broadcast_to(log_dt[:, None], shp)
    o = jax.ShapeDtypeStruct(shp, F32)
    lre, lim, bbre, bbim = pl.pallas_call(
        _s5_disc_kernel, out_shape=[o, o, o, o], name="s5_disc",
    )(bc(a_re), bc(a_im), dtb, b_re.reshape(shp), b_im.reshape(shp))
    un = lambda a: a.reshape(g, n, c)
    return un(lre)[..., 0], un(lim)[..., 0], un(bbre), un(bbim)


def _s5_kernel(u_ref, bblk_ref, cblk_ref, lre_ref, lim_ref, d_ref, wg_ref, bg_ref, y_ref,
               h_ref, hr_ref, hi_ref, *, tc, bsz, ns, lane_chunk):
    @pl.when(pl.program_id(0) == 0)
    def _():
        hr_ref[...] = jnp.zeros(hr_ref.shape, F32)
        hi_ref[...] = jnp.zeros(hi_ref.shape, F32)

    w = u_ref.shape[2]
    u = jnp.swapaxes(u_ref[...], 0, 1).reshape(tc * bsz, w)
    nsg = bblk_ref.shape[0]
    lanes = V7X_LANES
    for sg in range(nsg):
        c0 = sg * lane_chunk
        drive = _bdot(u[:, sg * lanes:(sg + 1) * lanes], bblk_ref[sg])
        h_ref[:, c0:c0 + lane_chunk] = drive[:, 0:lane_chunk]
        h_ref[:, ns + c0:ns + c0 + lane_chunk] = drive[:, lane_chunk:2 * lane_chunk]
    for c0 in range(0, ns, lane_chunk):
        lr = jnp.broadcast_to(lre_ref[:, c0:c0 + lane_chunk], (bsz, lane_chunk))
        li = jnp.broadcast_to(lim_ref[:, c0:c0 + lane_chunk], (bsz, lane_chunk))

        def step(t, carry, c0=c0, lr=lr, li=li):
            hr, hi = carry
            rows = pl.ds(pl.multiple_of(t * bsz, bsz), bsz)
            nr = lr * hr - li * hi + h_ref[rows, c0:c0 + lane_chunk]
            ni = lr * hi + li * hr + h_ref[rows, ns + c0:ns + c0 + lane_chunk]
            h_ref[rows, c0:c0 + lane_chunk] = nr
            h_ref[rows, ns + c0:ns + c0 + lane_chunk] = ni
            return nr, ni

        hr, hi = lax.fori_loop(0, tc, step,
                               (hr_ref[:, c0:c0 + lane_chunk], hi_ref[:, c0:c0 + lane_chunk]))
        hr_ref[:, c0:c0 + lane_chunk] = hr
        hi_ref[:, c0:c0 + lane_chunk] = hi
    outs = []
    for sg in range(nsg):
        c0 = sg * lane_chunk
        outs.append(_bdot(h_ref[:, c0:c0 + lane_chunk], cblk_ref[sg, 0:lane_chunk, :])
                    + _bdot(h_ref[:, ns + c0:ns + c0 + lane_chunk], cblk_ref[sg, lane_chunk:, :]))
    y = jnp.concatenate(outs, axis=-1) + d_ref[...] * u
    y = jax.nn.gelu(y)
    y = y * jax.nn.sigmoid(_bdot(y, wg_ref[...]) + bg_ref[...])
    y_ref[...] = jnp.swapaxes(y.reshape(tc, bsz, w), 0, 1).astype(y_ref.dtype)


def _s5(z3d, col0, w, p, tc):
    bsz, seq, _ = z3d.shape
    ns = p["lam_re"].shape[1]
    blk = tc * bsz
    full = lambda a: pl.BlockSpec(a.shape, lambda i: (0,) * a.ndim)
    return pl.pallas_call(
        functools.partial(_s5_kernel, tc=tc, bsz=bsz, ns=ns, lane_chunk=ns // p["bblk"].shape[0]),
        grid=(seq // tc,),
        in_specs=[pl.BlockSpec((bsz, tc, w), lambda i: (0, i, col0 // w)),
                  full(p["bblk"]), full(p["cblk"]), full(p["lam_re"]), full(p["lam_im"]),
                  full(p["d"]), full(p["w_glu"]), full(p["b_glu"])],
        out_specs=pl.BlockSpec((bsz, tc, w), lambda i: (0, i, 0)),
        out_shape=jax.ShapeDtypeStruct((bsz, seq, w), BF16),
        scratch_shapes=[pltpu.VMEM((blk, 2 * ns), F32),
                        pltpu.VMEM((bsz, ns), F32), pltpu.VMEM((bsz, ns), F32)],
        compiler_params=_cparams("arbitrary"),
        name="s5",
    )(z3d, p["bblk"], p["cblk"], p["lam_re"], p["lam_im"], p["d"], p["w_glu"], p["b_glu"])


def _mb_kernel(gate_ref, x_ref, bc_ref, dt_ref, cw_ref, cb_ref, dtb_ref, alog_ref, dsk_ref, nw_ref,
               expand_ref, tril_ref, after_ref, y_ref, prev_ref, st_ref,
               *, q, w, heads, groups, nstate):
    hd = w // heads
    gw = w // groups
    hpg = heads // groups

    @pl.when(pl.program_id(1) == 0)
    def _():
        prev_ref[...] = jnp.zeros(prev_ref.shape, F32)
        st_ref[...] = jnp.zeros(st_ref.shape, F32)

    rid = lax.broadcasted_iota(jnp.int32, (q, q), 0)
    cid = lax.broadcasted_iota(jnp.int32, (q, q), 1)
    causal = rid >= cid
    lane_w = lax.broadcasted_iota(jnp.int32, (q, gw), 1)
    expand = expand_ref[...]
    prev = prev_ref[...]
    for ci in range(x_ref.shape[1] // q):
        rows = pl.ds(ci * q, q)
        prev = _mb_chunk(rows, prev, causal, lane_w, expand, gate_ref, x_ref, bc_ref, dt_ref, cw_ref,
                         cb_ref, dtb_ref, alog_ref, dsk_ref, nw_ref, tril_ref, y_ref, st_ref,
                         q=q, w=w, groups=groups, nstate=nstate, gw=gw, hpg=hpg, hd=hd)
    prev_ref[...] = prev


def _mb_chunk(rows, prev, causal, lane_w, expand, gate_ref, x_ref, bc_ref, dt_ref, cw_ref, cb_ref,
              dtb_ref, alog_ref, dsk_ref, nw_ref, tril_ref, y_ref, st_ref,
              *, q, w, groups, nstate, gw, hpg, hd):
    sub = V7X_SUBLANES
    xbc = jnp.concatenate([x_ref[0, rows, :], bc_ref[0, rows, :]], axis=-1)
    full = jnp.concatenate([prev, xbc], axis=0)
    conv = jnp.broadcast_to(cb_ref[...], xbc.shape)
    for j in range(MB_CONV):
        shift = MB_CONV - 1 - j
        src = full if shift == 0 else pltpu.roll(full, shift, axis=0)
        conv = conv + src[sub:sub + q, :] * cw_ref[j:j + 1, :]
    act = _silu(conv)
    xs = act[:, 0:w]
    bmat = act[:, w:w + groups * nstate]
    cmat = act[:, w + groups * nstate:w + 2 * groups * nstate]

    dt = _softplus(dt_ref[0, rows, :] + dtb_ref[...])
    a = -jnp.exp(alog_ref[...]) * dt
    a_cum = _dot_exact_lhs(tril_ref[...], a)
    dt_e = _dot_exact_rhs(dt, expand)
    acum_e = _dot_exact_rhs(a_cum, expand)
    alast_e = acum_e[q - 1:q, :]
    xdt = xs * dt_e
    xdec = xdt * jnp.exp(alast_e - acum_e)
    a_cum_t = a_cum.T

    y_parts = []
    for g in range(groups):
        bg = bmat[:, g * nstate:(g + 1) * nstate]
        cg = cmat[:, g * nstate:(g + 1) * nstate]
        scores = lax.dot_general(cg.astype(BF16), bg.astype(BF16), (((1,), (1,)), ((), ())),
                                 preferred_element_type=F32)
        xg = xdt[:, g * gw:(g + 1) * gw]
        yg = _bdot(cg, st_ref[g]) * jnp.exp(acum_e[:, g * gw:(g + 1) * gw])
        for hh in range(hpg):
            h = g * hpg + hh
            col = jnp.broadcast_to(a_cum[:, h:h + 1], (q, q))
            rowv = jnp.broadcast_to(a_cum_t[h:h + 1, :], (q, q))
            decay = jnp.where(causal, jnp.exp(col - rowv), 0.0)
            xh = jnp.where((lane_w >= hh * hd) & (lane_w < (hh + 1) * hd), xg, 0.0)
            yg = yg + _bdot(scores * decay, xh)
        y_parts.append(yg)
        upd = _bdot(bg.T, xdec[:, g * gw:(g + 1) * gw])
        st_ref[g] = st_ref[g] * jnp.exp(alast_e[:, g * gw:(g + 1) * gw]) + upd
    y = jnp.concatenate(y_parts, axis=-1) + dsk_ref[...] * xs
    y_ref[0, rows, :] = _rms(y * _silu(gate_ref[0, rows, :]), nw_ref[...]).astype(y_ref.dtype)
    return xbc[q - sub:q, :]


def _mamba2(z3d, p, w, cols, q, tb, after):
    bsz, seq, _ = z3d.shape
    heads = w // MB_HEADDIM
    lanes = V7X_LANES
    cg, cx, cbc, cdt = cols
    blk = lambda width, col: pl.BlockSpec((1, tb, width), lambda b, c: (b, c, col // width))
    full = lambda a: pl.BlockSpec(a.shape, lambda b, c: (0,) * a.ndim)
    consts = [p["conv_w"], p["conv_b"], p["dt_bias"], p["a_log"], p["d"], p["norm_w"],
              p["expand"], p["tril"]]
    return pl.pallas_call(
        functools.partial(_mb_kernel, q=q, w=w, heads=heads, groups=MB_GROUPS, nstate=MB_STATE),
        grid=(bsz, seq // tb),
        in_specs=[blk(w, cg), blk(w, cx), blk(w, cbc), blk(lanes, cdt)] + [full(a) for a in consts]
        + [pl.BlockSpec((1, V7X_SUBLANES, lanes), lambda b, c: (0, 0, 0))],
        out_specs=pl.BlockSpec((1, tb, w), lambda b, c: (b, c, 0)),
        out_shape=jax.ShapeDtypeStruct((bsz, seq, w), BF16),
        scratch_shapes=[pltpu.VMEM((V7X_SUBLANES, 2 * w), F32),
                        pltpu.VMEM((MB_GROUPS, MB_STATE, w // MB_GROUPS), F32)],
        compiler_params=_cparams("parallel", "arbitrary"),
        name="mamba2",
    )(z3d, z3d, z3d, z3d, *consts, after)


def _pad_rows(a, rows, at):
    out = jnp.zeros((rows, a.shape[1]), a.dtype)
    return lax.dynamic_update_slice(out, a, (at, 0))


def _block_ones(w, head):
    idx = jnp.arange(w) // head
    return (idx[:, None] == idx[None, :]).astype(BF16)


def _pick_tile(n, target):
    t = min(n, target)
    while n % t:
        t //= 2
    return t


def kernel(x, norm_mix_w, w_in, w_branch, w_out, norm_ffn_w, w_ffn_in, w_ffn_out, norm_final_w, hgrn_lower_bounds, hgrn_norm_w, rwkv_mu, rwkv_w0, rwkv_w2, rwkv_a0, rwkv_a2, rwkv_g2, rwkv_k_k, rwkv_k_a, rwkv_r_k, rwkv_ln_w, rwkv_ln_b, rwkv_v0, rwkv_v1, rwkv_v2, s5_a_re, s5_a_im, s5_b_re, s5_b_im, s5_c_re, s5_c_im, s5_d, s5_log_dt, s5_w_glu, s5_b_glu, mamba_conv_w, mamba_conv_b, mamba_dt_bias, mamba_a_log, mamba_d, mamba_norm_w):
    bsz, seq, d = x.shape
    depth = w_in.shape[0]
    w = d // 2
    lanes = V7X_LANES
    t = bsz * seq
    mb_heads = w // MB_HEADDIM
    mb_bc = 2 * MB_GROUPS * MB_STATE
    lw = RW_DECAY_LORA + RW_A_LORA + RW_G_LORA
    assert mb_bc == w and 3 * w % lw == 0

    o_gate = 0
    o_hg = o_gate + N_BRANCH * d
    o_rw = o_hg + 4 * w
    o_rwl = o_rw + 3 * w
    o_s5 = o_rwl + lw
    o_mbg = o_s5 + w
    o_mbx = o_mbg + w
    o_mbbc = o_mbx + w
    o_mbdt = o_mbbc + mb_bc
    c_rw = 0
    c_hg = c_rw + 3 * w
    c_s5 = c_hg + 4 * w
    c_mbg = c_s5 + w
    c_mbx = c_mbg + w
    c_mbbc = c_mbx + w
    c_rwl = c_mbbc + mb_bc
    c_mbdt = c_rwl + lw
    n_cols = c_mbdt + lanes
    tn = 512
    n_pad = -(-n_cols // tn) * tn

    def mixer_cols(wl):
        sl = lambda a, b: wl[:, a:b]
        pieces = [sl(o_rw, o_rwl), sl(o_hg, o_rw), sl(o_s5, o_mbdt), sl(o_rwl, o_s5),
                  sl(o_mbdt, o_mbdt + mb_heads),
                  jnp.zeros((d, n_pad - c_mbdt - mb_heads), wl.dtype)]
        return jnp.concatenate(pieces, axis=1).astype(BF16)

    tm = _pick_tile(seq, 256)
    tm_in = _pick_tile(seq, 512)
    tb_hg = _pick_tile(seq, 512)
    c_hg_chunk = min(HG_CHUNK, tb_hg)
    tc_rw = _pick_tile(seq, 64)
    tc_s5 = _pick_tile(seq, 128)
    q_mb = min(MB_CHUNK, seq)
    tb_mb = _pick_tile(seq, 4 * q_mb)
    ff = w_ffn_out.shape[1]
    tn_ff = 256 if ff % 256 == 0 else lanes

    lower_bounds = _hg_bounds(hgrn_lower_bounds)
    ones_rw = _block_ones(w, RW_HEAD)
    eye_g = jnp.eye(lanes // S5_GROUP, dtype=F32)
    expand = (jnp.arange(lanes)[:, None] == (jnp.arange(w) // MB_HEADDIM)[None, :]).astype(BF16)
    tril = (jnp.arange(q_mb)[:, None] >= jnp.arange(q_mb)[None, :]).astype(BF16)

    x2d = x.reshape(t, d)
    v_first = jnp.zeros((t, w), F32)
    for l in range(depth):
        x3d = x2d.reshape(bsz, seq, d)
        z2d = _inproj(x2d, norm_mix_w[l], mixer_cols(w_in[l]), tm_in, tn, False, x3d)
        z3d = z2d.reshape(bsz, seq, n_pad)

        mu = rwkv_mu[l]
        has_vres = l > 0
        lv = max(l - 1, 0)
        rp = {
            "mu_main": mu[None, :3 * w], "mu_lora": mu[None, 3 * w:],
            "w0": rwkv_w0[l][None], "a0": rwkv_a0[l][None],
            "w2p": _pad_rows(rwkv_w2[l], lw, 0).astype(BF16),
            "a2p": _pad_rows(rwkv_a2[l], lw, RW_DECAY_LORA).astype(BF16),
            "g2p": _pad_rows(rwkv_g2[l], lw, RW_DECAY_LORA + RW_A_LORA).astype(BF16),
            "k_k": rwkv_k_k[l][None], "k_a": rwkv_k_a[l][None], "r_k": rwkv_r_k[l].reshape(1, w),
            "v0": rwkv_v0[lv][None],
            "v1p": jnp.pad(rwkv_v1[lv], ((0, 0), (0, lanes - rwkv_v1.shape[2]))).astype(BF16),
            "v2p": _pad_rows(rwkv_v2[lv], lanes, 0).astype(BF16),
            "ones": ones_rw, "ln_w": rwkv_ln_w[l], "ln_b": rwkv_ln_b[l],
        }
        rw_x, v_l, rw_g, rw_bonus = _rwkv7_front(z2d, v_first, rp, bsz, seq, w, c_rw, c_rwl, lw, tm,
                                                 tc_rw, has_vres)
        if l == 0:
            v_first = v_l

        lam_re, lam_im, bb_re, bb_im = _s5_discretize(s5_a_re[l], s5_a_im[l], s5_log_dt[l],
                                                      s5_b_re[l], s5_b_im[l])
        ns = lam_re.size
        nsg = w // lanes
        gps = lanes // S5_GROUP
        blk_in = lambda bb: jnp.einsum(
            "sgnc,gh->sgchn", bb.reshape(nsg, gps, S5_STATE, S5_GROUP), eye_g
        ).reshape(nsg, lanes, gps * S5_STATE)
        blk_out = lambda cc: jnp.einsum(
            "sgcn,gh->sgnhc", cc.reshape(nsg, gps, S5_GROUP, S5_STATE), eye_g
        ).reshape(nsg, gps * S5_STATE, lanes)
        sp = {
            "bblk": jnp.concatenate([blk_in(bb_re), blk_in(bb_im)], axis=2).astype(BF16),
            "cblk": jnp.concatenate([blk_out(s5_c_re[l]), -blk_out(s5_c_im[l])], axis=1).astype(BF16),
            "lam_re": lam_re.reshape(1, ns), "lam_im": lam_im.reshape(1, ns),
            "d": s5_d[l][None], "w_glu": s5_w_glu[l].astype(BF16), "b_glu": s5_b_glu[l][None],
        }
        y_s5 = _s5(z3d, c_s5, w, sp, tc_s5).reshape(t, w)

        mp = {
            "conv_w": mamba_conv_w[l], "conv_b": mamba_conv_b[l][None],
            "dt_bias": jnp.pad(mamba_dt_bias[l], (0, lanes - mb_heads))[None],
            "a_log": jnp.pad(mamba_a_log[l], (0, lanes - mb_heads))[None],
            "d": jnp.repeat(mamba_d[l], MB_HEADDIM)[None], "norm_w": mamba_norm_w[l][None],
            "expand": expand, "tril": tril,
        }
        y_hg = _hgrn2(z3d, lower_bounds[l], hgrn_norm_w[l], w, c_hg, tb_hg, c_hg_chunk,
                      HG_SUBBLOCK, z3d).reshape(t, w)
        rw_y = _rw_scan(rw_x, lanes // (bsz * (w // RW_HEAD)), tc_rw, (y_hg, y_s5))
        gates = _inproj(x2d, norm_mix_w[l], w_in[l][:, o_gate:o_hg].astype(BF16), tm_in, tn, True,
                        rw_y)
        y_mb = _mamba2(z3d, mp, w, (c_mbg, c_mbx, c_mbbc, c_mbdt), q_mb, tb_mb, rw_y).reshape(t, w)
        rw_heads = w // RW_HEAD
        rw_raw = _lanes_to_chains_v(rw_y, bsz, seq, rw_heads, RW_HEAD, lanes // (bsz * rw_heads))
        x2d = _merge(y_hg, y_s5, y_mb, (rw_raw, rw_g, rw_bonus, rp["ln_w"], rp["ln_b"], rp["ones"]),
                     gates, w_branch[l].astype(BF16), w_out[l].astype(BF16), x2d, tm)
        act = _ffn_in(x2d, norm_ffn_w[l], w_ffn_in[l].astype(BF16), tm_in, tn_ff)
        x2d = _ffn_out(act, w_ffn_out[l].astype(BF16), x2d, norm_final_w, l == depth - 1, tm)
    return x2d.reshape(bsz, seq, d)
```

```python
import functools

import jax
import jax.numpy as jnp
from jax import lax
from jax.experimental import pallas as pl
from jax.experimental.pallas import tpu as pltpu

F32 = jnp.float32
BF16 = jnp.bfloat16

V7X_LANES = 128
V7X_SUBLANES = 8
V7X_VMEM_LIMIT_BYTES = 56 * 1024 * 1024

RMS_EPS = 1e-6
N_BRANCH = 4
HG_DK = 128
HG_TINY = 1e-30
HG_CHUNK = 128
HG_SUBBLOCK = 4
RW_HEAD = 64
RW_DECAY_LORA = 64
RW_A_LORA = 64
RW_G_LORA = 128
RW_LN_EPS = 64e-5
RW_NVEC = 6
S5_GROUP = 16
S5_STATE = 64
MB_HEADDIM = 64
MB_GROUPS = 2
MB_STATE = 128
MB_CONV = 4
MB_CHUNK = 128


def _cparams(*sem):
    return pltpu.CompilerParams(dimension_semantics=sem, vmem_limit_bytes=V7X_VMEM_LIMIT_BYTES)


def _bdot(a, b):
    return jnp.dot(a.astype(BF16), b.astype(BF16), preferred_element_type=F32)


def _split(x, parts):
    pieces, rest = [], x
    for i in range(parts):
        piece = rest.astype(BF16)
        pieces.append(piece)
        if i + 1 < parts:
            rest = rest - piece.astype(F32)
    return pieces


def _dot_exact_rhs(x, m, parts=3):
    return sum(jnp.dot(p, m, preferred_element_type=F32) for p in _split(x, parts))


def _dot_exact_lhs(m, x, parts=3):
    return sum(jnp.dot(m, p, preferred_element_type=F32) for p in _split(x, parts))


def _softplus(x):
    return jnp.maximum(x, 0.0) + jnp.log(1.0 + jnp.exp(-jnp.abs(x)))


def _silu(x):
    return x * jax.nn.sigmoid(x)


def _rms(x, w):
    return x * lax.rsqrt(jnp.mean(x * x, axis=-1, keepdims=True) + RMS_EPS) * w


def _resident(shape):
    return pl.BlockSpec(shape, lambda i: (0,) * len(shape), pipeline_mode=pl.Buffered(1))


def _inproj_kernel(x_ref, nw_ref, w_ref, after_ref, o_ref, *, gate, tn):
    u = _rms(x_ref[...], nw_ref[...]).astype(BF16)
    for n0 in range(0, o_ref.shape[1], tn):
        z = jnp.dot(u, w_ref[:, n0:n0 + tn], preferred_element_type=F32)
        o_ref[:, n0:n0 + tn] = jax.nn.sigmoid(z).astype(o_ref.dtype) if gate else z


def _inproj(x2d, norm_w, w_bf16, tm, tn, gate, after):
    t, d = x2d.shape
    n = w_bf16.shape[1]
    return pl.pallas_call(
        functools.partial(_inproj_kernel, gate=gate, tn=tn),
        grid=(t // tm,),
        in_specs=[pl.BlockSpec((tm, d), lambda i: (i, 0)),
                  pl.BlockSpec((1, d), lambda i: (0, 0)),
                  _resident((d, n)),
                  pl.BlockSpec((1, V7X_SUBLANES, V7X_LANES), lambda i: (0, 0, 0))],
        out_specs=pl.BlockSpec((tm, n), lambda i: (i, 0)),
        out_shape=jax.ShapeDtypeStruct((t, n), BF16 if gate else F32),
        compiler_params=_cparams("parallel"),
        name="inproj_gate" if gate else "inproj",
    )(x2d, norm_w.reshape(1, d), w_bf16, after)


def _ffn_in_kernel(x_ref, nw_ref, w_ref, o_ref, *, tn):
    ff = o_ref.shape[1]
    u = _rms(x_ref[...], nw_ref[...]).astype(BF16)
    for n0 in range(0, ff, tn):
        gate = jnp.dot(u, w_ref[:, n0:n0 + tn], preferred_element_type=F32)
        up = jnp.dot(u, w_ref[:, ff + n0:ff + n0 + tn], preferred_element_type=F32)
        o_ref[:, n0:n0 + tn] = (_silu(gate) * up).astype(BF16)


def _ffn_in(x2d, norm_w, w_bf16, tm, tn):
    t, d = x2d.shape
    ff = w_bf16.shape[1] // 2
    return pl.pallas_call(
        functools.partial(_ffn_in_kernel, tn=tn),
        grid=(t // tm,),
        in_specs=[pl.BlockSpec((tm, d), lambda i: (i, 0)),
                  pl.BlockSpec((1, d), lambda i: (0, 0)),
                  _resident((d, 2 * ff))],
        out_specs=pl.BlockSpec((tm, ff), lambda i: (i, 0)),
        out_shape=jax.ShapeDtypeStruct((t, ff), BF16),
        compiler_params=_cparams("parallel"),
        name="ffn_in",
    )(x2d, norm_w.reshape(1, d), w_bf16)


def _ffn_out_kernel(a_ref, w_ref, x_ref, fw_ref, o_ref, *, final_norm):
    y = x_ref[...] + jnp.dot(a_ref[...], w_ref[...], preferred_element_type=F32)
    if final_norm:
        y = _rms(y, fw_ref[...])
    o_ref[...] = y


def _ffn_out(act, w_bf16, x2d, final_w, final_norm, tm):
    t, d = x2d.shape
    ff = act.shape[1]
    return pl.pallas_call(
        functools.partial(_ffn_out_kernel, final_norm=final_norm),
        grid=(t // tm,),
        in_specs=[pl.BlockSpec((tm, ff), lambda i: (i, 0)),
                  _resident((ff, d)),
                  pl.BlockSpec((tm, d), lambda i: (i, 0)),
                  pl.BlockSpec((1, d), lambda i: (0, 0))],
        out_specs=pl.BlockSpec((tm, d), lambda i: (i, 0)),
        out_shape=jax.ShapeDtypeStruct((t, d), F32),
        compiler_params=_cparams("parallel"),
        name="ffn_out",
    )(act, w_bf16, x2d, final_w.reshape(1, d))


def _merge_kernel(yh_ref, ys_ref, ym_ref, yr_ref, rg_ref, rb_ref, lnw_ref, lnb_ref, ones_ref,
                  zg_ref, wb_ref, wo_ref, x_ref, o_ref):
    d = x_ref.shape[1]
    y_rw = _rw_finish(yr_ref[...], rg_ref[...], rb_ref[...], lnw_ref[...], lnb_ref[...],
                      ones_ref[...])
    acc = jnp.zeros(x_ref.shape, F32)
    for k, y in enumerate((yh_ref[...], y_rw, ys_ref[...], ym_ref[...])):
        proj = jnp.dot(y.astype(BF16), wb_ref[k], preferred_element_type=F32)
        acc = acc + zg_ref[:, k * d:(k + 1) * d].astype(F32) * proj
    o_ref[...] = x_ref[...] + jnp.dot(acc.astype(BF16), wo_ref[...], preferred_element_type=F32)


def _merge(y_hg, y_s5, y_mb, rw, gates, wb_bf16, wo_bf16, x2d, tm):
    t, d = x2d.shape
    w = y_hg.shape[1]
    yspec = pl.BlockSpec((tm, w), lambda i: (i, 0))
    row = pl.BlockSpec((1, w), lambda i: (0, 0))
    y_raw, g, bonus, ln_w, ln_b, ones = rw
    return pl.pallas_call(
        _merge_kernel,
        grid=(t // tm,),
        in_specs=[yspec, yspec, yspec, yspec, yspec, yspec, row, row, _resident((w, w)),
                  pl.BlockSpec((tm, N_BRANCH * d), lambda i: (i, 0)),
                  _resident((N_BRANCH, w, d)),
                  _resident((d, d)),
                  pl.BlockSpec((tm, d), lambda i: (i, 0))],
        out_specs=pl.BlockSpec((tm, d), lambda i: (i, 0)),
        out_shape=jax.ShapeDtypeStruct((t, d), F32),
        compiler_params=_cparams("parallel"),
        name="merge",
    )(y_hg, y_s5, y_mb, y_raw, g, bonus, ln_w.reshape(1, w), ln_b.reshape(1, w), ones,
      gates, wb_bf16, wo_bf16, x2d)


def _hg_bounds_kernel(h_ref, o_ref):
    h = h_ref[...]
    depth = h.shape[0]
    m = jnp.max(h, axis=0, keepdims=True)
    e = jnp.exp(h - m)
    p = e / jnp.sum(e, axis=0, keepdims=True)
    run = jnp.zeros_like(p[0:1])
    rows = []
    for l in range(depth):
        run = run + p[l:l + 1]
        rows.append(run - p[0:1])
    o_ref[...] = jnp.concatenate(rows, axis=0)


def _hg_bounds(hgrn_lower_bounds):
    return pl.pallas_call(
        _hg_bounds_kernel,
        out_shape=jax.ShapeDtypeStruct(hgrn_lower_bounds.shape, F32),
        name="hg_bounds",
    )(hgrn_lower_bounds)


def _hg_chunk_kernel(q_ref, f_ref, i_ref, g_ref, lb_ref, nw_ref, tril_ref, after_ref, y_ref,
                     st_ref, *, tb, c, cs, heads, dk):
    sub = V7X_SUBLANES
    nt = (((1,), (1,)), ((), ()))

    @pl.when(pl.program_id(1) == 0)
    def _():
        st_ref[...] = jnp.zeros(st_ref.shape, F32)

    def shift_rows(x, dlt):
        return pltpu.roll(x.reshape(c // sub, sub, dk), dlt, axis=1).reshape(c, dk)

    lb = lb_ref[...]
    tril = tril_ref[...]
    rid = lax.broadcasted_iota(jnp.int32, (c, dk), 0)
    rid1 = lax.broadcasted_iota(jnp.int32, (c, 1), 0)
    pr = lax.broadcasted_iota(jnp.int32, (c, c), 0)
    pc = lax.broadcasted_iota(jnp.int32, (c, c), 1)

    def chunk(ci):
        rows = pl.ds(ci * c, c)
        ff = f_ref[0, rows, :]
        q = _silu(q_ref[0, rows, :])
        dec = jnp.maximum(lb + (1.0 - lb) * jax.nn.sigmoid(ff), HG_TINY)
        k = (1.0 - lb) * jax.nn.sigmoid(-ff)
        v = i_ref[0, rows, :]
        b = _dot_exact_lhs(tril, jnp.log(dec))
        outs = []
        for h in range(heads):
            hs = slice(h * dk, (h + 1) * dk)
            qh, kh, bh, vh = q[:, hs], k[:, hs], b[:, hs], v[:, hs]
            blast = bh[c - 1:c, :]
            st = st_ref[h]
            o = lax.dot_general((qh * jnp.exp(bh)).astype(BF16), st.astype(BF16), nt,
                                preferred_element_type=F32)
            att = jnp.zeros((c, c), F32)
            grp = 2 * cs
            while grp <= c:
                half = grp // 2
                qparts, kparts = [], []
                for r0 in range(0, c, grp):
                    d = bh[r0:r0 + grp, :] - bh[r0 + half - 1:r0 + half, :]
                    if half % sub == 0:
                        zero = jnp.zeros((half, dk), F32)
                        kparts += [kh[r0:r0 + half, :] * jnp.exp(-d[0:half, :]), zero]
                        qparts += [zero, qh[r0 + half:r0 + grp, :] * jnp.exp(d[half:grp, :])]
                    else:
                        upper = (rid[0:grp, :] % grp) >= half
                        e = jnp.exp(jnp.where(upper, d, -d))
                        kparts.append(jnp.where(upper, 0.0, kh[r0:r0 + grp, :] * e))
                        qparts.append(jnp.where(upper, qh[r0:r0 + grp, :] * e, 0.0))
                a = lax.dot_general(jnp.concatenate(qparts, axis=0).astype(BF16),
                                    jnp.concatenate(kparts, axis=0).astype(BF16), nt,
                                    preferred_element_type=F32)
                att = att + (a if grp == c else jnp.where(pr // grp == pc // grp, a, 0.0))
                grp *= 2
            o = o + _bdot(att, vh)
            for dlt in range(cs):
                if dlt == 0:
                    a = jnp.sum(qh * kh, axis=-1, keepdims=True)
                    o = o + a * vh
                else:
                    e = jnp.exp(jnp.minimum(bh - shift_rows(bh, dlt), 0.0))
                    a = jnp.sum(qh * shift_rows(kh, dlt) * e, axis=-1, keepdims=True)
                    a = jnp.where((rid1 % cs) >= dlt, a, 0.0)
                    o = o + a * shift_rows(vh, dlt)
            kd = kh * jnp.exp(blast - bh)
            st_ref[h] = st * jnp.exp(blast) + _bdot(vh.T, kd)
            outs.append(o * lax.rsqrt(jnp.mean(o * o, axis=-1, keepdims=True) + RMS_EPS))
        y = jnp.concatenate(outs, axis=-1) * nw_ref[...]
        y_ref[0, rows, :] = (y * _silu(g_ref[0, rows, :])).astype(y_ref.dtype)

    for ci in range(tb // c):
        chunk(ci)


def _hgrn2(z3d, lb, norm_w, w, col0, tb, c, cs, after):
    bsz, seq, _ = z3d.shape
    heads = w // HG_DK
    cb = col0 // w
    assert V7X_SUBLANES % cs == 0 and c % V7X_SUBLANES == 0
    tril = (jnp.arange(c)[:, None] >= jnp.arange(c)[None, :]).astype(BF16)
    zspec = lambda j: pl.BlockSpec((1, tb, w), lambda b, i: (b, i, cb + j))
    row = pl.BlockSpec((1, w), lambda b, i: (0, 0))
    return pl.pallas_call(
        functools.partial(_hg_chunk_kernel, tb=tb, c=c, cs=cs, heads=heads, dk=HG_DK),
        grid=(bsz, seq // tb),
        in_specs=[zspec(0), zspec(1), zspec(2), zspec(3), row, row,
                  pl.BlockSpec((c, c), lambda b, i: (0, 0)),
                  pl.BlockSpec((1, V7X_SUBLANES, V7X_LANES), lambda b, i: (0, 0, 0))],
        out_specs=pl.BlockSpec((1, tb, w), lambda b, i: (b, i, 0)),
        out_shape=jax.ShapeDtypeStruct((bsz, seq, w), BF16),
        scratch_shapes=[pltpu.VMEM((heads, HG_DK, HG_DK), F32)],
        compiler_params=_cparams("parallel", "arbitrary"),
        name="hgrn2",
    )(z3d, z3d, z3d, z3d, lb.reshape(1, w), norm_w.reshape(1, w), tril, after)


def _lanes_to_chains_v(a, bsz, seq, heads, dv, nq):
    a = a.reshape(seq, dv // nq, nq, bsz, heads).transpose(3, 0, 4, 2, 1)
    return a.reshape(bsz * seq, heads * dv)


def _shift_rows(cur, prev_last, first):
    rolled = pltpu.roll(cur, 1, axis=0)
    row0 = jnp.where(first, 0.0, prev_last)
    rid = lax.broadcasted_iota(jnp.int32, cur.shape, 0)
    return jnp.where(rid == 0, jnp.broadcast_to(row0, cur.shape), rolled)


def _rw_prep_kernel(zm_ref, zmp_ref, zl_ref, zlp_ref, vf_ref,
                    mum_ref, mul_ref, w0_ref, w2_ref, a0_ref, a2_ref, g2_ref,
                    kk_ref, ka_ref, rk_ref, v0_ref, v1_ref, v2_ref, ones_ref, cum_ref,
                    kvec_out, v_out, g_out, bonus_out,
                    *, w, tiles_per_seq, has_vres):
    first = (pl.program_id(0) % tiles_per_seq) == 0
    sub = V7X_SUBLANES
    zm = zm_ref[...]
    zl = zl_ref[...]
    zms = zm + (_shift_rows(zm, zmp_ref[sub - 1:sub, :], first) - zm) * mum_ref[...]
    zls = zl + (_shift_rows(zl, zlp_ref[sub - 1:sub, :], first) - zl) * mul_ref[...]
    r = zms[:, 0:w]
    k = zms[:, w:2 * w]
    v = zms[:, 2 * w:3 * w]
    w_log = -_softplus(-(w0_ref[...] + _bdot(jnp.tanh(zls), w2_ref[...]))) - 0.5
    log_decay = -jnp.exp(w_log)
    log_gam = _dot_exact_lhs(cum_ref[...], log_decay)
    gam = jnp.exp(log_gam)
    inv_gam = jnp.exp(-log_gam)
    gam_prev = jnp.exp(log_gam - log_decay)
    if has_vres:
        mix = jax.nn.sigmoid(v0_ref[...] + _bdot(_bdot(v, v1_ref[...]), v2_ref[...]))
        v = v + (vf_ref[...] - v) * mix
    a = jax.nn.sigmoid(a0_ref[...] + _bdot(zls, a2_ref[...]))
    g = _bdot(jax.nn.sigmoid(zls), g2_ref[...])
    ones = ones_ref[...]
    kk = k * kk_ref[...]
    ss = _dot_exact_rhs(kk * kk, ones)
    kk = kk / jnp.maximum(jnp.sqrt(ss), 1e-12)
    k2 = k * (1.0 + (a - 1.0) * ka_ref[...])
    vecs = (r * gam, gam, k2 * inv_gam, -kk * gam_prev, kk * a * inv_gam, v)
    for h in range(w // RW_HEAD):
        hs = slice(h * RW_HEAD, (h + 1) * RW_HEAD)
        row = jnp.concatenate([x[:, hs] for x in vecs], axis=1)
        for q in range(kvec_out.shape[0]):
            kvec_out[q, 0, h] = row
    v_out[...] = v
    g_out[...] = g.astype(g_out.dtype)
    bonus_out[...] = (_dot_exact_rhs(r * k2 * rk_ref[...], ones) * v).astype(bonus_out.dtype)


def _rw_prep(z2d, v_first, p, col_main, col_lora, w, lw, seq, tm, tc, has_vres):
    assert tm % tc == 0 and tc <= 128
    blk = jnp.arange(tm) // tc
    cum = ((jnp.arange(tm)[:, None] >= jnp.arange(tm)[None, :])
           & (blk[:, None] == blk[None, :])).astype(BF16)
    t = z2d.shape[0]
    sub = V7X_SUBLANES
    mb = col_main // (3 * w)
    lbk = col_lora // lw
    rows8 = tm // sub
    heads = w // RW_HEAD
    tps = seq // tm
    nq = V7X_LANES // (t // seq * heads)

    def prev_idx(i):
        return jnp.maximum(i * rows8 - 1, 0)

    row = lambda n: pl.BlockSpec((1, n), lambda i: (0, 0))
    full = lambda a: pl.BlockSpec(a.shape, lambda i: (0,) * a.ndim)
    ospec = pl.BlockSpec((tm, w), lambda i: (i, 0))
    oshape = jax.ShapeDtypeStruct((t, w), F32)
    args = [z2d, z2d, z2d, z2d, v_first,
            p["mu_main"], p["mu_lora"], p["w0"], p["w2p"], p["a0"], p["a2p"], p["g2p"],
            p["k_k"], p["k_a"], p["r_k"], p["v0"], p["v1p"], p["v2p"], p["ones"], cum]
    in_specs = [pl.BlockSpec((tm, 3 * w), lambda i: (i, mb)),
                pl.BlockSpec((sub, 3 * w), lambda i: (prev_idx(i), mb)),
                pl.BlockSpec((tm, lw), lambda i: (i, lbk)),
                pl.BlockSpec((sub, lw), lambda i: (prev_idx(i), lbk)),
                ospec,
                row(3 * w), row(lw), row(w), full(p["w2p"]), row(w), full(p["a2p"]), full(p["g2p"]),
                row(w), row(w), row(w), row(w), full(p["v1p"]), full(p["v2p"]), full(p["ones"]),
                full(cum)]
    return pl.pallas_call(
        functools.partial(_rw_prep_kernel, w=w, tiles_per_seq=seq // tm, has_vres=has_vres),
        grid=(t // tm,),
        in_specs=in_specs,
        out_specs=[pl.BlockSpec((nq, 1, heads, tm, RW_NVEC * RW_HEAD),
                                lambda i: (0, i // tps, 0, i % tps, 0)), ospec, ospec, ospec],
        out_shape=[jax.ShapeDtypeStruct((nq, t // seq, heads, seq, RW_NVEC * RW_HEAD), F32),
                   oshape, jax.ShapeDtypeStruct((t, w), BF16), jax.ShapeDtypeStruct((t, w), BF16)],
        compiler_params=_cparams("parallel"),
        name="rw_prep",
    )(*args)


def _rw_scan_kernel(x_ref, *rest, tc, dk, nvb, nq):
    y_ref, s_ref = rest[-2:]
    sub = V7X_SUBLANES
    lanes = V7X_LANES
    jr, jw, jk, ja, jb, jv = range(RW_NVEC)
    vl = nvb * sub
    qid = lax.broadcasted_iota(jnp.int32, (sub, lanes), 1) // (lanes // nq)

    @pl.when(pl.program_id(0) == 0)
    def _():
        s_ref[...] = jnp.zeros(s_ref.shape, F32)

    def bc(j, t, kk):
        return jnp.broadcast_to(x_ref[t, j, pl.ds(kk, 1), :], (sub, lanes))

    sa0 = [jnp.zeros((sub, lanes), F32) for _ in range(nvb)]
    for kk in range(dk):
        arow = bc(ja, 0, kk)
        for j in range(nvb):
            sa0[j] = sa0[j] + s_ref[kk, pl.ds(sub * j, sub), :] * arow

    def step(t, sa):
        tn = jnp.minimum(t + 1, tc - 1)
        vb = []
        for j in range(nvb):
            vj = x_ref[t, jv, pl.ds(sub * j, sub), :]
            for q in range(1, nq):
                vj = jnp.where(qid == q, x_ref[t, jv, pl.ds(q * vl + sub * j, sub), :], vj)
            vb.append(vj)
        yacc = [jnp.zeros((sub, lanes), F32) for _ in range(nvb)]
        sacc = [jnp.zeros((sub, lanes), F32) for _ in range(nvb)]
        for kk in range(dk):
            brow = bc(jb, t, kk)
            krow = bc(jk, t, kk)
            rrow = bc(jr, t, kk)
            anext = bc(ja, tn, kk)
            for j in range(nvb):
                s = s_ref[kk, pl.ds(sub * j, sub), :] + sa[j] * brow + vb[j] * krow
                s_ref[kk, pl.ds(sub * j, sub), :] = s
                yacc[j] = yacc[j] + s * rrow
                sacc[j] = sacc[j] + s * anext
        for j in range(nvb):
            y_ref[t, pl.ds(sub * j, sub), :] = yacc[j]
        return tuple(sacc)

    lax.fori_loop(0, tc, step, tuple(sa0))
    for kk in range(dk):
        gam = bc(jw, tc - 1, kk)
        for j in range(nvb):
            s_ref[kk, pl.ds(sub * j, sub), :] = s_ref[kk, pl.ds(sub * j, sub), :] * gam


def _rw_scan(x_l, nq, tc, after):
    s, nvec, dk, lanes = x_l.shape
    vl = dk // nq
    kspec = pl.BlockSpec((tc, nvec, dk, lanes), lambda i: (i, 0, 0, 0))
    vspec = pl.BlockSpec((tc, vl, lanes), lambda i: (i, 0, 0))
    order = [pl.BlockSpec((16, lanes), lambda i: (0, 0))] * len(after)
    return pl.pallas_call(
        functools.partial(_rw_scan_kernel, tc=tc, dk=dk, nvb=vl // V7X_SUBLANES, nq=nq),
        grid=(s // tc,),
        in_specs=[kspec] + order,
        out_specs=vspec,
        out_shape=jax.ShapeDtypeStruct((s, vl, lanes), F32),
        scratch_shapes=[pltpu.VMEM((dk, vl, lanes), F32)],
        compiler_params=_cparams("arbitrary"),
        name="rw_scan",
    )(x_l, *after)


def _rw_finish(y, g, bonus, ln_w, ln_b, ones):
    inv_n = 1.0 / RW_HEAD
    mean = _dot_exact_rhs(y, ones, parts=2) * inv_n
    yc = y - mean
    var = _dot_exact_rhs(yc * yc, ones, parts=2) * inv_n
    y = yc * lax.rsqrt(var + RW_LN_EPS) * ln_w + ln_b
    return (y + bonus) * g


def _rwkv7_front(z2d, v_first, p, bsz, seq, w, col_main, col_lora, lw, tm, tc, has_vres):
    kvec, v, g, bonus = _rw_prep(z2d, v_first, p, col_main, col_lora, w, lw, seq, tm, tc, has_vres)
    lanes = kvec.shape[0] * kvec.shape[1] * kvec.shape[2]
    x_l = kvec.reshape(lanes, seq, RW_NVEC * RW_HEAD).transpose(1, 2, 0)
    return x_l.reshape(seq, RW_NVEC, RW_HEAD, lanes), v, g, bonus


def _s5_disc_kernel(are_ref, aim_ref, dt_ref, bre_ref, bim_ref, lre_ref, lim_ref, bbre_ref, bbim_ref):
    a_re = are_ref[...]
    a_im = aim_ref[...]
    dt = jnp.exp(dt_ref[...])
    mag = jnp.exp(dt * a_re)
    lam_re = mag * jnp.cos(dt * a_im)
    lam_im = mag * jnp.sin(dt * a_im)
    den = a_re * a_re + a_im * a_im
    coef_re = ((lam_re - 1.0) * a_re + lam_im * a_im) / den
    coef_im = (lam_im * a_re - (lam_re - 1.0) * a_im) / den
    b_re = bre_ref[...]
    b_im = bim_ref[...]
    lre_ref[...] = lam_re
    lim_ref[...] = lam_im
    bbre_ref[...] = coef_re * b_re - coef_im * b_im
    bbim_ref[...] = coef_re * b_im + coef_im * b_re


def _s5_discretize(a_re, a_im, log_dt, b_re, b_im):
    g, n, c = b_re.shape
    shp = (g, n * c)
    bc = lambda a: jnp.broadcast_to(a[..., None], (g, n, c)).reshape(shp)
    dtb = jnp.broadcast_to(log_dt[:, None], shp)
    o = jax.ShapeDtypeStruct(shp, F32)
    lre, lim, bbre, bbim = pl.pallas_call(
        _s5_disc_kernel, out_shape=[o, o, o, o], name="s5_disc",
    )(bc(a_re), bc(a_im), dtb, b_re.reshape(shp), b_im.reshape(shp))
    un = lambda a: a.reshape(g, n, c)
    return un(lre)[..., 0], un(lim)[..., 0], un(bbre), un(bbim)


def _s5_kernel(u_ref, bblk_ref, cblk_ref, lre_ref, lim_ref, d_ref, wg_ref, bg_ref, y_ref,
               h_ref, hr_ref, hi_ref, *, tc, bsz, ns, lane_chunk):
    @pl.when(pl.program_id(0) == 0)
    def _():
        hr_ref[...] = jnp.zeros(hr_ref.shape, F32)
        hi_ref[...] = jnp.zeros(hi_ref.shape, F32)

    w = u_ref.shape[2]
    u = jnp.swapaxes(u_ref[...], 0, 1).reshape(tc * bsz, w)
    nsg = bblk_ref.shape[0]
    lanes = V7X_LANES
    for sg in range(nsg):
        c0 = sg * lane_chunk
        drive = _bdot(u[:, sg * lanes:(sg + 1) * lanes], bblk_ref[sg])
        h_ref[:, c0:c0 + lane_chunk] = drive[:, 0:lane_chunk]
        h_ref[:, ns + c0:ns + c0 + lane_chunk] = drive[:, lane_chunk:2 * lane_chunk]
    for c0 in range(0, ns, lane_chunk):
        lr = jnp.broadcast_to(lre_ref[:, c0:c0 + lane_chunk], (bsz, lane_chunk))
        li = jnp.broadcast_to(lim_ref[:, c0:c0 + lane_chunk], (bsz, lane_chunk))

        def step(t, carry, c0=c0, lr=lr, li=li):
            hr, hi = carry
            rows = pl.ds(pl.multiple_of(t * bsz, bsz), bsz)
            nr = lr * hr - li * hi + h_ref[rows, c0:c0 + lane_chunk]
            ni = lr * hi + li * hr + h_ref[rows, ns + c0:ns + c0 + lane_chunk]
            h_ref[rows, c0:c0 + lane_chunk] = nr
            h_ref[rows, ns + c0:ns + c0 + lane_chunk] = ni
            return nr, ni

        hr, hi = lax.fori_loop(0, tc, step,
                               (hr_ref[:, c0:c0 + lane_chunk], hi_ref[:, c0:c0 + lane_chunk]))
        hr_ref[:, c0:c0 + lane_chunk] = hr
        hi_ref[:, c0:c0 + lane_chunk] = hi
    outs = []
    for sg in range(nsg):
        c0 = sg * lane_chunk
        outs.append(_bdot(h_ref[:, c0:c0 + lane_chunk], cblk_ref[sg, 0:lane_chunk, :])
                    + _bdot(h_ref[:, ns + c0:ns + c0 + lane_chunk], cblk_ref[sg, lane_chunk:, :]))
    y = jnp.concatenate(outs, axis=-1) + d_ref[...] * u
    y = jax.nn.gelu(y)
    y = y * jax.nn.sigmoid(_bdot(y, wg_ref[...]) + bg_ref[...])
    y_ref[...] = jnp.swapaxes(y.reshape(tc, bsz, w), 0, 1).astype(y_ref.dtype)


def _s5(z3d, col0, w, p, tc):
    bsz, seq, _ = z3d.shape
    ns = p["lam_re"].shape[1]
    blk = tc * bsz
    full = lambda a: pl.BlockSpec(a.shape, lambda i: (0,) * a.ndim)
    return pl.pallas_call(
        functools.partial(_s5_kernel, tc=tc, bsz=bsz, ns=ns, lane_chunk=ns // p["bblk"].shape[0]),
        grid=(seq // tc,),
        in_specs=[pl.BlockSpec((bsz, tc, w), lambda i: (0, i, col0 // w)),
                  full(p["bblk"]), full(p["cblk"]), full(p["lam_re"]), full(p["lam_im"]),
                  full(p["d"]), full(p["w_glu"]), full(p["b_glu"])],
        out_specs=pl.BlockSpec((bsz, tc, w), lambda i: (0, i, 0)),
        out_shape=jax.ShapeDtypeStruct((bsz, seq, w), BF16),
        scratch_shapes=[pltpu.VMEM((blk, 2 * ns), F32),
                        pltpu.VMEM((bsz, ns), F32), pltpu.VMEM((bsz, ns), F32)],
        compiler_params=_cparams("arbitrary"),
        name="s5",
    )(z3d, p["bblk"], p["cblk"], p["lam_re"], p["lam_im"], p["d"], p["w_glu"], p["b_glu"])


def _mb_kernel(gate_ref, x_ref, bc_ref, dt_ref, cw_ref, cb_ref, dtb_ref, alog_ref, dsk_ref, nw_ref,
               expand_ref, tril_ref, after_ref, y_ref, prev_ref, st_ref,
               *, q, w, heads, groups, nstate):
    hd = w // heads
    gw = w // groups
    hpg = heads // groups

    @pl.when(pl.program_id(1) == 0)
    def _():
        prev_ref[...] = jnp.zeros(prev_ref.shape, F32)
        st_ref[...] = jnp.zeros(st_ref.shape, F32)

    rid = lax.broadcasted_iota(jnp.int32, (q, q), 0)
    cid = lax.broadcasted_iota(jnp.int32, (q, q), 1)
    causal = rid >= cid
    lane_w = lax.broadcasted_iota(jnp.int32, (q, gw), 1)
    expand = expand_ref[...]
    prev = prev_ref[...]
    for ci in range(x_ref.shape[1] // q):
        rows = pl.ds(ci * q, q)
        prev = _mb_chunk(rows, prev, causal, lane_w, expand, gate_ref, x_ref, bc_ref, dt_ref, cw_ref,
                         cb_ref, dtb_ref, alog_ref, dsk_ref, nw_ref, tril_ref, y_ref, st_ref,
                         q=q, w=w, groups=groups, nstate=nstate, gw=gw, hpg=hpg, hd=hd)
    prev_ref[...] = prev


def _mb_chunk(rows, prev, causal, lane_w, expand, gate_ref, x_ref, bc_ref, dt_ref, cw_ref, cb_ref,
              dtb_ref, alog_ref, dsk_ref, nw_ref, tril_ref, y_ref, st_ref,
              *, q, w, groups, nstate, gw, hpg, hd):
    sub = V7X_SUBLANES
    xbc = jnp.concatenate([x_ref[0, rows, :], bc_ref[0, rows, :]], axis=-1)
    full = jnp.concatenate([prev, xbc], axis=0)
    conv = jnp.broadcast_to(cb_ref[...], xbc.shape)
    for j in range(MB_CONV):
        shift = MB_CONV - 1 - j
        src = full if shift == 0 else pltpu.roll(full, shift, axis=0)
        conv = conv + src[sub:sub + q, :] * cw_ref[j:j + 1, :]
    act = _silu(conv)
    xs = act[:, 0:w]
    bmat = act[:, w:w + groups * nstate]
    cmat = act[:, w + groups * nstate:w + 2 * groups * nstate]

    dt = _softplus(dt_ref[0, rows, :] + dtb_ref[...])
    a = -jnp.exp(alog_ref[...]) * dt
    a_cum = _dot_exact_lhs(tril_ref[...], a)
    dt_e = _dot_exact_rhs(dt, expand)
    acum_e = _dot_exact_rhs(a_cum, expand)
    alast_e = acum_e[q - 1:q, :]
    xdt = xs * dt_e
    xdec = xdt * jnp.exp(alast_e - acum_e)
    a_cum_t = a_cum.T

    y_parts = []
    for g in range(groups):
        bg = bmat[:, g * nstate:(g + 1) * nstate]
        cg = cmat[:, g * nstate:(g + 1) * nstate]
        scores = lax.dot_general(cg.astype(BF16), bg.astype(BF16), (((1,), (1,)), ((), ())),
                                 preferred_element_type=F32)
        xg = xdt[:, g * gw:(g + 1) * gw]
        yg = _bdot(cg, st_ref[g]) * jnp.exp(acum_e[:, g * gw:(g + 1) * gw])
        for hh in range(hpg):
            h = g * hpg + hh
            col = jnp.broadcast_to(a_cum[:, h:h + 1], (q, q))
            rowv = jnp.broadcast_to(a_cum_t[h:h + 1, :], (q, q))
            decay = jnp.where(causal, jnp.exp(col - rowv), 0.0)
            xh = jnp.where((lane_w >= hh * hd) & (lane_w < (hh + 1) * hd), xg, 0.0)
            yg = yg + _bdot(scores * decay, xh)
        y_parts.append(yg)
        upd = _bdot(bg.T, xdec[:, g * gw:(g + 1) * gw])
        st_ref[g] = st_ref[g] * jnp.exp(alast_e[:, g * gw:(g + 1) * gw]) + upd
    y = jnp.concatenate(y_parts, axis=-1) + dsk_ref[...] * xs
    y_ref[0, rows, :] = _rms(y * _silu(gate_ref[0, rows, :]), nw_ref[...]).astype(y_ref.dtype)
    return xbc[q - sub:q, :]


def _mamba2(z3d, p, w, cols, q, tb, after):
    bsz, seq, _ = z3d.shape
    heads = w // MB_HEADDIM
    lanes = V7X_LANES
    cg, cx, cbc, cdt = cols
    blk = lambda width, col: pl.BlockSpec((1, tb, width), lambda b, c: (b, c, col // width))
    full = lambda a: pl.BlockSpec(a.shape, lambda b, c: (0,) * a.ndim)
    consts = [p["conv_w"], p["conv_b"], p["dt_bias"], p["a_log"], p["d"], p["norm_w"],
              p["expand"], p["tril"]]
    return pl.pallas_call(
        functools.partial(_mb_kernel, q=q, w=w, heads=heads, groups=MB_GROUPS, nstate=MB_STATE),
        grid=(bsz, seq // tb),
        in_specs=[blk(w, cg), blk(w, cx), blk(w, cbc), blk(lanes, cdt)] + [full(a) for a in consts]
        + [pl.BlockSpec((1, V7X_SUBLANES, lanes), lambda b, c: (0, 0, 0))],
        out_specs=pl.BlockSpec((1, tb, w), lambda b, c: (b, c, 0)),
        out_shape=jax.ShapeDtypeStruct((bsz, seq, w), BF16),
        scratch_shapes=[pltpu.VMEM((V7X_SUBLANES, 2 * w), F32),
                        pltpu.VMEM((MB_GROUPS, MB_STATE, w // MB_GROUPS), F32)],
        compiler_params=_cparams("parallel", "arbitrary"),
        name="mamba2",
    )(z3d, z3d, z3d, z3d, *consts, after)


def _pad_rows(a, rows, at):
    out = jnp.zeros((rows, a.shape[1]), a.dtype)
    return lax.dynamic_update_slice(out, a, (at, 0))


def _block_ones(w, head):
    idx = jnp.arange(w) // head
    return (idx[:, None] == idx[None, :]).astype(BF16)


def _pick_tile(n, target):
    t = min(n, target)
    while n % t:
        t //= 2
    return t


def kernel(x, norm_mix_w, w_in, w_branch, w_out, norm_ffn_w, w_ffn_in, w_ffn_out, norm_final_w, hgrn_lower_bounds, hgrn_norm_w, rwkv_mu, rwkv_w0, rwkv_w2, rwkv_a0, rwkv_a2, rwkv_g2, rwkv_k_k, rwkv_k_a, rwkv_r_k, rwkv_ln_w, rwkv_ln_b, rwkv_v0, rwkv_v1, rwkv_v2, s5_a_re, s5_a_im, s5_b_re, s5_b_im, s5_c_re, s5_c_im, s5_d, s5_log_dt, s5_w_glu, s5_b_glu, mamba_conv_w, mamba_conv_b, mamba_dt_bias, mamba_a_log, mamba_d, mamba_norm_w):
    bsz, seq, d = x.shape
    depth = w_in.shape[0]
    w = d // 2
    lanes = V7X_LANES
    t = bsz * seq
    mb_heads = w // MB_HEADDIM
    mb_bc = 2 * MB_GROUPS * MB_STATE
    lw = RW_DECAY_LORA + RW_A_LORA + RW_G_LORA
    assert mb_bc == w and 3 * w % lw == 0

    o_gate = 0
    o_hg = o_gate + N_BRANCH * d
    o_rw = o_hg + 4 * w
    o_rwl = o_rw + 3 * w
    o_s5 = o_rwl + lw
    o_mbg = o_s5 + w
    o_mbx = o_mbg + w
    o_mbbc = o_mbx + w
    o_mbdt = o_mbbc + mb_bc
    c_rw = 0
    c_hg = c_rw + 3 * w
    c_s5 = c_hg + 4 * w
    c_mbg = c_s5 + w
    c_mbx = c_mbg + w
    c_mbbc = c_mbx + w
    c_rwl = c_mbbc + mb_bc
    c_mbdt = c_rwl + lw
    n_cols = c_mbdt + lanes
    tn = 512
    n_pad = -(-n_cols // tn) * tn

    def mixer_cols(wl):
        sl = lambda a, b: wl[:, a:b]
        pieces = [sl(o_rw, o_rwl), sl(o_hg, o_rw), sl(o_s5, o_mbdt), sl(o_rwl, o_s5),
                  sl(o_mbdt, o_mbdt + mb_heads),
                  jnp.zeros((d, n_pad - c_mbdt - mb_heads), wl.dtype)]
        return jnp.concatenate(pieces, axis=1).astype(BF16)

    tm = _pick_tile(seq, 256)
    tm_in = _pick_tile(seq, 512)
    tb_hg = _pick_tile(seq, 512)
    c_hg_chunk = min(HG_CHUNK, tb_hg)
    tc_rw = _pick_tile(seq, 64)
    tc_s5 = _pick_tile(seq, 128)
    q_mb = min(MB_CHUNK, seq)
    tb_mb = _pick_tile(seq, 4 * q_mb)
    ff = w_ffn_out.shape[1]
    tn_ff = 256 if ff % 256 == 0 else lanes

    lower_bounds = _hg_bounds(hgrn_lower_bounds)
    ones_rw = _block_ones(w, RW_HEAD)
    eye_g = jnp.eye(lanes // S5_GROUP, dtype=F32)
    expand = (jnp.arange(lanes)[:, None] == (jnp.arange(w) // MB_HEADDIM)[None, :]).astype(BF16)
    tril = (jnp.arange(q_mb)[:, None] >= jnp.arange(q_mb)[None, :]).astype(BF16)

    x2d = x.reshape(t, d)
    v_first = jnp.zeros((t, w), F32)
    for l in range(depth):
        x3d = x2d.reshape(bsz, seq, d)
        z2d = _inproj(x2d, norm_mix_w[l], mixer_cols(w_in[l]), tm_in, tn, False, x3d)
        z3d = z2d.reshape(bsz, seq, n_pad)

        mu = rwkv_mu[l]
        has_vres = l > 0
        lv = max(l - 1, 0)
        rp = {
            "mu_main": mu[None, :3 * w], "mu_lora": mu[None, 3 * w:],
            "w0": rwkv_w0[l][None], "a0": rwkv_a0[l][None],
            "w2p": _pad_rows(rwkv_w2[l], lw, 0).astype(BF16),
            "a2p": _pad_rows(rwkv_a2[l], lw, RW_DECAY_LORA).astype(BF16),
            "g2p": _pad_rows(rwkv_g2[l], lw, RW_DECAY_LORA + RW_A_LORA).astype(BF16),
            "k_k": rwkv_k_k[l][None], "k_a": rwkv_k_a[l][None], "r_k": rwkv_r_k[l].reshape(1, w),
            "v0": rwkv_v0[lv][None],
            "v1p": jnp.pad(rwkv_v1[lv], ((0, 0), (0, lanes - rwkv_v1.shape[2]))).astype(BF16),
            "v2p": _pad_rows(rwkv_v2[lv], lanes, 0).astype(BF16),
            "ones": ones_rw, "ln_w": rwkv_ln_w[l], "ln_b": rwkv_ln_b[l],
        }
        rw_x, v_l, rw_g, rw_bonus = _rwkv7_front(z2d, v_first, rp, bsz, seq, w, c_rw, c_rwl, lw, tm,
                                                 tc_rw, has_vres)
        if l == 0:
            v_first = v_l

        lam_re, lam_im, bb_re, bb_im = _s5_discretize(s5_a_re[l], s5_a_im[l], s5_log_dt[l],
                                                      s5_b_re[l], s5_b_im[l])
        ns = lam_re.size
        nsg = w // lanes
        gps = lanes // S5_GROUP
        blk_in = lambda bb: jnp.einsum(
            "sgnc,gh->sgchn", bb.reshape(nsg, gps, S5_STATE, S5_GROUP), eye_g
        ).reshape(nsg, lanes, gps * S5_STATE)
        blk_out = lambda cc: jnp.einsum(
            "sgcn,gh->sgnhc", cc.reshape(nsg, gps, S5_GROUP, S5_STATE), eye_g
        ).reshape(nsg, gps * S5_STATE, lanes)
        sp = {
            "bblk": jnp.concatenate([blk_in(bb_re), blk_in(bb_im)], axis=2).astype(BF16),
            "cblk": jnp.concatenate([blk_out(s5_c_re[l]), -blk_out(s5_c_im[l])], axis=1).astype(BF16),
            "lam_re": lam_re.reshape(1, ns), "lam_im": lam_im.reshape(1, ns),
            "d": s5_d[l][None], "w_glu": s5_w_glu[l].astype(BF16), "b_glu": s5_b_glu[l][None],
        }
        y_s5 = _s5(z3d, c_s5, w, sp, tc_s5).reshape(t, w)

        mp = {
            "conv_w": mamba_conv_w[l], "conv_b": mamba_conv_b[l][None],
            "dt_bias": jnp.pad(mamba_dt_bias[l], (0, lanes - mb_heads))[None],
            "a_log": jnp.pad(mamba_a_log[l], (0, lanes - mb_heads))[None],
            "d": jnp.repeat(mamba_d[l], MB_HEADDIM)[None], "norm_w": mamba_norm_w[l][None],
            "expand": expand, "tril": tril,
        }
        y_hg = _hgrn2(z3d, lower_bounds[l], hgrn_norm_w[l], w, c_hg, tb_hg, c_hg_chunk,
                      HG_SUBBLOCK, z3d).reshape(t, w)
        rw_y = _rw_scan(rw_x, lanes // (bsz * (w // RW_HEAD)), tc_rw, (y_hg, y_s5))
        gates = _inproj(x2d, norm_mix_w[l], w_in[l][:, o_gate:o_hg].astype(BF16), tm_in, tn, True,
                        rw_y)
        y_mb = _mamba2(z3d, mp, w, (c_mbg, c_mbx, c_mbbc, c_mbdt), q_mb, tb_mb, rw_y).reshape(t, w)
        rw_heads = w // RW_HEAD
        rw_raw = _lanes_to_chains_v(rw_y, bsz, seq, rw_heads, RW_HEAD, lanes // (bsz * rw_heads))
        x2d = _merge(y_hg, y_s5, y_mb, (rw_raw, rw_g, rw_bonus, rp["ln_w"], rp["ln_b"], rp["ones"]),
                     gates, w_branch[l].astype(BF16), w_out[l].astype(BF16), x2d, tm)
        act = _ffn_in(x2d, norm_ffn_w[l], w_ffn_in[l].astype(BF16), tm_in, tn_ff)
        x2d = _ffn_out(act, w_ffn_out[l].astype(BF16), x2d, norm_final_w, l == depth - 1, tm)
    return x2d.reshape(bsz, seq, d)
```
